```python
import math
import jax, jax.numpy as jnp
from jax import lax
import numpy as np

D_MODEL = 2048
BATCH = 16
SEQ = 2048
DEPTH = 1
DEC_BATCH = 8
DEC_SEQ = 64
PAST_LEN = 1024

CHUNK = 64
Q_BLOCK = 128
N_MEM = 256
EPS = 1e-6
SSD_D_INNER = D_MODEL
SSD_HEAD_DIM = 64
SSD_HEADS = SSD_D_INNER // SSD_HEAD_DIM
SSD_GROUPS = 4
SSD_D_STATE = 128
SSD_CONV = 4
SSD_CONV_CH = SSD_D_INNER + 2 * SSD_GROUPS * SSD_D_STATE
FOX_HEADS = 16
FOX_HEAD_DIM = 128
FOX_WIDTH = FOX_HEADS * FOX_HEAD_DIM
FORGET_BIAS_INIT = 3.0
MEM_HEADS = 4
MEM_HEAD_DIM = 512
MEM_WIDTH = MEM_HEADS * MEM_HEAD_DIM
SIZES = [SSD_D_INNER, SSD_CONV_CH, SSD_HEADS, FOX_WIDTH, FOX_WIDTH, FOX_WIDTH, FOX_WIDTH, FOX_HEADS,
         MEM_WIDTH, MEM_WIDTH, D_MODEL, D_MODEL, D_MODEL]
IN_WIDTH = sum(SIZES)
SPLITS = [int(v) for v in np.cumsum(SIZES)[:-1]]

kernel_name = 'hybrid_ssd_fox_mem_stream_step'


def rmsnorm(x, g):
    xf = x.astype(jnp.float32)
    y = xf * lax.rsqrt(jnp.mean(xf * xf, axis=-1, keepdims=True) + EPS)
    return (y * g.astype(jnp.float32)).astype(x.dtype)


def causal_conv(x_pad, w, b):
    T = x_pad.shape[1] - (SSD_CONV - 1)
    out = b
    for i in range(SSD_CONV):
        out = out + x_pad[:, i:i + T] * w[i]
    return out


def ssd_scan(x, dt, A, Bm, Cm, h0, chunk):
    f32 = jnp.float32
    b, T, H, P = x.shape
    G, N = Bm.shape[2], Bm.shape[3]
    J = H // G
    nc = T // chunk
    xc = x.astype(f32).reshape(b, nc, chunk, G, J, P)
    Bc = Bm.astype(f32).reshape(b, nc, chunk, G, N)
    Cc = Cm.astype(f32).reshape(b, nc, chunk, G, N)
    dtc = dt.reshape(b, nc, chunk, G, J)
    acum = jnp.cumsum(dtc * A.reshape(G, J), axis=2)
    tri = (jnp.arange(chunk)[:, None] >= jnp.arange(chunk)[None, :])[:, :, None, None]
    diff = acum[:, :, :, None] - acum[:, :, None]
    ldec = jnp.exp(jnp.where(tri, diff, -jnp.inf))
    cb = jnp.einsum('bclgn,bcsgn->bclsg', Cc, Bc)
    m = cb[..., None] * ldec * dtc[:, :, None]
    y_diag = jnp.einsum('bclsgj,bcsgjp->bclgjp', m, xc)
    w_end = jnp.exp(acum[:, :, -1:] - acum) * dtc
    states = jnp.einsum('bcsgn,bcsgjp->bcgjpn', Bc, xc * w_end[..., None])
    chunk_dec = jnp.exp(acum[:, :, -1])

    def step(h, inp):
        dec, st = inp
        return h * dec[..., None, None] + st, h

    h_init = h0.astype(f32).reshape(b, G, J, P, N)
    h_final, h_in = lax.scan(step, h_init, (jnp.moveaxis(chunk_dec, 1, 0), jnp.moveaxis(states, 1, 0)))
    h_in = jnp.moveaxis(h_in, 0, 1)
    y_off = jnp.einsum('bclgn,bcgjpn->bclgjp', Cc, h_in) * jnp.exp(acum)[..., None]
    y = (y_diag + y_off).reshape(b, T, H, P)
    return y, h_final.reshape(b, H, P, N)


def fox_attend(q, k, v, cq, ck, q_pos, k_pos):
    s = jnp.einsum('bqhd,bkhd->bhqk', q, k).astype(jnp.float32) * (FOX_HEAD_DIM ** -0.5)
    s = s + jnp.transpose(cq, (0, 2, 1))[..., None] - jnp.transpose(ck, (0, 2, 1))[:, :, None, :]
    mask = k_pos[None, :] <= q_pos[:, None]
    p = jax.nn.softmax(jnp.where(mask, s, -jnp.inf), axis=-1).astype(v.dtype)
    return jnp.einsum('bhqk,bkhd->bqhd', p, v)


def mem_kv(mem, g_mem, w_mem_kv):
    b = mem.shape[0]
    kv = rmsnorm(mem, g_mem) @ w_mem_kv
    k, v = jnp.split(kv, [MEM_WIDTH], axis=-1)
    return (k.reshape(b, N_MEM, MEM_HEADS, MEM_HEAD_DIM), v.reshape(b, N_MEM, MEM_HEADS, MEM_HEAD_DIM))


def mem_attend(q, k, v):
    s = jnp.einsum('bqhd,bkhd->bhqk', q, k).astype(jnp.float32) * (MEM_HEAD_DIM ** -0.5)
    p = jax.nn.softmax(s, axis=-1).astype(v.dtype)
    return jnp.einsum('bhqk,bkhd->bqhd', p, v)


def mixer_layer(x, conv_buf, h0, k_past, v_past, logf_past, mem_k, mem_v,
                g_norm, w_in, w_conv, b_conv, dt_bias, a_log, d_skip, g_ssd_out, b_forget,
                w_o_ssd, w_o_fox, w_o_mem, w_out, ssd_chunk, q_block):
    f32 = jnp.float32
    b, T, _ = x.shape
    P = k_past.shape[1]
    h = rmsnorm(x, g_norm)
    u = h @ w_in
    z, xbc, dt_raw, fq, fk, fv, fg, ff, mq, mg, gs, gf, gm = jnp.split(u, SPLITS, axis=-1)

    xbc_pad = jnp.concatenate([conv_buf.astype(xbc.dtype), xbc], axis=1)
    new_conv = xbc_pad[:, -(SSD_CONV - 1):]
    xbc_c = jax.nn.silu(causal_conv(xbc_pad, w_conv, b_conv))
    xs, Bm, Cm = jnp.split(xbc_c, [SSD_D_INNER, SSD_D_INNER + SSD_GROUPS * SSD_D_STATE], axis=-1)
    dt = jax.nn.softplus(dt_raw.astype(f32) + dt_bias.astype(f32))
    A = -jnp.exp(a_log.astype(f32))
    xh = xs.reshape(b, T, SSD_HEADS, SSD_HEAD_DIM)
    y, h_final = ssd_scan(xh, dt, A, Bm.reshape(b, T, SSD_GROUPS, SSD_D_STATE),
                          Cm.reshape(b, T, SSD_GROUPS, SSD_D_STATE), h0, ssd_chunk)
    y = (y + d_skip.astype(f32)[:, None] * xh.astype(f32)).reshape(b, T, SSD_D_INNER).astype(x.dtype)
    y_ssd = rmsnorm(y * jax.nn.silu(z), g_ssd_out)

    q = fq.reshape(b, T, FOX_HEADS, FOX_HEAD_DIM)
    k_new = fk.reshape(b, T, FOX_HEADS, FOX_HEAD_DIM)
    v_new = fv.reshape(b, T, FOX_HEADS, FOX_HEAD_DIM)
    logf_new = jax.nn.log_sigmoid(ff.astype(f32) + b_forget.astype(f32))
    k_all = jnp.concatenate([k_past.astype(k_new.dtype), k_new], axis=1)
    v_all = jnp.concatenate([v_past.astype(v_new.dtype), v_new], axis=1)
    c_all = jnp.cumsum(jnp.concatenate([logf_past.astype(f32), logf_new], axis=1), axis=1)
    k_pos = jnp.arange(P + T)
    q_pos = P + jnp.arange(T)
    cq = c_all[:, P:]
    if q_block is None:
        o = fox_attend(q, k_all, v_all, cq, c_all, q_pos, k_pos)
    else:
        nb = T // q_block
        qb = q.reshape(b, nb, q_block, FOX_HEADS, FOX_HEAD_DIM).transpose(1, 0, 2, 3, 4)
        cqb = cq.reshape(b, nb, q_block, FOX_HEADS).transpose(1, 0, 2, 3)
        pb = q_pos.reshape(nb, q_block)
        o = lax.map(lambda a: fox_attend(a[0], k_all, v_all, a[1], c_all, a[2], k_pos), (qb, cqb, pb))
        o = o.transpose(1, 0, 2, 3, 4)
    y_fox = o.reshape(b, T, FOX_WIDTH) * jax.nn.silu(fg)

    om = mem_attend(mq.reshape(b, T, MEM_HEADS, MEM_HEAD_DIM), mem_k.astype(mq.dtype), mem_v.astype(mq.dtype))
    y_mem = om.reshape(b, T, MEM_WIDTH) * jax.nn.silu(mg)

    merged = (jax.nn.sigmoid(gs) * (y_ssd @ w_o_ssd) + jax.nn.sigmoid(gf) * (y_fox @ w_o_fox)
              + jax.nn.sigmoid(gm) * (y_mem @ w_o_mem))
    x_out = x + merged @ w_out
    return x_out, new_conv, h_final, k_new, v_new, logf_new


def setup_inputs(seed: int = 0) -> dict:
    key = jax.random.key(seed)
    ks = jax.random.split(key, 32)
    f32 = jnp.float32

    def nrm(k, shape, scale):
        return jax.random.normal(k, shape, f32) * scale

    L = DEPTH
    dt0 = jnp.exp(jax.random.uniform(ks[14], (L, SSD_HEADS), f32, math.log(1e-3), math.log(1e-1)))
    return {
        'x_prompt': nrm(ks[0], (BATCH, SEQ, D_MODEL), 1.0),
        'x_sample': nrm(ks[1], (DEC_BATCH, DEC_SEQ, D_MODEL), 1.0),
        'mem_prompt': nrm(ks[2], (BATCH, N_MEM, D_MODEL), 1.0),
        'cache_fox_k': nrm(ks[3], (L, DEC_BATCH, PAST_LEN, FOX_HEADS, FOX_HEAD_DIM), 1.0),
        'cache_fox_v': nrm(ks[4], (L, DEC_BATCH, PAST_LEN, FOX_HEADS, FOX_HEAD_DIM), 1.0),
        'cache_fox_logf': jax.nn.log_sigmoid(FORGET_BIAS_INIT + nrm(ks[5], (L, DEC_BATCH, PAST_LEN, FOX_HEADS), 1.0)),
        'state_ssd': nrm(ks[6], (L, DEC_BATCH, SSD_HEADS, SSD_HEAD_DIM, SSD_D_STATE), 0.1),
        'state_ssd_conv': nrm(ks[7], (L, DEC_BATCH, SSD_CONV - 1, SSD_CONV_CH), 1.0),
        'cache_mem_k': nrm(ks[8], (L, DEC_BATCH, N_MEM, MEM_HEADS, MEM_HEAD_DIM), 1.0),
        'cache_mem_v': nrm(ks[9], (L, DEC_BATCH, N_MEM, MEM_HEADS, MEM_HEAD_DIM), 1.0),
        'g_norm': 1.0 + nrm(ks[10], (L, D_MODEL), 0.01),
        'w_in': nrm(ks[11], (L, D_MODEL, IN_WIDTH), D_MODEL ** -0.5),
        'w_conv': nrm(ks[12], (L, SSD_CONV, SSD_CONV_CH), SSD_CONV ** -0.5),
        'b_conv': nrm(ks[13], (L, SSD_CONV_CH), 0.01),
        'dt_bias': dt0 + jnp.log(-jnp.expm1(-dt0)),
        'a_log': jnp.log(jax.random.uniform(ks[15], (L, SSD_HEADS), f32, 1.0, 16.0)),
        'd_skip': 1.0 + nrm(ks[16], (L, SSD_HEADS), 0.01),
        'g_ssd_out': 1.0 + nrm(ks[17], (L, SSD_D_INNER), 0.01),
        'b_forget': FORGET_BIAS_INIT + nrm(ks[18], (L, FOX_HEADS), 0.5),
        'g_mem': 1.0 + nrm(ks[19], (L, D_MODEL), 0.01),
        'w_mem_kv': nrm(ks[20], (L, D_MODEL, 2 * MEM_WIDTH), D_MODEL ** -0.5),
        'w_o_ssd': nrm(ks[21], (L, SSD_D_INNER, D_MODEL), SSD_D_INNER ** -0.5),
        'w_o_fox': nrm(ks[22], (L, FOX_WIDTH, D_MODEL), FOX_WIDTH ** -0.5),
        'w_o_mem': nrm(ks[23], (L, MEM_WIDTH, D_MODEL), MEM_WIDTH ** -0.5),
        'w_out': nrm(ks[24], (L, D_MODEL, D_MODEL), D_MODEL ** -0.5),
        'g_final': 1.0 + nrm(ks[25], (D_MODEL,), 0.01),
    }


def reference(x_prompt, x_sample, mem_prompt, cache_fox_k, cache_fox_v, cache_fox_logf, state_ssd,
              state_ssd_conv, cache_mem_k, cache_mem_v, g_norm, w_in, w_conv, b_conv, dt_bias, a_log,
              d_skip, g_ssd_out, b_forget, g_mem, w_mem_kv, w_o_ssd, w_o_fox, w_o_mem, w_out, g_final):
    f32 = jnp.float32
    xp, xs = x_prompt, x_sample
    bp = xp.shape[0]
    fk_p, fv_p, fl_p, ssd_p, conv_p, mk_p, mv_p = [], [], [], [], [], [], []
    fk_s, fv_s, fl_s, ssd_s, conv_s = [], [], [], [], []
    for l in range(DEPTH):
        lw = (g_norm[l], w_in[l], w_conv[l], b_conv[l], dt_bias[l], a_log[l], d_skip[l], g_ssd_out[l],
              b_forget[l], w_o_ssd[l], w_o_fox[l], w_o_mem[l], w_out[l])
        mk, mv = mem_kv(mem_prompt, g_mem[l], w_mem_kv[l])
        xp, c_new, h_new, k_new, v_new, lf_new = mixer_layer(
            xp, jnp.zeros((bp, SSD_CONV - 1, SSD_CONV_CH), xp.dtype),
            jnp.zeros((bp, SSD_HEADS, SSD_HEAD_DIM, SSD_D_STATE), f32),
            jnp.zeros((bp, 0, FOX_HEADS, FOX_HEAD_DIM), xp.dtype),
            jnp.zeros((bp, 0, FOX_HEADS, FOX_HEAD_DIM), xp.dtype),
            jnp.zeros((bp, 0, FOX_HEADS), f32), mk, mv, *lw, CHUNK, Q_BLOCK)
        fk_p.append(k_new); fv_p.append(v_new); fl_p.append(lf_new)
        ssd_p.append(h_new); conv_p.append(c_new); mk_p.append(mk); mv_p.append(mv)
        xs, c_new, h_new, k_new, v_new, lf_new = mixer_layer(
            xs, state_ssd_conv[l], state_ssd[l], cache_fox_k[l], cache_fox_v[l], cache_fox_logf[l],
            cache_mem_k[l], cache_mem_v[l], *lw, xs.shape[1], None)
        fk_s.append(k_new); fv_s.append(v_new); fl_s.append(lf_new)
        ssd_s.append(h_new); conv_s.append(c_new)
    y_prompt = rmsnorm(xp, g_final)
    y_sample = rmsnorm(xs, g_final)
    return (y_prompt, y_sample,
            jnp.stack(fk_p), jnp.stack(fv_p), jnp.stack(fl_p), jnp.stack(ssd_p), jnp.stack(conv_p),
            jnp.stack(mk_p), jnp.stack(mv_p),
            jnp.stack(fk_s), jnp.stack(fv_s), jnp.stack(fl_s), jnp.stack(ssd_s), jnp.stack(conv_s))
```

```python
import functools

import numpy as np
import jax
import jax.numpy as jnp
from jax import lax
from jax.experimental import pallas as pl
from jax.experimental.pallas import tpu as pltpu

F32 = jnp.float32
BF16 = jnp.bfloat16

EPS = 1e-6
D_MODEL = 2048
SSD_HEAD_DIM = 64
SSD_HEADS = 32
SSD_GROUPS = 4
SSD_D_STATE = 128
SSD_CONV = 4
SSD_BC = SSD_GROUPS * SSD_D_STATE
SSD_CONV_CH = D_MODEL + 2 * SSD_BC
FOX_HEADS = 16
FOX_HEAD_DIM = 128
MEM_HEADS = 4
MEM_HEAD_DIM = 512
SSD_CHUNK = 64
NEG_BIG = -1e30

VMEM_LIMIT = 56 * 1024 * 1024

COL_Z, COL_FQ, COL_FG, COL_MQ, COL_MG, COL_GS, COL_GF, COL_GM, COL_XS = range(9)
N_MAIN = 8 * D_MODEL + SSD_CONV_CH


def _params(*sem):
    return pltpu.CompilerParams(dimension_semantics=sem, vmem_limit_bytes=VMEM_LIMIT)


def _split3(x):
    hi = x.astype(BF16)
    r1 = x - hi.astype(F32)
    mid = r1.astype(BF16)
    lo = (r1 - mid.astype(F32)).astype(BF16)
    return hi, mid, lo


def _softplus(x):
    return jnp.maximum(x, 0.0) + jnp.log1p(jnp.exp(-jnp.abs(x)))


def _silu(x):
    return x * jax.nn.sigmoid(x)


def _rmsnorm_kernel(x_ref, g_ref, o_ref):
    x = x_ref[...]
    ms = jnp.mean(x * x, axis=-1, keepdims=True)
    o_ref[...] = (x * lax.rsqrt(ms + EPS) * g_ref[...]).astype(o_ref.dtype)


def _rmsnorm(x, g, tm):
    m, d = x.shape
    return pl.pallas_call(
        _rmsnorm_kernel,
        out_shape=jax.ShapeDtypeStruct((m, d), BF16),
        grid=(m // tm,),
        in_specs=[pl.BlockSpec((tm, d), lambda i: (i, 0)), pl.BlockSpec((1, d), lambda i: (0, 0))],
        out_specs=pl.BlockSpec((tm, d), lambda i: (i, 0)),
        compiler_params=_params("parallel"),
        name="rmsnorm",
    )(x, g.reshape(1, d))


def _mm_kernel(h_ref, w_ref, o_ref):
    o_ref[...] = jnp.dot(h_ref[...], w_ref[...], preferred_element_type=F32).astype(o_ref.dtype)


def _mm(h, w, out_dtype, tm, tn, name):
    m, k = h.shape
    n = w.shape[1]
    return pl.pallas_call(
        _mm_kernel,
        out_shape=jax.ShapeDtypeStruct((m, n), out_dtype),
        grid=(m // tm, n // tn),
        in_specs=[pl.BlockSpec((tm, k), lambda i, j: (i, 0)), pl.BlockSpec((k, tn), lambda i, j: (0, j))],
        out_specs=pl.BlockSpec((tm, tn), lambda i, j: (i, j)),
        compiler_params=_params("parallel", "arbitrary"),
        name=name,
    )(h, w)


def _mm_rows_kernel(wt_ref, h_ref, o_ref):
    o_ref[...] = lax.dot_general(wt_ref[...], h_ref[...], (((1,), (1,)), ((), ())), preferred_element_type=F32)


def _mm_rows(wt, h, tm, name):
    r, k = wt.shape
    m = h.shape[0]
    return pl.pallas_call(
        _mm_rows_kernel,
        out_shape=jax.ShapeDtypeStruct((r, m), F32),
        grid=(m // tm,),
        in_specs=[pl.BlockSpec((r, k), lambda i: (0, 0)), pl.BlockSpec((tm, k), lambda i: (i, 0))],
        out_specs=pl.BlockSpec((r, tm), lambda i: (0, i)),
        compiler_params=_params("parallel"),
        name=name,
    )(wt, h)


def _upper3(tb):
    k = lax.broadcasted_iota(jnp.int32, (3 * tb, tb), 0) % tb
    t = lax.broadcasted_iota(jnp.int32, (3 * tb, tb), 1)
    return jnp.where(k <= t, 1.0, 0.0).astype(BF16)


def _cumsum_lanes(x, carry, tb):
    n = x.shape[1]
    u3 = _upper3(tb)
    out = []
    for s in range(0, n, tb):
        hi, mid, lo = _split3(x[:, s:s + tb])
        c = jnp.dot(jnp.concatenate([hi, mid, lo], axis=1), u3, preferred_element_type=F32) + carry
        carry = c[:, tb - 1:tb]
        out.append(c)
    return out, carry


def _logf_kernel(*refs, nb, t, p):
    if p:
        fft_ref, bias_ref, past_ref, lf_ref, c_ref = refs
    else:
        fft_ref, bias_ref, lf_ref, c_ref = refs
    lf_all = -_softplus(-(fft_ref[...] + bias_ref[...]))
    for b in range(nb):
        lf = lf_all[:, b * t:(b + 1) * t]
        lf_ref[b] = lf
        carry = jnp.zeros((FOX_HEADS, 1), F32)
        if p:
            tbp = min(256, p)
            blocks, carry = _cumsum_lanes(past_ref[b], carry, tbp)
            for i, c in enumerate(blocks):
                c_ref[b, :, i * tbp:(i + 1) * tbp] = c
        tb = min(256, t)
        blocks, carry = _cumsum_lanes(lf, carry, tb)
        for i, c in enumerate(blocks):
            c_ref[b, :, p + i * tb:p + (i + 1) * tb] = c


def _logf(fft, b_forget, past_t, nbatch, t):
    p = 0 if past_t is None else past_t.shape[2]
    nb = 1 if t % 128 == 0 else nbatch
    grid = (nbatch // nb,)
    in_specs = [pl.BlockSpec((FOX_HEADS, nb * t), lambda i: (0, i)),
                pl.BlockSpec((FOX_HEADS, 1), lambda i: (0, 0))]
    args = [fft, b_forget.reshape(FOX_HEADS, 1)]
    if p:
        in_specs.append(pl.BlockSpec((nb, FOX_HEADS, p), lambda i: (i, 0, 0)))
        args.append(past_t)
    return pl.pallas_call(
        functools.partial(_logf_kernel, nb=nb, t=t, p=p),
        out_shape=(jax.ShapeDtypeStruct((nbatch, FOX_HEADS, t), F32),
                   jax.ShapeDtypeStruct((nbatch, FOX_HEADS, p + t), F32)),
        grid=grid,
        in_specs=in_specs,
        out_specs=(pl.BlockSpec((nb, FOX_HEADS, t), lambda i: (i, 0, 0)),
                   pl.BlockSpec((nb, FOX_HEADS, p + t), lambda i: (i, 0, 0))),
        compiler_params=_params("parallel"),
        name="logf_cumsum",
    )(*args)


def _expand_matrix():
    r = np.arange(128)[:, None]
    c = np.arange(D_MODEL)[None, :]
    return jnp.asarray(((r < 96) & ((r % SSD_HEADS) == (c // SSD_HEAD_DIM))).astype(np.float32), dtype=BF16)


def _pack3(x):
    lane = lax.broadcasted_iota(jnp.int32, x.shape, 1)
    x = jnp.where(lane < SSD_HEADS, x, 0.0)
    hi, mid, lo = _split3(x)
    packed = hi.astype(F32) + pltpu.roll(mid.astype(F32), SSD_HEADS, 1) + pltpu.roll(lo.astype(F32), 2 * SSD_HEADS, 1)
    return packed.astype(BF16)


def _ssd_kernel(xs_ref, b_ref, c_ref, z_ref, sm_ref, conv0_ref, h0_ref,
                wconv_ref, bconv_ref, dtb_ref, alog_ref, dskip_ref, gout_ref, e3_ref,
                y_ref, hfin_ref, convout_ref,
                st_ref, xpad_ref):
    ci = pl.program_id(1)
    L = xs_ref.shape[0]
    P = SSD_HEAD_DIM
    W = D_MODEL

    @pl.when(ci == 0)
    def _init():
        st_ref[...] = h0_ref[...].T
        xpad_ref[0:8, :] = conv0_ref[...]

    xpad_ref[8:8 + L, 0:W] = xs_ref[...].astype(F32)
    xpad_ref[8:8 + L, W:W + SSD_BC] = b_ref[...].astype(F32)
    xpad_ref[8:8 + L, W + SSD_BC:] = c_ref[...].astype(F32)

    def conv(lo, hi):
        acc = bconv_ref[:, lo:hi] + xpad_ref[8:8 + L, lo:hi] * wconv_ref[3:4, lo:hi]
        for i in range(SSD_CONV - 1):
            acc = acc + xpad_ref[5 + i:5 + i + L, lo:hi] * wconv_ref[i:i + 1, lo:hi]
        return _silu(acc)

    xs = conv(0, W)
    bm = conv(W, W + SSD_BC).astype(BF16)
    cm = conv(W + SSD_BC, W + 2 * SSD_BC).astype(BF16)

    @pl.when(ci == pl.num_programs(1) - 1)
    def _conv_out():
        convout_ref[...] = xpad_ref[L:L + 8, :]
    xpad_ref[0:8, :] = xpad_ref[L:L + 8, :]

    dt = _softplus(sm_ref[...] + dtb_ref[...])
    a_small = -jnp.exp(alog_ref[...])
    lane = lax.broadcasted_iota(jnp.int32, (L, 128), 1)
    dta = jnp.where(lane < SSD_HEADS, dt * a_small, 0.0)
    hi, mid, lo = _split3(dta)
    kk = lax.broadcasted_iota(jnp.int32, (L, 3 * L), 1) % L
    ll = lax.broadcasted_iota(jnp.int32, (L, 3 * L), 0)
    tri3 = jnp.where(kk <= ll, 1.0, 0.0).astype(BF16)
    acum = jnp.dot(tri3, jnp.concatenate([hi, mid, lo], axis=0), preferred_element_type=F32)

    e3 = e3_ref[...]
    a_x = jnp.dot(_pack3(acum), e3, preferred_element_type=F32)
    dt_x = jnp.dot(_pack3(dt), e3, preferred_element_type=F32)

    row = lax.broadcasted_iota(jnp.int32, (L, W), 0)
    sidx = lax.broadcasted_iota(jnp.int32, (L, W), 1) % L
    a_row = jnp.sum(jnp.where(row == sidx, a_x, 0.0), axis=0, keepdims=True)
    ldec = jnp.exp(jnp.where(row >= sidx, a_x - a_row, NEG_BIG))

    hpm = 256 // L
    cbx = []
    for g in range(SSD_GROUPS):
        cg = cm[:, g * SSD_D_STATE:(g + 1) * SSD_D_STATE]
        bg = bm[:, g * SSD_D_STATE:(g + 1) * SSD_D_STATE]
        brep = jnp.concatenate([bg] * (SSD_HEADS // SSD_GROUPS), axis=0)
        cbx.append(lax.dot_general(cg, brep, (((1,), (1,)), ((), ())), preferred_element_type=F32))
    m = (jnp.concatenate(cbx, axis=1) * ldec).astype(BF16)

    xdt = xs * dt_x
    xdt_b = xdt.astype(BF16)
    rb = lax.broadcasted_iota(jnp.int32, (256, hpm * P), 0) // L
    cb = lax.broadcasted_iota(jnp.int32, (256, hpm * P), 1) // P
    blockmask = rb == cb
    y_parts = []
    for j in range(SSD_HEADS // hpm):
        xj = xdt_b[:, j * hpm * P:(j + 1) * hpm * P]
        bd = jnp.where(blockmask, jnp.concatenate([xj] * hpm, axis=0), jnp.zeros((), BF16))
        y_parts.append(jnp.dot(m[:, j * 256:(j + 1) * 256], bd, preferred_element_type=F32))
    y = jnp.concatenate(y_parts, axis=1)

    st = st_ref[...]
    st_b = st.astype(BF16)
    gw = W // SSD_GROUPS
    yo = [jnp.dot(cm[:, g * SSD_D_STATE:(g + 1) * SSD_D_STATE], st_b[:, g * gw:(g + 1) * gw],
                  preferred_element_type=F32) for g in range(SSD_GROUPS)]
    y = y + jnp.concatenate(yo, axis=1) * jnp.exp(a_x)

    a_last = a_x[L - 1:L, :]
    xw = (xdt * jnp.exp(a_last - a_x)).astype(BF16)
    upd = [lax.dot_general(bm[:, g * SSD_D_STATE:(g + 1) * SSD_D_STATE], xw[:, g * gw:(g + 1) * gw],
                           (((0,), (0,)), ((), ())), preferred_element_type=F32) for g in range(SSD_GROUPS)]
    st_new = st * jnp.exp(a_last) + jnp.concatenate(upd, axis=1)
    st_ref[...] = st_new

    @pl.when(ci == pl.num_programs(1) - 1)
    def _state_out():
        hfin_ref[...] = st_new.T

    y = y + dskip_ref[...] * xs
    gz = y * _silu(z_ref[...].astype(F32))
    ms = jnp.mean(gz * gz, axis=-1, keepdims=True)
    y_ref[...] = (gz * lax.rsqrt(ms + EPS) * gout_ref[...]).astype(y_ref.dtype)


def _ssd(u, small, conv0, h0, w_conv, b_conv, dt_bias, a_log, d_skip, g_ssd_out, nbatch, t):
    L = SSD_CHUNK
    nc = t // L
    m = nbatch * t
    pad128 = lambda v: jnp.pad(v.astype(F32), (0, 128 - v.shape[0])).reshape(1, 128)
    rep = lambda v: jnp.repeat(v.astype(F32), SSD_HEAD_DIM).reshape(1, D_MODEL)
    row = lambda b, c: b * nc + c
    const2 = lambda shape: pl.BlockSpec(shape, lambda b, c: (0, 0))
    y, hfin, convout = pl.pallas_call(
        _ssd_kernel,
        out_shape=(jax.ShapeDtypeStruct((m, D_MODEL), BF16),
                   jax.ShapeDtypeStruct((nbatch, D_MODEL, SSD_D_STATE), F32),
                   jax.ShapeDtypeStruct((nbatch, 8, SSD_CONV_CH), F32)),
        grid=(nbatch, nc),
        in_specs=[
            pl.BlockSpec((L, D_MODEL), lambda b, c: (row(b, c), COL_XS)),
            pl.BlockSpec((L, SSD_BC), lambda b, c: (row(b, c), (COL_XS + 1) * D_MODEL // SSD_BC)),
            pl.BlockSpec((L, SSD_BC), lambda b, c: (row(b, c), (COL_XS + 1) * D_MODEL // SSD_BC + 1)),
            pl.BlockSpec((L, D_MODEL), lambda b, c: (row(b, c), COL_Z)),
            pl.BlockSpec((L, 128), lambda b, c: (row(b, c), 0)),
            pl.BlockSpec((None, 8, SSD_CONV_CH), lambda b, c: (b, 0, 0)),
            pl.BlockSpec((None, D_MODEL, SSD_D_STATE), lambda b, c: (b, 0, 0)),
            const2((SSD_CONV, SSD_CONV_CH)), const2((1, SSD_CONV_CH)),
            const2((1, 128)), const2((1, 128)), const2((1, D_MODEL)), const2((1, D_MODEL)),
            const2((128, D_MODEL)),
        ],
        out_specs=(pl.BlockSpec((L, D_MODEL), lambda b, c: (row(b, c), 0)),
                   pl.BlockSpec((None, D_MODEL, SSD_D_STATE), lambda b, c: (b, 0, 0)),
                   pl.BlockSpec((None, 8, SSD_CONV_CH), lambda b, c: (b, 0, 0))),
        scratch_shapes=[pltpu.VMEM((SSD_D_STATE, D_MODEL), F32), pltpu.VMEM((L + 8, SSD_CONV_CH), F32)],
        compiler_params=_params("parallel", "arbitrary"),
        name="ssd_scan",
    )(u, u, u, u, small, conv0, h0.reshape(nbatch, D_MODEL, SSD_D_STATE),
      w_conv, b_conv.reshape(1, SSD_CONV_CH), pad128(dt_bias), pad128(a_log), rep(d_skip),
      g_ssd_out.reshape(1, D_MODEL), _expand_matrix())
    return y, hfin.reshape(nbatch, SSD_HEADS, SSD_HEAD_DIM, SSD_D_STATE), convout[:, 8 - (SSD_CONV - 1):]


def _col_from_row(row_vals):
    n = row_vals.shape[1]
    r = lax.broadcasted_iota(jnp.int32, (n, n), 0)
    c = lax.broadcasted_iota(jnp.int32, (n, n), 1)
    return jnp.sum(jnp.where(r == c, row_vals, 0.0), axis=1, keepdims=True)


def _fox_prompt_kernel(q_ref, k_ref, v_ref, fg_ref, c_ref, o_ref, kb_ref, vb_ref, *, tq):
    qi = pl.program_id(2)

    @pl.when(qi == 0)
    def _cast():
        kb_ref[...] = k_ref[...].astype(BF16)
        vb_ref[...] = v_ref[...].astype(BF16)

    scale = FOX_HEAD_DIM ** -0.5
    q = q_ref[...]
    q0 = pl.multiple_of(qi * tq, tq)
    cq = _col_from_row(c_ref[:, pl.ds(q0, tq)])

    def tile(k0, carry, masked):
        m_i, l_i, acc = carry
        s = lax.dot_general(q, kb_ref[pl.ds(k0, tq), :], (((1,), (1,)), ((), ())), preferred_element_type=F32)
        s = s * scale + cq - c_ref[:, pl.ds(k0, tq)]
        if masked:
            r = lax.broadcasted_iota(jnp.int32, (tq, tq), 0)
            c = lax.broadcasted_iota(jnp.int32, (tq, tq), 1)
            s = jnp.where(c <= r, s, NEG_BIG)
        m_new = jnp.maximum(m_i, jnp.max(s, axis=1, keepdims=True))
        alpha = jnp.exp(m_i - m_new)
        p = jnp.exp(s - m_new)
        l_new = alpha * l_i + jnp.sum(p, axis=1, keepdims=True)
        acc = alpha * acc + jnp.dot(p.astype(BF16), vb_ref[pl.ds(k0, tq), :], preferred_element_type=F32)
        return m_new, l_new, acc

    init = (jnp.full((tq, 1), NEG_BIG, F32), jnp.zeros((tq, 1), F32), jnp.zeros((tq, FOX_HEAD_DIM), F32))
    carry = lax.fori_loop(0, qi, lambda i, cr: tile(pl.multiple_of(i * tq, tq), cr, False), init)
    _, l_i, acc = tile(q0, carry, True)
    o_ref[...] = (acc / l_i * _silu(fg_ref[...].astype(F32))).astype(o_ref.dtype)


def _fox_prompt(u, k, v, ct, nbatch, t):
    tq = min(512, t)
    nq = t // tq
    return pl.pallas_call(
        functools.partial(_fox_prompt_kernel, tq=tq),
        out_shape=jax.ShapeDtypeStruct((nbatch * t, D_MODEL), BF16),
        grid=(nbatch, FOX_HEADS, nq),
        in_specs=[
            pl.BlockSpec((tq, FOX_HEAD_DIM), lambda b, h, i: (b * nq + i, COL_FQ * FOX_HEADS + h)),
            pl.BlockSpec((t, FOX_HEAD_DIM), lambda b, h, i: (b, h)),
            pl.BlockSpec((t, FOX_HEAD_DIM), lambda b, h, i: (b, h)),
            pl.BlockSpec((tq, FOX_HEAD_DIM), lambda b, h, i: (b * nq + i, COL_FG * FOX_HEADS + h)),
            pl.BlockSpec((None, None, 1, t), lambda b, h, i: (b, h, 0, 0)),
        ],
        out_specs=pl.BlockSpec((tq, FOX_HEAD_DIM), lambda b, h, i: (b * nq + i, h)),
        scratch_shapes=[pltpu.VMEM((t, FOX_HEAD_DIM), BF16), pltpu.VMEM((t, FOX_HEAD_DIM), BF16)],
        compiler_params=_params("parallel", "parallel", "arbitrary"),
        name="fox_prompt",
    )(u, k, v, u, ct)


def _fox_sample_kernel(q_ref, kp_ref, vp_ref, kn_ref, vn_ref, fg_ref, c_ref, o_ref, *, p, t):
    scale = FOX_HEAD_DIM ** -0.5
    q = q_ref[...]
    nt = (((1,), (1,)), ((), ()))
    cq = _col_from_row(c_ref[:, p:p + t])
    s_p = lax.dot_general(q, kp_ref[...].astype(BF16), nt, preferred_element_type=F32) * scale + cq - c_ref[:, 0:p]
    s_n = lax.dot_general(q, kn_ref[...].astype(BF16), nt, preferred_element_type=F32) * scale + cq - c_ref[:, p:p + t]
    r = lax.broadcasted_iota(jnp.int32, (t, t), 0)
    c = lax.broadcasted_iota(jnp.int32, (t, t), 1)
    s_n = jnp.where(c <= r, s_n, NEG_BIG)
    m = jnp.maximum(jnp.max(s_p, axis=1, keepdims=True), jnp.max(s_n, axis=1, keepdims=True))
    e_p = jnp.exp(s_p - m)
    e_n = jnp.exp(s_n - m)
    inv = 1.0 / (jnp.sum(e_p, axis=1, keepdims=True) + jnp.sum(e_n, axis=1, keepdims=True))
    o = (jnp.dot((e_p * inv).astype(BF16), vp_ref[...].astype(BF16), preferred_element_type=F32)
         + jnp.dot((e_n * inv).astype(BF16), vn_ref[...].astype(BF16), preferred_element_type=F32))
    o_ref[...] = (o * _silu(fg_ref[...].astype(F32))).astype(o_ref.dtype)


def _fox_sample(u, k, v, kpast, vpast, ct, nbatch, t):
    p = kpast.shape[1]
    hd = FOX_HEAD_DIM
    return pl.pallas_call(
        functools.partial(_fox_sample_kernel, p=p, t=t),
        out_shape=jax.ShapeDtypeStruct((nbatch * t, D_MODEL), BF16),
        grid=(nbatch, FOX_HEADS),
        in_specs=[
            pl.BlockSpec((t, hd), lambda b, h: (b, COL_FQ * FOX_HEADS + h)),
            pl.BlockSpec((None, p, hd), lambda b, h: (b, 0, h)),
            pl.BlockSpec((None, p, hd), lambda b, h: (b, 0, h)),
            pl.BlockSpec((t, hd), lambda b, h: (b, h)),
            pl.BlockSpec((t, hd), lambda b, h: (b, h)),
            pl.BlockSpec((t, hd), lambda b, h: (b, COL_FG * FOX_HEADS + h)),
            pl.BlockSpec((None, None, 1, p + t), lambda b, h: (b, h, 0, 0)),
        ],
        out_specs=pl.BlockSpec((t, hd), lambda b, h: (b, h)),
        compiler_params=_params("parallel", "parallel"),
        name="fox_sample",
    )(u, kpast, vpast, k, v, u, ct)


def _mem_kernel(q_ref, g_ref, k_ref, v_ref, o_ref):
    scale = MEM_HEAD_DIM ** -0.5
    for h in range(MEM_HEADS):
        sl = slice(h * MEM_HEAD_DIM, (h + 1) * MEM_HEAD_DIM)
        s = lax.dot_general(q_ref[:, sl], k_ref[:, sl].astype(BF16), (((1,), (1,)), ((), ())),
                            preferred_element_type=F32) * scale
        e = jnp.exp(s - jnp.max(s, axis=1, keepdims=True))
        p = e * (1.0 / jnp.sum(e, axis=1, keepdims=True))
        o = jnp.dot(p.astype(BF16), v_ref[:, sl].astype(BF16), preferred_element_type=F32)
        o_ref[:, sl] = (o * _silu(g_ref[:, sl].astype(F32))).astype(o_ref.dtype)


def _mem_attend(u, mk, mv, nbatch, t):
    tq = min(512, t)
    nq = t // tq
    nm = mk.shape[1]
    return pl.pallas_call(
        _mem_kernel,
        out_shape=jax.ShapeDtypeStruct((nbatch * t, D_MODEL), BF16),
        grid=(nbatch, nq),
        in_specs=[
            pl.BlockSpec((tq, D_MODEL), lambda b, i: (b * nq + i, COL_MQ)),
            pl.BlockSpec((tq, D_MODEL), lambda b, i: (b * nq + i, COL_MG)),
            pl.BlockSpec((None, nm, D_MODEL), lambda b, i: (b, 0, 0)),
            pl.BlockSpec((None, nm, D_MODEL), lambda b, i: (b, 0, 0)),
        ],
        out_specs=pl.BlockSpec((tq, D_MODEL), lambda b, i: (b * nq + i, 0)),
        compiler_params=_params("parallel", "arbitrary"),
        name="mem_attend",
    )(u, u, mk, mv)


def _merge_kernel(ys_ref, yf_ref, ym_ref, ws_ref, wf_ref, wm_ref, gs_ref, gf_ref, gm_ref, o_ref):
    def branch(y_ref, w_ref, g_ref):
        return jax.nn.sigmoid(g_ref[...].astype(F32)) * jnp.dot(y_ref[...], w_ref[...], preferred_element_type=F32)
    o_ref[...] = (branch(ys_ref, ws_ref, gs_ref) + branch(yf_ref, wf_ref, gf_ref)
                  + branch(ym_ref, wm_ref, gm_ref)).astype(o_ref.dtype)


def _merge(u, y_ssd, y_fox, y_mem, w_s, w_f, w_m, tm, tn):
    m = y_ssd.shape[0]
    nj = D_MODEL // tn
    yspec = pl.BlockSpec((tm, D_MODEL), lambda i, j: (i, 0))
    wspec = pl.BlockSpec((D_MODEL, tn), lambda i, j: (0, j))
    gspec = lambda col: pl.BlockSpec((tm, tn), lambda i, j: (i, col * nj + j))
    return pl.pallas_call(
        _merge_kernel,
        out_shape=jax.ShapeDtypeStruct((m, D_MODEL), BF16),
        grid=(m // tm, nj),
        in_specs=[yspec, yspec, yspec, wspec, wspec, wspec, gspec(COL_GS), gspec(COL_GF), gspec(COL_GM)],
        out_specs=pl.BlockSpec((tm, tn), lambda i, j: (i, j)),
        compiler_params=_params("parallel", "arbitrary"),
        name="gated_merge",
    )(y_ssd, y_fox, y_mem, w_s, w_f, w_m, u, u, u)


def _final_kernel(mg_ref, w_ref, x_ref, g_ref, o_ref, *, normalize):
    xo = x_ref[...] + jnp.dot(mg_ref[...], w_ref[...], preferred_element_type=F32)
    if normalize:
        ms = jnp.mean(xo * xo, axis=-1, keepdims=True)
        xo = xo * lax.rsqrt(ms + EPS) * g_ref[...]
    o_ref[...] = xo


def _final(merged, w_out, x, g_final, tm, normalize):
    m = x.shape[0]
    return pl.pallas_call(
        functools.partial(_final_kernel, normalize=normalize),
        out_shape=jax.ShapeDtypeStruct((m, D_MODEL), F32),
        grid=(m // tm,),
        in_specs=[pl.BlockSpec((tm, D_MODEL), lambda i: (i, 0)),
                  pl.BlockSpec((D_MODEL, D_MODEL), lambda i: (0, 0)),
                  pl.BlockSpec((tm, D_MODEL), lambda i: (i, 0)),
                  pl.BlockSpec((1, D_MODEL), lambda i: (0, 0))],
        out_specs=pl.BlockSpec((tm, D_MODEL), lambda i: (i, 0)),
        compiler_params=_params("parallel"),
        name="out_proj_norm",
    )(merged, w_out, x, g_final.reshape(1, D_MODEL))


def _row_tile(m, pref):
    t = pref
    while m % t:
        t //= 2
    return t


def _layer(x, conv0, h0, k_past, v_past, logf_past, mem_k, mem_v, wd, g_final, last):
    nbatch, t, d = x.shape
    m = nbatch * t
    x2 = x.reshape(m, d)
    tm = _row_tile(m, 1024)

    h = _rmsnorm(x2, wd["g_norm"], _row_tile(m, 256))
    u = _mm(h, wd["w_main"], BF16, tm, 1024, "in_proj_main")
    k_new = _mm(h, wd["w_k"], F32, tm, 1024, "in_proj_k")
    v_new = _mm(h, wd["w_v"], F32, tm, 1024, "in_proj_v")
    small = _mm(h, wd["w_dt"], F32, tm, 128, "in_proj_dt")
    fft = _mm_rows(wd["w_ff_t"], h, tm, "in_proj_forget")

    past_t = None if logf_past is None else jnp.transpose(logf_past, (0, 2, 1))
    logf_t, ct = _logf(fft, wd["b_forget"], past_t, nbatch, t)
    ct = ct.reshape(nbatch, FOX_HEADS, 1, ct.shape[-1])

    conv0p = jnp.pad(conv0, ((0, 0), (8 - (SSD_CONV - 1), 0), (0, 0)))
    y_ssd, h_final, new_conv = _ssd(u, small, conv0p, h0, wd["w_conv"], wd["b_conv"], wd["dt_bias"], wd["a_log"],
                                    wd["d_skip"], wd["g_ssd_out"], nbatch, t)

    if k_past is None:
        y_fox = _fox_prompt(u, k_new, v_new, ct, nbatch, t)
    else:
        p = k_past.shape[1]
        y_fox = _fox_sample(u, k_new, v_new, k_past.reshape(nbatch, p, D_MODEL), v_past.reshape(nbatch, p, D_MODEL),
                            ct, nbatch, t)

    y_mem = _mem_attend(u, mem_k, mem_v, nbatch, t)
    merged = _merge(u, y_ssd, y_fox, y_mem, wd["w_o_ssd"], wd["w_o_fox"], wd["w_o_mem"], tm, 512)
    y = _final(merged, wd["w_out"], x2, g_final, _row_tile(m, 512), last)

    return (y.reshape(nbatch, t, d), new_conv, h_final,
            k_new.reshape(nbatch, t, FOX_HEADS, FOX_HEAD_DIM), v_new.reshape(nbatch, t, FOX_HEADS, FOX_HEAD_DIM),
            jnp.transpose(logf_t, (0, 2, 1)))


def kernel(x_prompt, x_sample, mem_prompt, cache_fox_k, cache_fox_v, cache_fox_logf, state_ssd, state_ssd_conv,
           cache_mem_k, cache_mem_v, g_norm, w_in, w_conv, b_conv, dt_bias, a_log, d_skip, g_ssd_out, b_forget,
           g_mem, w_mem_kv, w_o_ssd, w_o_fox, w_o_mem, w_out, g_final):
    depth = w_in.shape[0]
    xp, xs = x_prompt, x_sample
    bp = xp.shape[0]
    n_mem = mem_prompt.shape[1]
    sizes = [D_MODEL, SSD_CONV_CH, SSD_HEADS, D_MODEL, D_MODEL, D_MODEL, D_MODEL, FOX_HEADS,
             D_MODEL, D_MODEL, D_MODEL, D_MODEL, D_MODEL]
    splits = [int(s) for s in np.cumsum(sizes)[:-1]]
    outs = [[] for _ in range(12)]
    for l in range(depth):
        wz, wxbc, wdt, wfq, wfk, wfv, wfg, wff, wmq, wmg, wgs, wgf, wgm = jnp.split(w_in[l], splits, axis=1)
        wd = {
            "g_norm": g_norm[l],
            "w_main": jnp.concatenate([wz, wfq, wfg, wmq, wmg, wgs, wgf, wgm, wxbc], axis=1).astype(BF16),
            "w_k": wfk.astype(BF16), "w_v": wfv.astype(BF16),
            "w_dt": jnp.pad(wdt, ((0, 0), (0, 128 - SSD_HEADS))).astype(BF16),
            "w_ff_t": wff.T.astype(BF16),
            "w_conv": w_conv[l], "b_conv": b_conv[l], "dt_bias": dt_bias[l], "a_log": a_log[l], "d_skip": d_skip[l],
            "g_ssd_out": g_ssd_out[l], "b_forget": b_forget[l],
            "w_o_ssd": w_o_ssd[l].astype(BF16), "w_o_fox": w_o_fox[l].astype(BF16),
            "w_o_mem": w_o_mem[l].astype(BF16), "w_out": w_out[l].astype(BF16),
        }
        mem2 = mem_prompt.reshape(bp * n_mem, D_MODEL)
        hm = _rmsnorm(mem2, g_mem[l], _row_tile(bp * n_mem, 256))
        tmm = _row_tile(bp * n_mem, 1024)
        mk = _mm(hm, w_mem_kv[l][:, :D_MODEL].astype(BF16), F32, tmm, 1024, "mem_k_proj")
        mv = _mm(hm, w_mem_kv[l][:, D_MODEL:].astype(BF16), F32, tmm, 1024, "mem_v_proj")

        xp, c_p, h_p, k_p, v_p, lf_p = _layer(
            xp, jnp.zeros((bp, SSD_CONV - 1, SSD_CONV_CH), F32),
            jnp.zeros((bp, SSD_HEADS, SSD_HEAD_DIM, SSD_D_STATE), F32), None, None, None,
            mk.reshape(bp, n_mem, D_MODEL), mv.reshape(bp, n_mem, D_MODEL), wd, g_final, l == depth - 1)
        bs = xs.shape[0]
        xs, c_s, h_s, k_s, v_s, lf_s = _layer(
            xs, state_ssd_conv[l], state_ssd[l], cache_fox_k[l], cache_fox_v[l], cache_fox_logf[l],
            cache_mem_k[l].reshape(bs, n_mem, D_MODEL), cache_mem_v[l].reshape(bs, n_mem, D_MODEL), wd, g_final,
            l == depth - 1)
        for lst, val in zip(outs, (k_p, v_p, lf_p, h_p, c_p,
                                   mk.reshape(bp, n_mem, MEM_HEADS, MEM_HEAD_DIM),
                                   mv.reshape(bp, n_mem, MEM_HEADS, MEM_HEAD_DIM),
                                   k_s, v_s, lf_s, h_s, c_s)):
            lst.append(val)
    return (xp, xs) + tuple(jnp.stack(o) for o in outs)
```

```python
import functools

import numpy as np
import jax
import jax.numpy as jnp
from jax import lax
from jax.experimental import pallas as pl
from jax.experimental.pallas import tpu as pltpu

F32 = jnp.float32
BF16 = jnp.bfloat16

EPS = 1e-6
D_MODEL = 2048
SSD_HEAD_DIM = 64
SSD_HEADS = 32
SSD_GROUPS = 4
SSD_D_STATE = 128
SSD_CONV = 4
SSD_BC = SSD_GROUPS * SSD_D_STATE
SSD_CONV_CH = D_MODEL + 2 * SSD_BC
FOX_HEADS = 16
FOX_HEAD_DIM = 128
MEM_HEADS = 4
MEM_HEAD_DIM = 512
SSD_CHUNK = 64
NEG_BIG = -1e30

VMEM_LIMIT = 56 * 1024 * 1024

W_IN_SEGMENTS = ((0, D_MODEL + SSD_CONV_CH),
                 (D_MODEL + SSD_CONV_CH + SSD_HEADS, 4 * D_MODEL),
                 (D_MODEL + SSD_CONV_CH + SSD_HEADS + 4 * D_MODEL + FOX_HEADS, 5 * D_MODEL))
W_IN_DT = D_MODEL + SSD_CONV_CH
W_IN_FF = W_IN_SEGMENTS[1][0] + 4 * D_MODEL
U1_Z, U1_XS = 0, 1
U2_Q, U2_K, U2_V, U2_G = 0, 1, 2, 3
U3_Q, U3_G, U3_GS, U3_GF, U3_GM = 0, 1, 2, 3, 4
REPACK_COLS = 1024
LOG2E = 1.4426950408889634


def _params(*sem):
    return pltpu.CompilerParams(dimension_semantics=sem, vmem_limit_bytes=VMEM_LIMIT)


def _split3(x):
    hi = x.astype(BF16)
    r1 = x - hi.astype(F32)
    mid = r1.astype(BF16)
    lo = (r1 - mid.astype(F32)).astype(BF16)
    return hi, mid, lo


def _softplus(x):
    return jnp.maximum(x, 0.0) + jnp.log1p(jnp.exp(-jnp.abs(x)))


def _silu(x):
    return x * jax.nn.sigmoid(x)


def _rmsnorm_kernel(x_ref, g_ref, o_ref):
    x = x_ref[...]
    ms = jnp.mean(x * x, axis=-1, keepdims=True)
    o_ref[...] = (x * lax.rsqrt(ms + EPS) * g_ref[...]).astype(o_ref.dtype)


def _rmsnorm(x, g, tm):
    m, d = x.shape
    return pl.pallas_call(
        _rmsnorm_kernel,
        out_shape=jax.ShapeDtypeStruct((m, d), BF16),
        grid=(m // tm,),
        in_specs=[pl.BlockSpec((tm, d), lambda i: (i, 0)), pl.BlockSpec((1, d), lambda i: (0, 0))],
        out_specs=pl.BlockSpec((tm, d), lambda i: (i, 0)),
        compiler_params=_params("parallel"),
        name="rmsnorm",
    )(x, g.reshape(1, d))


def _mm_kernel(h_ref, w_ref, o_ref):
    o_ref[...] = jnp.dot(h_ref[...], w_ref[...], preferred_element_type=F32).astype(o_ref.dtype)


def _mm(h, w, out_dtype, tm, tn, name, w_col0=0, n=None):
    m, k = h.shape
    n = w.shape[1] if n is None else n
    j0 = w_col0 // tn
    return pl.pallas_call(
        _mm_kernel,
        out_shape=jax.ShapeDtypeStruct((m, n), out_dtype),
        grid=(m // tm, n // tn),
        in_specs=[pl.BlockSpec((tm, k), lambda i, j: (i, 0)), pl.BlockSpec((k, tn), lambda i, j: (0, j0 + j))],
        out_specs=pl.BlockSpec((tm, tn), lambda i, j: (i, j)),
        compiler_params=_params("parallel", "arbitrary"),
        name=name,
    )(h, w)


def _mm_rows_kernel(wt_ref, h_ref, o_ref):
    o_ref[...] = lax.dot_general(wt_ref[...], h_ref[...], (((1,), (1,)), ((), ())), preferred_element_type=F32)


def _mm_rows(wt, h, tm, name):
    r, k = wt.shape
    m = h.shape[0]
    return pl.pallas_call(
        _mm_rows_kernel,
        out_shape=jax.ShapeDtypeStruct((r, m), F32),
        grid=(m // tm,),
        in_specs=[pl.BlockSpec((r, k), lambda i: (0, 0)), pl.BlockSpec((tm, k), lambda i: (i, 0))],
        out_specs=pl.BlockSpec((r, tm), lambda i: (0, i)),
        compiler_params=_params("parallel"),
        name=name,
    )(wt, h)


def _mm_qkvg_kernel(h_ref, w_ref, o_ref, k_ref, v_ref, *, nj):
    j = pl.program_id(1)
    acc = jnp.dot(h_ref[...], w_ref[...], preferred_element_type=F32)
    o_ref[...] = acc.astype(o_ref.dtype)

    @pl.when((j >= nj) & (j < 2 * nj))
    def _k():
        k_ref[...] = acc

    @pl.when((j >= 2 * nj) & (j < 3 * nj))
    def _v():
        v_ref[...] = acc


def _mm_qkvg(h, w, tm, tn):
    m, k = h.shape
    nj = D_MODEL // tn
    return pl.pallas_call(
        functools.partial(_mm_qkvg_kernel, nj=nj),
        out_shape=(jax.ShapeDtypeStruct((m, 4 * D_MODEL), BF16),
                   jax.ShapeDtypeStruct((m, D_MODEL), F32), jax.ShapeDtypeStruct((m, D_MODEL), F32)),
        grid=(m // tm, 4 * nj),
        in_specs=[pl.BlockSpec((tm, k), lambda i, j: (i, 0)), pl.BlockSpec((k, tn), lambda i, j: (0, j))],
        out_specs=(pl.BlockSpec((tm, tn), lambda i, j: (i, j)),
                   pl.BlockSpec((tm, tn), lambda i, j: (i, jnp.clip(j - nj, 0, nj - 1))),
                   pl.BlockSpec((tm, tn), lambda i, j: (i, jnp.clip(j - 2 * nj, 0, nj - 1)))),
        compiler_params=_params("parallel", "arbitrary"),
        name="in_proj_qkvg",
    )(h, w)


def _repack_kernel(a_ref, b_ref, o_ref, *, shift):
    x = jnp.concatenate([a_ref[...], b_ref[...]], axis=1)
    o_ref[...] = x[:, shift:shift + REPACK_COLS].astype(o_ref.dtype)


def _cast_kernel(a_ref, o_ref):
    o_ref[...] = a_ref[...].astype(o_ref.dtype)


def _repack(w, start, width):
    rows = w.shape[0]
    shift = start % 128
    base = start - shift
    assert base % REPACK_COLS == 0 and width % REPACK_COLS == 0
    tr = _row_tile(rows, 512)
    grid = (rows // tr, width // REPACK_COLS)
    a_spec = pl.BlockSpec((tr, REPACK_COLS), lambda i, j: (i, base // REPACK_COLS + j))
    o_spec = pl.BlockSpec((tr, REPACK_COLS), lambda i, j: (i, j))
    out_shape = jax.ShapeDtypeStruct((rows, width), BF16)
    if shift == 0:
        return pl.pallas_call(_cast_kernel, out_shape=out_shape, grid=grid, in_specs=[a_spec], out_specs=o_spec,
                              compiler_params=_params("parallel", "parallel"), name="weight_cast")(w)
    b_spec = pl.BlockSpec((tr, 128), lambda i, j: (i, (base + (j + 1) * REPACK_COLS) // 128))
    return pl.pallas_call(functools.partial(_repack_kernel, shift=shift), out_shape=out_shape, grid=grid,
                          in_specs=[a_spec, b_spec], out_specs=o_spec,
                          compiler_params=_params("parallel", "parallel"), name="weight_repack")(w, w)


def _narrow_weights_kernel(dt_ref, ff_ref, wdt_ref, wfft_ref):
    lane = lax.broadcasted_iota(jnp.int32, dt_ref.shape, 1)
    wdt_ref[...] = jnp.where(lane < SSD_HEADS, dt_ref[...], 0.0).astype(BF16)
    r0 = W_IN_FF % 128
    wfft_ref[...] = ff_ref[...].T[r0:r0 + FOX_HEADS, :].astype(BF16)


def _narrow_weights(w):
    assert W_IN_DT % 128 == 0 and W_IN_FF % 128 % 8 == 0
    rows = w.shape[0]
    blk = lambda c: pl.BlockSpec((rows, 128), lambda i: (0, c // 128))
    return pl.pallas_call(
        _narrow_weights_kernel,
        out_shape=(jax.ShapeDtypeStruct((rows, 128), BF16), jax.ShapeDtypeStruct((FOX_HEADS, rows), BF16)),
        grid=(1,),
        in_specs=[blk(W_IN_DT), blk(W_IN_FF)],
        out_specs=(pl.BlockSpec((rows, 128), lambda i: (0, 0)), pl.BlockSpec((FOX_HEADS, rows), lambda i: (0, 0))),
        compiler_params=_params("arbitrary"),
        name="weight_narrow",
    )(w, w)


def _upper3(tb):
    k = lax.broadcasted_iota(jnp.int32, (3 * tb, tb), 0) % tb
    t = lax.broadcasted_iota(jnp.int32, (3 * tb, tb), 1)
    return jnp.where(k <= t, 1.0, 0.0).astype(BF16)


def _cumsum_lanes(x, carry, tb):
    n = x.shape[1]
    u3 = _upper3(tb)
    out = []
    for s in range(0, n, tb):
        hi, mid, lo = _split3(x[:, s:s + tb])
        c = jnp.dot(jnp.concatenate([hi, mid, lo], axis=1), u3, preferred_element_type=F32) + carry
        carry = c[:, tb - 1:tb]
        out.append(c)
    return out, carry


def _logf_kernel(*refs, nb, t, p):
    if p:
        fft_ref, bias_ref, past_ref, lf_ref, c_ref = refs
    else:
        fft_ref, bias_ref, lf_ref, c_ref = refs
    lf_all = -_softplus(-(fft_ref[...] + bias_ref[...]))
    for b in range(nb):
        lf = lf_all[:, b * t:(b + 1) * t]
        lf_ref[b] = lf
        carry = jnp.zeros((FOX_HEADS, 1), F32)
        if p:
            tbp = min(256, p)
            blocks, carry = _cumsum_lanes(past_ref[b], carry, tbp)
            for i, c in enumerate(blocks):
                c_ref[b, :, i * tbp:(i + 1) * tbp] = c
        tb = min(256, t)
        blocks, carry = _cumsum_lanes(lf, carry, tb)
        for i, c in enumerate(blocks):
            c_ref[b, :, p + i * tb:p + (i + 1) * tb] = c


def _logf(fft, b_forget, past_t, nbatch, t):
    p = 0 if past_t is None else past_t.shape[2]
    nb = 1 if t % 128 == 0 else nbatch
    grid = (nbatch // nb,)
    in_specs = [pl.BlockSpec((FOX_HEADS, nb * t), lambda i: (0, i)),
                pl.BlockSpec((FOX_HEADS, 1), lambda i: (0, 0))]
    args = [fft, b_forget.reshape(FOX_HEADS, 1)]
    if p:
        in_specs.append(pl.BlockSpec((nb, FOX_HEADS, p), lambda i: (i, 0, 0)))
        args.append(past_t)
    return pl.pallas_call(
        functools.partial(_logf_kernel, nb=nb, t=t, p=p),
        out_shape=(jax.ShapeDtypeStruct((nbatch, FOX_HEADS, t), F32),
                   jax.ShapeDtypeStruct((nbatch, FOX_HEADS, p + t), F32)),
        grid=grid,
        in_specs=in_specs,
        out_specs=(pl.BlockSpec((nb, FOX_HEADS, t), lambda i: (i, 0, 0)),
                   pl.BlockSpec((nb, FOX_HEADS, p + t), lambda i: (i, 0, 0))),
        compiler_params=_params("parallel"),
        name="logf_cumsum",
    )(*args)


def _expand_matrix():
    r = np.arange(128)[:, None]
    c = np.arange(D_MODEL)[None, :]
    return jnp.asarray(((r < 96) & ((r % SSD_HEADS) == (c // SSD_HEAD_DIM))).astype(np.float32), dtype=BF16)


def _pack3(x):
    lane = lax.broadcasted_iota(jnp.int32, x.shape, 1)
    x = jnp.where(lane < SSD_HEADS, x, 0.0)
    hi, mid, lo = _split3(x)
    packed = hi.astype(F32) + pltpu.roll(mid.astype(F32), SSD_HEADS, 1) + pltpu.roll(lo.astype(F32), 2 * SSD_HEADS, 1)
    return packed.astype(BF16)


def _ssd_kernel(xs_ref, b_ref, c_ref, z_ref, sm_ref, conv0_ref, h0_ref,
                wconv_ref, bconv_ref, dtb_ref, alog_ref, dskip_ref, gout_ref, e3_ref,
                y_ref, hfin_ref, convout_ref,
                st_ref, xpad_ref):
    ci = pl.program_id(1)
    L = xs_ref.shape[0]
    P = SSD_HEAD_DIM
    W = D_MODEL

    @pl.when(ci == 0)
    def _init():
        st_ref[...] = h0_ref[...].T
        xpad_ref[0:8, :] = conv0_ref[...]

    xpad_ref[8:8 + L, 0:W] = xs_ref[...].astype(F32)
    xpad_ref[8:8 + L, W:W + SSD_BC] = b_ref[...].astype(F32)
    xpad_ref[8:8 + L, W + SSD_BC:] = c_ref[...].astype(F32)

    def conv(lo, hi):
        acc = bconv_ref[:, lo:hi] + xpad_ref[8:8 + L, lo:hi] * wconv_ref[3:4, lo:hi]
        for i in range(SSD_CONV - 1):
            acc = acc + xpad_ref[5 + i:5 + i + L, lo:hi] * wconv_ref[i:i + 1, lo:hi]
        return _silu(acc)

    xs = conv(0, W)
    bm = conv(W, W + SSD_BC).astype(BF16)
    cm = conv(W + SSD_BC, W + 2 * SSD_BC).astype(BF16)

    @pl.when(ci == pl.num_programs(1) - 1)
    def _conv_out():
        convout_ref[...] = xpad_ref[L:L + 8, :]
    xpad_ref[0:8, :] = xpad_ref[L:L + 8, :]

    dt = _softplus(sm_ref[...] + dtb_ref[...])
    a_small = -jnp.exp(alog_ref[...])
    lane = lax.broadcasted_iota(jnp.int32, (L, 128), 1)
    dta = jnp.where(lane < SSD_HEADS, dt * a_small, 0.0)
    hi, mid, lo = _split3(dta)
    kk = lax.broadcasted_iota(jnp.int32, (L, 3 * L), 1) % L
    ll = lax.broadcasted_iota(jnp.int32, (L, 3 * L), 0)
    tri3 = jnp.where(kk <= ll, 1.0, 0.0).astype(BF16)
    acum = jnp.dot(tri3, jnp.concatenate([hi, mid, lo], axis=0), preferred_element_type=F32)

    e3 = e3_ref[...]
    a_x = jnp.dot(_pack3(acum), e3, preferred_element_type=F32)
    dt_x = jnp.dot(_pack3(dt), e3, preferred_element_type=F32)

    row = lax.broadcasted_iota(jnp.int32, (L, W), 0)
    sidx = lax.broadcasted_iota(jnp.int32, (L, W), 1) % L
    a_row = jnp.sum(jnp.where(row == sidx, a_x, 0.0), axis=0, keepdims=True)
    ldec = jnp.exp(jnp.where(row >= sidx, a_x - a_row, NEG_BIG))

    hpm = 256 // L
    cbx = []
    for g in range(SSD_GROUPS):
        cg = cm[:, g * SSD_D_STATE:(g + 1) * SSD_D_STATE]
        bg = bm[:, g * SSD_D_STATE:(g + 1) * SSD_D_STATE]
        brep = jnp.concatenate([bg] * (SSD_HEADS // SSD_GROUPS), axis=0)
        cbx.append(lax.dot_general(cg, brep, (((1,), (1,)), ((), ())), preferred_element_type=F32))
    m = (jnp.concatenate(cbx, axis=1) * ldec).astype(BF16)

    xdt = xs * dt_x
    xdt_b = xdt.astype(BF16)
    rb = lax.broadcasted_iota(jnp.int32, (256, hpm * P), 0) // L
    cb = lax.broadcasted_iota(jnp.int32, (256, hpm * P), 1) // P
    blockmask = rb == cb
    y_parts = []
    for j in range(SSD_HEADS // hpm):
        xj = xdt_b[:, j * hpm * P:(j + 1) * hpm * P]
        bd = jnp.where(blockmask, jnp.concatenate([xj] * hpm, axis=0), jnp.zeros((), BF16))
        y_parts.append(jnp.dot(m[:, j * 256:(j + 1) * 256], bd, preferred_element_type=F32))
    y = jnp.concatenate(y_parts, axis=1)

    st = st_ref[...]
    st_b = st.astype(BF16)
    gw = W // SSD_GROUPS
    yo = [jnp.dot(cm[:, g * SSD_D_STATE:(g + 1) * SSD_D_STATE], st_b[:, g * gw:(g + 1) * gw],
                  preferred_element_type=F32) for g in range(SSD_GROUPS)]
    y = y + jnp.concatenate(yo, axis=1) * jnp.exp(a_x)

    a_last = a_x[L - 1:L, :]
    xw = (xdt * jnp.exp(a_last - a_x)).astype(BF16)
    upd = [lax.dot_general(bm[:, g * SSD_D_STATE:(g + 1) * SSD_D_STATE], xw[:, g * gw:(g + 1) * gw],
                           (((0,), (0,)), ((), ())), preferred_element_type=F32) for g in range(SSD_GROUPS)]
    st_new = st * jnp.exp(a_last) + jnp.concatenate(upd, axis=1)
    st_ref[...] = st_new

    @pl.when(ci == pl.num_programs(1) - 1)
    def _state_out():
        hfin_ref[...] = st_new.T

    y = y + dskip_ref[...] * xs
    gz = y * _silu(z_ref[...].astype(F32))
    ms = jnp.mean(gz * gz, axis=-1, keepdims=True)
    y_ref[...] = (gz * lax.rsqrt(ms + EPS) * gout_ref[...]).astype(y_ref.dtype)


def _ssd(u, small, conv0, h0, w_conv, b_conv, dt_bias, a_log, d_skip, g_ssd_out, nbatch, t):
    L = SSD_CHUNK
    nc = t // L
    m = nbatch * t
    pad128 = lambda v: jnp.pad(v.astype(F32), (0, 128 - v.shape[0])).reshape(1, 128)
    rep = lambda v: jnp.repeat(v.astype(F32), SSD_HEAD_DIM).reshape(1, D_MODEL)
    row = lambda b, c: b * nc + c
    const2 = lambda shape: pl.BlockSpec(shape, lambda b, c: (0, 0))
    y, hfin, convout = pl.pallas_call(
        _ssd_kernel,
        out_shape=(jax.ShapeDtypeStruct((m, D_MODEL), BF16),
                   jax.ShapeDtypeStruct((nbatch, D_MODEL, SSD_D_STATE), F32),
                   jax.ShapeDtypeStruct((nbatch, 8, SSD_CONV_CH), F32)),
        grid=(nbatch, nc),
        in_specs=[
            pl.BlockSpec((L, D_MODEL), lambda b, c: (row(b, c), U1_XS)),
            pl.BlockSpec((L, SSD_BC), lambda b, c: (row(b, c), (U1_XS + 1) * D_MODEL // SSD_BC)),
            pl.BlockSpec((L, SSD_BC), lambda b, c: (row(b, c), (U1_XS + 1) * D_MODEL // SSD_BC + 1)),
            pl.BlockSpec((L, D_MODEL), lambda b, c: (row(b, c), U1_Z)),
            pl.BlockSpec((L, 128), lambda b, c: (row(b, c), 0)),
            pl.BlockSpec((None, 8, SSD_CONV_CH), lambda b, c: (b, 0, 0)),
            pl.BlockSpec((None, D_MODEL, SSD_D_STATE), lambda b, c: (b, 0, 0)),
            const2((SSD_CONV, SSD_CONV_CH)), const2((1, SSD_CONV_CH)),
            const2((1, 128)), const2((1, 128)), const2((1, D_MODEL)), const2((1, D_MODEL)),
            const2((128, D_MODEL)),
        ],
        out_specs=(pl.BlockSpec((L, D_MODEL), lambda b, c: (row(b, c), 0)),
                   pl.BlockSpec((None, D_MODEL, SSD_D_STATE), lambda b, c: (b, 0, 0)),
                   pl.BlockSpec((None, 8, SSD_CONV_CH), lambda b, c: (b, 0, 0))),
        scratch_shapes=[pltpu.VMEM((SSD_D_STATE, D_MODEL), F32), pltpu.VMEM((L + 8, SSD_CONV_CH), F32)],
        compiler_params=_params("parallel", "arbitrary"),
        name="ssd_scan",
    )(u, u, u, u, small, conv0, h0.reshape(nbatch, D_MODEL, SSD_D_STATE),
      w_conv, b_conv.reshape(1, SSD_CONV_CH), pad128(dt_bias), pad128(a_log), rep(d_skip),
      g_ssd_out.reshape(1, D_MODEL), _expand_matrix())
    return y, hfin.reshape(nbatch, SSD_HEADS, SSD_HEAD_DIM, SSD_D_STATE), convout[:, 8 - (SSD_CONV - 1):]


def _col_from_row(row_vals):
    n = row_vals.shape[1]
    r = lax.broadcasted_iota(jnp.int32, (n, n), 0)
    c = lax.broadcasted_iota(jnp.int32, (n, n), 1)
    return jnp.sum(jnp.where(r == c, row_vals, 0.0), axis=1, keepdims=True)


FOX_PAIR = 2
NT_DIMS = (((1,), (1,)), ((), ()))


FOX_TQ = 512


def _fox_prompt_kernel(q_ref, k_ref, v_ref, fg_ref, c_ref, o_ref, *, tq):
    qi = pl.program_id(2)
    hd = FOX_HEAD_DIM
    c1 = hd ** -0.5 * LOG2E
    q0 = pl.multiple_of(qi * tq, tq)
    lanes = [slice(j * hd, (j + 1) * hd) for j in range(FOX_PAIR)]
    qs = [q_ref[:, sl] for sl in lanes]
    cq2 = [_col_from_row(c_ref[j, :, pl.ds(q0, tq)]) * LOG2E for j in range(FOX_PAIR)]

    def tile(k0, carry, masked):
        out = []
        for j in range(FOX_PAIR):
            m_i, l_i, acc = carry[j]
            s = lax.dot_general(qs[j], k_ref[pl.ds(k0, tq), lanes[j]], NT_DIMS, preferred_element_type=F32)
            t2 = s * c1 - c_ref[j, :, pl.ds(k0, tq)] * LOG2E
            if masked:
                r = lax.broadcasted_iota(jnp.int32, (tq, tq), 0)
                c = lax.broadcasted_iota(jnp.int32, (tq, tq), 1)
                t2 = jnp.where(c <= r, t2, NEG_BIG)
            m_new = jnp.maximum(m_i, jnp.max(t2, axis=1, keepdims=True) + cq2[j])
            p = jnp.exp2(t2 - (m_new - cq2[j]))
            alpha = jnp.exp2(m_i - m_new)
            l_new = alpha * l_i + jnp.sum(p, axis=1, keepdims=True)
            acc = alpha * acc + jnp.dot(p.astype(BF16), v_ref[pl.ds(k0, tq), lanes[j]], preferred_element_type=F32)
            out.append((m_new, l_new, acc))
        return tuple(out)

    init = tuple((jnp.full((tq, 1), NEG_BIG, F32), jnp.zeros((tq, 1), F32), jnp.zeros((tq, hd), F32))
                 for _ in range(FOX_PAIR))
    carry = lax.fori_loop(0, qi, lambda i, cr: tile(pl.multiple_of(i * tq, tq), cr, False), init)
    carry = tile(q0, carry, True)
    for j in range(FOX_PAIR):
        _, l_i, acc = carry[j]
        o_ref[:, lanes[j]] = (acc / l_i * _silu(fg_ref[:, lanes[j]].astype(F32))).astype(o_ref.dtype)


def _fox_prompt(u2, ct, nbatch, t):
    tq = min(FOX_TQ, t)
    nq = t // tq
    w = FOX_PAIR * FOX_HEAD_DIM
    nb = D_MODEL // w
    return pl.pallas_call(
        functools.partial(_fox_prompt_kernel, tq=tq),
        out_shape=jax.ShapeDtypeStruct((nbatch * t, D_MODEL), BF16),
        grid=(nbatch, nb, nq),
        in_specs=[
            pl.BlockSpec((tq, w), lambda b, h, i: (b * nq + i, U2_Q * nb + h)),
            pl.BlockSpec((t, w), lambda b, h, i: (b, U2_K * nb + h)),
            pl.BlockSpec((t, w), lambda b, h, i: (b, U2_V * nb + h)),
            pl.BlockSpec((tq, w), lambda b, h, i: (b * nq + i, U2_G * nb + h)),
            pl.BlockSpec((None, FOX_PAIR, 1, t), lambda b, h, i: (b, h, 0, 0)),
        ],
        out_specs=pl.BlockSpec((tq, w), lambda b, h, i: (b * nq + i, h)),
        compiler_params=_params("parallel", "parallel", "arbitrary"),
        name="fox_prompt",
    )(u2, u2, u2, u2, ct)


def _fox_sample_kernel(q_ref, kp_ref, vp_ref, kn_ref, vn_ref, fg_ref, c_ref, o_ref, *, p, t):
    c1 = FOX_HEAD_DIM ** -0.5 * LOG2E
    r = lax.broadcasted_iota(jnp.int32, (t, t), 0)
    c = lax.broadcasted_iota(jnp.int32, (t, t), 1)
    for j in range(FOX_PAIR):
        sl = slice(j * FOX_HEAD_DIM, (j + 1) * FOX_HEAD_DIM)
        q = q_ref[:, sl]
        cq2 = _col_from_row(c_ref[j, :, p:p + t]) * LOG2E
        s_p = (lax.dot_general(q, kp_ref[:, sl].astype(BF16), NT_DIMS, preferred_element_type=F32) * c1
               - c_ref[j, :, 0:p] * LOG2E)
        s_n = (lax.dot_general(q, kn_ref[:, sl], NT_DIMS, preferred_element_type=F32) * c1
               - c_ref[j, :, p:p + t] * LOG2E)
        s_n = jnp.where(c <= r, s_n, NEG_BIG)
        m = jnp.maximum(jnp.max(s_p, axis=1, keepdims=True), jnp.max(s_n, axis=1, keepdims=True)) + cq2
        e_p = jnp.exp2(s_p - (m - cq2))
        e_n = jnp.exp2(s_n - (m - cq2))
        inv = 1.0 / (jnp.sum(e_p, axis=1, keepdims=True) + jnp.sum(e_n, axis=1, keepdims=True))
        o = (jnp.dot((e_p * inv).astype(BF16), vp_ref[:, sl].astype(BF16), preferred_element_type=F32)
             + jnp.dot((e_n * inv).astype(BF16), vn_ref[:, sl], preferred_element_type=F32))
        o_ref[:, sl] = (o * _silu(fg_ref[:, sl].astype(F32))).astype(o_ref.dtype)


def _fox_sample(u2, k_past, v_past, ct, nbatch, t):
    p = k_past.shape[1]
    w = FOX_PAIR * FOX_HEAD_DIM
    nb = D_MODEL // w
    past = pl.BlockSpec((None, p, w), lambda b, h: (b, 0, h))
    col = lambda seg: pl.BlockSpec((t, w), lambda b, h: (b, seg * nb + h))
    return pl.pallas_call(
        functools.partial(_fox_sample_kernel, p=p, t=t),
        out_shape=jax.ShapeDtypeStruct((nbatch * t, D_MODEL), BF16),
        grid=(nbatch, nb),
        in_specs=[col(U2_Q), past, past, col(U2_K), col(U2_V), col(U2_G),
                  pl.BlockSpec((None, FOX_PAIR, 1, p + t), lambda b, h: (b, h, 0, 0))],
        out_specs=pl.BlockSpec((t, w), lambda b, h: (b, h)),
        compiler_params=_params("parallel", "parallel"),
        name="fox_sample",
    )(u2, k_past, v_past, u2, u2, u2, ct)


def _mem_kernel(q_ref, g_ref, k_ref, v_ref, o_ref):
    scale = MEM_HEAD_DIM ** -0.5
    for h in range(MEM_HEADS):
        sl = slice(h * MEM_HEAD_DIM, (h + 1) * MEM_HEAD_DIM)
        s = lax.dot_general(q_ref[:, sl], k_ref[:, sl].astype(BF16), NT_DIMS, preferred_element_type=F32) * scale
        e = jnp.exp(s - jnp.max(s, axis=1, keepdims=True))
        p = e * (1.0 / jnp.sum(e, axis=1, keepdims=True))
        o = jnp.dot(p.astype(BF16), v_ref[:, sl].astype(BF16), preferred_element_type=F32)
        o_ref[:, sl] = (o * _silu(g_ref[:, sl].astype(F32))).astype(o_ref.dtype)


def _mem_attend(u3, mk, mv, nbatch, t):
    tq = min(512, t)
    nq = t // tq
    nm = mk.shape[1]
    kv = pl.BlockSpec((None, nm, D_MODEL), lambda b, i: (b, 0, 0))
    return pl.pallas_call(
        _mem_kernel,
        out_shape=jax.ShapeDtypeStruct((nbatch * t, D_MODEL), BF16),
        grid=(nbatch, nq),
        in_specs=[pl.BlockSpec((tq, D_MODEL), lambda b, i: (b * nq + i, U3_Q)),
                  pl.BlockSpec((tq, D_MODEL), lambda b, i: (b * nq + i, U3_G)), kv, kv],
        out_specs=pl.BlockSpec((tq, D_MODEL), lambda b, i: (b * nq + i, 0)),
        compiler_params=_params("parallel", "arbitrary"),
        name="mem_attend",
    )(u3, u3, mk, mv)


def _merge_kernel(ys_ref, yf_ref, ym_ref, ws_ref, wf_ref, wm_ref, gs_ref, gf_ref, gm_ref, o_ref):
    def branch(y_ref, w_ref, g_ref):
        return jax.nn.sigmoid(g_ref[...].astype(F32)) * jnp.dot(y_ref[...], w_ref[...], preferred_element_type=F32)
    o_ref[...] = (branch(ys_ref, ws_ref, gs_ref) + branch(yf_ref, wf_ref, gf_ref)
                  + branch(ym_ref, wm_ref, gm_ref)).astype(o_ref.dtype)


def _merge(u, y_ssd, y_fox, y_mem, w_s, w_f, w_m, tm, tn):
    m = y_ssd.shape[0]
    nj = D_MODEL // tn
    yspec = pl.BlockSpec((tm, D_MODEL), lambda i, j: (i, 0))
    wspec = pl.BlockSpec((D_MODEL, tn), lambda i, j: (0, j))
    gspec = lambda col: pl.BlockSpec((tm, tn), lambda i, j: (i, col * nj + j))
    return pl.pallas_call(
        _merge_kernel,
        out_shape=jax.ShapeDtypeStruct((m, D_MODEL), BF16),
        grid=(m // tm, nj),
        in_specs=[yspec, yspec, yspec, wspec, wspec, wspec, gspec(U3_GS), gspec(U3_GF), gspec(U3_GM)],
        out_specs=pl.BlockSpec((tm, tn), lambda i, j: (i, j)),
        compiler_params=_params("parallel", "arbitrary"),
        name="gated_merge",
    )(y_ssd, y_fox, y_mem, w_s, w_f, w_m, u, u, u)


def _final_kernel(mg_ref, w_ref, x_ref, g_ref, o_ref, *, normalize):
    xo = x_ref[...] + jnp.dot(mg_ref[...], w_ref[...], preferred_element_type=F32)
    if normalize:
        ms = jnp.mean(xo * xo, axis=-1, keepdims=True)
        xo = xo * lax.rsqrt(ms + EPS) * g_ref[...]
    o_ref[...] = xo


def _final(merged, w_out, x, g_final, tm, normalize):
    m = x.shape[0]
    return pl.pallas_call(
        functools.partial(_final_kernel, normalize=normalize),
        out_shape=jax.ShapeDtypeStruct((m, D_MODEL), F32),
        grid=(m // tm,),
        in_specs=[pl.BlockSpec((tm, D_MODEL), lambda i: (i, 0)),
                  pl.BlockSpec((D_MODEL, D_MODEL), lambda i: (0, 0)),
                  pl.BlockSpec((tm, D_MODEL), lambda i: (i, 0)),
                  pl.BlockSpec((1, D_MODEL), lambda i: (0, 0))],
        out_specs=pl.BlockSpec((tm, D_MODEL), lambda i: (i, 0)),
        compiler_params=_params("parallel"),
        name="out_proj_norm",
    )(merged, w_out, x, g_final.reshape(1, D_MODEL))


def _row_tile(m, pref):
    t = pref
    while m % t:
        t //= 2
    return t


def _layer(x, conv0, h0, fox_cache, logf_past, mem_k, mem_v, wd, g_final, last):
    nbatch, t, d = x.shape
    m = nbatch * t
    x2 = x.reshape(m, d)
    tm = _row_tile(m, 1024)

    h = _rmsnorm(x2, wd["g_norm"], _row_tile(m, 256))
    u1 = _mm(h, wd["w1"], BF16, tm, 1024, "in_proj_ssd")
    u2, k_new, v_new = _mm_qkvg(h, wd["w2"], tm, 1024)
    u3 = _mm(h, wd["w3"], BF16, tm, 1024, "in_proj_mem_gates")
    small = _mm(h, wd["w_dt"], F32, tm, 128, "in_proj_dt")
    fft = _mm_rows(wd["w_ff_t"], h, tm, "in_proj_forget")

    past_t = None if logf_past is None else jnp.transpose(logf_past, (0, 2, 1))
    logf_t, ct = _logf(fft, wd["b_forget"], past_t, nbatch, t)
    ct = ct.reshape(nbatch, FOX_HEADS, 1, ct.shape[-1])

    conv0p = jnp.pad(conv0, ((0, 0), (8 - (SSD_CONV - 1), 0), (0, 0)))
    y_ssd, h_final, new_conv = _ssd(u1, small, conv0p, h0, wd["w_conv"], wd["b_conv"], wd["dt_bias"], wd["a_log"],
                                    wd["d_skip"], wd["g_ssd_out"], nbatch, t)

    if fox_cache is None:
        y_fox = _fox_prompt(u2, ct, nbatch, t)
    else:
        p = fox_cache[0].shape[1]
        y_fox = _fox_sample(u2, fox_cache[0].reshape(nbatch, p, D_MODEL), fox_cache[1].reshape(nbatch, p, D_MODEL),
                            ct, nbatch, t)

    y_mem = _mem_attend(u3, mem_k, mem_v, nbatch, t)
    merged = _merge(u3, y_ssd, y_fox, y_mem, wd["w_o_ssd"], wd["w_o_fox"], wd["w_o_mem"], tm, 512)
    y = _final(merged, wd["w_out"], x2, g_final, _row_tile(m, 512), last)

    return (y.reshape(nbatch, t, d), new_conv, h_final,
            k_new.reshape(nbatch, t, FOX_HEADS, FOX_HEAD_DIM), v_new.reshape(nbatch, t, FOX_HEADS, FOX_HEAD_DIM),
            jnp.transpose(logf_t, (0, 2, 1)))


def kernel(x_prompt, x_sample, mem_prompt, cache_fox_k, cache_fox_v, cache_fox_logf, state_ssd, state_ssd_conv,
           cache_mem_k, cache_mem_v, g_norm, w_in, w_conv, b_conv, dt_bias, a_log, d_skip, g_ssd_out, b_forget,
           g_mem, w_mem_kv, w_o_ssd, w_o_fox, w_o_mem, w_out, g_final):
    depth = w_in.shape[0]
    xp, xs = x_prompt, x_sample
    bp = xp.shape[0]
    n_mem = mem_prompt.shape[1]
    outs = [[] for _ in range(12)]
    for l in range(depth):
        w = w_in[l]
        wkv = _repack(w_mem_kv[l], 0, 2 * D_MODEL)
        w_dt, w_ff_t = _narrow_weights(w)
        wd = {
            "g_norm": g_norm[l],
            "w1": _repack(w, *W_IN_SEGMENTS[0]), "w2": _repack(w, *W_IN_SEGMENTS[1]),
            "w3": _repack(w, *W_IN_SEGMENTS[2]),
            "w_dt": w_dt, "w_ff_t": w_ff_t,
            "w_conv": w_conv[l], "b_conv": b_conv[l], "dt_bias": dt_bias[l], "a_log": a_log[l], "d_skip": d_skip[l],
            "g_ssd_out": g_ssd_out[l], "b_forget": b_forget[l],
            "w_o_ssd": _repack(w_o_ssd[l], 0, D_MODEL), "w_o_fox": _repack(w_o_fox[l], 0, D_MODEL),
            "w_o_mem": _repack(w_o_mem[l], 0, D_MODEL), "w_out": _repack(w_out[l], 0, D_MODEL),
        }
        hm = _rmsnorm(mem_prompt.reshape(bp * n_mem, D_MODEL), g_mem[l], _row_tile(bp * n_mem, 256))
        tmm = _row_tile(bp * n_mem, 1024)
        mk = _mm(hm, wkv, F32, tmm, 1024, "mem_k_proj", 0, D_MODEL).reshape(bp, n_mem, D_MODEL)
        mv = _mm(hm, wkv, F32, tmm, 1024, "mem_v_proj", D_MODEL, D_MODEL).reshape(bp, n_mem, D_MODEL)

        xp, c_p, h_p, k_p, v_p, lf_p = _layer(
            xp, jnp.zeros((bp, SSD_CONV - 1, SSD_CONV_CH), F32),
            jnp.zeros((bp, SSD_HEADS, SSD_HEAD_DIM, SSD_D_STATE), F32), None, None,
            mk, mv, wd, g_final, l == depth - 1)
        bs = xs.shape[0]
        xs, c_s, h_s, k_s, v_s, lf_s = _layer(
            xs, state_ssd_conv[l], state_ssd[l], (cache_fox_k[l], cache_fox_v[l]), cache_fox_logf[l],
            cache_mem_k[l].reshape(bs, n_mem, D_MODEL), cache_mem_v[l].reshape(bs, n_mem, D_MODEL),
            wd, g_final, l == depth - 1)
        for lst, val in zip(outs, (k_p, v_p, lf_p, h_p, c_p,
                                   mk.reshape(bp, n_mem, MEM_HEADS, MEM_HEAD_DIM),
                                   mv.reshape(bp, n_mem, MEM_HEADS, MEM_HEAD_DIM),
                                   k_s, v_s, lf_s, h_s, c_s)):
            lst.append(val)
    return (xp, xs) + tuple(jnp.stack(o) for o in outs)
```

```python
import functools

import numpy as np
import jax
import jax.numpy as jnp
from jax import lax
from jax.experimental import pallas as pl
from jax.experimental.pallas import tpu as pltpu

F32 = jnp.float32
BF16 = jnp.bfloat16

EPS = 1e-6
D_MODEL = 2048
SSD_HEAD_DIM = 64
SSD_HEADS = 32
SSD_GROUPS = 4
SSD_D_STATE = 128
SSD_CONV = 4
SSD_BC = SSD_GROUPS * SSD_D_STATE
SSD_CONV_CH = D_MODEL + 2 * SSD_BC
FOX_HEADS = 16
FOX_HEAD_DIM = 128
MEM_HEADS = 4
MEM_HEAD_DIM = 512
SSD_CHUNK = 64
NEG_BIG = -1e30

VMEM_LIMIT = 56 * 1024 * 1024

W_IN_SEGMENTS = ((0, D_MODEL + SSD_CONV_CH),
                 (D_MODEL + SSD_CONV_CH + SSD_HEADS, 4 * D_MODEL),
                 (D_MODEL + SSD_CONV_CH + SSD_HEADS + 4 * D_MODEL + FOX_HEADS, 5 * D_MODEL))
W_IN_DT = D_MODEL + SSD_CONV_CH
W_IN_FF = W_IN_SEGMENTS[1][0] + 4 * D_MODEL
U1_Z, U1_XS = 0, 1
U2_Q, U2_K, U2_V, U2_G = 0, 1, 2, 3
U3_Q, U3_G, U3_GS, U3_GF, U3_GM = 0, 1, 2, 3, 4
LOG2E = 1.4426950408889634


def _params(*sem):
    return pltpu.CompilerParams(dimension_semantics=sem, vmem_limit_bytes=VMEM_LIMIT)


def _split3(x):
    hi = x.astype(BF16)
    r1 = x - hi.astype(F32)
    mid = r1.astype(BF16)
    lo = (r1 - mid.astype(F32)).astype(BF16)
    return hi, mid, lo


def _softplus(x):
    return jnp.maximum(x, 0.0) + jnp.log1p(jnp.exp(-jnp.abs(x)))


def _silu(x):
    return x * jax.nn.sigmoid(x)


def _rmsnorm_kernel(x_ref, g_ref, o_ref):
    x = x_ref[...]
    ms = jnp.mean(x * x, axis=-1, keepdims=True)
    o_ref[...] = (x * lax.rsqrt(ms + EPS) * g_ref[...]).astype(o_ref.dtype)


def _rmsnorm(x, g, tm):
    m, d = x.shape
    return pl.pallas_call(
        _rmsnorm_kernel,
        out_shape=jax.ShapeDtypeStruct((m, d), BF16),
        grid=(m // tm,),
        in_specs=[pl.BlockSpec((tm, d), lambda i: (i, 0)), pl.BlockSpec((1, d), lambda i: (0, 0))],
        out_specs=pl.BlockSpec((tm, d), lambda i: (i, 0)),
        compiler_params=_params("parallel"),
        name="rmsnorm",
    )(x, g.reshape(1, d))


def _mm_kernel(h_ref, w_ref, o_ref):
    o_ref[...] = jnp.dot(h_ref[...], w_ref[...], preferred_element_type=F32).astype(o_ref.dtype)


def _mm(h, w, out_dtype, tm, tn, name, w_col0=0, n=None):
    m, k = h.shape
    n = w.shape[1] if n is None else n
    j0 = w_col0 // tn
    return pl.pallas_call(
        _mm_kernel,
        out_shape=jax.ShapeDtypeStruct((m, n), out_dtype),
        grid=(m // tm, n // tn),
        in_specs=[pl.BlockSpec((tm, k), lambda i, j: (i, 0)), pl.BlockSpec((k, tn), lambda i, j: (0, j0 + j))],
        out_specs=pl.BlockSpec((tm, tn), lambda i, j: (i, j)),
        compiler_params=_params("parallel", "arbitrary"),
        name=name,
    )(h, w)


def _mm_nt_kernel(h_ref, wt_ref, o_ref):
    o_ref[...] = lax.dot_general(h_ref[...], wt_ref[...], NT_DIMS, preferred_element_type=F32).astype(o_ref.dtype)


def _wt_rows_spec(tn, k, row0, grid_rank):
    g = 16
    assert row0 % g == 0 and tn % g == 0
    if grid_rank == 1:
        return pl.BlockSpec((pl.Element(tn), pl.Element(k)), lambda i: (row0, 0))
    return pl.BlockSpec((pl.Element(tn), pl.Element(k)), lambda i, j: ((row0 // g + j * (tn // g)) * g, 0))


def _mm_nt(h, wt, row0, n, out_dtype, tm, tn, name):
    m, k = h.shape
    return pl.pallas_call(
        _mm_nt_kernel,
        out_shape=jax.ShapeDtypeStruct((m, n), out_dtype),
        grid=(m // tm, n // tn),
        in_specs=[pl.BlockSpec((tm, k), lambda i, j: (i, 0)), _wt_rows_spec(tn, k, row0, 2)],
        out_specs=pl.BlockSpec((tm, tn), lambda i, j: (i, j)),
        compiler_params=_params("parallel", "arbitrary"),
        name=name,
    )(h, wt)


def _mm_rows_kernel(wt_ref, h_ref, o_ref):
    o_ref[...] = lax.dot_general(wt_ref[...], h_ref[...], NT_DIMS, preferred_element_type=F32)


def _mm_rows(wt, row0, r, h, tm, name):
    k = wt.shape[1]
    m = h.shape[0]
    return pl.pallas_call(
        _mm_rows_kernel,
        out_shape=jax.ShapeDtypeStruct((r, m), F32),
        grid=(m // tm,),
        in_specs=[_wt_rows_spec(r, k, row0, 1), pl.BlockSpec((tm, k), lambda i: (i, 0))],
        out_specs=pl.BlockSpec((r, tm), lambda i: (0, i)),
        compiler_params=_params("parallel"),
        name=name,
    )(wt, h)


def _mm_qkvg_kernel(h_ref, wt_ref, o_ref, k_ref, v_ref, *, nj):
    j = pl.program_id(1)
    acc = lax.dot_general(h_ref[...], wt_ref[...], NT_DIMS, preferred_element_type=F32)
    o_ref[...] = acc.astype(o_ref.dtype)

    @pl.when((j >= nj) & (j < 2 * nj))
    def _k():
        k_ref[...] = acc

    @pl.when((j >= 2 * nj) & (j < 3 * nj))
    def _v():
        v_ref[...] = acc


def _mm_qkvg(h, wt, row0, tm, tn):
    m, k = h.shape
    nj = D_MODEL // tn
    return pl.pallas_call(
        functools.partial(_mm_qkvg_kernel, nj=nj),
        out_shape=(jax.ShapeDtypeStruct((m, 4 * D_MODEL), BF16),
                   jax.ShapeDtypeStruct((m, D_MODEL), F32), jax.ShapeDtypeStruct((m, D_MODEL), F32)),
        grid=(m // tm, 4 * nj),
        in_specs=[pl.BlockSpec((tm, k), lambda i, j: (i, 0)), _wt_rows_spec(tn, k, row0, 2)],
        out_specs=(pl.BlockSpec((tm, tn), lambda i, j: (i, j)),
                   pl.BlockSpec((tm, tn), lambda i, j: (i, jnp.clip(j - nj, 0, nj - 1))),
                   pl.BlockSpec((tm, tn), lambda i, j: (i, jnp.clip(j - 2 * nj, 0, nj - 1)))),
        compiler_params=_params("parallel", "arbitrary"),
        name="in_proj_qkvg",
    )(h, wt)


def _cast_kernel(a_ref, o_ref):
    o_ref[...] = a_ref[...].astype(o_ref.dtype)


def _cast_bf16(w):
    rows, cols = w.shape
    tr = max(d for d in range(16, 1025, 16) if rows % d == 0)
    return pl.pallas_call(
        _cast_kernel,
        out_shape=jax.ShapeDtypeStruct((rows, cols), BF16),
        grid=(rows // tr,),
        in_specs=[pl.BlockSpec((tr, cols), lambda i: (i, 0))],
        out_specs=pl.BlockSpec((tr, cols), lambda i: (i, 0)),
        compiler_params=_params("parallel"),
        name="weight_cast",
    )(w)


def _upper3(tb):
    k = lax.broadcasted_iota(jnp.int32, (3 * tb, tb), 0) % tb
    t = lax.broadcasted_iota(jnp.int32, (3 * tb, tb), 1)
    return jnp.where(k <= t, 1.0, 0.0).astype(BF16)


def _cumsum_lanes(x, carry, tb):
    n = x.shape[1]
    u3 = _upper3(tb)
    out = []
    for s in range(0, n, tb):
        hi, mid, lo = _split3(x[:, s:s + tb])
        c = jnp.dot(jnp.concatenate([hi, mid, lo], axis=1), u3, preferred_element_type=F32) + carry
        carry = c[:, tb - 1:tb]
        out.append(c)
    return out, carry


def _logf_kernel(*refs, nb, t, p):
    if p:
        fft_ref, bias_ref, past_ref, lf_ref, c_ref = refs
    else:
        fft_ref, bias_ref, lf_ref, c_ref = refs
    lf_all = -_softplus(-(fft_ref[...] + bias_ref[...]))
    for b in range(nb):
        lf = lf_all[:, b * t:(b + 1) * t]
        lf_ref[b] = lf
        carry = jnp.zeros((FOX_HEADS, 1), F32)
        if p:
            tbp = min(256, p)
            blocks, carry = _cumsum_lanes(past_ref[b], carry, tbp)
            for i, c in enumerate(blocks):
                c_ref[b, :, i * tbp:(i + 1) * tbp] = c
        tb = min(256, t)
        blocks, carry = _cumsum_lanes(lf, carry, tb)
        for i, c in enumerate(blocks):
            c_ref[b, :, p + i * tb:p + (i + 1) * tb] = c


def _logf(fft, b_forget, past_t, nbatch, t):
    p = 0 if past_t is None else past_t.shape[2]
    nb = 1 if t % 128 == 0 else nbatch
    grid = (nbatch // nb,)
    in_specs = [pl.BlockSpec((FOX_HEADS, nb * t), lambda i: (0, i)),
                pl.BlockSpec((FOX_HEADS, 1), lambda i: (0, 0))]
    args = [fft, b_forget.reshape(FOX_HEADS, 1)]
    if p:
        in_specs.append(pl.BlockSpec((nb, FOX_HEADS, p), lambda i: (i, 0, 0)))
        args.append(past_t)
    return pl.pallas_call(
        functools.partial(_logf_kernel, nb=nb, t=t, p=p),
        out_shape=(jax.ShapeDtypeStruct((nbatch, FOX_HEADS, t), F32),
                   jax.ShapeDtypeStruct((nbatch, FOX_HEADS, p + t), F32)),
        grid=grid,
        in_specs=in_specs,
        out_specs=(pl.BlockSpec((nb, FOX_HEADS, t), lambda i: (i, 0, 0)),
                   pl.BlockSpec((nb, FOX_HEADS, p + t), lambda i: (i, 0, 0))),
        compiler_params=_params("parallel"),
        name="logf_cumsum",
    )(*args)


def _expand_matrix():
    r = np.arange(128)[:, None]
    c = np.arange(D_MODEL)[None, :]
    return jnp.asarray(((r < 96) & ((r % SSD_HEADS) == (c // SSD_HEAD_DIM))).astype(np.float32), dtype=BF16)


def _pack3(x):
    lane = lax.broadcasted_iota(jnp.int32, x.shape, 1)
    x = jnp.where(lane < SSD_HEADS, x, 0.0)
    hi, mid, lo = _split3(x)
    packed = hi.astype(F32) + pltpu.roll(mid.astype(F32), SSD_HEADS, 1) + pltpu.roll(lo.astype(F32), 2 * SSD_HEADS, 1)
    return packed.astype(BF16)


def _ssd_kernel(xs_ref, b_ref, c_ref, z_ref, sm_ref, conv0_ref, h0_ref,
                wconv_ref, bconv_ref, dtb_ref, alog_ref, dskip_ref, gout_ref, e3_ref,
                y_ref, hfin_ref, convout_ref,
                st_ref, xpad_ref):
    ci = pl.program_id(1)
    L = xs_ref.shape[0]
    P = SSD_HEAD_DIM
    W = D_MODEL

    @pl.when(ci == 0)
    def _init():
        st_ref[...] = h0_ref[...].T
        xpad_ref[0:8, :] = conv0_ref[...]

    xpad_ref[8:8 + L, 0:W] = xs_ref[...].astype(F32)
    xpad_ref[8:8 + L, W:W + SSD_BC] = b_ref[...].astype(F32)
    xpad_ref[8:8 + L, W + SSD_BC:] = c_ref[...].astype(F32)

    def conv(lo, hi):
        acc = bconv_ref[:, lo:hi] + xpad_ref[8:8 + L, lo:hi] * wconv_ref[3:4, lo:hi]
        for i in range(SSD_CONV - 1):
            acc = acc + xpad_ref[5 + i:5 + i + L, lo:hi] * wconv_ref[i:i + 1, lo:hi]
        return _silu(acc)

    xs = conv(0, W)
    bm = conv(W, W + SSD_BC).astype(BF16)
    cm = conv(W + SSD_BC, W + 2 * SSD_BC).astype(BF16)

    @pl.when(ci == pl.num_programs(1) - 1)
    def _conv_out():
        convout_ref[...] = xpad_ref[L:L + 8, :]
    xpad_ref[0:8, :] = xpad_ref[L:L + 8, :]

    dt = _softplus(sm_ref[...] + dtb_ref[...])
    a_small = -jnp.exp(alog_ref[...])
    lane = lax.broadcasted_iota(jnp.int32, (L, 128), 1)
    dta = jnp.where(lane < SSD_HEADS, dt * a_small, 0.0)
    hi, mid, lo = _split3(dta)
    kk = lax.broadcasted_iota(jnp.int32, (L, 3 * L), 1) % L
    ll = lax.broadcasted_iota(jnp.int32, (L, 3 * L), 0)
    tri3 = jnp.where(kk <= ll, 1.0, 0.0).astype(BF16)
    acum = jnp.dot(tri3, jnp.concatenate([hi, mid, lo], axis=0), preferred_element_type=F32)

    e3 = e3_ref[...]
    a_x = jnp.dot(_pack3(acum), e3, preferred_element_type=F32)
    dt_x = jnp.dot(_pack3(dt), e3, preferred_element_type=F32)

    row = lax.broadcasted_iota(jnp.int32, (L, W), 0)
    sidx = lax.broadcasted_iota(jnp.int32, (L, W), 1) % L
    a_row = jnp.sum(jnp.where(row == sidx, a_x, 0.0), axis=0, keepdims=True)
    ldec = jnp.exp(jnp.where(row >= sidx, a_x - a_row, NEG_BIG))

    hpm = 256 // L
    cbx = []
    for g in range(SSD_GROUPS):
        cg = cm[:, g * SSD_D_STATE:(g + 1) * SSD_D_STATE]
        bg = bm[:, g * SSD_D_STATE:(g + 1) * SSD_D_STATE]
        brep = jnp.concatenate([bg] * (SSD_HEADS // SSD_GROUPS), axis=0)
        cbx.append(lax.dot_general(cg, brep, (((1,), (1,)), ((), ())), preferred_element_type=F32))
    m = (jnp.concatenate(cbx, axis=1) * ldec).astype(BF16)

    xdt = xs * dt_x
    xdt_b = xdt.astype(BF16)
    rb = lax.broadcasted_iota(jnp.int32, (256, hpm * P), 0) // L
    cb = lax.broadcasted_iota(jnp.int32, (256, hpm * P), 1) // P
    blockmask = rb == cb
    y_parts = []
    for j in range(SSD_HEADS // hpm):
        xj = xdt_b[:, j * hpm * P:(j + 1) * hpm * P]
        bd = jnp.where(blockmask, jnp.concatenate([xj] * hpm, axis=0), jnp.zeros((), BF16))
        y_parts.append(jnp.dot(m[:, j * 256:(j + 1) * 256], bd, preferred_element_type=F32))
    y = jnp.concatenate(y_parts, axis=1)

    st = st_ref[...]
    st_b = st.astype(BF16)
    gw = W // SSD_GROUPS
    yo = [jnp.dot(cm[:, g * SSD_D_STATE:(g + 1) * SSD_D_STATE], st_b[:, g * gw:(g + 1) * gw],
                  preferred_element_type=F32) for g in range(SSD_GROUPS)]
    y = y + jnp.concatenate(yo, axis=1) * jnp.exp(a_x)

    a_last = a_x[L - 1:L, :]
    xw = (xdt * jnp.exp(a_last - a_x)).astype(BF16)
    upd = [lax.dot_general(bm[:, g * SSD_D_STATE:(g + 1) * SSD_D_STATE], xw[:, g * gw:(g + 1) * gw],
                           (((0,), (0,)), ((), ())), preferred_element_type=F32) for g in range(SSD_GROUPS)]
    st_new = st * jnp.exp(a_last) + jnp.concatenate(upd, axis=1)
    st_ref[...] = st_new

    @pl.when(ci == pl.num_programs(1) - 1)
    def _state_out():
        hfin_ref[...] = st_new.T

    y = y + dskip_ref[...] * xs
    gz = y * _silu(z_ref[...].astype(F32))
    ms = jnp.mean(gz * gz, axis=-1, keepdims=True)
    y_ref[...] = (gz * lax.rsqrt(ms + EPS) * gout_ref[...]).astype(y_ref.dtype)


def _ssd(u, small, conv0, h0, w_conv, b_conv, dt_bias, a_log, d_skip, g_ssd_out, nbatch, t):
    L = SSD_CHUNK
    nc = t // L
    m = nbatch * t
    pad128 = lambda v: jnp.pad(v.astype(F32), (0, 128 - v.shape[0])).reshape(1, 128)
    rep = lambda v: jnp.repeat(v.astype(F32), SSD_HEAD_DIM).reshape(1, D_MODEL)
    row = lambda b, c: b * nc + c
    const2 = lambda shape: pl.BlockSpec(shape, lambda b, c: (0, 0))
    y, hfin, convout = pl.pallas_call(
        _ssd_kernel,
        out_shape=(jax.ShapeDtypeStruct((m, D_MODEL), BF16),
                   jax.ShapeDtypeStruct((nbatch, D_MODEL, SSD_D_STATE), F32),
                   jax.ShapeDtypeStruct((nbatch, 8, SSD_CONV_CH), F32)),
        grid=(nbatch, nc),
        in_specs=[
            pl.BlockSpec((L, D_MODEL), lambda b, c: (row(b, c), U1_XS)),
            pl.BlockSpec((L, SSD_BC), lambda b, c: (row(b, c), (U1_XS + 1) * D_MODEL // SSD_BC)),
            pl.BlockSpec((L, SSD_BC), lambda b, c: (row(b, c), (U1_XS + 1) * D_MODEL // SSD_BC + 1)),
            pl.BlockSpec((L, D_MODEL), lambda b, c: (row(b, c), U1_Z)),
            pl.BlockSpec((L, 128), lambda b, c: (row(b, c), 0)),
            pl.BlockSpec((None, 8, SSD_CONV_CH), lambda b, c: (b, 0, 0)),
            pl.BlockSpec((None, D_MODEL, SSD_D_STATE), lambda b, c: (b, 0, 0)),
            const2((SSD_CONV, SSD_CONV_CH)), const2((1, SSD_CONV_CH)),
            const2((1, 128)), const2((1, 128)), const2((1, D_MODEL)), const2((1, D_MODEL)),
            const2((128, D_MODEL)),
        ],
        out_specs=(pl.BlockSpec((L, D_MODEL), lambda b, c: (row(b, c), 0)),
                   pl.BlockSpec((None, D_MODEL, SSD_D_STATE), lambda b, c: (b, 0, 0)),
                   pl.BlockSpec((None, 8, SSD_CONV_CH), lambda b, c: (b, 0, 0))),
        scratch_shapes=[pltpu.VMEM((SSD_D_STATE, D_MODEL), F32), pltpu.VMEM((L + 8, SSD_CONV_CH), F32)],
        compiler_params=_params("parallel", "arbitrary"),
        name="ssd_scan",
    )(u, u, u, u, small, conv0, h0.reshape(nbatch, D_MODEL, SSD_D_STATE),
      w_conv, b_conv.reshape(1, SSD_CONV_CH), pad128(dt_bias), pad128(a_log), rep(d_skip),
      g_ssd_out.reshape(1, D_MODEL), _expand_matrix())
    return y, hfin.reshape(nbatch, SSD_HEADS, SSD_HEAD_DIM, SSD_D_STATE), convout[:, 8 - (SSD_CONV - 1):]


def _col_from_row(row_vals):
    n = row_vals.shape[1]
    r = lax.broadcasted_iota(jnp.int32, (n, n), 0)
    c = lax.broadcasted_iota(jnp.int32, (n, n), 1)
    return jnp.sum(jnp.where(r == c, row_vals, 0.0), axis=1, keepdims=True)


FOX_PAIR = 4
NT_DIMS = (((1,), (1,)), ((), ()))
FOX_TQ = 512


def _fox_prompt_kernel(q_ref, k_ref, v_ref, fg_ref, c_ref, o_ref, va_ref, *, tq):
    qi = pl.program_id(2)
    hd = FOX_HEAD_DIM
    t = k_ref.shape[0]
    c1 = hd ** -0.5 * LOG2E
    q0 = pl.multiple_of(qi * tq, tq)
    lanes = [slice(j * hd, (j + 1) * hd) for j in range(FOX_PAIR)]

    @pl.when(qi == 0)
    def _stage_values():
        for j in range(FOX_PAIR):
            va_ref[j, :, 0:hd] = v_ref[:, lanes[j]]
            va_ref[j, :, hd:2 * hd] = jnp.ones((t, hd), BF16)

    cq2 = [_col_from_row(c_ref[j, :, pl.ds(q0, tq)]) * LOG2E for j in range(FOX_PAIR)]
    row = lax.broadcasted_iota(jnp.int32, (tq, tq), 0)
    col = lax.broadcasted_iota(jnp.int32, (tq, tq), 1)

    def scores(j, k0, masked):
        s = lax.dot_general(q_ref[:, lanes[j]], k_ref[pl.ds(k0, tq), lanes[j]], NT_DIMS, preferred_element_type=F32)
        t2 = s * c1 - c_ref[j, :, pl.ds(k0, tq)] * LOG2E
        return jnp.where(col <= row, t2, NEG_BIG) if masked else t2

    shift = []
    for j in range(FOX_PAIR):
        qk = q_ref[:, lanes[j]].astype(F32) * k_ref[pl.ds(q0, tq), lanes[j]].astype(F32)
        own = jnp.sum(qk, axis=1, keepdims=True) * c1 - cq2[j]
        rep = jnp.broadcast_to(own, (tq, hd))
        shift.append(jnp.concatenate([rep] * (tq // hd), axis=1))

    def fast_tile(k0, accs, masked):
        out = []
        for j in range(FOX_PAIR):
            p = jnp.exp2(scores(j, k0, masked) - shift[j]).astype(BF16)
            out.append(accs[j] + jnp.dot(p, va_ref[j, pl.ds(k0, tq), :], preferred_element_type=F32))
        return tuple(out)

    accs = tuple(jnp.zeros((tq, 2 * hd), F32) for _ in range(FOX_PAIR))
    accs = lax.fori_loop(0, qi, lambda i, a: fast_tile(pl.multiple_of(i * tq, tq), a, False), accs)
    accs = fast_tile(q0, accs, True)
    bad = jnp.float32(0.0)
    for j in range(FOX_PAIR):
        num, den = accs[j][:, 0:hd], accs[j][:, hd:2 * hd]
        o_ref[:, lanes[j]] = (num / den * _silu(fg_ref[:, lanes[j]].astype(F32))).astype(o_ref.dtype)
        bad = jnp.maximum(bad, jnp.max(jnp.where(jnp.isfinite(accs[j]), 0.0, 1.0)))

    @pl.when(bad > 0.0)
    def _running_max():
        def tile(k0, carry, masked):
            out = []
            for j in range(FOX_PAIR):
                m_i, l_i, acc = carry[j]
                t2 = scores(j, k0, masked)
                m_new = jnp.maximum(m_i, jnp.max(t2, axis=1, keepdims=True) + cq2[j])
                p = jnp.exp2(t2 - (m_new - cq2[j]))
                alpha = jnp.exp2(m_i - m_new)
                l_new = alpha * l_i + jnp.sum(p, axis=1, keepdims=True)
                acc = alpha * acc + jnp.dot(p.astype(BF16), v_ref[pl.ds(k0, tq), lanes[j]],
                                            preferred_element_type=F32)
                out.append((m_new, l_new, acc))
            return tuple(out)

        init = tuple((jnp.full((tq, 1), NEG_BIG, F32), jnp.zeros((tq, 1), F32), jnp.zeros((tq, hd), F32))
                     for _ in range(FOX_PAIR))
        carry = lax.fori_loop(0, qi, lambda i, cr: tile(pl.multiple_of(i * tq, tq), cr, False), init)
        carry = tile(q0, carry, True)
        for j in range(FOX_PAIR):
            _, l_i, acc = carry[j]
            o_ref[:, lanes[j]] = (acc / l_i * _silu(fg_ref[:, lanes[j]].astype(F32))).astype(o_ref.dtype)


def _fox_prompt(u2, ct, nbatch, t):
    tq = min(FOX_TQ, t)
    nq = t // tq
    w = FOX_PAIR * FOX_HEAD_DIM
    nb = D_MODEL // w
    return pl.pallas_call(
        functools.partial(_fox_prompt_kernel, tq=tq),
        out_shape=jax.ShapeDtypeStruct((nbatch * t, D_MODEL), BF16),
        grid=(nbatch, nb, nq),
        in_specs=[
            pl.BlockSpec((tq, w), lambda b, h, i: (b * nq + i, U2_Q * nb + h)),
            pl.BlockSpec((t, w), lambda b, h, i: (b, U2_K * nb + h)),
            pl.BlockSpec((t, w), lambda b, h, i: (b, U2_V * nb + h)),
            pl.BlockSpec((tq, w), lambda b, h, i: (b * nq + i, U2_G * nb + h)),
            pl.BlockSpec((None, FOX_PAIR, 1, t), lambda b, h, i: (b, h, 0, 0)),
        ],
        out_specs=pl.BlockSpec((tq, w), lambda b, h, i: (b * nq + i, h)),
        scratch_shapes=[pltpu.VMEM((FOX_PAIR, t, 2 * FOX_HEAD_DIM), BF16)],
        compiler_params=_params("parallel", "parallel", "arbitrary"),
        name="fox_prompt",
    )(u2, u2, u2, u2, ct)


def _fox_sample_kernel(q_ref, kp_ref, vp_ref, kn_ref, vn_ref, fg_ref, c_ref, o_ref, *, p, t):
    c1 = FOX_HEAD_DIM ** -0.5 * LOG2E
    r = lax.broadcasted_iota(jnp.int32, (t, t), 0)
    c = lax.broadcasted_iota(jnp.int32, (t, t), 1)
    for j in range(FOX_PAIR):
        sl = slice(j * FOX_HEAD_DIM, (j + 1) * FOX_HEAD_DIM)
        q = q_ref[:, sl]
        cq2 = _col_from_row(c_ref[j, :, p:p + t]) * LOG2E
        s_p = (lax.dot_general(q, kp_ref[:, sl].astype(BF16), NT_DIMS, preferred_element_type=F32) * c1
               - c_ref[j, :, 0:p] * LOG2E)
        s_n = (lax.dot_general(q, kn_ref[:, sl], NT_DIMS, preferred_element_type=F32) * c1
               - c_ref[j, :, p:p + t] * LOG2E)
        s_n = jnp.where(c <= r, s_n, NEG_BIG)
        m = jnp.maximum(jnp.max(s_p, axis=1, keepdims=True), jnp.max(s_n, axis=1, keepdims=True)) + cq2
        e_p = jnp.exp2(s_p - (m - cq2))
        e_n = jnp.exp2(s_n - (m - cq2))
        inv = 1.0 / (jnp.sum(e_p, axis=1, keepdims=True) + jnp.sum(e_n, axis=1, keepdims=True))
        o = (jnp.dot((e_p * inv).astype(BF16), vp_ref[:, sl].astype(BF16), preferred_element_type=F32)
             + jnp.dot((e_n * inv).astype(BF16), vn_ref[:, sl], preferred_element_type=F32))
        o_ref[:, sl] = (o * _silu(fg_ref[:, sl].astype(F32))).astype(o_ref.dtype)


def _fox_sample(u2, k_past, v_past, ct, nbatch, t):
    p = k_past.shape[1]
    w = FOX_PAIR * FOX_HEAD_DIM
    nb = D_MODEL // w
    past = pl.BlockSpec((None, p, w), lambda b, h: (b, 0, h))
    col = lambda seg: pl.BlockSpec((t, w), lambda b, h: (b, seg * nb + h))
    return pl.pallas_call(
        functools.partial(_fox_sample_kernel, p=p, t=t),
        out_shape=jax.ShapeDtypeStruct((nbatch * t, D_MODEL), BF16),
        grid=(nbatch, nb),
        in_specs=[col(U2_Q), past, past, col(U2_K), col(U2_V), col(U2_G),
                  pl.BlockSpec((None, FOX_PAIR, 1, p + t), lambda b, h: (b, h, 0, 0))],
        out_specs=pl.BlockSpec((t, w), lambda b, h: (b, h)),
        compiler_params=_params("parallel", "parallel"),
        name="fox_sample",
    )(u2, k_past, v_past, u2, u2, u2, ct)


def _mem_kernel(q_ref, g_ref, k_ref, v_ref, o_ref):
    scale = MEM_HEAD_DIM ** -0.5
    for h in range(MEM_HEADS):
        sl = slice(h * MEM_HEAD_DIM, (h + 1) * MEM_HEAD_DIM)
        s = lax.dot_general(q_ref[:, sl], k_ref[:, sl].astype(BF16), NT_DIMS, preferred_element_type=F32) * scale
        e = jnp.exp(s - jnp.max(s, axis=1, keepdims=True))
        p = e * (1.0 / jnp.sum(e, axis=1, keepdims=True))
        o = jnp.dot(p.astype(BF16), v_ref[:, sl].astype(BF16), preferred_element_type=F32)
        o_ref[:, sl] = (o * _silu(g_ref[:, sl].astype(F32))).astype(o_ref.dtype)


def _mem_attend(u3, mk, mv, nbatch, t):
    tq = min(512, t)
    nq = t // tq
    nm = mk.shape[1]
    kv = pl.BlockSpec((None, nm, D_MODEL), lambda b, i: (b, 0, 0))
    return pl.pallas_call(
        _mem_kernel,
        out_shape=jax.ShapeDtypeStruct((nbatch * t, D_MODEL), BF16),
        grid=(nbatch, nq),
        in_specs=[pl.BlockSpec((tq, D_MODEL), lambda b, i: (b * nq + i, U3_Q)),
                  pl.BlockSpec((tq, D_MODEL), lambda b, i: (b * nq + i, U3_G)), kv, kv],
        out_specs=pl.BlockSpec((tq, D_MODEL), lambda b, i: (b * nq + i, 0)),
        compiler_params=_params("parallel", "arbitrary"),
        name="mem_attend",
    )(u3, u3, mk, mv)


def _merge_kernel(ys_ref, yf_ref, ym_ref, ws_ref, wf_ref, wm_ref, gs_ref, gf_ref, gm_ref, o_ref):
    def branch(y_ref, w_ref, g_ref):
        return jax.nn.sigmoid(g_ref[...].astype(F32)) * jnp.dot(y_ref[...], w_ref[...], preferred_element_type=F32)
    o_ref[...] = (branch(ys_ref, ws_ref, gs_ref) + branch(yf_ref, wf_ref, gf_ref)
                  + branch(ym_ref, wm_ref, gm_ref)).astype(o_ref.dtype)


def _merge(u, y_ssd, y_fox, y_mem, w_s, w_f, w_m, tm, tn):
    m = y_ssd.shape[0]
    nj = D_MODEL // tn
    yspec = pl.BlockSpec((tm, D_MODEL), lambda i, j: (i, 0))
    wspec = pl.BlockSpec((D_MODEL, tn), lambda i, j: (0, j))
    gspec = lambda col: pl.BlockSpec((tm, tn), lambda i, j: (i, col * nj + j))
    return pl.pallas_call(
        _merge_kernel,
        out_shape=jax.ShapeDtypeStruct((m, D_MODEL), BF16),
        grid=(m // tm, nj),
        in_specs=[yspec, yspec, yspec, wspec, wspec, wspec, gspec(U3_GS), gspec(U3_GF), gspec(U3_GM)],
        out_specs=pl.BlockSpec((tm, tn), lambda i, j: (i, j)),
        compiler_params=_params("parallel", "arbitrary"),
        name="gated_merge",
    )(y_ssd, y_fox, y_mem, w_s, w_f, w_m, u, u, u)


def _final_kernel(mg_ref, w_ref, x_ref, g_ref, o_ref, *, normalize):
    xo = x_ref[...] + jnp.dot(mg_ref[...], w_ref[...], preferred_element_type=F32)
    if normalize:
        ms = jnp.mean(xo * xo, axis=-1, keepdims=True)
        xo = xo * lax.rsqrt(ms + EPS) * g_ref[...]
    o_ref[...] = xo


def _final(merged, w_out, x, g_final, tm, normalize):
    m = x.shape[0]
    return pl.pallas_call(
        functools.partial(_final_kernel, normalize=normalize),
        out_shape=jax.ShapeDtypeStruct((m, D_MODEL), F32),
        grid=(m // tm,),
        in_specs=[pl.BlockSpec((tm, D_MODEL), lambda i: (i, 0)),
                  pl.BlockSpec((D_MODEL, D_MODEL), lambda i: (0, 0)),
                  pl.BlockSpec((tm, D_MODEL), lambda i: (i, 0)),
                  pl.BlockSpec((1, D_MODEL), lambda i: (0, 0))],
        out_specs=pl.BlockSpec((tm, D_MODEL), lambda i: (i, 0)),
        compiler_params=_params("parallel"),
        name="out_proj_norm",
    )(merged, w_out, x, g_final.reshape(1, D_MODEL))


def _row_tile(m, pref):
    t = pref
    while m % t:
        t //= 2
    return t


def _layer(x, conv0, h0, fox_cache, logf_past, mem_k, mem_v, wd, g_final, last):
    nbatch, t, d = x.shape
    m = nbatch * t
    x2 = x.reshape(m, d)
    tm = _row_tile(m, 1024)

    h = _rmsnorm(x2, wd["g_norm"], _row_tile(m, 256))
    wt = wd["w_in_t"]
    (r1, n1), (r2, _), (r3, n3) = W_IN_SEGMENTS
    u1 = _mm_nt(h, wt, r1, n1, BF16, tm, 1024, "in_proj_ssd")
    u2, k_new, v_new = _mm_qkvg(h, wt, r2, tm, 1024)
    u3 = _mm_nt(h, wt, r3, n3, BF16, tm, 1024, "in_proj_mem_gates")
    small = _mm_nt(h, wt, W_IN_DT, 128, F32, tm, 128, "in_proj_dt")
    fft = _mm_rows(wt, W_IN_FF, FOX_HEADS, h, tm, "in_proj_forget")

    past_t = None if logf_past is None else jnp.transpose(logf_past, (0, 2, 1))
    logf_t, ct = _logf(fft, wd["b_forget"], past_t, nbatch, t)
    ct = ct.reshape(nbatch, FOX_HEADS, 1, ct.shape[-1])

    conv0p = jnp.pad(conv0, ((0, 0), (8 - (SSD_CONV - 1), 0), (0, 0)))
    y_ssd, h_final, new_conv = _ssd(u1, small, conv0p, h0, wd["w_conv"], wd["b_conv"], wd["dt_bias"], wd["a_log"],
                                    wd["d_skip"], wd["g_ssd_out"], nbatch, t)

    if fox_cache is None:
        y_fox = _fox_prompt(u2, ct, nbatch, t)
    else:
        p = fox_cache[0].shape[1]
        y_fox = _fox_sample(u2, fox_cache[0].reshape(nbatch, p, D_MODEL), fox_cache[1].reshape(nbatch, p, D_MODEL),
                            ct, nbatch, t)

    y_mem = _mem_attend(u3, mem_k, mem_v, nbatch, t)
    merged = _merge(u3, y_ssd, y_fox, y_mem, wd["w_o_ssd"], wd["w_o_fox"], wd["w_o_mem"], tm, 512)
    y = _final(merged, wd["w_out"], x2, g_final, _row_tile(m, 512), last)

    return (y.reshape(nbatch, t, d), new_conv, h_final,
            k_new.reshape(nbatch, t, FOX_HEADS, FOX_HEAD_DIM), v_new.reshape(nbatch, t, FOX_HEADS, FOX_HEAD_DIM),
            jnp.transpose(logf_t, (0, 2, 1)))


def kernel(x_prompt, x_sample, mem_prompt, cache_fox_k, cache_fox_v, cache_fox_logf, state_ssd, state_ssd_conv,
           cache_mem_k, cache_mem_v, g_norm, w_in, w_conv, b_conv, dt_bias, a_log, d_skip, g_ssd_out, b_forget,
           g_mem, w_mem_kv, w_o_ssd, w_o_fox, w_o_mem, w_out, g_final):
    depth = w_in.shape[0]
    xp, xs = x_prompt, x_sample
    bp = xp.shape[0]
    n_mem = mem_prompt.shape[1]
    outs = [[] for _ in range(12)]
    for l in range(depth):
        wkv = _cast_bf16(w_mem_kv[l])
        wd = {
            "g_norm": g_norm[l],
            "w_in_t": _cast_bf16(jnp.transpose(w_in[l])),
            "w_conv": w_conv[l], "b_conv": b_conv[l], "dt_bias": dt_bias[l], "a_log": a_log[l], "d_skip": d_skip[l],
            "g_ssd_out": g_ssd_out[l], "b_forget": b_forget[l],
            "w_o_ssd": _cast_bf16(w_o_ssd[l]), "w_o_fox": _cast_bf16(w_o_fox[l]),
            "w_o_mem": _cast_bf16(w_o_mem[l]), "w_out": _cast_bf16(w_out[l]),
        }
        hm = _rmsnorm(mem_prompt.reshape(bp * n_mem, D_MODEL), g_mem[l], _row_tile(bp * n_mem, 256))
        tmm = _row_tile(bp * n_mem, 1024)
        mk = _mm(hm, wkv, F32, tmm, 1024, "mem_k_proj", 0, D_MODEL).reshape(bp, n_mem, D_MODEL)
        mv = _mm(hm, wkv, F32, tmm, 1024, "mem_v_proj", D_MODEL, D_MODEL).reshape(bp, n_mem, D_MODEL)

        xp, c_p, h_p, k_p, v_p, lf_p = _layer(
            xp, jnp.zeros((bp, SSD_CONV - 1, SSD_CONV_CH), F32),
            jnp.zeros((bp, SSD_HEADS, SSD_HEAD_DIM, SSD_D_STATE), F32), None, None,
            mk, mv, wd, g_final, l == depth - 1)
        bs = xs.shape[0]
        xs, c_s, h_s, k_s, v_s, lf_s = _layer(
            xs, state_ssd_conv[l], state_ssd[l], (cache_fox_k[l], cache_fox_v[l]), cache_fox_logf[l],
            cache_mem_k[l].reshape(bs, n_mem, D_MODEL), cache_mem_v[l].reshape(bs, n_mem, D_MODEL),
            wd, g_final, l == depth - 1)
        for lst, val in zip(outs, (k_p, v_p, lf_p, h_p, c_p,
                                   mk.reshape(bp, n_mem, MEM_HEADS, MEM_HEAD_DIM),
                                   mv.reshape(bp, n_mem, MEM_HEADS, MEM_HEAD_DIM),
                                   k_s, v_s, lf_s, h_s, c_s)):
            lst.append(val)
    return (xp, xs) + tuple(jnp.stack(o) for o in outs)
```

```python
import functools

import numpy as np
import jax
import jax.numpy as jnp
from jax import lax
from jax.experimental import pallas as pl
from jax.experimental.pallas import tpu as pltpu

F32 = jnp.float32
BF16 = jnp.bfloat16

EPS = 1e-6
D_MODEL = 2048
SSD_HEAD_DIM = 64
SSD_HEADS = 32
SSD_GROUPS = 4
SSD_D_STATE = 128
SSD_CONV = 4
SSD_BC = SSD_GROUPS * SSD_D_STATE
SSD_CONV_CH = D_MODEL + 2 * SSD_BC
FOX_HEADS = 16
FOX_HEAD_DIM = 128
MEM_HEADS = 4
MEM_HEAD_DIM = 512
SSD_CHUNK = 64
SSD_NB = 2
NEG_BIG = -1e30

VMEM_LIMIT = 56 * 1024 * 1024

W_IN_SEGMENTS = ((0, D_MODEL + SSD_CONV_CH),
                 (D_MODEL + SSD_CONV_CH + SSD_HEADS, 4 * D_MODEL),
                 (D_MODEL + SSD_CONV_CH + SSD_HEADS + 4 * D_MODEL + FOX_HEADS, 5 * D_MODEL))
W_IN_DT = D_MODEL + SSD_CONV_CH
W_IN_FF = W_IN_SEGMENTS[1][0] + 4 * D_MODEL
U1_Z, U1_XS = 0, 1
U2_Q, U2_K, U2_V, U2_G = 0, 1, 2, 3
U3_Q, U3_G, U3_GS, U3_GF, U3_GM = 0, 1, 2, 3, 4
LOG2E = 1.4426950408889634


def _params(*sem):
    return pltpu.CompilerParams(dimension_semantics=sem, vmem_limit_bytes=VMEM_LIMIT)


def _split3(x):
    hi = x.astype(BF16)
    r1 = x - hi.astype(F32)
    mid = r1.astype(BF16)
    lo = (r1 - mid.astype(F32)).astype(BF16)
    return hi, mid, lo


def _softplus(x):
    return jnp.maximum(x, 0.0) + jnp.log1p(jnp.exp(-jnp.abs(x)))


def _silu(x):
    return x * jax.nn.sigmoid(x)


def _rmsnorm_kernel(x_ref, g_ref, o_ref):
    x = x_ref[...]
    ms = jnp.mean(x * x, axis=-1, keepdims=True)
    o_ref[...] = (x * lax.rsqrt(ms + EPS) * g_ref[...]).astype(o_ref.dtype)


def _rmsnorm(x, g, tm):
    m, d = x.shape
    return pl.pallas_call(
        _rmsnorm_kernel,
        out_shape=jax.ShapeDtypeStruct((m, d), BF16),
        grid=(m // tm,),
        in_specs=[pl.BlockSpec((tm, d), lambda i: (i, 0)), pl.BlockSpec((1, d), lambda i: (0, 0))],
        out_specs=pl.BlockSpec((tm, d), lambda i: (i, 0)),
        compiler_params=_params("parallel"),
        name="rmsnorm",
    )(x, g.reshape(1, d))


def _mm_kernel(h_ref, w_ref, o_ref):
    o_ref[...] = jnp.dot(h_ref[...], w_ref[...], preferred_element_type=F32).astype(o_ref.dtype)


def _mm(h, w, out_dtype, tm, tn, name, w_col0=0, n=None):
    m, k = h.shape
    n = w.shape[1] if n is None else n
    j0 = w_col0 // tn
    return pl.pallas_call(
        _mm_kernel,
        out_shape=jax.ShapeDtypeStruct((m, n), out_dtype),
        grid=(m // tm, n // tn),
        in_specs=[pl.BlockSpec((tm, k), lambda i, j: (i, 0)), pl.BlockSpec((k, tn), lambda i, j: (0, j0 + j))],
        out_specs=pl.BlockSpec((tm, tn), lambda i, j: (i, j)),
        compiler_params=_params("parallel", "arbitrary"),
        name=name,
    )(h, w)


def _mm_nt_kernel(h_ref, wt_ref, o_ref):
    o_ref[...] = lax.dot_general(h_ref[...], wt_ref[...], NT_DIMS, preferred_element_type=F32).astype(o_ref.dtype)


def _wt_rows_spec(tn, k, row0, grid_rank):
    g = 16
    assert row0 % g == 0 and tn % g == 0
    if grid_rank == 1:
        return pl.BlockSpec((pl.Element(tn), pl.Element(k)), lambda i: (row0, 0))
    return pl.BlockSpec((pl.Element(tn), pl.Element(k)), lambda i, j: ((row0 // g + j * (tn // g)) * g, 0))


def _mm_nt(h, wt, row0, n, out_dtype, tm, tn, name):
    m, k = h.shape
    return pl.pallas_call(
        _mm_nt_kernel,
        out_shape=jax.ShapeDtypeStruct((m, n), out_dtype),
        grid=(m // tm, n // tn),
        in_specs=[pl.BlockSpec((tm, k), lambda i, j: (i, 0)), _wt_rows_spec(tn, k, row0, 2)],
        out_specs=pl.BlockSpec((tm, tn), lambda i, j: (i, j)),
        compiler_params=_params("parallel", "arbitrary"),
        name=name,
    )(h, wt)


def _mm_ssd_kernel(h_ref, wt_ref, wdt_ref, wff_ref, o_ref, dt_ref, fft_ref):
    o_ref[...] = lax.dot_general(h_ref[...], wt_ref[...], NT_DIMS, preferred_element_type=F32).astype(o_ref.dtype)

    @pl.when(pl.program_id(1) == 0)
    def _narrow():
        dt_ref[...] = lax.dot_general(h_ref[...], wdt_ref[...], NT_DIMS, preferred_element_type=F32)
        fft_ref[...] = lax.dot_general(wff_ref[...], h_ref[...], NT_DIMS, preferred_element_type=F32)


def _mm_ssd(h, wt, tm, tn):
    m, k = h.shape
    row0, n = W_IN_SEGMENTS[0]
    fixed = lambda rows, r0: pl.BlockSpec((pl.Element(rows), pl.Element(k)), lambda i, j: (r0, 0))
    return pl.pallas_call(
        _mm_ssd_kernel,
        out_shape=(jax.ShapeDtypeStruct((m, n), BF16), jax.ShapeDtypeStruct((m, 128), F32),
                   jax.ShapeDtypeStruct((FOX_HEADS, m), F32)),
        grid=(m // tm, n // tn),
        in_specs=[pl.BlockSpec((tm, k), lambda i, j: (i, 0)), _wt_rows_spec(tn, k, row0, 2),
                  fixed(128, W_IN_DT), fixed(FOX_HEADS, W_IN_FF)],
        out_specs=(pl.BlockSpec((tm, tn), lambda i, j: (i, j)), pl.BlockSpec((tm, 128), lambda i, j: (i, 0)),
                   pl.BlockSpec((FOX_HEADS, tm), lambda i, j: (0, i))),
        compiler_params=_params("parallel", "arbitrary"),
        name="in_proj_ssd",
    )(h, wt, wt, wt)


def _mm_qkvg_kernel(h_ref, wt_ref, o_ref, k_ref, v_ref, *, nj):
    j = pl.program_id(1)
    acc = lax.dot_general(h_ref[...], wt_ref[...], NT_DIMS, preferred_element_type=F32)
    o_ref[...] = acc.astype(o_ref.dtype)

    @pl.when((j >= nj) & (j < 2 * nj))
    def _k():
        k_ref[...] = acc

    @pl.when((j >= 2 * nj) & (j < 3 * nj))
    def _v():
        v_ref[...] = acc


def _mm_qkvg(h, wt, row0, tm, tn):
    m, k = h.shape
    nj = D_MODEL // tn
    return pl.pallas_call(
        functools.partial(_mm_qkvg_kernel, nj=nj),
        out_shape=(jax.ShapeDtypeStruct((m, 4 * D_MODEL), BF16),
                   jax.ShapeDtypeStruct((m, D_MODEL), F32), jax.ShapeDtypeStruct((m, D_MODEL), F32)),
        grid=(m // tm, 4 * nj),
        in_specs=[pl.BlockSpec((tm, k), lambda i, j: (i, 0)), _wt_rows_spec(tn, k, row0, 2)],
        out_specs=(pl.BlockSpec((tm, tn), lambda i, j: (i, j)),
                   pl.BlockSpec((tm, tn), lambda i, j: (i, jnp.clip(j - nj, 0, nj - 1))),
                   pl.BlockSpec((tm, tn), lambda i, j: (i, jnp.clip(j - 2 * nj, 0, nj - 1)))),
        compiler_params=_params("parallel", "arbitrary"),
        name="in_proj_qkvg",
    )(h, wt)


def _cast_kernel(a_ref, o_ref):
    o_ref[...] = a_ref[...].astype(o_ref.dtype)


def _cast_bf16(w):
    rows, cols = w.shape
    tr = max(d for d in range(16, 1025, 16) if rows % d == 0)
    return pl.pallas_call(
        _cast_kernel,
        out_shape=jax.ShapeDtypeStruct((rows, cols), BF16),
        grid=(rows // tr,),
        in_specs=[pl.BlockSpec((tr, cols), lambda i: (i, 0))],
        out_specs=pl.BlockSpec((tr, cols), lambda i: (i, 0)),
        compiler_params=_params("parallel"),
        name="weight_cast",
    )(w)


def _upper3(tb):
    k = lax.broadcasted_iota(jnp.int32, (3 * tb, tb), 0) % tb
    t = lax.broadcasted_iota(jnp.int32, (3 * tb, tb), 1)
    return jnp.where(k <= t, 1.0, 0.0).astype(BF16)


def _cumsum_lanes(x, carry, tb):
    n = x.shape[1]
    u3 = _upper3(tb)
    out = []
    for s in range(0, n, tb):
        hi, mid, lo = _split3(x[:, s:s + tb])
        c = jnp.dot(jnp.concatenate([hi, mid, lo], axis=1), u3, preferred_element_type=F32) + carry
        carry = c[:, tb - 1:tb]
        out.append(c)
    return out, carry


def _logf_kernel(*refs, nb, t, p):
    if p:
        fft_ref, bias_ref, past_ref, lf_ref, c_ref = refs
    else:
        fft_ref, bias_ref, lf_ref, c_ref = refs
    lf_all = -_softplus(-(fft_ref[...] + bias_ref[...]))
    for b in range(nb):
        lf = lf_all[:, b * t:(b + 1) * t]
        lf_ref[b] = lf
        carry = jnp.zeros((FOX_HEADS, 1), F32)
        if p:
            tbp = min(256, p)
            blocks, carry = _cumsum_lanes(past_ref[b], carry, tbp)
            for i, c in enumerate(blocks):
                c_ref[b, :, i * tbp:(i + 1) * tbp] = c
        tb = min(256, t)
        blocks, carry = _cumsum_lanes(lf, carry, tb)
        for i, c in enumerate(blocks):
            c_ref[b, :, p + i * tb:p + (i + 1) * tb] = c


def _logf(fft, b_forget, past_t, nbatch, t):
    p = 0 if past_t is None else past_t.shape[2]
    nb = 1 if t % 128 == 0 else nbatch
    grid = (nbatch // nb,)
    in_specs = [pl.BlockSpec((FOX_HEADS, nb * t), lambda i: (0, i)),
                pl.BlockSpec((FOX_HEADS, 1), lambda i: (0, 0))]
    args = [fft, b_forget.reshape(FOX_HEADS, 1)]
    if p:
        in_specs.append(pl.BlockSpec((nb, FOX_HEADS, p), lambda i: (i, 0, 0)))
        args.append(past_t)
    return pl.pallas_call(
        functools.partial(_logf_kernel, nb=nb, t=t, p=p),
        out_shape=(jax.ShapeDtypeStruct((nbatch, FOX_HEADS, t), F32),
                   jax.ShapeDtypeStruct((nbatch, FOX_HEADS, p + t), F32)),
        grid=grid,
        in_specs=in_specs,
        out_specs=(pl.BlockSpec((nb, FOX_HEADS, t), lambda i: (i, 0, 0)),
                   pl.BlockSpec((nb, FOX_HEADS, p + t), lambda i: (i, 0, 0))),
        compiler_params=_params("parallel"),
        name="logf_cumsum",
    )(*args)


def _expand_matrix():
    r = np.arange(128)[:, None]
    c = np.arange(D_MODEL)[None, :]
    return jnp.asarray(((r < 96) & ((r % SSD_HEADS) == (c // SSD_HEAD_DIM))).astype(np.float32), dtype=BF16)


def _pack3(x):
    lane = lax.broadcasted_iota(jnp.int32, x.shape, 1)
    x = jnp.where(lane < SSD_HEADS, x, 0.0)
    hi, mid, lo = _split3(x)
    packed = hi.astype(F32) + pltpu.roll(mid.astype(F32), SSD_HEADS, 1) + pltpu.roll(lo.astype(F32), 2 * SSD_HEADS, 1)
    return packed.astype(BF16)


def _ssd_kernel(xs_ref, b_ref, c_ref, z_ref, sm_ref, conv0_ref, h0_ref,
                wconv_ref, bconv_ref, dtb_ref, alog_ref, dskip_ref, gout_ref, e3_ref,
                y_ref, hfin_ref, convout_ref,
                st_ref, xpad_ref):
    ci = pl.program_id(1)
    nb, L = xs_ref.shape[0], xs_ref.shape[1]
    P = SSD_HEAD_DIM
    W = D_MODEL
    gw = W // SSD_GROUPS
    hpm = 256 // L

    @pl.when(ci == 0)
    def _init():
        for bi in range(nb):
            st_ref[bi] = h0_ref[bi].T
            xpad_ref[bi, 0:8, :] = conv0_ref[bi]

    lane = lax.broadcasted_iota(jnp.int32, (L, 128), 1)
    kk = lax.broadcasted_iota(jnp.int32, (L, 3 * L), 1) % L
    ll = lax.broadcasted_iota(jnp.int32, (L, 3 * L), 0)
    tri3 = jnp.where(kk <= ll, 1.0, 0.0).astype(BF16)
    row = lax.broadcasted_iota(jnp.int32, (L, W), 0)
    sidx = lax.broadcasted_iota(jnp.int32, (L, W), 1) % L
    rb = lax.broadcasted_iota(jnp.int32, (256, hpm * P), 0) // L
    cb = lax.broadcasted_iota(jnp.int32, (256, hpm * P), 1) // P
    blockmask = rb == cb
    a_small = -jnp.exp(alog_ref[...])
    e3 = e3_ref[...]

    def chunk(bi):
        xpad_ref[bi, 8:8 + L, 0:W] = xs_ref[bi].astype(F32)
        xpad_ref[bi, 8:8 + L, W:W + SSD_BC] = b_ref[bi].astype(F32)
        xpad_ref[bi, 8:8 + L, W + SSD_BC:] = c_ref[bi].astype(F32)

        def conv(lo, hi):
            acc = bconv_ref[:, lo:hi] + xpad_ref[bi, 8:8 + L, lo:hi] * wconv_ref[3:4, lo:hi]
            for i in range(SSD_CONV - 1):
                acc = acc + xpad_ref[bi, 5 + i:5 + i + L, lo:hi] * wconv_ref[i:i + 1, lo:hi]
            return _silu(acc)

        xs = conv(0, W)
        bm = conv(W, W + SSD_BC).astype(BF16)
        cm = conv(W + SSD_BC, W + 2 * SSD_BC).astype(BF16)
        xpad_ref[bi, 0:8, :] = xpad_ref[bi, L:L + 8, :]

        dt = _softplus(sm_ref[bi] + dtb_ref[...])
        dta = jnp.where(lane < SSD_HEADS, dt * a_small, 0.0)
        hi, mid, lo = _split3(dta)
        acum = jnp.dot(tri3, jnp.concatenate([hi, mid, lo], axis=0), preferred_element_type=F32)
        a_x = jnp.dot(_pack3(acum), e3, preferred_element_type=F32)
        dt_x = jnp.dot(_pack3(dt), e3, preferred_element_type=F32)

        a_row = jnp.sum(jnp.where(row == sidx, a_x, 0.0), axis=0, keepdims=True)
        ldec = jnp.exp(jnp.where(row >= sidx, a_x - a_row, NEG_BIG))
        cbx = []
        for g in range(SSD_GROUPS):
            cg = cm[:, g * SSD_D_STATE:(g + 1) * SSD_D_STATE]
            bg = bm[:, g * SSD_D_STATE:(g + 1) * SSD_D_STATE]
            brep = jnp.concatenate([bg] * (SSD_HEADS // SSD_GROUPS), axis=0)
            cbx.append(lax.dot_general(cg, brep, NT_DIMS, preferred_element_type=F32))
        m = (jnp.concatenate(cbx, axis=1) * ldec).astype(BF16)

        xdt = xs * dt_x
        xdt_b = xdt.astype(BF16)
        y_parts = []
        for j in range(SSD_HEADS // hpm):
            xj = xdt_b[:, j * hpm * P:(j + 1) * hpm * P]
            bd = jnp.where(blockmask, jnp.concatenate([xj] * hpm, axis=0), jnp.zeros((), BF16))
            y_parts.append(jnp.dot(m[:, j * 256:(j + 1) * 256], bd, preferred_element_type=F32))
        y = jnp.concatenate(y_parts, axis=1)

        st = st_ref[bi]
        st_b = st.astype(BF16)
        yo = [jnp.dot(cm[:, g * SSD_D_STATE:(g + 1) * SSD_D_STATE], st_b[:, g * gw:(g + 1) * gw],
                      preferred_element_type=F32) for g in range(SSD_GROUPS)]
        y = y + jnp.concatenate(yo, axis=1) * jnp.exp(a_x)

        a_last = a_x[L - 1:L, :]
        xw = (xdt * jnp.exp(a_last - a_x)).astype(BF16)
        upd = [lax.dot_general(bm[:, g * SSD_D_STATE:(g + 1) * SSD_D_STATE], xw[:, g * gw:(g + 1) * gw],
                               (((0,), (0,)), ((), ())), preferred_element_type=F32) for g in range(SSD_GROUPS)]
        st_ref[bi] = st * jnp.exp(a_last) + jnp.concatenate(upd, axis=1)

        y = y + dskip_ref[...] * xs
        gz = y * _silu(z_ref[bi].astype(F32))
        ms = jnp.mean(gz * gz, axis=-1, keepdims=True)
        y_ref[bi] = (gz * lax.rsqrt(ms + EPS) * gout_ref[...]).astype(y_ref.dtype)

    for bi in range(nb):
        chunk(bi)

    @pl.when(ci == pl.num_programs(1) - 1)
    def _final():
        for bi in range(nb):
            hfin_ref[bi] = st_ref[bi].T
            convout_ref[bi] = xpad_ref[bi, 0:8, :]


def _ssd(u, small, conv0, h0, w_conv, b_conv, dt_bias, a_log, d_skip, g_ssd_out, nbatch, t):
    L = SSD_CHUNK
    nc = t // L
    nb = SSD_NB if nbatch % SSD_NB == 0 else 1
    pad128 = lambda v: jnp.pad(v.astype(F32), (0, 128 - v.shape[0])).reshape(1, 128)
    rep = lambda v: jnp.repeat(v.astype(F32), SSD_HEAD_DIM).reshape(1, D_MODEL)
    const2 = lambda shape: pl.BlockSpec(shape, lambda b, c: (0, 0))
    seq = lambda width, col: pl.BlockSpec((nb, L, width), lambda b, c: (b, c, col))
    per_seq = lambda rows, width: pl.BlockSpec((nb, rows, width), lambda b, c: (b, 0, 0))
    u3d = u.reshape(nbatch, t, u.shape[1])
    bc0 = (U1_XS + 1) * D_MODEL // SSD_BC
    y, hfin, convout = pl.pallas_call(
        _ssd_kernel,
        out_shape=(jax.ShapeDtypeStruct((nbatch, t, D_MODEL), BF16),
                   jax.ShapeDtypeStruct((nbatch, D_MODEL, SSD_D_STATE), F32),
                   jax.ShapeDtypeStruct((nbatch, 8, SSD_CONV_CH), F32)),
        grid=(nbatch // nb, nc),
        in_specs=[
            seq(D_MODEL, U1_XS), seq(SSD_BC, bc0), seq(SSD_BC, bc0 + 1), seq(D_MODEL, U1_Z), seq(128, 0),
            per_seq(8, SSD_CONV_CH), per_seq(D_MODEL, SSD_D_STATE),
            const2((SSD_CONV, SSD_CONV_CH)), const2((1, SSD_CONV_CH)),
            const2((1, 128)), const2((1, 128)), const2((1, D_MODEL)), const2((1, D_MODEL)),
            const2((128, D_MODEL)),
        ],
        out_specs=(seq(D_MODEL, 0), per_seq(D_MODEL, SSD_D_STATE), per_seq(8, SSD_CONV_CH)),
        scratch_shapes=[pltpu.VMEM((nb, SSD_D_STATE, D_MODEL), F32), pltpu.VMEM((nb, L + 8, SSD_CONV_CH), F32)],
        compiler_params=_params("parallel", "arbitrary"),
        name="ssd_scan",
    )(u3d, u3d, u3d, u3d, small.reshape(nbatch, t, 128), conv0, h0.reshape(nbatch, D_MODEL, SSD_D_STATE),
      w_conv, b_conv.reshape(1, SSD_CONV_CH), pad128(dt_bias), pad128(a_log), rep(d_skip),
      g_ssd_out.reshape(1, D_MODEL), _expand_matrix())
    return (y.reshape(nbatch * t, D_MODEL), hfin.reshape(nbatch, SSD_HEADS, SSD_HEAD_DIM, SSD_D_STATE),
            convout[:, 8 - (SSD_CONV - 1):])


def _col_from_row(row_vals):
    n = row_vals.shape[1]
    r = lax.broadcasted_iota(jnp.int32, (n, n), 0)
    c = lax.broadcasted_iota(jnp.int32, (n, n), 1)
    return jnp.sum(jnp.where(r == c, row_vals, 0.0), axis=1, keepdims=True)


FOX_PAIR = 4
NT_DIMS = (((1,), (1,)), ((), ()))
FOX_TQ = 512


def _fox_prompt_kernel(q_ref, k_ref, v_ref, fg_ref, c_ref, o_ref, va_ref, *, tq):
    qi = pl.program_id(2)
    hd = FOX_HEAD_DIM
    t = k_ref.shape[0]
    c1 = hd ** -0.5 * LOG2E
    q0 = pl.multiple_of(qi * tq, tq)
    lanes = [slice(j * hd, (j + 1) * hd) for j in range(FOX_PAIR)]

    @pl.when(qi == 0)
    def _stage_values():
        for j in range(FOX_PAIR):
            va_ref[j, :, 0:hd] = v_ref[:, lanes[j]]
            va_ref[j, :, hd:2 * hd] = jnp.ones((t, hd), BF16)

    cq2_rep = [jnp.broadcast_to(c_ref[j, :, pl.ds(q0, tq)], (hd, tq)).T * LOG2E for j in range(FOX_PAIR)]
    cq2 = [cr[:, 0:1] for cr in cq2_rep]
    row = lax.broadcasted_iota(jnp.int32, (tq, tq), 0)
    col = lax.broadcasted_iota(jnp.int32, (tq, tq), 1)

    def scores(j, k0, masked):
        s = lax.dot_general(q_ref[:, lanes[j]], k_ref[pl.ds(k0, tq), lanes[j]], NT_DIMS, preferred_element_type=F32)
        t2 = s * c1 - c_ref[j, :, pl.ds(k0, tq)] * LOG2E
        return jnp.where(col <= row, t2, NEG_BIG) if masked else t2

    shift = []
    ones = jnp.ones((hd, hd), BF16)
    for j in range(FOX_PAIR):
        qk = q_ref[:, lanes[j]] * k_ref[pl.ds(q0, tq), lanes[j]]
        shift.append(jnp.dot(qk, ones, preferred_element_type=F32) * c1 - cq2_rep[j])

    def fast_tile(k0, accs, masked):
        out = []
        for j in range(FOX_PAIR):
            t2 = scores(j, k0, masked)
            e = jnp.concatenate([t2[:, i:i + hd] - shift[j] for i in range(0, tq, hd)], axis=1)
            out.append(accs[j] + jnp.dot(jnp.exp2(e).astype(BF16), va_ref[j, pl.ds(k0, tq), :],
                                         preferred_element_type=F32))
        return tuple(out)

    accs = tuple(jnp.zeros((tq, 2 * hd), F32) for _ in range(FOX_PAIR))
    accs = lax.fori_loop(0, qi, lambda i, a: fast_tile(pl.multiple_of(i * tq, tq), a, False), accs)
    accs = fast_tile(q0, accs, True)
    bad = jnp.float32(0.0)
    for j in range(FOX_PAIR):
        num, den = accs[j][:, 0:hd], accs[j][:, hd:2 * hd]
        o_ref[:, lanes[j]] = (num / den * _silu(fg_ref[:, lanes[j]].astype(F32))).astype(o_ref.dtype)
        bad = jnp.maximum(bad, jnp.max(jnp.where(jnp.isfinite(accs[j]), 0.0, 1.0)))

    @pl.when(bad > 0.0)
    def _running_max():
        def tile(k0, carry, masked):
            out = []
            for j in range(FOX_PAIR):
                m_i, l_i, acc = carry[j]
                t2 = scores(j, k0, masked)
                m_new = jnp.maximum(m_i, jnp.max(t2, axis=1, keepdims=True) + cq2[j])
                p = jnp.exp2(t2 - (m_new - cq2[j]))
                alpha = jnp.exp2(m_i - m_new)
                l_new = alpha * l_i + jnp.sum(p, axis=1, keepdims=True)
                acc = alpha * acc + jnp.dot(p.astype(BF16), v_ref[pl.ds(k0, tq), lanes[j]],
                                            preferred_element_type=F32)
                out.append((m_new, l_new, acc))
            return tuple(out)

        init = tuple((jnp.full((tq, 1), NEG_BIG, F32), jnp.zeros((tq, 1), F32), jnp.zeros((tq, hd), F32))
                     for _ in range(FOX_PAIR))
        carry = lax.fori_loop(0, qi, lambda i, cr: tile(pl.multiple_of(i * tq, tq), cr, False), init)
        carry = tile(q0, carry, True)
        for j in range(FOX_PAIR):
            _, l_i, acc = carry[j]
            o_ref[:, lanes[j]] = (acc / l_i * _silu(fg_ref[:, lanes[j]].astype(F32))).astype(o_ref.dtype)


def _fox_prompt(u2, ct, nbatch, t):
    tq = min(FOX_TQ, t)
    nq = t // tq
    w = FOX_PAIR * FOX_HEAD_DIM
    nb = D_MODEL // w
    return pl.pallas_call(
        functools.partial(_fox_prompt_kernel, tq=tq),
        out_shape=jax.ShapeDtypeStruct((nbatch * t, D_MODEL), BF16),
        grid=(nbatch, nb, nq),
        in_specs=[
            pl.BlockSpec((tq, w), lambda b, h, i: (b * nq + i, U2_Q * nb + h)),
            pl.BlockSpec((t, w), lambda b, h, i: (b, U2_K * nb + h)),
            pl.BlockSpec((t, w), lambda b, h, i: (b, U2_V * nb + h)),
            pl.BlockSpec((tq, w), lambda b, h, i: (b * nq + i, U2_G * nb + h)),
            pl.BlockSpec((None, FOX_PAIR, 1, t), lambda b, h, i: (b, h, 0, 0)),
        ],
        out_specs=pl.BlockSpec((tq, w), lambda b, h, i: (b * nq + i, h)),
        scratch_shapes=[pltpu.VMEM((FOX_PAIR, t, 2 * FOX_HEAD_DIM), BF16)],
        compiler_params=_params("parallel", "parallel", "arbitrary"),
        name="fox_prompt",
    )(u2, u2, u2, u2, ct)


def _fox_sample_kernel(q_ref, kp_ref, vp_ref, kn_ref, vn_ref, fg_ref, c_ref, o_ref, *, p, t):
    c1 = FOX_HEAD_DIM ** -0.5 * LOG2E
    r = lax.broadcasted_iota(jnp.int32, (t, t), 0)
    c = lax.broadcasted_iota(jnp.int32, (t, t), 1)
    for j in range(FOX_PAIR):
        sl = slice(j * FOX_HEAD_DIM, (j + 1) * FOX_HEAD_DIM)
        q = q_ref[:, sl]
        cq2 = _col_from_row(c_ref[j, :, p:p + t]) * LOG2E
        s_p = (lax.dot_general(q, kp_ref[:, sl].astype(BF16), NT_DIMS, preferred_element_type=F32) * c1
               - c_ref[j, :, 0:p] * LOG2E)
        s_n = (lax.dot_general(q, kn_ref[:, sl], NT_DIMS, preferred_element_type=F32) * c1
               - c_ref[j, :, p:p + t] * LOG2E)
        s_n = jnp.where(c <= r, s_n, NEG_BIG)
        m = jnp.maximum(jnp.max(s_p, axis=1, keepdims=True), jnp.max(s_n, axis=1, keepdims=True)) + cq2
        e_p = jnp.exp2(s_p - (m - cq2))
        e_n = jnp.exp2(s_n - (m - cq2))
        inv = 1.0 / (jnp.sum(e_p, axis=1, keepdims=True) + jnp.sum(e_n, axis=1, keepdims=True))
        o = (jnp.dot((e_p * inv).astype(BF16), vp_ref[:, sl].astype(BF16), preferred_element_type=F32)
             + jnp.dot((e_n * inv).astype(BF16), vn_ref[:, sl], preferred_element_type=F32))
        o_ref[:, sl] = (o * _silu(fg_ref[:, sl].astype(F32))).astype(o_ref.dtype)


def _fox_sample(u2, k_past, v_past, ct, nbatch, t):
    p = k_past.shape[1]
    w = FOX_PAIR * FOX_HEAD_DIM
    nb = D_MODEL // w
    past = pl.BlockSpec((None, p, w), lambda b, h: (b, 0, h))
    col = lambda seg: pl.BlockSpec((t, w), lambda b, h: (b, seg * nb + h))
    return pl.pallas_call(
        functools.partial(_fox_sample_kernel, p=p, t=t),
        out_shape=jax.ShapeDtypeStruct((nbatch * t, D_MODEL), BF16),
        grid=(nbatch, nb),
        in_specs=[col(U2_Q), past, past, col(U2_K), col(U2_V), col(U2_G),
                  pl.BlockSpec((None, FOX_PAIR, 1, p + t), lambda b, h: (b, h, 0, 0))],
        out_specs=pl.BlockSpec((t, w), lambda b, h: (b, h)),
        compiler_params=_params("parallel", "parallel"),
        name="fox_sample",
    )(u2, k_past, v_past, u2, u2, u2, ct)


def _mem_kernel(q_ref, g_ref, k_ref, v_ref, o_ref):
    scale = MEM_HEAD_DIM ** -0.5
    for h in range(MEM_HEADS):
        sl = slice(h * MEM_HEAD_DIM, (h + 1) * MEM_HEAD_DIM)
        s = lax.dot_general(q_ref[:, sl], k_ref[:, sl].astype(BF16), NT_DIMS, preferred_element_type=F32) * scale
        e = jnp.exp(s - jnp.max(s, axis=1, keepdims=True))
        p = e * (1.0 / jnp.sum(e, axis=1, keepdims=True))
        o = jnp.dot(p.astype(BF16), v_ref[:, sl].astype(BF16), preferred_element_type=F32)
        o_ref[:, sl] = (o * _silu(g_ref[:, sl].astype(F32))).astype(o_ref.dtype)


def _mem_attend(u3, mk, mv, nbatch, t):
    tq = min(512, t)
    nq = t // tq
    nm = mk.shape[1]
    kv = pl.BlockSpec((None, nm, D_MODEL), lambda b, i: (b, 0, 0))
    return pl.pallas_call(
        _mem_kernel,
        out_shape=jax.ShapeDtypeStruct((nbatch * t, D_MODEL), BF16),
        grid=(nbatch, nq),
        in_specs=[pl.BlockSpec((tq, D_MODEL), lambda b, i: (b * nq + i, U3_Q)),
                  pl.BlockSpec((tq, D_MODEL), lambda b, i: (b * nq + i, U3_G)), kv, kv],
        out_specs=pl.BlockSpec((tq, D_MODEL), lambda b, i: (b * nq + i, 0)),
        compiler_params=_params("parallel", "arbitrary"),
        name="mem_attend",
    )(u3, u3, mk, mv)


def _merge_kernel(ys_ref, yf_ref, ym_ref, ws_ref, wf_ref, wm_ref, gs_ref, gf_ref, gm_ref, o_ref):
    def branch(y_ref, w_ref, g_ref):
        return jax.nn.sigmoid(g_ref[...].astype(F32)) * jnp.dot(y_ref[...], w_ref[...], preferred_element_type=F32)
    o_ref[...] = (branch(ys_ref, ws_ref, gs_ref) + branch(yf_ref, wf_ref, gf_ref)
                  + branch(ym_ref, wm_ref, gm_ref)).astype(o_ref.dtype)


def _merge(u, y_ssd, y_fox, y_mem, w_s, w_f, w_m, tm, tn):
    m = y_ssd.shape[0]
    nj = D_MODEL // tn
    yspec = pl.BlockSpec((tm, D_MODEL), lambda i, j: (i, 0))
    wspec = pl.BlockSpec((D_MODEL, tn), lambda i, j: (0, j))
    gspec = lambda col: pl.BlockSpec((tm, tn), lambda i, j: (i, col * nj + j))
    return pl.pallas_call(
        _merge_kernel,
        out_shape=jax.ShapeDtypeStruct((m, D_MODEL), BF16),
        grid=(m // tm, nj),
        in_specs=[yspec, yspec, yspec, wspec, wspec, wspec, gspec(U3_GS), gspec(U3_GF), gspec(U3_GM)],
        out_specs=pl.BlockSpec((tm, tn), lambda i, j: (i, j)),
        compiler_params=_params("parallel", "arbitrary"),
        name="gated_merge",
    )(y_ssd, y_fox, y_mem, w_s, w_f, w_m, u, u, u)


def _final_kernel(mg_ref, w_ref, x_ref, g_ref, o_ref, *, normalize):
    xo = x_ref[...] + jnp.dot(mg_ref[...], w_ref[...], preferred_element_type=F32)
    if normalize:
        ms = jnp.mean(xo * xo, axis=-1, keepdims=True)
        xo = xo * lax.rsqrt(ms + EPS) * g_ref[...]
    o_ref[...] = xo


def _final(merged, w_out, x, g_final, tm, normalize):
    m = x.shape[0]
    return pl.pallas_call(
        functools.partial(_final_kernel, normalize=normalize),
        out_shape=jax.ShapeDtypeStruct((m, D_MODEL), F32),
        grid=(m // tm,),
        in_specs=[pl.BlockSpec((tm, D_MODEL), lambda i: (i, 0)),
                  pl.BlockSpec((D_MODEL, D_MODEL), lambda i: (0, 0)),
                  pl.BlockSpec((tm, D_MODEL), lambda i: (i, 0)),
                  pl.BlockSpec((1, D_MODEL), lambda i: (0, 0))],
        out_specs=pl.BlockSpec((tm, D_MODEL), lambda i: (i, 0)),
        compiler_params=_params("parallel"),
        name="out_proj_norm",
    )(merged, w_out, x, g_final.reshape(1, D_MODEL))


def _row_tile(m, pref):
    t = pref
    while m % t:
        t //= 2
    return t


def _layer(x, conv0, h0, fox_cache, logf_past, mem_k, mem_v, wd, g_final, last):
    nbatch, t, d = x.shape
    m = nbatch * t
    x2 = x.reshape(m, d)
    tm = _row_tile(m, 1024)

    h = _rmsnorm(x2, wd["g_norm"], _row_tile(m, 256))
    wt = wd["w_in_t"]
    _, (r2, _), (r3, n3) = W_IN_SEGMENTS
    u1, small, fft = _mm_ssd(h, wt, tm, 1024)
    u2, k_new, v_new = _mm_qkvg(h, wt, r2, tm, 1024)
    u3 = _mm_nt(h, wt, r3, n3, BF16, tm, 1024, "in_proj_mem_gates")

    past_t = None if logf_past is None else jnp.transpose(logf_past, (0, 2, 1))
    logf_t, ct = _logf(fft, wd["b_forget"], past_t, nbatch, t)
    ct = ct.reshape(nbatch, FOX_HEADS, 1, ct.shape[-1])

    conv0p = jnp.pad(conv0, ((0, 0), (8 - (SSD_CONV - 1), 0), (0, 0)))
    y_ssd, h_final, new_conv = _ssd(u1, small, conv0p, h0, wd["w_conv"], wd["b_conv"], wd["dt_bias"], wd["a_log"],
                                    wd["d_skip"], wd["g_ssd_out"], nbatch, t)

    if fox_cache is None:
        y_fox = _fox_prompt(u2, ct, nbatch, t)
    else:
        p = fox_cache[0].shape[1]
        y_fox = _fox_sample(u2, fox_cache[0].reshape(nbatch, p, D_MODEL), fox_cache[1].reshape(nbatch, p, D_MODEL),
                            ct, nbatch, t)

    y_mem = _mem_attend(u3, mem_k, mem_v, nbatch, t)
    merged = _merge(u3, y_ssd, y_fox, y_mem, wd["w_o_ssd"], wd["w_o_fox"], wd["w_o_mem"], tm, 512)
    y = _final(merged, wd["w_out"], x2, g_final, _row_tile(m, 512), last)

    return (y.reshape(nbatch, t, d), new_conv, h_final,
            k_new.reshape(nbatch, t, FOX_HEADS, FOX_HEAD_DIM), v_new.reshape(nbatch, t, FOX_HEADS, FOX_HEAD_DIM),
            jnp.transpose(logf_t, (0, 2, 1)))


def kernel(x_prompt, x_sample, mem_prompt, cache_fox_k, cache_fox_v, cache_fox_logf, state_ssd, state_ssd_conv,
           cache_mem_k, cache_mem_v, g_norm, w_in, w_conv, b_conv, dt_bias, a_log, d_skip, g_ssd_out, b_forget,
           g_mem, w_mem_kv, w_o_ssd, w_o_fox, w_o_mem, w_out, g_final):
    depth = w_in.shape[0]
    xp, xs = x_prompt, x_sample
    bp = xp.shape[0]
    n_mem = mem_prompt.shape[1]
    outs = [[] for _ in range(12)]
    for l in range(depth):
        wkv = _cast_bf16(w_mem_kv[l])
        wd = {
            "g_norm": g_norm[l],
            "w_in_t": _cast_bf16(jnp.transpose(w_in[l])),
            "w_conv": w_conv[l], "b_conv": b_conv[l], "dt_bias": dt_bias[l], "a_log": a_log[l], "d_skip": d_skip[l],
            "g_ssd_out": g_ssd_out[l], "b_forget": b_forget[l],
            "w_o_ssd": _cast_bf16(w_o_ssd[l]), "w_o_fox": _cast_bf16(w_o_fox[l]),
            "w_o_mem": _cast_bf16(w_o_mem[l]), "w_out": _cast_bf16(w_out[l]),
        }
        hm = _rmsnorm(mem_prompt.reshape(bp * n_mem, D_MODEL), g_mem[l], _row_tile(bp * n_mem, 256))
        tmm = _row_tile(bp * n_mem, 1024)
        mk = _mm(hm, wkv, F32, tmm, 1024, "mem_k_proj", 0, D_MODEL).reshape(bp, n_mem, D_MODEL)
        mv = _mm(hm, wkv, F32, tmm, 1024, "mem_v_proj", D_MODEL, D_MODEL).reshape(bp, n_mem, D_MODEL)

        xp, c_p, h_p, k_p, v_p, lf_p = _layer(
            xp, jnp.zeros((bp, SSD_CONV - 1, SSD_CONV_CH), F32),
            jnp.zeros((bp, SSD_HEADS, SSD_HEAD_DIM, SSD_D_STATE), F32), None, None,
            mk, mv, wd, g_final, l == depth - 1)
        bs = xs.shape[0]
        xs, c_s, h_s, k_s, v_s, lf_s = _layer(
            xs, state_ssd_conv[l], state_ssd[l], (cache_fox_k[l], cache_fox_v[l]), cache_fox_logf[l],
            cache_mem_k[l].reshape(bs, n_mem, D_MODEL), cache_mem_v[l].reshape(bs, n_mem, D_MODEL),
            wd, g_final, l == depth - 1)
        for lst, val in zip(outs, (k_p, v_p, lf_p, h_p, c_p,
                                   mk.reshape(bp, n_mem, MEM_HEADS, MEM_HEAD_DIM),
                                   mv.reshape(bp, n_mem, MEM_HEADS, MEM_HEAD_DIM),
                                   k_s, v_s, lf_s, h_s, c_s)):
            lst.append(val)
    return (xp, xs) + tuple(jnp.stack(o) for o in outs)
```

```python
import functools

import numpy as np
import jax
import jax.numpy as jnp
from jax import lax
from jax.experimental import pallas as pl
from jax.experimental.pallas import tpu as pltpu

F32 = jnp.float32
BF16 = jnp.bfloat16

EPS = 1e-6
D_MODEL = 2048
SSD_HEAD_DIM = 64
SSD_HEADS = 32
SSD_GROUPS = 4
SSD_D_STATE = 128
SSD_CONV = 4
SSD_BC = SSD_GROUPS * SSD_D_STATE
SSD_CONV_CH = D_MODEL + 2 * SSD_BC
FOX_HEADS = 16
FOX_HEAD_DIM = 128
MEM_HEADS = 4
MEM_HEAD_DIM = 512
SSD_CHUNK = 64
SSD_NB = 2
NEG_BIG = -1e30

VMEM_LIMIT = 56 * 1024 * 1024

W_IN_SEGMENTS = ((0, D_MODEL + SSD_CONV_CH),
                 (D_MODEL + SSD_CONV_CH + SSD_HEADS, 4 * D_MODEL),
                 (D_MODEL + SSD_CONV_CH + SSD_HEADS + 4 * D_MODEL + FOX_HEADS, 5 * D_MODEL))
W_IN_DT = D_MODEL + SSD_CONV_CH
W_IN_FF = W_IN_SEGMENTS[1][0] + 4 * D_MODEL
U1_Z, U1_XS = 0, 1
U2_Q, U2_K, U2_V, U2_G = 0, 1, 2, 3
U3_Q, U3_G, U3_GS, U3_GF, U3_GM = 0, 1, 2, 3, 4
LOG2E = 1.4426950408889634


def _params(*sem):
    return pltpu.CompilerParams(dimension_semantics=sem, vmem_limit_bytes=VMEM_LIMIT)


def _split3(x):
    hi = x.astype(BF16)
    r1 = x - hi.astype(F32)
    mid = r1.astype(BF16)
    lo = (r1 - mid.astype(F32)).astype(BF16)
    return hi, mid, lo


def _softplus(x):
    return jnp.maximum(x, 0.0) + jnp.log1p(jnp.exp(-jnp.abs(x)))


def _silu(x):
    return x * jax.nn.sigmoid(x)


def _rmsnorm_kernel(x_ref, g_ref, o_ref):
    x = x_ref[...]
    ms = jnp.mean(x * x, axis=-1, keepdims=True)
    o_ref[...] = (x * lax.rsqrt(ms + EPS) * g_ref[...]).astype(o_ref.dtype)


def _rmsnorm(x, g, tm):
    m, d = x.shape
    return pl.pallas_call(
        _rmsnorm_kernel,
        out_shape=jax.ShapeDtypeStruct((m, d), BF16),
        grid=(m // tm,),
        in_specs=[pl.BlockSpec((tm, d), lambda i: (i, 0)), pl.BlockSpec((1, d), lambda i: (0, 0))],
        out_specs=pl.BlockSpec((tm, d), lambda i: (i, 0)),
        compiler_params=_params("parallel"),
        name="rmsnorm",
    )(x, g.reshape(1, d))


def _mm_kernel(h_ref, w_ref, o_ref):
    o_ref[...] = jnp.dot(h_ref[...], w_ref[...], preferred_element_type=F32).astype(o_ref.dtype)


def _mm(h, w, out_dtype, tm, tn, name, w_col0=0, n=None):
    m, k = h.shape
    n = w.shape[1] if n is None else n
    j0 = w_col0 // tn
    return pl.pallas_call(
        _mm_kernel,
        out_shape=jax.ShapeDtypeStruct((m, n), out_dtype),
        grid=(m // tm, n // tn),
        in_specs=[pl.BlockSpec((tm, k), lambda i, j: (i, 0)), pl.BlockSpec((k, tn), lambda i, j: (0, j0 + j))],
        out_specs=pl.BlockSpec((tm, tn), lambda i, j: (i, j)),
        compiler_params=_params("parallel", "arbitrary"),
        name=name,
    )(h, w)


def _mm_nt_kernel(h_ref, wt_ref, o_ref):
    o_ref[...] = lax.dot_general(h_ref[...], wt_ref[...], NT_DIMS, preferred_element_type=F32).astype(o_ref.dtype)


def _wt_rows_spec(tn, k, row0, grid_rank):
    g = 16
    assert row0 % g == 0 and tn % g == 0
    if grid_rank == 1:
        return pl.BlockSpec((pl.Element(tn), pl.Element(k)), lambda i: (row0, 0))
    return pl.BlockSpec((pl.Element(tn), pl.Element(k)), lambda i, j: ((row0 // g + j * (tn // g)) * g, 0))


def _mm_nt(h, wt, row0, n, out_dtype, tm, tn, name):
    m, k = h.shape
    return pl.pallas_call(
        _mm_nt_kernel,
        out_shape=jax.ShapeDtypeStruct((m, n), out_dtype),
        grid=(m // tm, n // tn),
        in_specs=[pl.BlockSpec((tm, k), lambda i, j: (i, 0)), _wt_rows_spec(tn, k, row0, 2)],
        out_specs=pl.BlockSpec((tm, tn), lambda i, j: (i, j)),
        compiler_params=_params("parallel", "arbitrary"),
        name=name,
    )(h, wt)


def _mm_ssd_kernel(h_ref, wt_ref, wdt_ref, wff_ref, o_ref, small_ref):
    o_ref[...] = lax.dot_general(h_ref[...], wt_ref[...], NT_DIMS, preferred_element_type=F32).astype(o_ref.dtype)

    @pl.when(pl.program_id(1) == 0)
    def _narrow():
        w = jnp.concatenate([wdt_ref[...], wff_ref[...]], axis=0)
        small_ref[...] = lax.dot_general(h_ref[...], w, NT_DIMS, preferred_element_type=F32)


def _mm_ssd(h, wt, tm, tn):
    m, k = h.shape
    row0, n = W_IN_SEGMENTS[0]
    fixed = lambda r0: pl.BlockSpec((pl.Element(128), pl.Element(k)), lambda i, j: (r0, 0))
    return pl.pallas_call(
        _mm_ssd_kernel,
        out_shape=(jax.ShapeDtypeStruct((m, n), BF16), jax.ShapeDtypeStruct((m, 256), F32)),
        grid=(m // tm, n // tn),
        in_specs=[pl.BlockSpec((tm, k), lambda i, j: (i, 0)), _wt_rows_spec(tn, k, row0, 2),
                  fixed(W_IN_DT), fixed(W_IN_FF)],
        out_specs=(pl.BlockSpec((tm, tn), lambda i, j: (i, j)), pl.BlockSpec((tm, 256), lambda i, j: (i, 0))),
        compiler_params=_params("parallel", "arbitrary"),
        name="in_proj_ssd",
    )(h, wt, wt, wt)


def _mm_qkvg_kernel(h_ref, wt_ref, o_ref, k_ref, v_ref, *, nj):
    j = pl.program_id(1)
    acc = lax.dot_general(h_ref[...], wt_ref[...], NT_DIMS, preferred_element_type=F32)
    o_ref[...] = acc.astype(o_ref.dtype)

    @pl.when((j >= nj) & (j < 2 * nj))
    def _k():
        k_ref[...] = acc

    @pl.when((j >= 2 * nj) & (j < 3 * nj))
    def _v():
        v_ref[...] = acc


def _mm_qkvg(h, wt, row0, tm, tn):
    m, k = h.shape
    nj = D_MODEL // tn
    return pl.pallas_call(
        functools.partial(_mm_qkvg_kernel, nj=nj),
        out_shape=(jax.ShapeDtypeStruct((m, 4 * D_MODEL), BF16),
                   jax.ShapeDtypeStruct((m, D_MODEL), F32), jax.ShapeDtypeStruct((m, D_MODEL), F32)),
        grid=(m // tm, 4 * nj),
        in_specs=[pl.BlockSpec((tm, k), lambda i, j: (i, 0)), _wt_rows_spec(tn, k, row0, 2)],
        out_specs=(pl.BlockSpec((tm, tn), lambda i, j: (i, j)),
                   pl.BlockSpec((tm, tn), lambda i, j: (i, jnp.clip(j - nj, 0, nj - 1))),
                   pl.BlockSpec((tm, tn), lambda i, j: (i, jnp.clip(j - 2 * nj, 0, nj - 1)))),
        compiler_params=_params("parallel", "arbitrary"),
        name="in_proj_qkvg",
    )(h, wt)


def _cast_kernel(a_ref, o_ref):
    o_ref[...] = a_ref[...].astype(o_ref.dtype)


def _cast_bf16(w):
    rows, cols = w.shape
    tr = max(d for d in range(16, 1025, 16) if rows % d == 0)
    return pl.pallas_call(
        _cast_kernel,
        out_shape=jax.ShapeDtypeStruct((rows, cols), BF16),
        grid=(rows // tr,),
        in_specs=[pl.BlockSpec((tr, cols), lambda i: (i, 0))],
        out_specs=pl.BlockSpec((tr, cols), lambda i: (i, 0)),
        compiler_params=_params("parallel"),
        name="weight_cast",
    )(w)


def _upper3(tb):
    k = lax.broadcasted_iota(jnp.int32, (3 * tb, tb), 0) % tb
    t = lax.broadcasted_iota(jnp.int32, (3 * tb, tb), 1)
    return jnp.where(k <= t, 1.0, 0.0).astype(BF16)


def _cumsum_lanes(x, carry, tb):
    n = x.shape[1]
    u3 = _upper3(tb)
    out = []
    for s in range(0, n, tb):
        hi, mid, lo = _split3(x[:, s:s + tb])
        c = jnp.dot(jnp.concatenate([hi, mid, lo], axis=1), u3, preferred_element_type=F32) + carry
        carry = c[:, tb - 1:tb]
        out.append(c)
    return out, carry


def _logf_kernel(*refs, nb, t, p):
    if p:
        ffc_ref, bias_ref, past_ref, lf_ref, c_ref = refs
    else:
        ffc_ref, bias_ref, lf_ref, c_ref = refs
    fft = ffc_ref[...].T[0:FOX_HEADS, :]
    lf_all = -_softplus(-(fft + bias_ref[...]))
    for b in range(nb):
        lf = lf_all[:, b * t:(b + 1) * t]
        lf_ref[b] = lf
        carry = jnp.zeros((FOX_HEADS, 1), F32)
        if p:
            tbp = min(256, p)
            blocks, carry = _cumsum_lanes(past_ref[b], carry, tbp)
            for i, c in enumerate(blocks):
                c_ref[b, :, i * tbp:(i + 1) * tbp] = c
        tb = min(256, t)
        blocks, carry = _cumsum_lanes(lf, carry, tb)
        for i, c in enumerate(blocks):
            c_ref[b, :, p + i * tb:p + (i + 1) * tb] = c


def _logf(small, b_forget, past_t, nbatch, t):
    p = 0 if past_t is None else past_t.shape[2]
    nb = 1 if t % 128 == 0 else nbatch
    grid = (nbatch // nb,)
    in_specs = [pl.BlockSpec((nb * t, 128), lambda i: (i, 1)),
                pl.BlockSpec((FOX_HEADS, 1), lambda i: (0, 0))]
    args = [small, b_forget.reshape(FOX_HEADS, 1)]
    if p:
        in_specs.append(pl.BlockSpec((nb, FOX_HEADS, p), lambda i: (i, 0, 0)))
        args.append(past_t)
    return pl.pallas_call(
        functools.partial(_logf_kernel, nb=nb, t=t, p=p),
        out_shape=(jax.ShapeDtypeStruct((nbatch, FOX_HEADS, t), F32),
                   jax.ShapeDtypeStruct((nbatch, FOX_HEADS, p + t), F32)),
        grid=grid,
        in_specs=in_specs,
        out_specs=(pl.BlockSpec((nb, FOX_HEADS, t), lambda i: (i, 0, 0)),
                   pl.BlockSpec((nb, FOX_HEADS, p + t), lambda i: (i, 0, 0))),
        compiler_params=_params("parallel"),
        name="logf_cumsum",
    )(*args)


def _expand_matrix():
    r = np.arange(128)[:, None]
    c = np.arange(D_MODEL)[None, :]
    return jnp.asarray(((r < 96) & ((r % SSD_HEADS) == (c // SSD_HEAD_DIM))).astype(np.float32), dtype=BF16)


def _pack3(x):
    lane = lax.broadcasted_iota(jnp.int32, x.shape, 1)
    x = jnp.where(lane < SSD_HEADS, x, 0.0)
    hi, mid, lo = _split3(x)
    packed = hi.astype(F32) + pltpu.roll(mid.astype(F32), SSD_HEADS, 1) + pltpu.roll(lo.astype(F32), 2 * SSD_HEADS, 1)
    return packed.astype(BF16)


HIST = 16


def _shift_matrix(L):
    sh = np.zeros((SSD_CONV * L, 2 * HIST + L), np.float32)
    for d in range(SSD_CONV):
        for t in range(L):
            sh[d * L + t, 2 * HIST + t - d] = 1.0
            if t - d < 0:
                sh[d * L + t, HIST + t - d] = 1.0
    return jnp.asarray(sh, dtype=BF16)


def _ssd_kernel(xs_ref, b_ref, c_ref, z_ref, sm_ref, conv0_ref, h0_ref,
                wconv_ref, bconv_ref, dtb_ref, alog_ref, dskip_ref, gout_ref, e3_ref, sh_ref,
                y_ref, hfin_ref, convout_ref,
                st_ref, xq_ref):
    ci = pl.program_id(1)
    nb, L = xs_ref.shape[0], xs_ref.shape[1]
    P = SSD_HEAD_DIM
    W = D_MODEL
    gw = W // SSD_GROUPS
    hpm = 256 // L

    @pl.when(ci == 0)
    def _init():
        for bi in range(nb):
            st_ref[bi] = h0_ref[bi].T
            hist = jnp.concatenate([jnp.zeros((HIST - 8, SSD_CONV_CH), F32), conv0_ref[bi]], axis=0)
            hist_hi = hist.astype(BF16)
            xq_ref[bi, 0:HIST, :] = hist_hi
            xq_ref[bi, HIST:2 * HIST, :] = (hist - hist_hi.astype(F32)).astype(BF16)

    lane = lax.broadcasted_iota(jnp.int32, (L, 128), 1)
    kk = lax.broadcasted_iota(jnp.int32, (L, 3 * L), 1) % L
    ll = lax.broadcasted_iota(jnp.int32, (L, 3 * L), 0)
    tri3 = jnp.where(kk <= ll, 1.0, 0.0).astype(BF16)
    row = lax.broadcasted_iota(jnp.int32, (L, W), 0)
    sidx = lax.broadcasted_iota(jnp.int32, (L, W), 1) % L
    rb = lax.broadcasted_iota(jnp.int32, (256, hpm * P), 0) // L
    cb = lax.broadcasted_iota(jnp.int32, (256, hpm * P), 1) // P
    blockmask = rb == cb
    a_small = -jnp.exp(alog_ref[...])
    e3 = e3_ref[...]

    def chunk(bi):
        xq_ref[bi, 2 * HIST:, 0:W] = xs_ref[bi]
        xq_ref[bi, 2 * HIST:, W:W + SSD_BC] = b_ref[bi]
        xq_ref[bi, 2 * HIST:, W + SSD_BC:] = c_ref[bi]

        def conv(lo, hi):
            sh = jnp.dot(sh_ref[...], xq_ref[bi, :, lo:hi], preferred_element_type=F32)
            acc = bconv_ref[:, lo:hi]
            for d in range(SSD_CONV):
                acc = acc + sh[d * L:(d + 1) * L] * wconv_ref[SSD_CONV - 1 - d:SSD_CONV - d, lo:hi]
            return _silu(acc)

        xs = conv(0, W)
        bm = conv(W, W + SSD_BC).astype(BF16)
        cm = conv(W + SSD_BC, W + 2 * SSD_BC).astype(BF16)
        xq_ref[bi, 0:HIST, :] = jnp.zeros((HIST, SSD_CONV_CH), BF16)
        xq_ref[bi, HIST:2 * HIST, :] = xq_ref[bi, HIST + L:2 * HIST + L, :]

        dt = _softplus(sm_ref[bi] + dtb_ref[...])
        dta = jnp.where(lane < SSD_HEADS, dt * a_small, 0.0)
        hi, mid, lo = _split3(dta)
        acum = jnp.dot(tri3, jnp.concatenate([hi, mid, lo], axis=0), preferred_element_type=F32)
        a_x = jnp.dot(_pack3(acum), e3, preferred_element_type=F32)
        dt_x = jnp.dot(_pack3(dt), e3, preferred_element_type=F32)

        a_row = jnp.sum(jnp.where(row == sidx, a_x, 0.0), axis=0, keepdims=True)
        ldec = jnp.exp(jnp.where(row >= sidx, a_x - a_row, NEG_BIG))
        cbx = []
        for g in range(SSD_GROUPS):
            cg = cm[:, g * SSD_D_STATE:(g + 1) * SSD_D_STATE]
            bg = bm[:, g * SSD_D_STATE:(g + 1) * SSD_D_STATE]
            brep = jnp.concatenate([bg] * (SSD_HEADS // SSD_GROUPS), axis=0)
            cbx.append(lax.dot_general(cg, brep, NT_DIMS, preferred_element_type=F32))
        m = (jnp.concatenate(cbx, axis=1) * ldec).astype(BF16)

        xdt = xs * dt_x
        xdt_b = xdt.astype(BF16)
        y_parts = []
        for j in range(SSD_HEADS // hpm):
            xj = xdt_b[:, j * hpm * P:(j + 1) * hpm * P]
            bd = jnp.where(blockmask, jnp.concatenate([xj] * hpm, axis=0), jnp.zeros((), BF16))
            y_parts.append(jnp.dot(m[:, j * 256:(j + 1) * 256], bd, preferred_element_type=F32))
        y = jnp.concatenate(y_parts, axis=1)

        st = st_ref[bi]
        st_b = st.astype(BF16)
        yo = [jnp.dot(cm[:, g * SSD_D_STATE:(g + 1) * SSD_D_STATE], st_b[:, g * gw:(g + 1) * gw],
                      preferred_element_type=F32) for g in range(SSD_GROUPS)]
        y = y + jnp.concatenate(yo, axis=1) * jnp.exp(a_x)

        a_last = a_x[L - 1:L, :]
        xw = (xdt * jnp.exp(a_last - a_x)).astype(BF16)
        upd = [lax.dot_general(bm[:, g * SSD_D_STATE:(g + 1) * SSD_D_STATE], xw[:, g * gw:(g + 1) * gw],
                               (((0,), (0,)), ((), ())), preferred_element_type=F32) for g in range(SSD_GROUPS)]
        st_ref[bi] = st * jnp.exp(a_last) + jnp.concatenate(upd, axis=1)

        y = y + dskip_ref[...] * xs
        gz = y * _silu(z_ref[bi].astype(F32))
        ms = jnp.mean(gz * gz, axis=-1, keepdims=True)
        y_ref[bi] = (gz * lax.rsqrt(ms + EPS) * gout_ref[...]).astype(y_ref.dtype)

    for bi in range(nb):
        chunk(bi)

    @pl.when(ci == pl.num_programs(1) - 1)
    def _final():
        for bi in range(nb):
            hfin_ref[bi] = st_ref[bi].T
            convout_ref[bi] = xq_ref[bi, HIST:2 * HIST, :].astype(F32)[HIST - 8:, :]


def _ssd(u, small, conv0, h0, w_conv, b_conv, dt_bias, a_log, d_skip, g_ssd_out, nbatch, t):
    L = SSD_CHUNK
    nc = t // L
    nb = SSD_NB if nbatch % SSD_NB == 0 else 1
    pad128 = lambda v: jnp.pad(v.astype(F32), (0, 128 - v.shape[0])).reshape(1, 128)
    rep = lambda v: jnp.repeat(v.astype(F32), SSD_HEAD_DIM).reshape(1, D_MODEL)
    const2 = lambda shape: pl.BlockSpec(shape, lambda b, c: (0, 0))
    seq = lambda width, col: pl.BlockSpec((nb, L, width), lambda b, c: (b, c, col))
    per_seq = lambda rows, width: pl.BlockSpec((nb, rows, width), lambda b, c: (b, 0, 0))
    u3d = u.reshape(nbatch, t, u.shape[1])
    bc0 = (U1_XS + 1) * D_MODEL // SSD_BC
    y, hfin, convout = pl.pallas_call(
        _ssd_kernel,
        out_shape=(jax.ShapeDtypeStruct((nbatch, t, D_MODEL), BF16),
                   jax.ShapeDtypeStruct((nbatch, D_MODEL, SSD_D_STATE), F32),
                   jax.ShapeDtypeStruct((nbatch, 8, SSD_CONV_CH), F32)),
        grid=(nbatch // nb, nc),
        in_specs=[
            seq(D_MODEL, U1_XS), seq(SSD_BC, bc0), seq(SSD_BC, bc0 + 1), seq(D_MODEL, U1_Z), seq(128, 0),
            per_seq(8, SSD_CONV_CH), per_seq(D_MODEL, SSD_D_STATE),
            const2((SSD_CONV, SSD_CONV_CH)), const2((1, SSD_CONV_CH)),
            const2((1, 128)), const2((1, 128)), const2((1, D_MODEL)), const2((1, D_MODEL)),
            const2((128, D_MODEL)), const2((SSD_CONV * L, 2 * HIST + L)),
        ],
        out_specs=(seq(D_MODEL, 0), per_seq(D_MODEL, SSD_D_STATE), per_seq(8, SSD_CONV_CH)),
        scratch_shapes=[pltpu.VMEM((nb, SSD_D_STATE, D_MODEL), F32),
                        pltpu.VMEM((nb, 2 * HIST + L, SSD_CONV_CH), BF16)],
        compiler_params=_params("parallel", "arbitrary"),
        name="ssd_scan",
    )(u3d, u3d, u3d, u3d, small.reshape(nbatch, t, small.shape[1]), conv0, h0.reshape(nbatch, D_MODEL, SSD_D_STATE),
      w_conv, b_conv.reshape(1, SSD_CONV_CH), pad128(dt_bias), pad128(a_log), rep(d_skip),
      g_ssd_out.reshape(1, D_MODEL), _expand_matrix(), _shift_matrix(L))
    return (y.reshape(nbatch * t, D_MODEL), hfin.reshape(nbatch, SSD_HEADS, SSD_HEAD_DIM, SSD_D_STATE),
            convout[:, 8 - (SSD_CONV - 1):])


def _col_from_row(row_vals):
    n = row_vals.shape[1]
    r = lax.broadcasted_iota(jnp.int32, (n, n), 0)
    c = lax.broadcasted_iota(jnp.int32, (n, n), 1)
    return jnp.sum(jnp.where(r == c, row_vals, 0.0), axis=1, keepdims=True)


FOX_PAIR = 4
NT_DIMS = (((1,), (1,)), ((), ()))
FOX_TQ = 512


def _fox_prompt_kernel(q_ref, k_ref, v_ref, fg_ref, c_ref, o_ref, va_ref, *, tq):
    qi = pl.program_id(2)
    hd = FOX_HEAD_DIM
    t = k_ref.shape[0]
    c1 = hd ** -0.5 * LOG2E
    q0 = pl.multiple_of(qi * tq, tq)
    lanes = [slice(j * hd, (j + 1) * hd) for j in range(FOX_PAIR)]

    @pl.when(qi == 0)
    def _stage_values():
        for j in range(FOX_PAIR):
            va_ref[j, :, 0:hd] = v_ref[:, lanes[j]]
            va_ref[j, :, hd:2 * hd] = jnp.ones((t, hd), BF16)

    cq2_rep = [jnp.broadcast_to(c_ref[j, :, pl.ds(q0, tq)], (hd, tq)).T * LOG2E for j in range(FOX_PAIR)]
    cq2 = [cr[:, 0:1] for cr in cq2_rep]
    row = lax.broadcasted_iota(jnp.int32, (tq, tq), 0)
    col = lax.broadcasted_iota(jnp.int32, (tq, tq), 1)

    def scores(j, k0, masked):
        s = lax.dot_general(q_ref[:, lanes[j]], k_ref[pl.ds(k0, tq), lanes[j]], NT_DIMS, preferred_element_type=F32)
        t2 = s * c1 - c_ref[j, :, pl.ds(k0, tq)] * LOG2E
        return jnp.where(col <= row, t2, NEG_BIG) if masked else t2

    shift = []
    ones = jnp.ones((hd, hd), BF16)
    for j in range(FOX_PAIR):
        qk = q_ref[:, lanes[j]] * k_ref[pl.ds(q0, tq), lanes[j]]
        shift.append(jnp.dot(qk, ones, preferred_element_type=F32) * c1 - cq2_rep[j])

    def fast_tile(k0, accs, masked):
        out = []
        for j in range(FOX_PAIR):
            t2 = scores(j, k0, masked)
            e = jnp.concatenate([t2[:, i:i + hd] - shift[j] for i in range(0, tq, hd)], axis=1)
            out.append(accs[j] + jnp.dot(jnp.exp2(e).astype(BF16), va_ref[j, pl.ds(k0, tq), :],
                                         preferred_element_type=F32))
        return tuple(out)

    accs = tuple(jnp.zeros((tq, 2 * hd), F32) for _ in range(FOX_PAIR))
    accs = lax.fori_loop(0, qi, lambda i, a: fast_tile(pl.multiple_of(i * tq, tq), a, False), accs)
    accs = fast_tile(q0, accs, True)
    bad = jnp.float32(0.0)
    for j in range(FOX_PAIR):
        num, den = accs[j][:, 0:hd], accs[j][:, hd:2 * hd]
        o_ref[:, lanes[j]] = (num / den * _silu(fg_ref[:, lanes[j]].astype(F32))).astype(o_ref.dtype)
        bad = jnp.maximum(bad, jnp.max(jnp.where(jnp.isfinite(accs[j]), 0.0, 1.0)))

    @pl.when(bad > 0.0)
    def _running_max():
        def tile(k0, carry, masked):
            out = []
            for j in range(FOX_PAIR):
                m_i, l_i, acc = carry[j]
                t2 = scores(j, k0, masked)
                m_new = jnp.maximum(m_i, jnp.max(t2, axis=1, keepdims=True) + cq2[j])
                p = jnp.exp2(t2 - (m_new - cq2[j]))
                alpha = jnp.exp2(m_i - m_new)
                l_new = alpha * l_i + jnp.sum(p, axis=1, keepdims=True)
                acc = alpha * acc + jnp.dot(p.astype(BF16), v_ref[pl.ds(k0, tq), lanes[j]],
                                            preferred_element_type=F32)
                out.append((m_new, l_new, acc))
            return tuple(out)

        init = tuple((jnp.full((tq, 1), NEG_BIG, F32), jnp.zeros((tq, 1), F32), jnp.zeros((tq, hd), F32))
                     for _ in range(FOX_PAIR))
        carry = lax.fori_loop(0, qi, lambda i, cr: tile(pl.multiple_of(i * tq, tq), cr, False), init)
        carry = tile(q0, carry, True)
        for j in range(FOX_PAIR):
            _, l_i, acc = carry[j]
            o_ref[:, lanes[j]] = (acc / l_i * _silu(fg_ref[:, lanes[j]].astype(F32))).astype(o_ref.dtype)


def _fox_prompt(u2, ct, nbatch, t):
    tq = min(FOX_TQ, t)
    nq = t // tq
    w = FOX_PAIR * FOX_HEAD_DIM
    nb = D_MODEL // w
    return pl.pallas_call(
        functools.partial(_fox_prompt_kernel, tq=tq),
        out_shape=jax.ShapeDtypeStruct((nbatch * t, D_MODEL), BF16),
        grid=(nbatch, nb, nq),
        in_specs=[
            pl.BlockSpec((tq, w), lambda b, h, i: (b * nq + i, U2_Q * nb + h)),
            pl.BlockSpec((t, w), lambda b, h, i: (b, U2_K * nb + h)),
            pl.BlockSpec((t, w), lambda b, h, i: (b, U2_V * nb + h)),
            pl.BlockSpec((tq, w), lambda b, h, i: (b * nq + i, U2_G * nb + h)),
            pl.BlockSpec((None, FOX_PAIR, 1, t), lambda b, h, i: (b, h, 0, 0)),
        ],
        out_specs=pl.BlockSpec((tq, w), lambda b, h, i: (b * nq + i, h)),
        scratch_shapes=[pltpu.VMEM((FOX_PAIR, t, 2 * FOX_HEAD_DIM), BF16)],
        compiler_params=_params("parallel", "parallel", "arbitrary"),
        name="fox_prompt",
    )(u2, u2, u2, u2, ct)


def _fox_sample_kernel(q_ref, kp_ref, vp_ref, kn_ref, vn_ref, fg_ref, c_ref, o_ref, *, p, t):
    c1 = FOX_HEAD_DIM ** -0.5 * LOG2E
    r = lax.broadcasted_iota(jnp.int32, (t, t), 0)
    c = lax.broadcasted_iota(jnp.int32, (t, t), 1)
    for j in range(FOX_PAIR):
        sl = slice(j * FOX_HEAD_DIM, (j + 1) * FOX_HEAD_DIM)
        q = q_ref[:, sl]
        cq2 = _col_from_row(c_ref[j, :, p:p + t]) * LOG2E
        s_p = (lax.dot_general(q, kp_ref[:, sl].astype(BF16), NT_DIMS, preferred_element_type=F32) * c1
               - c_ref[j, :, 0:p] * LOG2E)
        s_n = (lax.dot_general(q, kn_ref[:, sl], NT_DIMS, preferred_element_type=F32) * c1
               - c_ref[j, :, p:p + t] * LOG2E)
        s_n = jnp.where(c <= r, s_n, NEG_BIG)
        m = jnp.maximum(jnp.max(s_p, axis=1, keepdims=True), jnp.max(s_n, axis=1, keepdims=True)) + cq2
        e_p = jnp.exp2(s_p - (m - cq2))
        e_n = jnp.exp2(s_n - (m - cq2))
        inv = 1.0 / (jnp.sum(e_p, axis=1, keepdims=True) + jnp.sum(e_n, axis=1, keepdims=True))
        o = (jnp.dot((e_p * inv).astype(BF16), vp_ref[:, sl].astype(BF16), preferred_element_type=F32)
             + jnp.dot((e_n * inv).astype(BF16), vn_ref[:, sl], preferred_element_type=F32))
        o_ref[:, sl] = (o * _silu(fg_ref[:, sl].astype(F32))).astype(o_ref.dtype)


def _fox_sample(u2, k_past, v_past, ct, nbatch, t):
    p = k_past.shape[1]
    w = FOX_PAIR * FOX_HEAD_DIM
    nb = D_MODEL // w
    past = pl.BlockSpec((None, p, w), lambda b, h: (b, 0, h))
    col = lambda seg: pl.BlockSpec((t, w), lambda b, h: (b, seg * nb + h))
    return pl.pallas_call(
        functools.partial(_fox_sample_kernel, p=p, t=t),
        out_shape=jax.ShapeDtypeStruct((nbatch * t, D_MODEL), BF16),
        grid=(nbatch, nb),
        in_specs=[col(U2_Q), past, past, col(U2_K), col(U2_V), col(U2_G),
                  pl.BlockSpec((None, FOX_PAIR, 1, p + t), lambda b, h: (b, h, 0, 0))],
        out_specs=pl.BlockSpec((t, w), lambda b, h: (b, h)),
        compiler_params=_params("parallel", "parallel"),
        name="fox_sample",
    )(u2, k_past, v_past, u2, u2, u2, ct)


def _mem_kernel(q_ref, g_ref, k_ref, v_ref, o_ref):
    scale = MEM_HEAD_DIM ** -0.5
    for h in range(MEM_HEADS):
        sl = slice(h * MEM_HEAD_DIM, (h + 1) * MEM_HEAD_DIM)
        s = lax.dot_general(q_ref[:, sl], k_ref[:, sl].astype(BF16), NT_DIMS, preferred_element_type=F32) * scale
        e = jnp.exp(s - jnp.max(s, axis=1, keepdims=True))
        p = e * (1.0 / jnp.sum(e, axis=1, keepdims=True))
        o = jnp.dot(p.astype(BF16), v_ref[:, sl].astype(BF16), preferred_element_type=F32)
        o_ref[:, sl] = (o * _silu(g_ref[:, sl].astype(F32))).astype(o_ref.dtype)


def _mem_attend(u3, mk, mv, nbatch, t):
    tq = min(512, t)
    nq = t // tq
    nm = mk.shape[1]
    kv = pl.BlockSpec((None, nm, D_MODEL), lambda b, i: (b, 0, 0))
    return pl.pallas_call(
        _mem_kernel,
        out_shape=jax.ShapeDtypeStruct((nbatch * t, D_MODEL), BF16),
        grid=(nbatch, nq),
        in_specs=[pl.BlockSpec((tq, D_MODEL), lambda b, i: (b * nq + i, U3_Q)),
                  pl.BlockSpec((tq, D_MODEL), lambda b, i: (b * nq + i, U3_G)), kv, kv],
        out_specs=pl.BlockSpec((tq, D_MODEL), lambda b, i: (b * nq + i, 0)),
        compiler_params=_params("parallel", "arbitrary"),
        name="mem_attend",
    )(u3, u3, mk, mv)


def _merge_kernel(ys_ref, yf_ref, ym_ref, ws_ref, wf_ref, wm_ref, gs_ref, gf_ref, gm_ref, o_ref):
    def branch(y_ref, w_ref, g_ref):
        return jax.nn.sigmoid(g_ref[...].astype(F32)) * jnp.dot(y_ref[...], w_ref[...], preferred_element_type=F32)
    o_ref[...] = (branch(ys_ref, ws_ref, gs_ref) + branch(yf_ref, wf_ref, gf_ref)
                  + branch(ym_ref, wm_ref, gm_ref)).astype(o_ref.dtype)


def _merge(u, y_ssd, y_fox, y_mem, w_s, w_f, w_m, tm, tn):
    m = y_ssd.shape[0]
    nj = D_MODEL // tn
    yspec = pl.BlockSpec((tm, D_MODEL), lambda i, j: (i, 0))
    wspec = pl.BlockSpec((D_MODEL, tn), lambda i, j: (0, j))
    gspec = lambda col: pl.BlockSpec((tm, tn), lambda i, j: (i, col * nj + j))
    return pl.pallas_call(
        _merge_kernel,
        out_shape=jax.ShapeDtypeStruct((m, D_MODEL), BF16),
        grid=(m // tm, nj),
        in_specs=[yspec, yspec, yspec, wspec, wspec, wspec, gspec(U3_GS), gspec(U3_GF), gspec(U3_GM)],
        out_specs=pl.BlockSpec((tm, tn), lambda i, j: (i, j)),
        compiler_params=_params("parallel", "arbitrary"),
        name="gated_merge",
    )(y_ssd, y_fox, y_mem, w_s, w_f, w_m, u, u, u)


def _final_kernel(mg_ref, w_ref, x_ref, g_ref, o_ref, *, normalize):
    xo = x_ref[...] + jnp.dot(mg_ref[...], w_ref[...], preferred_element_type=F32)
    if normalize:
        ms = jnp.mean(xo * xo, axis=-1, keepdims=True)
        xo = xo * lax.rsqrt(ms + EPS) * g_ref[...]
    o_ref[...] = xo


def _final(merged, w_out, x, g_final, tm, normalize):
    m = x.shape[0]
    return pl.pallas_call(
        functools.partial(_final_kernel, normalize=normalize),
        out_shape=jax.ShapeDtypeStruct((m, D_MODEL), F32),
        grid=(m // tm,),
        in_specs=[pl.BlockSpec((tm, D_MODEL), lambda i: (i, 0)),
                  pl.BlockSpec((D_MODEL, D_MODEL), lambda i: (0, 0)),
                  pl.BlockSpec((tm, D_MODEL), lambda i: (i, 0)),
                  pl.BlockSpec((1, D_MODEL), lambda i: (0, 0))],
        out_specs=pl.BlockSpec((tm, D_MODEL), lambda i: (i, 0)),
        compiler_params=_params("parallel"),
        name="out_proj_norm",
    )(merged, w_out, x, g_final.reshape(1, D_MODEL))


def _row_tile(m, pref):
    t = pref
    while m % t:
        t //= 2
    return t


def _layer(x, conv0, h0, fox_cache, logf_past, mem_k, mem_v, wd, g_final, last):
    nbatch, t, d = x.shape
    m = nbatch * t
    x2 = x.reshape(m, d)
    tm = _row_tile(m, 1024)

    h = _rmsnorm(x2, wd["g_norm"], _row_tile(m, 512))
    wt = wd["w_in_t"]
    _, (r2, _), (r3, n3) = W_IN_SEGMENTS
    u1, small = _mm_ssd(h, wt, tm, 1024)
    u2, k_new, v_new = _mm_qkvg(h, wt, r2, tm, 1024)
    u3 = _mm_nt(h, wt, r3, n3, BF16, _row_tile(m, 2048), 1024, "in_proj_mem_gates")

    past_t = None if logf_past is None else jnp.transpose(logf_past, (0, 2, 1))
    logf_t, ct = _logf(small, wd["b_forget"], past_t, nbatch, t)
    ct = ct.reshape(nbatch, FOX_HEADS, 1, ct.shape[-1])

    conv0p = jnp.pad(conv0, ((0, 0), (8 - (SSD_CONV - 1), 0), (0, 0)))
    y_ssd, h_final, new_conv = _ssd(u1, small, conv0p, h0, wd["w_conv"], wd["b_conv"], wd["dt_bias"], wd["a_log"],
                                    wd["d_skip"], wd["g_ssd_out"], nbatch, t)

    if fox_cache is None:
        y_fox = _fox_prompt(u2, ct, nbatch, t)
    else:
        p = fox_cache[0].shape[1]
        y_fox = _fox_sample(u2, fox_cache[0].reshape(nbatch, p, D_MODEL), fox_cache[1].reshape(nbatch, p, D_MODEL),
                            ct, nbatch, t)

    y_mem = _mem_attend(u3, mem_k, mem_v, nbatch, t)
    merged = _merge(u3, y_ssd, y_fox, y_mem, wd["w_o_ssd"], wd["w_o_fox"], wd["w_o_mem"], tm, 512)
    y = _final(merged, wd["w_out"], x2, g_final, _row_tile(m, 512), last)

    return (y.reshape(nbatch, t, d), new_conv, h_final,
            k_new.reshape(nbatch, t, FOX_HEADS, FOX_HEAD_DIM), v_new.reshape(nbatch, t, FOX_HEADS, FOX_HEAD_DIM),
            jnp.transpose(logf_t, (0, 2, 1)))


def kernel(x_prompt, x_sample, mem_prompt, cache_fox_k, cache_fox_v, cache_fox_logf, state_ssd, state_ssd_conv,
           cache_mem_k, cache_mem_v, g_norm, w_in, w_conv, b_conv, dt_bias, a_log, d_skip, g_ssd_out, b_forget,
           g_mem, w_mem_kv, w_o_ssd, w_o_fox, w_o_mem, w_out, g_final):
    depth = w_in.shape[0]
    xp, xs = x_prompt, x_sample
    bp = xp.shape[0]
    n_mem = mem_prompt.shape[1]
    outs = [[] for _ in range(12)]
    for l in range(depth):
        wkv = _cast_bf16(w_mem_kv[l])
        wd = {
            "g_norm": g_norm[l],
            "w_in_t": _cast_bf16(jnp.transpose(w_in[l])),
            "w_conv": w_conv[l], "b_conv": b_conv[l], "dt_bias": dt_bias[l], "a_log": a_log[l], "d_skip": d_skip[l],
            "g_ssd_out": g_ssd_out[l], "b_forget": b_forget[l],
            "w_o_ssd": _cast_bf16(w_o_ssd[l]), "w_o_fox": _cast_bf16(w_o_fox[l]),
            "w_o_mem": _cast_bf16(w_o_mem[l]), "w_out": _cast_bf16(w_out[l]),
        }
        hm = _rmsnorm(mem_prompt.reshape(bp * n_mem, D_MODEL), g_mem[l], _row_tile(bp * n_mem, 256))
        tmm = _row_tile(bp * n_mem, 1024)
        mk = _mm(hm, wkv, F32, tmm, 1024, "mem_k_proj", 0, D_MODEL).reshape(bp, n_mem, D_MODEL)
        mv = _mm(hm, wkv, F32, tmm, 1024, "mem_v_proj", D_MODEL, D_MODEL).reshape(bp, n_mem, D_MODEL)

        xp, c_p, h_p, k_p, v_p, lf_p = _layer(
            xp, jnp.zeros((bp, SSD_CONV - 1, SSD_CONV_CH), F32),
            jnp.zeros((bp, SSD_HEADS, SSD_HEAD_DIM, SSD_D_STATE), F32), None, None,
            mk, mv, wd, g_final, l == depth - 1)
        bs = xs.shape[0]
        xs, c_s, h_s, k_s, v_s, lf_s = _layer(
            xs, state_ssd_conv[l], state_ssd[l], (cache_fox_k[l], cache_fox_v[l]), cache_fox_logf[l],
            cache_mem_k[l].reshape(bs, n_mem, D_MODEL), cache_mem_v[l].reshape(bs, n_mem, D_MODEL),
            wd, g_final, l == depth - 1)
        for lst, val in zip(outs, (k_p, v_p, lf_p, h_p, c_p,
                                   mk.reshape(bp, n_mem, MEM_HEADS, MEM_HEAD_DIM),
                                   mv.reshape(bp, n_mem, MEM_HEADS, MEM_HEAD_DIM),
                                   k_s, v_s, lf_s, h_s, c_s)):
            lst.append(val)
    return (xp, xs) + tuple(jnp.stack(o) for o in outs)
```

```python
import functools

import numpy as np
import jax
import jax.numpy as jnp
from jax import lax
from jax.experimental import pallas as pl
from jax.experimental.pallas import tpu as pltpu

F32 = jnp.float32
BF16 = jnp.bfloat16

EPS = 1e-6
D_MODEL = 2048
SSD_HEAD_DIM = 64
SSD_HEADS = 32
SSD_GROUPS = 4
SSD_D_STATE = 128
SSD_CONV = 4
SSD_BC = SSD_GROUPS * SSD_D_STATE
SSD_CONV_CH = D_MODEL + 2 * SSD_BC
FOX_HEADS = 16
FOX_HEAD_DIM = 128
MEM_HEADS = 4
MEM_HEAD_DIM = 512
SSD_CHUNK = 64
SSD_NB = 2
NEG_BIG = -1e30

VMEM_LIMIT = 56 * 1024 * 1024

W_IN_SEGMENTS = ((0, D_MODEL + SSD_CONV_CH),
                 (D_MODEL + SSD_CONV_CH + SSD_HEADS, 4 * D_MODEL),
                 (D_MODEL + SSD_CONV_CH + SSD_HEADS + 4 * D_MODEL + FOX_HEADS, 5 * D_MODEL))
W_IN_DT = D_MODEL + SSD_CONV_CH
W_IN_FF = W_IN_SEGMENTS[1][0] + 4 * D_MODEL
U1_Z, U1_XS = 0, 1
U2_Q, U2_K, U2_V, U2_G = 0, 1, 2, 3
U3_Q, U3_G, U3_GS, U3_GF, U3_GM = 0, 1, 2, 3, 4
LOG2E = 1.4426950408889634


def _params(*sem):
    return pltpu.CompilerParams(dimension_semantics=sem, vmem_limit_bytes=VMEM_LIMIT)


def _split3(x):
    hi = x.astype(BF16)
    r1 = x - hi.astype(F32)
    mid = r1.astype(BF16)
    lo = (r1 - mid.astype(F32)).astype(BF16)
    return hi, mid, lo


def _softplus(x):
    return jnp.maximum(x, 0.0) + jnp.log1p(jnp.exp(-jnp.abs(x)))


def _silu(x):
    return x * jax.nn.sigmoid(x)


def _rmsnorm_kernel(x_ref, g_ref, o_ref):
    x = x_ref[...]
    ms = jnp.mean(x * x, axis=-1, keepdims=True)
    o_ref[...] = (x * lax.rsqrt(ms + EPS) * g_ref[...]).astype(o_ref.dtype)


def _rmsnorm(x, g, tm):
    m, d = x.shape
    return pl.pallas_call(
        _rmsnorm_kernel,
        out_shape=jax.ShapeDtypeStruct((m, d), BF16),
        grid=(m // tm,),
        in_specs=[pl.BlockSpec((tm, d), lambda i: (i, 0)), pl.BlockSpec((1, d), lambda i: (0, 0))],
        out_specs=pl.BlockSpec((tm, d), lambda i: (i, 0)),
        compiler_params=_params("parallel"),
        name="rmsnorm",
    )(x, g.reshape(1, d))


def _mm_kernel(h_ref, w_ref, o_ref):
    o_ref[...] = jnp.dot(h_ref[...], w_ref[...], preferred_element_type=F32).astype(o_ref.dtype)


def _mm(h, w, out_dtype, tm, tn, name, w_col0=0, n=None):
    m, k = h.shape
    n = w.shape[1] if n is None else n
    j0 = w_col0 // tn
    return pl.pallas_call(
        _mm_kernel,
        out_shape=jax.ShapeDtypeStruct((m, n), out_dtype),
        grid=(m // tm, n // tn),
        in_specs=[pl.BlockSpec((tm, k), lambda i, j: (i, 0)), pl.BlockSpec((k, tn), lambda i, j: (0, j0 + j))],
        out_specs=pl.BlockSpec((tm, tn), lambda i, j: (i, j)),
        compiler_params=_params("parallel", "arbitrary"),
        name=name,
    )(h, w)


def _mm_heads_kernel(h_ref, w_ref, o_ref, ob_ref):
    acc = jnp.dot(h_ref[...], w_ref[...], preferred_element_type=F32)
    o_ref[...] = acc.reshape(o_ref.shape)
    ob_ref[...] = acc.astype(ob_ref.dtype)


def _mm_heads(h, w, w_col0, heads, head_dim, tm, name):
    m, k = h.shape
    n = heads * head_dim
    return pl.pallas_call(
        _mm_heads_kernel,
        out_shape=(jax.ShapeDtypeStruct((m, heads, head_dim), F32), jax.ShapeDtypeStruct((m, n), BF16)),
        grid=(m // tm,),
        in_specs=[pl.BlockSpec((tm, k), lambda i: (i, 0)), pl.BlockSpec((k, n), lambda i: (0, w_col0 // n))],
        out_specs=(pl.BlockSpec((tm, heads, head_dim), lambda i: (i, 0, 0)), pl.BlockSpec((tm, n), lambda i: (i, 0))),
        compiler_params=_params("parallel"),
        name=name,
    )(h, w)


def _mm_nt_kernel(h_ref, wt_ref, o_ref):
    o_ref[...] = lax.dot_general(h_ref[...], wt_ref[...], NT_DIMS, preferred_element_type=F32).astype(o_ref.dtype)


def _wt_rows_spec(tn, k, row0, grid_rank):
    g = 16
    assert row0 % g == 0 and tn % g == 0
    if grid_rank == 1:
        return pl.BlockSpec((pl.Element(tn), pl.Element(k)), lambda i: (row0, 0))
    return pl.BlockSpec((pl.Element(tn), pl.Element(k)), lambda i, j: ((row0 // g + j * (tn // g)) * g, 0))


def _mm_nt(h, wt, row0, n, out_dtype, tm, tn, name):
    m, k = h.shape
    return pl.pallas_call(
        _mm_nt_kernel,
        out_shape=jax.ShapeDtypeStruct((m, n), out_dtype),
        grid=(m // tm, n // tn),
        in_specs=[pl.BlockSpec((tm, k), lambda i, j: (i, 0)), _wt_rows_spec(tn, k, row0, 2)],
        out_specs=pl.BlockSpec((tm, tn), lambda i, j: (i, j)),
        compiler_params=_params("parallel", "arbitrary"),
        name=name,
    )(h, wt)


def _mm_ssd_kernel(h_ref, wt_ref, wdt_ref, wff_ref, o_ref, small_ref):
    o_ref[...] = lax.dot_general(h_ref[...], wt_ref[...], NT_DIMS, preferred_element_type=F32).astype(o_ref.dtype)

    @pl.when(pl.program_id(1) == 0)
    def _narrow():
        w = jnp.concatenate([wdt_ref[...], wff_ref[...]], axis=0)
        small_ref[...] = lax.dot_general(h_ref[...], w, NT_DIMS, preferred_element_type=F32)


def _mm_ssd(h, wt, tm, tn):
    m, k = h.shape
    row0, n = W_IN_SEGMENTS[0]
    fixed = lambda r0: pl.BlockSpec((pl.Element(128), pl.Element(k)), lambda i, j: (r0, 0))
    return pl.pallas_call(
        _mm_ssd_kernel,
        out_shape=(jax.ShapeDtypeStruct((m, n), BF16), jax.ShapeDtypeStruct((m, 256), F32)),
        grid=(m // tm, n // tn),
        in_specs=[pl.BlockSpec((tm, k), lambda i, j: (i, 0)), _wt_rows_spec(tn, k, row0, 2),
                  fixed(W_IN_DT), fixed(W_IN_FF)],
        out_specs=(pl.BlockSpec((tm, tn), lambda i, j: (i, j)), pl.BlockSpec((tm, 256), lambda i, j: (i, 0))),
        compiler_params=_params("parallel", "arbitrary"),
        name="in_proj_ssd",
    )(h, wt, wt, wt)


def _mm_qkvg_kernel(h_ref, wt_ref, o_ref, k_ref, v_ref, *, nj):
    j = pl.program_id(1)
    acc = lax.dot_general(h_ref[...], wt_ref[...], NT_DIMS, preferred_element_type=F32)
    o_ref[...] = acc.astype(o_ref.dtype)

    @pl.when((j >= nj) & (j < 2 * nj))
    def _k():
        k_ref[...] = acc.reshape(k_ref.shape)

    @pl.when((j >= 2 * nj) & (j < 3 * nj))
    def _v():
        v_ref[...] = acc.reshape(v_ref.shape)


def _mm_qkvg(h, wt, row0, tm, tn):
    m, k = h.shape
    nj = D_MODEL // tn
    hb = tn // FOX_HEAD_DIM
    kv_shape = jax.ShapeDtypeStruct((m, FOX_HEADS, FOX_HEAD_DIM), F32)
    return pl.pallas_call(
        functools.partial(_mm_qkvg_kernel, nj=nj),
        out_shape=(jax.ShapeDtypeStruct((m, 4 * D_MODEL), BF16), kv_shape, kv_shape),
        grid=(m // tm, 4 * nj),
        in_specs=[pl.BlockSpec((tm, k), lambda i, j: (i, 0)), _wt_rows_spec(tn, k, row0, 2)],
        out_specs=(pl.BlockSpec((tm, tn), lambda i, j: (i, j)),
                   pl.BlockSpec((tm, hb, FOX_HEAD_DIM), lambda i, j: (i, jnp.clip(j - nj, 0, nj - 1), 0)),
                   pl.BlockSpec((tm, hb, FOX_HEAD_DIM), lambda i, j: (i, jnp.clip(j - 2 * nj, 0, nj - 1), 0))),
        compiler_params=_params("parallel", "arbitrary"),
        name="in_proj_qkvg",
    )(h, wt)


def _cast_kernel(a_ref, o_ref):
    o_ref[...] = a_ref[...].astype(o_ref.dtype)


def _cast_bf16(w):
    rows, cols = w.shape
    tr = max(d for d in range(16, 1025, 16) if rows % d == 0)
    return pl.pallas_call(
        _cast_kernel,
        out_shape=jax.ShapeDtypeStruct((rows, cols), BF16),
        grid=(rows // tr,),
        in_specs=[pl.BlockSpec((tr, cols), lambda i: (i, 0))],
        out_specs=pl.BlockSpec((tr, cols), lambda i: (i, 0)),
        compiler_params=_params("parallel"),
        name="weight_cast",
    )(w)


def _upper3(tb):
    k = lax.broadcasted_iota(jnp.int32, (3 * tb, tb), 0) % tb
    t = lax.broadcasted_iota(jnp.int32, (3 * tb, tb), 1)
    return jnp.where(k <= t, 1.0, 0.0).astype(BF16)


def _cumsum_lanes(x, carry, tb):
    n = x.shape[1]
    u3 = _upper3(tb)
    out = []
    for s in range(0, n, tb):
        hi, mid, lo = _split3(x[:, s:s + tb])
        c = jnp.dot(jnp.concatenate([hi, mid, lo], axis=1), u3, preferred_element_type=F32) + carry
        carry = c[:, tb - 1:tb]
        out.append(c)
    return out, carry


def _logf_kernel(*refs, nb, t, p):
    if p:
        ffc_ref, bias_ref, past_ref, lf_ref, c_ref = refs
    else:
        ffc_ref, bias_ref, lf_ref, c_ref = refs
    fft = ffc_ref[...].T[0:FOX_HEADS, :]
    lf_all = -_softplus(-(fft + bias_ref[...]))
    for b in range(nb):
        lf = lf_all[:, b * t:(b + 1) * t]
        lf_ref[b] = lf
        carry = jnp.zeros((FOX_HEADS, 1), F32)
        if p:
            tbp = min(256, p)
            blocks, carry = _cumsum_lanes(past_ref[b], carry, tbp)
            for i, c in enumerate(blocks):
                c_ref[b, :, i * tbp:(i + 1) * tbp] = c
        tb = min(256, t)
        blocks, carry = _cumsum_lanes(lf, carry, tb)
        for i, c in enumerate(blocks):
            c_ref[b, :, p + i * tb:p + (i + 1) * tb] = c


def _logf(small, b_forget, past_t, nbatch, t):
    p = 0 if past_t is None else past_t.shape[2]
    nb = 1 if t % 128 == 0 else nbatch
    grid = (nbatch // nb,)
    in_specs = [pl.BlockSpec((nb * t, 128), lambda i: (i, 1)),
                pl.BlockSpec((FOX_HEADS, 1), lambda i: (0, 0))]
    args = [small, b_forget.reshape(FOX_HEADS, 1)]
    if p:
        in_specs.append(pl.BlockSpec((nb, FOX_HEADS, p), lambda i: (i, 0, 0)))
        args.append(past_t)
    return pl.pallas_call(
        functools.partial(_logf_kernel, nb=nb, t=t, p=p),
        out_shape=(jax.ShapeDtypeStruct((nbatch, FOX_HEADS, t), F32),
                   jax.ShapeDtypeStruct((nbatch, FOX_HEADS, p + t), F32)),
        grid=grid,
        in_specs=in_specs,
        out_specs=(pl.BlockSpec((nb, FOX_HEADS, t), lambda i: (i, 0, 0)),
                   pl.BlockSpec((nb, FOX_HEADS, p + t), lambda i: (i, 0, 0))),
        compiler_params=_params("parallel"),
        name="logf_cumsum",
    )(*args)


def _expand_matrix():
    r = np.arange(128)[:, None]
    c = np.arange(D_MODEL)[None, :]
    return jnp.asarray(((r < 96) & ((r % SSD_HEADS) == (c // SSD_HEAD_DIM))).astype(np.float32), dtype=BF16)


def _pack3(x):
    lane = lax.broadcasted_iota(jnp.int32, x.shape, 1)
    x = jnp.where(lane < SSD_HEADS, x, 0.0)
    hi, mid, lo = _split3(x)
    packed = hi.astype(F32) + pltpu.roll(mid.astype(F32), SSD_HEADS, 1) + pltpu.roll(lo.astype(F32), 2 * SSD_HEADS, 1)
    return packed.astype(BF16)


HIST = 16


def _shift_matrix(L):
    sh = np.zeros((SSD_CONV * L, 2 * HIST + L), np.float32)
    for d in range(SSD_CONV):
        for t in range(L):
            sh[d * L + t, 2 * HIST + t - d] = 1.0
            if t - d < 0:
                sh[d * L + t, HIST + t - d] = 1.0
    return jnp.asarray(sh, dtype=BF16)


def _ssd_kernel(xs_ref, b_ref, c_ref, z_ref, sm_ref, conv0_ref, h0_ref,
                wconv_ref, bconv_ref, dtb_ref, alog_ref, dskip_ref, gout_ref, e3_ref, sh_ref,
                y_ref, hfin_ref, convout_ref,
                st_ref, xq_ref):
    ci = pl.program_id(1)
    nb, L = xs_ref.shape[0], xs_ref.shape[1]
    P = SSD_HEAD_DIM
    W = D_MODEL
    gw = W // SSD_GROUPS
    hpm = 256 // L

    @pl.when(ci == 0)
    def _init():
        for bi in range(nb):
            st_ref[bi] = h0_ref[bi].T
            hist = jnp.concatenate([jnp.zeros((HIST - 8, SSD_CONV_CH), F32), conv0_ref[bi]], axis=0)
            hist_hi = hist.astype(BF16)
            xq_ref[bi, 0:HIST, :] = hist_hi
            xq_ref[bi, HIST:2 * HIST, :] = (hist - hist_hi.astype(F32)).astype(BF16)

    lane = lax.broadcasted_iota(jnp.int32, (L, 128), 1)
    kk = lax.broadcasted_iota(jnp.int32, (L, 3 * L), 1) % L
    ll = lax.broadcasted_iota(jnp.int32, (L, 3 * L), 0)
    tri3 = jnp.where(kk <= ll, 1.0, 0.0).astype(BF16)
    row = lax.broadcasted_iota(jnp.int32, (L, W), 0)
    sidx = lax.broadcasted_iota(jnp.int32, (L, W), 1) % L
    rb = lax.broadcasted_iota(jnp.int32, (256, hpm * P), 0) // L
    cb = lax.broadcasted_iota(jnp.int32, (256, hpm * P), 1) // P
    blockmask = rb == cb
    a_small = -jnp.exp(alog_ref[...])
    e3 = e3_ref[...]

    def chunk(bi):
        xq_ref[bi, 2 * HIST:, 0:W] = xs_ref[bi]
        xq_ref[bi, 2 * HIST:, W:W + SSD_BC] = b_ref[bi]
        xq_ref[bi, 2 * HIST:, W + SSD_BC:] = c_ref[bi]

        def conv(lo, hi):
            sh = jnp.dot(sh_ref[...], xq_ref[bi, :, lo:hi], preferred_element_type=F32)
            acc = bconv_ref[:, lo:hi]
            for d in range(SSD_CONV):
                acc = acc + sh[d * L:(d + 1) * L] * wconv_ref[SSD_CONV - 1 - d:SSD_CONV - d, lo:hi]
            return _silu(acc)

        xs = conv(0, W)
        bm = conv(W, W + SSD_BC).astype(BF16)
        cm = conv(W + SSD_BC, W + 2 * SSD_BC).astype(BF16)
        xq_ref[bi, 0:HIST, :] = jnp.zeros((HIST, SSD_CONV_CH), BF16)
        xq_ref[bi, HIST:2 * HIST, :] = xq_ref[bi, HIST + L:2 * HIST + L, :]

        dt = _softplus(sm_ref[bi] + dtb_ref[...])
        dta = jnp.where(lane < SSD_HEADS, dt * a_small, 0.0)
        hi, mid, lo = _split3(dta)
        acum = jnp.dot(tri3, jnp.concatenate([hi, mid, lo], axis=0), preferred_element_type=F32)
        a_x = jnp.dot(_pack3(acum), e3, preferred_element_type=F32)
        dt_x = jnp.dot(_pack3(dt), e3, preferred_element_type=F32)

        a_row = jnp.sum(jnp.where(row == sidx, a_x, 0.0), axis=0, keepdims=True)
        ldec = jnp.exp(jnp.where(row >= sidx, a_x - a_row, NEG_BIG))
        cbx = []
        for g in range(SSD_GROUPS):
            cg = cm[:, g * SSD_D_STATE:(g + 1) * SSD_D_STATE]
            bg = bm[:, g * SSD_D_STATE:(g + 1) * SSD_D_STATE]
            brep = jnp.concatenate([bg] * (SSD_HEADS // SSD_GROUPS), axis=0)
            cbx.append(lax.dot_general(cg, brep, NT_DIMS, preferred_element_type=F32))
        m = (jnp.concatenate(cbx, axis=1) * ldec).astype(BF16)

        xdt = xs * dt_x
        xdt_b = xdt.astype(BF16)
        y_parts = []
        for j in range(SSD_HEADS // hpm):
            xj = xdt_b[:, j * hpm * P:(j + 1) * hpm * P]
            bd = jnp.where(blockmask, jnp.concatenate([xj] * hpm, axis=0), jnp.zeros((), BF16))
            y_parts.append(jnp.dot(m[:, j * 256:(j + 1) * 256], bd, preferred_element_type=F32))
        y = jnp.concatenate(y_parts, axis=1)

        st = st_ref[bi]
        st_b = st.astype(BF16)
        yo = [jnp.dot(cm[:, g * SSD_D_STATE:(g + 1) * SSD_D_STATE], st_b[:, g * gw:(g + 1) * gw],
                      preferred_element_type=F32) for g in range(SSD_GROUPS)]
        y = y + jnp.concatenate(yo, axis=1) * jnp.exp(a_x)

        a_last = a_x[L - 1:L, :]
        xw = (xdt * jnp.exp(a_last - a_x)).astype(BF16)
        upd = [lax.dot_general(bm[:, g * SSD_D_STATE:(g + 1) * SSD_D_STATE], xw[:, g * gw:(g + 1) * gw],
                               (((0,), (0,)), ((), ())), preferred_element_type=F32) for g in range(SSD_GROUPS)]
        st_ref[bi] = st * jnp.exp(a_last) + jnp.concatenate(upd, axis=1)

        y = y + dskip_ref[...] * xs
        gz = y * _silu(z_ref[bi].astype(F32))
        ms = jnp.mean(gz * gz, axis=-1, keepdims=True)
        y_ref[bi] = (gz * lax.rsqrt(ms + EPS) * gout_ref[...]).astype(y_ref.dtype)

    for bi in range(nb):
        chunk(bi)

    @pl.when(ci == pl.num_programs(1) - 1)
    def _final():
        for bi in range(nb):
            hfin_ref[bi] = st_ref[bi].T
            convout_ref[bi] = xq_ref[bi, HIST:2 * HIST, :].astype(F32)[HIST - 8:, :]


def _ssd(u, small, conv0, h0, w_conv, b_conv, dt_bias, a_log, d_skip, g_ssd_out, nbatch, t):
    L = SSD_CHUNK
    nc = t // L
    nb = SSD_NB if nbatch % SSD_NB == 0 else 1
    pad128 = lambda v: jnp.pad(v.astype(F32), (0, 128 - v.shape[0])).reshape(1, 128)
    rep = lambda v: jnp.repeat(v.astype(F32), SSD_HEAD_DIM).reshape(1, D_MODEL)
    const2 = lambda shape: pl.BlockSpec(shape, lambda b, c: (0, 0))
    seq = lambda width, col: pl.BlockSpec((nb, L, width), lambda b, c: (b, c, col))
    per_seq = lambda rows, width: pl.BlockSpec((nb, rows, width), lambda b, c: (b, 0, 0))
    u3d = u.reshape(nbatch, t, u.shape[1])
    bc0 = (U1_XS + 1) * D_MODEL // SSD_BC
    y, hfin, convout = pl.pallas_call(
        _ssd_kernel,
        out_shape=(jax.ShapeDtypeStruct((nbatch, t, D_MODEL), BF16),
                   jax.ShapeDtypeStruct((nbatch, D_MODEL, SSD_D_STATE), F32),
                   jax.ShapeDtypeStruct((nbatch, 8, SSD_CONV_CH), F32)),
        grid=(nbatch // nb, nc),
        in_specs=[
            seq(D_MODEL, U1_XS), seq(SSD_BC, bc0), seq(SSD_BC, bc0 + 1), seq(D_MODEL, U1_Z), seq(128, 0),
            per_seq(8, SSD_CONV_CH), per_seq(D_MODEL, SSD_D_STATE),
            const2((SSD_CONV, SSD_CONV_CH)), const2((1, SSD_CONV_CH)),
            const2((1, 128)), const2((1, 128)), const2((1, D_MODEL)), const2((1, D_MODEL)),
            const2((128, D_MODEL)), const2((SSD_CONV * L, 2 * HIST + L)),
        ],
        out_specs=(seq(D_MODEL, 0), per_seq(D_MODEL, SSD_D_STATE), per_seq(8, SSD_CONV_CH)),
        scratch_shapes=[pltpu.VMEM((nb, SSD_D_STATE, D_MODEL), F32),
                        pltpu.VMEM((nb, 2 * HIST + L, SSD_CONV_CH), BF16)],
        compiler_params=_params("parallel", "arbitrary"),
        name="ssd_scan",
    )(u3d, u3d, u3d, u3d, small.reshape(nbatch, t, small.shape[1]), conv0, h0.reshape(nbatch, D_MODEL, SSD_D_STATE),
      w_conv, b_conv.reshape(1, SSD_CONV_CH), pad128(dt_bias), pad128(a_log), rep(d_skip),
      g_ssd_out.reshape(1, D_MODEL), _expand_matrix(), _shift_matrix(L))
    return (y.reshape(nbatch * t, D_MODEL), hfin.reshape(nbatch, SSD_HEADS, SSD_HEAD_DIM, SSD_D_STATE),
            convout[:, 8 - (SSD_CONV - 1):])


def _col_from_row(row_vals):
    n = row_vals.shape[1]
    r = lax.broadcasted_iota(jnp.int32, (n, n), 0)
    c = lax.broadcasted_iota(jnp.int32, (n, n), 1)
    return jnp.sum(jnp.where(r == c, row_vals, 0.0), axis=1, keepdims=True)


FOX_PAIR = 4
NT_DIMS = (((1,), (1,)), ((), ()))
FOX_TQ = 512


def _fox_prompt_kernel(q_ref, k_ref, v_ref, fg_ref, c_ref, o_ref, va_ref, *, tq):
    qi = pl.program_id(2)
    hd = FOX_HEAD_DIM
    t = k_ref.shape[0]
    c1 = hd ** -0.5 * LOG2E
    q0 = pl.multiple_of(qi * tq, tq)
    lanes = [slice(j * hd, (j + 1) * hd) for j in range(FOX_PAIR)]

    @pl.when(qi == 0)
    def _stage_values():
        for j in range(FOX_PAIR):
            va_ref[j, :, 0:hd] = v_ref[:, lanes[j]]
            va_ref[j, :, hd:2 * hd] = jnp.ones((t, hd), BF16)

    cq2_rep = [jnp.broadcast_to(c_ref[j, :, pl.ds(q0, tq)], (hd, tq)).T * LOG2E for j in range(FOX_PAIR)]
    cq2 = [cr[:, 0:1] for cr in cq2_rep]
    row = lax.broadcasted_iota(jnp.int32, (tq, tq), 0)
    col = lax.broadcasted_iota(jnp.int32, (tq, tq), 1)

    def scores(j, k0, masked):
        s = lax.dot_general(q_ref[:, lanes[j]], k_ref[pl.ds(k0, tq), lanes[j]], NT_DIMS, preferred_element_type=F32)
        t2 = s * c1 - c_ref[j, :, pl.ds(k0, tq)] * LOG2E
        return jnp.where(col <= row, t2, NEG_BIG) if masked else t2

    shift = []
    ones = jnp.ones((hd, hd), BF16)
    for j in range(FOX_PAIR):
        qk = q_ref[:, lanes[j]] * k_ref[pl.ds(q0, tq), lanes[j]]
        shift.append(jnp.dot(qk, ones, preferred_element_type=F32) * c1 - cq2_rep[j])

    def fast_tile(k0, accs, masked):
        out = []
        for j in range(FOX_PAIR):
            t2 = scores(j, k0, masked)
            e = jnp.concatenate([t2[:, i:i + hd] - shift[j] for i in range(0, tq, hd)], axis=1)
            out.append(accs[j] + jnp.dot(jnp.exp2(e).astype(BF16), va_ref[j, pl.ds(k0, tq), :],
                                         preferred_element_type=F32))
        return tuple(out)

    accs = tuple(jnp.zeros((tq, 2 * hd), F32) for _ in range(FOX_PAIR))
    accs = lax.fori_loop(0, qi, lambda i, a: fast_tile(pl.multiple_of(i * tq, tq), a, False), accs)
    accs = fast_tile(q0, accs, True)
    bad = jnp.float32(0.0)
    for j in range(FOX_PAIR):
        num, den = accs[j][:, 0:hd], accs[j][:, hd:2 * hd]
        o_ref[:, lanes[j]] = (num / den * _silu(fg_ref[:, lanes[j]].astype(F32))).astype(o_ref.dtype)
        bad = jnp.maximum(bad, jnp.max(jnp.where(jnp.isfinite(accs[j]), 0.0, 1.0)))

    @pl.when(bad > 0.0)
    def _running_max():
        def tile(k0, carry, masked):
            out = []
            for j in range(FOX_PAIR):
                m_i, l_i, acc = carry[j]
                t2 = scores(j, k0, masked)
                m_new = jnp.maximum(m_i, jnp.max(t2, axis=1, keepdims=True) + cq2[j])
                p = jnp.exp2(t2 - (m_new - cq2[j]))
                alpha = jnp.exp2(m_i - m_new)
                l_new = alpha * l_i + jnp.sum(p, axis=1, keepdims=True)
                acc = alpha * acc + jnp.dot(p.astype(BF16), v_ref[pl.ds(k0, tq), lanes[j]],
                                            preferred_element_type=F32)
                out.append((m_new, l_new, acc))
            return tuple(out)

        init = tuple((jnp.full((tq, 1), NEG_BIG, F32), jnp.zeros((tq, 1), F32), jnp.zeros((tq, hd), F32))
                     for _ in range(FOX_PAIR))
        carry = lax.fori_loop(0, qi, lambda i, cr: tile(pl.multiple_of(i * tq, tq), cr, False), init)
        carry = tile(q0, carry, True)
        for j in range(FOX_PAIR):
            _, l_i, acc = carry[j]
            o_ref[:, lanes[j]] = (acc / l_i * _silu(fg_ref[:, lanes[j]].astype(F32))).astype(o_ref.dtype)


def _fox_prompt(u2, ct, nbatch, t):
    tq = min(FOX_TQ, t)
    nq = t // tq
    w = FOX_PAIR * FOX_HEAD_DIM
    nb = D_MODEL // w
    return pl.pallas_call(
        functools.partial(_fox_prompt_kernel, tq=tq),
        out_shape=jax.ShapeDtypeStruct((nbatch * t, D_MODEL), BF16),
        grid=(nbatch, nb, nq),
        in_specs=[
            pl.BlockSpec((tq, w), lambda b, h, i: (b * nq + i, U2_Q * nb + h)),
            pl.BlockSpec((t, w), lambda b, h, i: (b, U2_K * nb + h)),
            pl.BlockSpec((t, w), lambda b, h, i: (b, U2_V * nb + h)),
            pl.BlockSpec((tq, w), lambda b, h, i: (b * nq + i, U2_G * nb + h)),
            pl.BlockSpec((None, FOX_PAIR, 1, t), lambda b, h, i: (b, h, 0, 0)),
        ],
        out_specs=pl.BlockSpec((tq, w), lambda b, h, i: (b * nq + i, h)),
        scratch_shapes=[pltpu.VMEM((FOX_PAIR, t, 2 * FOX_HEAD_DIM), BF16)],
        compiler_params=_params("parallel", "parallel", "arbitrary"),
        name="fox_prompt",
    )(u2, u2, u2, u2, ct)


def _fox_sample_kernel(q_ref, kp_ref, vp_ref, kn_ref, vn_ref, fg_ref, c_ref, o_ref, *, p, t):
    c1 = FOX_HEAD_DIM ** -0.5 * LOG2E
    kp = kp_ref[...].reshape(p, D_MODEL).astype(BF16)
    vp = vp_ref[...].reshape(p, D_MODEL).astype(BF16)
    r = lax.broadcasted_iota(jnp.int32, (t, t), 0)
    c = lax.broadcasted_iota(jnp.int32, (t, t), 1)
    for j in range(FOX_HEADS):
        sl = slice(j * FOX_HEAD_DIM, (j + 1) * FOX_HEAD_DIM)
        q = q_ref[:, sl]
        cq2 = _col_from_row(c_ref[j, :, p:p + t]) * LOG2E
        s_p = lax.dot_general(q, kp[:, sl], NT_DIMS, preferred_element_type=F32) * c1 - c_ref[j, :, 0:p] * LOG2E
        s_n = (lax.dot_general(q, kn_ref[:, sl], NT_DIMS, preferred_element_type=F32) * c1
               - c_ref[j, :, p:p + t] * LOG2E)
        s_n = jnp.where(c <= r, s_n, NEG_BIG)
        m = jnp.maximum(jnp.max(s_p, axis=1, keepdims=True), jnp.max(s_n, axis=1, keepdims=True)) + cq2
        e_p = jnp.exp2(s_p - (m - cq2))
        e_n = jnp.exp2(s_n - (m - cq2))
        inv = 1.0 / (jnp.sum(e_p, axis=1, keepdims=True) + jnp.sum(e_n, axis=1, keepdims=True))
        o = (jnp.dot((e_p * inv).astype(BF16), vp[:, sl], preferred_element_type=F32)
             + jnp.dot((e_n * inv).astype(BF16), vn_ref[:, sl], preferred_element_type=F32))
        o_ref[:, sl] = (o * _silu(fg_ref[:, sl].astype(F32))).astype(o_ref.dtype)


def _fox_sample(u2, k_cache, v_cache, layer, ct, nbatch, t):
    p = k_cache.shape[2]
    past = pl.BlockSpec((None, None, p, FOX_HEADS, FOX_HEAD_DIM), lambda b: (layer, b, 0, 0, 0))
    col = lambda seg: pl.BlockSpec((t, D_MODEL), lambda b: (b, seg))
    return pl.pallas_call(
        functools.partial(_fox_sample_kernel, p=p, t=t),
        out_shape=jax.ShapeDtypeStruct((nbatch * t, D_MODEL), BF16),
        grid=(nbatch,),
        in_specs=[col(U2_Q), past, past, col(U2_K), col(U2_V), col(U2_G),
                  pl.BlockSpec((None, FOX_HEADS, 1, p + t), lambda b: (b, 0, 0, 0))],
        out_specs=pl.BlockSpec((t, D_MODEL), lambda b: (b, 0)),
        compiler_params=_params("parallel"),
        name="fox_sample",
    )(u2, k_cache, v_cache, u2, u2, u2, ct)


def _mem_kernel(q_ref, g_ref, k_ref, v_ref, o_ref):
    scale = MEM_HEAD_DIM ** -0.5
    nm = k_ref.shape[0]
    k = k_ref[...].reshape(nm, D_MODEL).astype(BF16)
    v = v_ref[...].reshape(nm, D_MODEL).astype(BF16)
    for h in range(MEM_HEADS):
        sl = slice(h * MEM_HEAD_DIM, (h + 1) * MEM_HEAD_DIM)
        s = lax.dot_general(q_ref[:, sl], k[:, sl], NT_DIMS, preferred_element_type=F32) * scale
        e = jnp.exp(s - jnp.max(s, axis=1, keepdims=True))
        p = e * (1.0 / jnp.sum(e, axis=1, keepdims=True))
        o = jnp.dot(p.astype(BF16), v[:, sl], preferred_element_type=F32)
        o_ref[:, sl] = (o * _silu(g_ref[:, sl].astype(F32))).astype(o_ref.dtype)


def _mem_attend(u3, mk, mv, nbatch, t, layer=None):
    tq = min(512, t)
    nq = t // tq
    if layer is None:
        nm = mk.shape[1]
        kv = pl.BlockSpec((None, nm, D_MODEL), lambda b, i: (b, 0, 0))
    else:
        nm = mk.shape[2]
        kv = pl.BlockSpec((None, None, nm, MEM_HEADS, MEM_HEAD_DIM), lambda b, i: (layer, b, 0, 0, 0))
    return pl.pallas_call(
        _mem_kernel,
        out_shape=jax.ShapeDtypeStruct((nbatch * t, D_MODEL), BF16),
        grid=(nbatch, nq),
        in_specs=[pl.BlockSpec((tq, D_MODEL), lambda b, i: (b * nq + i, U3_Q)),
                  pl.BlockSpec((tq, D_MODEL), lambda b, i: (b * nq + i, U3_G)), kv, kv],
        out_specs=pl.BlockSpec((tq, D_MODEL), lambda b, i: (b * nq + i, 0)),
        compiler_params=_params("parallel", "arbitrary"),
        name="mem_attend",
    )(u3, u3, mk, mv)


def _merge_kernel(ys_ref, yf_ref, ym_ref, ws_ref, wf_ref, wm_ref, gs_ref, gf_ref, gm_ref, o_ref):
    def branch(y_ref, w_ref, g_ref):
        return jax.nn.sigmoid(g_ref[...].astype(F32)) * jnp.dot(y_ref[...], w_ref[...], preferred_element_type=F32)
    o_ref[...] = (branch(ys_ref, ws_ref, gs_ref) + branch(yf_ref, wf_ref, gf_ref)
                  + branch(ym_ref, wm_ref, gm_ref)).astype(o_ref.dtype)


def _merge(u, y_ssd, y_fox, y_mem, w_s, w_f, w_m, tm, tn):
    m = y_ssd.shape[0]
    nj = D_MODEL // tn
    yspec = pl.BlockSpec((tm, D_MODEL), lambda i, j: (i, 0))
    wspec = pl.BlockSpec((D_MODEL, tn), lambda i, j: (0, j))
    gspec = lambda col: pl.BlockSpec((tm, tn), lambda i, j: (i, col * nj + j))
    return pl.pallas_call(
        _merge_kernel,
        out_shape=jax.ShapeDtypeStruct((m, D_MODEL), BF16),
        grid=(m // tm, nj),
        in_specs=[yspec, yspec, yspec, wspec, wspec, wspec, gspec(U3_GS), gspec(U3_GF), gspec(U3_GM)],
        out_specs=pl.BlockSpec((tm, tn), lambda i, j: (i, j)),
        compiler_params=_params("parallel", "arbitrary"),
        name="gated_merge",
    )(y_ssd, y_fox, y_mem, w_s, w_f, w_m, u, u, u)


def _final_kernel(mg_ref, w_ref, x_ref, g_ref, o_ref, *, normalize):
    xo = x_ref[...] + jnp.dot(mg_ref[...], w_ref[...], preferred_element_type=F32)
    if normalize:
        ms = jnp.mean(xo * xo, axis=-1, keepdims=True)
        xo = xo * lax.rsqrt(ms + EPS) * g_ref[...]
    o_ref[...] = xo


def _final(merged, w_out, x, g_final, tm, normalize):
    m = x.shape[0]
    return pl.pallas_call(
        functools.partial(_final_kernel, normalize=normalize),
        out_shape=jax.ShapeDtypeStruct((m, D_MODEL), F32),
        grid=(m // tm,),
        in_specs=[pl.BlockSpec((tm, D_MODEL), lambda i: (i, 0)),
                  pl.BlockSpec((D_MODEL, D_MODEL), lambda i: (0, 0)),
                  pl.BlockSpec((tm, D_MODEL), lambda i: (i, 0)),
                  pl.BlockSpec((1, D_MODEL), lambda i: (0, 0))],
        out_specs=pl.BlockSpec((tm, D_MODEL), lambda i: (i, 0)),
        compiler_params=_params("parallel"),
        name="out_proj_norm",
    )(merged, w_out, x, g_final.reshape(1, D_MODEL))


def _row_tile(m, pref):
    t = pref
    while m % t:
        t //= 2
    return t


def _layer(x, conv0, h0, caches, logf_past, mem_k, mem_v, wd, g_final, last):
    nbatch, t, d = x.shape
    m = nbatch * t
    x2 = x.reshape(m, d)
    tm = _row_tile(m, 1024)

    h = _rmsnorm(x2, wd["g_norm"], _row_tile(m, 512))
    wt = wd["w_in_t"]
    _, (r2, _), (r3, n3) = W_IN_SEGMENTS
    u1, small = _mm_ssd(h, wt, tm, 1024)
    u2, k_new, v_new = _mm_qkvg(h, wt, r2, tm, 1024)
    u3 = _mm_nt(h, wt, r3, n3, BF16, _row_tile(m, 2048), 1024, "in_proj_mem_gates")

    past_t = None if logf_past is None else jnp.transpose(logf_past, (0, 2, 1))
    logf_t, ct = _logf(small, wd["b_forget"], past_t, nbatch, t)
    ct = ct.reshape(nbatch, FOX_HEADS, 1, ct.shape[-1])

    conv0p = jnp.pad(conv0, ((0, 0), (8 - (SSD_CONV - 1), 0), (0, 0)))
    y_ssd, h_final, new_conv = _ssd(u1, small, conv0p, h0, wd["w_conv"], wd["b_conv"], wd["dt_bias"], wd["a_log"],
                                    wd["d_skip"], wd["g_ssd_out"], nbatch, t)

    if caches is None:
        y_fox = _fox_prompt(u2, ct, nbatch, t)
        y_mem = _mem_attend(u3, mem_k, mem_v, nbatch, t)
    else:
        layer, fox_k, fox_v = caches
        y_fox = _fox_sample(u2, fox_k, fox_v, layer, ct, nbatch, t)
        y_mem = _mem_attend(u3, mem_k, mem_v, nbatch, t, layer)
    merged = _merge(u3, y_ssd, y_fox, y_mem, wd["w_o_ssd"], wd["w_o_fox"], wd["w_o_mem"], tm, 512)
    y = _final(merged, wd["w_out"], x2, g_final, _row_tile(m, 512), last)

    return (y.reshape(nbatch, t, d), new_conv, h_final,
            k_new.reshape(nbatch, t, FOX_HEADS, FOX_HEAD_DIM), v_new.reshape(nbatch, t, FOX_HEADS, FOX_HEAD_DIM),
            jnp.transpose(logf_t, (0, 2, 1)))


def kernel(x_prompt, x_sample, mem_prompt, cache_fox_k, cache_fox_v, cache_fox_logf, state_ssd, state_ssd_conv,
           cache_mem_k, cache_mem_v, g_norm, w_in, w_conv, b_conv, dt_bias, a_log, d_skip, g_ssd_out, b_forget,
           g_mem, w_mem_kv, w_o_ssd, w_o_fox, w_o_mem, w_out, g_final):
    depth = w_in.shape[0]
    xp, xs = x_prompt, x_sample
    bp = xp.shape[0]
    n_mem = mem_prompt.shape[1]
    outs = [[] for _ in range(12)]
    for l in range(depth):
        wkv = _cast_bf16(w_mem_kv[l])
        wd = {
            "g_norm": g_norm[l],
            "w_in_t": _cast_bf16(jnp.transpose(w_in[l])),
            "w_conv": w_conv[l], "b_conv": b_conv[l], "dt_bias": dt_bias[l], "a_log": a_log[l], "d_skip": d_skip[l],
            "g_ssd_out": g_ssd_out[l], "b_forget": b_forget[l],
            "w_o_ssd": _cast_bf16(w_o_ssd[l]), "w_o_fox": _cast_bf16(w_o_fox[l]),
            "w_o_mem": _cast_bf16(w_o_mem[l]), "w_out": _cast_bf16(w_out[l]),
        }
        hm = _rmsnorm(mem_prompt.reshape(bp * n_mem, D_MODEL), g_mem[l], _row_tile(bp * n_mem, 256))
        tmm = _row_tile(bp * n_mem, 512)
        mk, mk_b = _mm_heads(hm, wkv, 0, MEM_HEADS, MEM_HEAD_DIM, tmm, "mem_k_proj")
        mv, mv_b = _mm_heads(hm, wkv, D_MODEL, MEM_HEADS, MEM_HEAD_DIM, tmm, "mem_v_proj")

        xp, c_p, h_p, k_p, v_p, lf_p = _layer(
            xp, jnp.zeros((bp, SSD_CONV - 1, SSD_CONV_CH), F32),
            jnp.zeros((bp, SSD_HEADS, SSD_HEAD_DIM, SSD_D_STATE), F32), None, None,
            mk_b.reshape(bp, n_mem, D_MODEL), mv_b.reshape(bp, n_mem, D_MODEL), wd, g_final, l == depth - 1)
        xs, c_s, h_s, k_s, v_s, lf_s = _layer(
            xs, state_ssd_conv[l], state_ssd[l], (l, cache_fox_k, cache_fox_v), cache_fox_logf[l],
            cache_mem_k, cache_mem_v, wd, g_final, l == depth - 1)
        for lst, val in zip(outs, (k_p, v_p, lf_p, h_p, c_p,
                                   mk.reshape(bp, n_mem, MEM_HEADS, MEM_HEAD_DIM),
                                   mv.reshape(bp, n_mem, MEM_HEADS, MEM_HEAD_DIM),
                                   k_s, v_s, lf_s, h_s, c_s)):
            lst.append(val)
    return (xp, xs) + tuple(jnp.stack(o) for o in outs)
```

```python
import functools

import numpy as np
import jax
import jax.numpy as jnp
from jax import lax
from jax.experimental import pallas as pl
from jax.experimental.pallas import tpu as pltpu

F32 = jnp.float32
BF16 = jnp.bfloat16

EPS = 1e-6
D_MODEL = 2048
SSD_HEAD_DIM = 64
SSD_HEADS = 32
SSD_GROUPS = 4
SSD_D_STATE = 128
SSD_CONV = 4
SSD_BC = SSD_GROUPS * SSD_D_STATE
SSD_CONV_CH = D_MODEL + 2 * SSD_BC
FOX_HEADS = 16
FOX_HEAD_DIM = 128
MEM_HEADS = 4
MEM_HEAD_DIM = 512
SSD_CHUNK = 64
SSD_NB = 2
NEG_BIG = -1e30

VMEM_LIMIT = 56 * 1024 * 1024

W_IN_SEGMENTS = ((0, D_MODEL + SSD_CONV_CH),
                 (D_MODEL + SSD_CONV_CH + SSD_HEADS, 4 * D_MODEL),
                 (D_MODEL + SSD_CONV_CH + SSD_HEADS + 4 * D_MODEL + FOX_HEADS, 5 * D_MODEL))
W_IN_DT = D_MODEL + SSD_CONV_CH
W_IN_FF = W_IN_SEGMENTS[1][0] + 4 * D_MODEL
U1_Z, U1_XS = 0, 1
U3_Q, U3_G, U3_GS, U3_GF, U3_GM = 0, 1, 2, 3, 4
LOG2E = 1.4426950408889634


def _params(*sem):
    return pltpu.CompilerParams(dimension_semantics=sem, vmem_limit_bytes=VMEM_LIMIT)


def _split3(x):
    hi = x.astype(BF16)
    r1 = x - hi.astype(F32)
    mid = r1.astype(BF16)
    lo = (r1 - mid.astype(F32)).astype(BF16)
    return hi, mid, lo


def _softplus(x):
    return jnp.maximum(x, 0.0) + jnp.log1p(jnp.exp(-jnp.abs(x)))


def _silu(x):
    return x * jax.nn.sigmoid(x)


def _rmsnorm_kernel(x_ref, g_ref, o_ref):
    x = x_ref[...]
    ms = jnp.mean(x * x, axis=-1, keepdims=True)
    o_ref[...] = (x * lax.rsqrt(ms + EPS) * g_ref[...]).astype(o_ref.dtype)


def _rmsnorm(x, g, tm):
    m, d = x.shape
    return pl.pallas_call(
        _rmsnorm_kernel,
        out_shape=jax.ShapeDtypeStruct((m, d), BF16),
        grid=(m // tm,),
        in_specs=[pl.BlockSpec((tm, d), lambda i: (i, 0)), pl.BlockSpec((1, d), lambda i: (0, 0))],
        out_specs=pl.BlockSpec((tm, d), lambda i: (i, 0)),
        compiler_params=_params("parallel"),
        name="rmsnorm",
    )(x, g.reshape(1, d))


def _mm_kernel(h_ref, w_ref, o_ref):
    o_ref[...] = jnp.dot(h_ref[...], w_ref[...], preferred_element_type=F32).astype(o_ref.dtype)


def _mm(h, w, out_dtype, tm, tn, name, w_col0=0, n=None):
    m, k = h.shape
    n = w.shape[1] if n is None else n
    j0 = w_col0 // tn
    return pl.pallas_call(
        _mm_kernel,
        out_shape=jax.ShapeDtypeStruct((m, n), out_dtype),
        grid=(m // tm, n // tn),
        in_specs=[pl.BlockSpec((tm, k), lambda i, j: (i, 0)), pl.BlockSpec((k, tn), lambda i, j: (0, j0 + j))],
        out_specs=pl.BlockSpec((tm, tn), lambda i, j: (i, j)),
        compiler_params=_params("parallel", "arbitrary"),
        name=name,
    )(h, w)


def _mm_heads_kernel(h_ref, w_ref, o_ref, ob_ref):
    acc = jnp.dot(h_ref[...], w_ref[...], preferred_element_type=F32)
    o_ref[...] = acc.reshape(o_ref.shape)
    ob_ref[...] = acc.astype(ob_ref.dtype)


def _mm_heads(h, w, w_col0, heads, head_dim, tm, name):
    m, k = h.shape
    n = heads * head_dim
    return pl.pallas_call(
        _mm_heads_kernel,
        out_shape=(jax.ShapeDtypeStruct((m, heads, head_dim), F32), jax.ShapeDtypeStruct((m, n), BF16)),
        grid=(m // tm,),
        in_specs=[pl.BlockSpec((tm, k), lambda i: (i, 0)), pl.BlockSpec((k, n), lambda i: (0, w_col0 // n))],
        out_specs=(pl.BlockSpec((tm, heads, head_dim), lambda i: (i, 0, 0)), pl.BlockSpec((tm, n), lambda i: (i, 0))),
        compiler_params=_params("parallel"),
        name=name,
    )(h, w)


def _mm_nt_kernel(h_ref, wt_ref, o_ref):
    o_ref[...] = lax.dot_general(h_ref[...], wt_ref[...], NT_DIMS, preferred_element_type=F32).astype(o_ref.dtype)


def _wt_rows_spec(tn, k, row0, grid_rank):
    g = 16
    assert row0 % g == 0 and tn % g == 0
    if grid_rank == 1:
        return pl.BlockSpec((pl.Element(tn), pl.Element(k)), lambda i: (row0, 0))
    return pl.BlockSpec((pl.Element(tn), pl.Element(k)), lambda i, j: ((row0 // g + j * (tn // g)) * g, 0))


def _mm_nt(h, wt, row0, n, out_dtype, tm, tn, name):
    m, k = h.shape
    return pl.pallas_call(
        _mm_nt_kernel,
        out_shape=jax.ShapeDtypeStruct((m, n), out_dtype),
        grid=(m // tm, n // tn),
        in_specs=[pl.BlockSpec((tm, k), lambda i, j: (i, 0)), _wt_rows_spec(tn, k, row0, 2)],
        out_specs=pl.BlockSpec((tm, tn), lambda i, j: (i, j)),
        compiler_params=_params("parallel", "arbitrary"),
        name=name,
    )(h, wt)


def _mm_ssd_kernel(h_ref, wt_ref, wdt_ref, wff_ref, o_ref, small_ref):
    o_ref[...] = lax.dot_general(h_ref[...], wt_ref[...], NT_DIMS, preferred_element_type=F32).astype(o_ref.dtype)

    @pl.when(pl.program_id(1) == 0)
    def _narrow():
        w = jnp.concatenate([wdt_ref[...], wff_ref[...]], axis=0)
        small_ref[...] = lax.dot_general(h_ref[...], w, NT_DIMS, preferred_element_type=F32)


def _mm_ssd(h, wt, tm, tn):
    m, k = h.shape
    row0, n = W_IN_SEGMENTS[0]
    fixed = lambda r0: pl.BlockSpec((pl.Element(128), pl.Element(k)), lambda i, j: (r0, 0))
    return pl.pallas_call(
        _mm_ssd_kernel,
        out_shape=(jax.ShapeDtypeStruct((m, n), BF16), jax.ShapeDtypeStruct((m, 256), F32)),
        grid=(m // tm, n // tn),
        in_specs=[pl.BlockSpec((tm, k), lambda i, j: (i, 0)), _wt_rows_spec(tn, k, row0, 2),
                  fixed(W_IN_DT), fixed(W_IN_FF)],
        out_specs=(pl.BlockSpec((tm, tn), lambda i, j: (i, j)), pl.BlockSpec((tm, 256), lambda i, j: (i, 0))),
        compiler_params=_params("parallel", "arbitrary"),
        name="in_proj_ssd",
    )(h, wt, wt, wt)


def _mm_nt_heads_kernel(h_ref, wt_ref, o_ref, oh_ref):
    acc = lax.dot_general(h_ref[...], wt_ref[...], NT_DIMS, preferred_element_type=F32)
    o_ref[...] = acc.astype(o_ref.dtype)
    oh_ref[...] = acc.reshape(oh_ref.shape)


def _mm_nt_heads(h, wt, row0, tm, tn, name):
    m, k = h.shape
    hb = tn // FOX_HEAD_DIM
    return pl.pallas_call(
        _mm_nt_heads_kernel,
        out_shape=(jax.ShapeDtypeStruct((m, D_MODEL), BF16),
                   jax.ShapeDtypeStruct((m, FOX_HEADS, FOX_HEAD_DIM), F32)),
        grid=(m // tm, D_MODEL // tn),
        in_specs=[pl.BlockSpec((tm, k), lambda i, j: (i, 0)), _wt_rows_spec(tn, k, row0, 2)],
        out_specs=(pl.BlockSpec((tm, tn), lambda i, j: (i, j)),
                   pl.BlockSpec((tm, hb, FOX_HEAD_DIM), lambda i, j: (i, j, 0))),
        compiler_params=_params("parallel", "arbitrary"),
        name=name,
    )(h, wt)


def _cast_kernel(a_ref, o_ref):
    o_ref[...] = a_ref[...].astype(o_ref.dtype)


def _cast_bf16(w):
    rows, cols = w.shape
    tr = max(d for d in range(16, 1025, 16) if rows % d == 0)
    return pl.pallas_call(
        _cast_kernel,
        out_shape=jax.ShapeDtypeStruct((rows, cols), BF16),
        grid=(rows // tr,),
        in_specs=[pl.BlockSpec((tr, cols), lambda i: (i, 0))],
        out_specs=pl.BlockSpec((tr, cols), lambda i: (i, 0)),
        compiler_params=_params("parallel"),
        name="weight_cast",
    )(w)


def _upper3(tb):
    k = lax.broadcasted_iota(jnp.int32, (3 * tb, tb), 0) % tb
    t = lax.broadcasted_iota(jnp.int32, (3 * tb, tb), 1)
    return jnp.where(k <= t, 1.0, 0.0).astype(BF16)


def _cumsum_lanes(x, carry, tb):
    n = x.shape[1]
    u3 = _upper3(tb)
    out = []
    for s in range(0, n, tb):
        hi, mid, lo = _split3(x[:, s:s + tb])
        c = jnp.dot(jnp.concatenate([hi, mid, lo], axis=1), u3, preferred_element_type=F32) + carry
        carry = c[:, tb - 1:tb]
        out.append(c)
    return out, carry


def _logf_kernel(*refs, nb, t, p):
    if p:
        ffc_ref, bias_ref, past_ref, lf_ref, c_ref = refs
    else:
        ffc_ref, bias_ref, lf_ref, c_ref = refs
    fft = ffc_ref[...].T[0:FOX_HEADS, :]
    lf_all = -_softplus(-(fft + bias_ref[...]))
    for b in range(nb):
        lf = lf_all[:, b * t:(b + 1) * t]
        lf_ref[b] = lf
        carry = jnp.zeros((FOX_HEADS, 1), F32)
        if p:
            tbp = min(256, p)
            blocks, carry = _cumsum_lanes(past_ref[b], carry, tbp)
            for i, c in enumerate(blocks):
                c_ref[b, :, i * tbp:(i + 1) * tbp] = c
        tb = min(256, t)
        blocks, carry = _cumsum_lanes(lf, carry, tb)
        for i, c in enumerate(blocks):
            c_ref[b, :, p + i * tb:p + (i + 1) * tb] = c


def _logf(small, b_forget, past_t, nbatch, t):
    p = 0 if past_t is None else past_t.shape[2]
    nb = 1 if t % 128 == 0 else nbatch
    grid = (nbatch // nb,)
    in_specs = [pl.BlockSpec((nb * t, 128), lambda i: (i, 1)),
                pl.BlockSpec((FOX_HEADS, 1), lambda i: (0, 0))]
    args = [small, b_forget.reshape(FOX_HEADS, 1)]
    if p:
        in_specs.append(pl.BlockSpec((nb, FOX_HEADS, p), lambda i: (i, 0, 0)))
        args.append(past_t)
    return pl.pallas_call(
        functools.partial(_logf_kernel, nb=nb, t=t, p=p),
        out_shape=(jax.ShapeDtypeStruct((nbatch, FOX_HEADS, t), F32),
                   jax.ShapeDtypeStruct((nbatch, FOX_HEADS, p + t), F32)),
        grid=grid,
        in_specs=in_specs,
        out_specs=(pl.BlockSpec((nb, FOX_HEADS, t), lambda i: (i, 0, 0)),
                   pl.BlockSpec((nb, FOX_HEADS, p + t), lambda i: (i, 0, 0))),
        compiler_params=_params("parallel"),
        name="logf_cumsum",
    )(*args)


def _expand_matrix():
    r = np.arange(128)[:, None]
    c = np.arange(D_MODEL)[None, :]
    return jnp.asarray(((r < 96) & ((r % SSD_HEADS) == (c // SSD_HEAD_DIM))).astype(np.float32), dtype=BF16)


def _pack3(x):
    lane = lax.broadcasted_iota(jnp.int32, x.shape, 1)
    x = jnp.where(lane < SSD_HEADS, x, 0.0)
    hi, mid, lo = _split3(x)
    packed = hi.astype(F32) + pltpu.roll(mid.astype(F32), SSD_HEADS, 1) + pltpu.roll(lo.astype(F32), 2 * SSD_HEADS, 1)
    return packed.astype(BF16)


HIST = 16


def _shift_matrix(L):
    sh = np.zeros((SSD_CONV * L, 2 * HIST + L), np.float32)
    for d in range(SSD_CONV):
        for t in range(L):
            sh[d * L + t, 2 * HIST + t - d] = 1.0
            if t - d < 0:
                sh[d * L + t, HIST + t - d] = 1.0
    return jnp.asarray(sh, dtype=BF16)


def _ssd_kernel(xs_ref, b_ref, c_ref, z_ref, sm_ref, conv0_ref, h0_ref,
                wconv_ref, bconv_ref, dtb_ref, alog_ref, dskip_ref, gout_ref, e3_ref, sh_ref,
                y_ref, hfin_ref, convout_ref,
                st_ref, xq_ref):
    ci = pl.program_id(1)
    nb, L = xs_ref.shape[0], xs_ref.shape[1]
    P = SSD_HEAD_DIM
    W = D_MODEL
    gw = W // SSD_GROUPS
    hpm = 256 // L

    @pl.when(ci == 0)
    def _init():
        for bi in range(nb):
            st_ref[bi] = h0_ref[bi].T
            hist = jnp.concatenate([jnp.zeros((HIST - 8, SSD_CONV_CH), F32), conv0_ref[bi]], axis=0)
            hist_hi = hist.astype(BF16)
            xq_ref[bi, 0:HIST, :] = hist_hi
            xq_ref[bi, HIST:2 * HIST, :] = (hist - hist_hi.astype(F32)).astype(BF16)

    lane = lax.broadcasted_iota(jnp.int32, (L, 128), 1)
    kk = lax.broadcasted_iota(jnp.int32, (L, 3 * L), 1) % L
    ll = lax.broadcasted_iota(jnp.int32, (L, 3 * L), 0)
    tri3 = jnp.where(kk <= ll, 1.0, 0.0).astype(BF16)
    row = lax.broadcasted_iota(jnp.int32, (L, W), 0)
    sidx = lax.broadcasted_iota(jnp.int32, (L, W), 1) % L
    rb = lax.broadcasted_iota(jnp.int32, (256, hpm * P), 0) // L
    cb = lax.broadcasted_iota(jnp.int32, (256, hpm * P), 1) // P
    blockmask = rb == cb
    a_small = -jnp.exp(alog_ref[...])
    e3 = e3_ref[...]

    def chunk(bi):
        xq_ref[bi, 2 * HIST:, 0:W] = xs_ref[bi]
        xq_ref[bi, 2 * HIST:, W:W + SSD_BC] = b_ref[bi]
        xq_ref[bi, 2 * HIST:, W + SSD_BC:] = c_ref[bi]

        def conv(lo, hi):
            sh = jnp.dot(sh_ref[...], xq_ref[bi, :, lo:hi], preferred_element_type=F32)
            acc = bconv_ref[:, lo:hi]
            for d in range(SSD_CONV):
                acc = acc + sh[d * L:(d + 1) * L] * wconv_ref[SSD_CONV - 1 - d:SSD_CONV - d, lo:hi]
            return _silu(acc)

        xs = conv(0, W)
        bm = conv(W, W + SSD_BC).astype(BF16)
        cm = conv(W + SSD_BC, W + 2 * SSD_BC).astype(BF16)
        xq_ref[bi, 0:HIST, :] = jnp.zeros((HIST, SSD_CONV_CH), BF16)
        xq_ref[bi, HIST:2 * HIST, :] = xq_ref[bi, HIST + L:2 * HIST + L, :]

        dt = _softplus(sm_ref[bi] + dtb_ref[...])
        dta = jnp.where(lane < SSD_HEADS, dt * a_small, 0.0)
        hi, mid, lo = _split3(dta)
        acum = jnp.dot(tri3, jnp.concatenate([hi, mid, lo], axis=0), preferred_element_type=F32)
        a_x = jnp.dot(_pack3(acum), e3, preferred_element_type=F32)
        dt_x = jnp.dot(_pack3(dt), e3, preferred_element_type=F32)

        a_row = jnp.sum(jnp.where(row == sidx, a_x, 0.0), axis=0, keepdims=True)
        ldec = jnp.exp(jnp.where(row >= sidx, a_x - a_row, NEG_BIG))
        cbx = []
        for g in range(SSD_GROUPS):
            cg = cm[:, g * SSD_D_STATE:(g + 1) * SSD_D_STATE]
            bg = bm[:, g * SSD_D_STATE:(g + 1) * SSD_D_STATE]
            brep = jnp.concatenate([bg] * (SSD_HEADS // SSD_GROUPS), axis=0)
            cbx.append(lax.dot_general(cg, brep, NT_DIMS, preferred_element_type=F32))
        m = (jnp.concatenate(cbx, axis=1) * ldec).astype(BF16)

        xdt = xs * dt_x
        xdt_b = xdt.astype(BF16)
        y_parts = []
        for j in range(SSD_HEADS // hpm):
            xj = xdt_b[:, j * hpm * P:(j + 1) * hpm * P]
            bd = jnp.where(blockmask, jnp.concatenate([xj] * hpm, axis=0), jnp.zeros((), BF16))
            y_parts.append(jnp.dot(m[:, j * 256:(j + 1) * 256], bd, preferred_element_type=F32))
        y = jnp.concatenate(y_parts, axis=1)

        st = st_ref[bi]
        st_b = st.astype(BF16)
        yo = [jnp.dot(cm[:, g * SSD_D_STATE:(g + 1) * SSD_D_STATE], st_b[:, g * gw:(g + 1) * gw],
                      preferred_element_type=F32) for g in range(SSD_GROUPS)]
        y = y + jnp.concatenate(yo, axis=1) * jnp.exp(a_x)

        a_last = a_x[L - 1:L, :]
        xw = (xdt * jnp.exp(a_last - a_x)).astype(BF16)
        upd = [lax.dot_general(bm[:, g * SSD_D_STATE:(g + 1) * SSD_D_STATE], xw[:, g * gw:(g + 1) * gw],
                               (((0,), (0,)), ((), ())), preferred_element_type=F32) for g in range(SSD_GROUPS)]
        st_ref[bi] = st * jnp.exp(a_last) + jnp.concatenate(upd, axis=1)

        y = y + dskip_ref[...] * xs
        gz = y * _silu(z_ref[bi].astype(F32))
        ms = jnp.mean(gz * gz, axis=-1, keepdims=True)
        y_ref[bi] = (gz * lax.rsqrt(ms + EPS) * gout_ref[...]).astype(y_ref.dtype)

    for bi in range(nb):
        chunk(bi)

    @pl.when(ci == pl.num_programs(1) - 1)
    def _final():
        for bi in range(nb):
            hfin_ref[bi] = st_ref[bi].T
            convout_ref[bi] = xq_ref[bi, HIST:2 * HIST, :].astype(F32)[HIST - 8:, :]


def _ssd(u, small, conv0, h0, w_conv, b_conv, dt_bias, a_log, d_skip, g_ssd_out, nbatch, t):
    L = SSD_CHUNK
    nc = t // L
    nb = SSD_NB if nbatch % SSD_NB == 0 else 1
    pad128 = lambda v: jnp.pad(v.astype(F32), (0, 128 - v.shape[0])).reshape(1, 128)
    rep = lambda v: jnp.repeat(v.astype(F32), SSD_HEAD_DIM).reshape(1, D_MODEL)
    const2 = lambda shape: pl.BlockSpec(shape, lambda b, c: (0, 0))
    seq = lambda width, col: pl.BlockSpec((nb, L, width), lambda b, c: (b, c, col))
    per_seq = lambda rows, width: pl.BlockSpec((nb, rows, width), lambda b, c: (b, 0, 0))
    u3d = u.reshape(nbatch, t, u.shape[1])
    bc0 = (U1_XS + 1) * D_MODEL // SSD_BC
    y, hfin, convout = pl.pallas_call(
        _ssd_kernel,
        out_shape=(jax.ShapeDtypeStruct((nbatch, t, D_MODEL), BF16),
                   jax.ShapeDtypeStruct((nbatch, D_MODEL, SSD_D_STATE), F32),
                   jax.ShapeDtypeStruct((nbatch, 8, SSD_CONV_CH), F32)),
        grid=(nbatch // nb, nc),
        in_specs=[
            seq(D_MODEL, U1_XS), seq(SSD_BC, bc0), seq(SSD_BC, bc0 + 1), seq(D_MODEL, U1_Z), seq(128, 0),
            per_seq(8, SSD_CONV_CH), per_seq(D_MODEL, SSD_D_STATE),
            const2((SSD_CONV, SSD_CONV_CH)), const2((1, SSD_CONV_CH)),
            const2((1, 128)), const2((1, 128)), const2((1, D_MODEL)), const2((1, D_MODEL)),
            const2((128, D_MODEL)), const2((SSD_CONV * L, 2 * HIST + L)),
        ],
        out_specs=(seq(D_MODEL, 0), per_seq(D_MODEL, SSD_D_STATE), per_seq(8, SSD_CONV_CH)),
        scratch_shapes=[pltpu.VMEM((nb, SSD_D_STATE, D_MODEL), F32),
                        pltpu.VMEM((nb, 2 * HIST + L, SSD_CONV_CH), BF16)],
        compiler_params=_params("parallel", "arbitrary"),
        name="ssd_scan",
    )(u3d, u3d, u3d, u3d, small.reshape(nbatch, t, small.shape[1]), conv0, h0.reshape(nbatch, D_MODEL, SSD_D_STATE),
      w_conv, b_conv.reshape(1, SSD_CONV_CH), pad128(dt_bias), pad128(a_log), rep(d_skip),
      g_ssd_out.reshape(1, D_MODEL), _expand_matrix(), _shift_matrix(L))
    return (y.reshape(nbatch * t, D_MODEL), hfin.reshape(nbatch, SSD_HEADS, SSD_HEAD_DIM, SSD_D_STATE),
            convout[:, 8 - (SSD_CONV - 1):])


def _col_from_row(row_vals):
    n = row_vals.shape[1]
    r = lax.broadcasted_iota(jnp.int32, (n, n), 0)
    c = lax.broadcasted_iota(jnp.int32, (n, n), 1)
    return jnp.sum(jnp.where(r == c, row_vals, 0.0), axis=1, keepdims=True)


FOX_PAIR = 4
NT_DIMS = (((1,), (1,)), ((), ()))
FOX_TQ = 512


def _fox_prompt_kernel(q_ref, k_ref, v_ref, fg_ref, c_ref, o_ref, va_ref, *, tq):
    qi = pl.program_id(2)
    hd = FOX_HEAD_DIM
    t = k_ref.shape[0]
    c1 = hd ** -0.5 * LOG2E
    q0 = pl.multiple_of(qi * tq, tq)
    lanes = [slice(j * hd, (j + 1) * hd) for j in range(FOX_PAIR)]

    @pl.when(qi == 0)
    def _stage_values():
        for j in range(FOX_PAIR):
            va_ref[j, :, 0:hd] = v_ref[:, lanes[j]]
            va_ref[j, :, hd:2 * hd] = jnp.ones((t, hd), BF16)

    cq2_rep = [jnp.broadcast_to(c_ref[j, :, pl.ds(q0, tq)], (hd, tq)).T * LOG2E for j in range(FOX_PAIR)]
    cq2 = [cr[:, 0:1] for cr in cq2_rep]
    row = lax.broadcasted_iota(jnp.int32, (tq, tq), 0)
    col = lax.broadcasted_iota(jnp.int32, (tq, tq), 1)

    def scores(j, k0, masked):
        s = lax.dot_general(q_ref[:, lanes[j]], k_ref[pl.ds(k0, tq), lanes[j]], NT_DIMS, preferred_element_type=F32)
        t2 = s * c1 - c_ref[j, :, pl.ds(k0, tq)] * LOG2E
        return jnp.where(col <= row, t2, NEG_BIG) if masked else t2

    shift = []
    ones = jnp.ones((hd, hd), BF16)
    for j in range(FOX_PAIR):
        qk = q_ref[:, lanes[j]] * k_ref[pl.ds(q0, tq), lanes[j]]
        shift.append(jnp.dot(qk, ones, preferred_element_type=F32) * c1 - cq2_rep[j])

    def fast_tile(k0, accs, masked):
        out = []
        for j in range(FOX_PAIR):
            t2 = scores(j, k0, masked)
            e = jnp.concatenate([t2[:, i:i + hd] - shift[j] for i in range(0, tq, hd)], axis=1)
            out.append(accs[j] + jnp.dot(jnp.exp2(e).astype(BF16), va_ref[j, pl.ds(k0, tq), :],
                                         preferred_element_type=F32))
        return tuple(out)

    def fast_part(j, r0, nr, kk, nk):
        s = lax.dot_general(q_ref[r0:r0 + nr, lanes[j]], k_ref[pl.ds(q0 + kk, nk), lanes[j]], NT_DIMS,
                            preferred_element_type=F32)
        t2 = s * c1 - c_ref[j, :, pl.ds(q0 + kk, nk)] * LOG2E
        rr = lax.broadcasted_iota(jnp.int32, (nr, nk), 0) + r0
        cc = lax.broadcasted_iota(jnp.int32, (nr, nk), 1) + kk
        t2 = jnp.where(cc <= rr, t2, NEG_BIG)
        e = jnp.concatenate([t2[:, i:i + hd] - shift[j][r0:r0 + nr] for i in range(0, nk, hd)], axis=1)
        return jnp.dot(jnp.exp2(e).astype(BF16), va_ref[j, pl.ds(q0 + kk, nk), :], preferred_element_type=F32)

    def fast_diag(accs):
        half = tq // 2
        out = []
        for j in range(FOX_PAIR):
            left = fast_part(j, 0, tq, 0, half)
            right = fast_part(j, half, half, half, half)
            out.append(accs[j] + left + jnp.concatenate([jnp.zeros((half, 2 * hd), F32), right], axis=0))
        return tuple(out)

    accs = tuple(jnp.zeros((tq, 2 * hd), F32) for _ in range(FOX_PAIR))
    accs = lax.fori_loop(0, qi, lambda i, a: fast_tile(pl.multiple_of(i * tq, tq), a, False), accs)
    accs = fast_diag(accs)
    bad = jnp.float32(0.0)
    for j in range(FOX_PAIR):
        num, den = accs[j][:, 0:hd], accs[j][:, hd:2 * hd]
        o_ref[:, lanes[j]] = (num / den * _silu(fg_ref[:, lanes[j]].astype(F32))).astype(o_ref.dtype)
        bad = jnp.maximum(bad, jnp.max(jnp.where(jnp.isfinite(accs[j]), 0.0, 1.0)))

    @pl.when(bad > 0.0)
    def _running_max():
        def tile(k0, carry, masked):
            out = []
            for j in range(FOX_PAIR):
                m_i, l_i, acc = carry[j]
                t2 = scores(j, k0, masked)
                m_new = jnp.maximum(m_i, jnp.max(t2, axis=1, keepdims=True) + cq2[j])
                p = jnp.exp2(t2 - (m_new - cq2[j]))
                alpha = jnp.exp2(m_i - m_new)
                l_new = alpha * l_i + jnp.sum(p, axis=1, keepdims=True)
                acc = alpha * acc + jnp.dot(p.astype(BF16), v_ref[pl.ds(k0, tq), lanes[j]],
                                            preferred_element_type=F32)
                out.append((m_new, l_new, acc))
            return tuple(out)

        init = tuple((jnp.full((tq, 1), NEG_BIG, F32), jnp.zeros((tq, 1), F32), jnp.zeros((tq, hd), F32))
                     for _ in range(FOX_PAIR))
        carry = lax.fori_loop(0, qi, lambda i, cr: tile(pl.multiple_of(i * tq, tq), cr, False), init)
        carry = tile(q0, carry, True)
        for j in range(FOX_PAIR):
            _, l_i, acc = carry[j]
            o_ref[:, lanes[j]] = (acc / l_i * _silu(fg_ref[:, lanes[j]].astype(F32))).astype(o_ref.dtype)


def _fox_prompt(uq, uk, uv, ug, ct, nbatch, t):
    tq = min(FOX_TQ, t)
    nq = t // tq
    w = FOX_PAIR * FOX_HEAD_DIM
    nb = D_MODEL // w
    return pl.pallas_call(
        functools.partial(_fox_prompt_kernel, tq=tq),
        out_shape=jax.ShapeDtypeStruct((nbatch * t, D_MODEL), BF16),
        grid=(nbatch, nb, nq),
        in_specs=[
            pl.BlockSpec((tq, w), lambda b, h, i: (b * nq + i, h)),
            pl.BlockSpec((t, w), lambda b, h, i: (b, h)),
            pl.BlockSpec((t, w), lambda b, h, i: (b, h)),
            pl.BlockSpec((tq, w), lambda b, h, i: (b * nq + i, h)),
            pl.BlockSpec((None, FOX_PAIR, 1, t), lambda b, h, i: (b, h, 0, 0)),
        ],
        out_specs=pl.BlockSpec((tq, w), lambda b, h, i: (b * nq + i, h)),
        scratch_shapes=[pltpu.VMEM((FOX_PAIR, t, 2 * FOX_HEAD_DIM), BF16)],
        compiler_params=_params("parallel", "parallel", "arbitrary"),
        name="fox_prompt",
    )(uq, uk, uv, ug, ct)


def _fox_sample_kernel(q_ref, kp_ref, vp_ref, kn_ref, vn_ref, fg_ref, c_ref, o_ref, *, p, t):
    c1 = FOX_HEAD_DIM ** -0.5 * LOG2E
    kp = kp_ref[...].reshape(p, D_MODEL).astype(BF16)
    vp = vp_ref[...].reshape(p, D_MODEL).astype(BF16)
    r = lax.broadcasted_iota(jnp.int32, (t, t), 0)
    c = lax.broadcasted_iota(jnp.int32, (t, t), 1)
    for j in range(FOX_HEADS):
        sl = slice(j * FOX_HEAD_DIM, (j + 1) * FOX_HEAD_DIM)
        q = q_ref[:, sl]
        cq2 = _col_from_row(c_ref[j, :, p:p + t]) * LOG2E
        s_p = lax.dot_general(q, kp[:, sl], NT_DIMS, preferred_element_type=F32) * c1 - c_ref[j, :, 0:p] * LOG2E
        s_n = (lax.dot_general(q, kn_ref[:, sl], NT_DIMS, preferred_element_type=F32) * c1
               - c_ref[j, :, p:p + t] * LOG2E)
        s_n = jnp.where(c <= r, s_n, NEG_BIG)
        m = jnp.maximum(jnp.max(s_p, axis=1, keepdims=True), jnp.max(s_n, axis=1, keepdims=True)) + cq2
        e_p = jnp.exp2(s_p - (m - cq2))
        e_n = jnp.exp2(s_n - (m - cq2))
        inv = 1.0 / (jnp.sum(e_p, axis=1, keepdims=True) + jnp.sum(e_n, axis=1, keepdims=True))
        o = (jnp.dot((e_p * inv).astype(BF16), vp[:, sl], preferred_element_type=F32)
             + jnp.dot((e_n * inv).astype(BF16), vn_ref[:, sl], preferred_element_type=F32))
        o_ref[:, sl] = (o * _silu(fg_ref[:, sl].astype(F32))).astype(o_ref.dtype)


def _fox_sample(uq, uk, uv, ug, k_cache, v_cache, layer, ct, nbatch, t):
    p = k_cache.shape[2]
    past = pl.BlockSpec((None, None, p, FOX_HEADS, FOX_HEAD_DIM), lambda b: (layer, b, 0, 0, 0))
    col = pl.BlockSpec((t, D_MODEL), lambda b: (b, 0))
    return pl.pallas_call(
        functools.partial(_fox_sample_kernel, p=p, t=t),
        out_shape=jax.ShapeDtypeStruct((nbatch * t, D_MODEL), BF16),
        grid=(nbatch,),
        in_specs=[col, past, past, col, col, col,
                  pl.BlockSpec((None, FOX_HEADS, 1, p + t), lambda b: (b, 0, 0, 0))],
        out_specs=pl.BlockSpec((t, D_MODEL), lambda b: (b, 0)),
        compiler_params=_params("parallel"),
        name="fox_sample",
    )(uq, k_cache, v_cache, uk, uv, ug, ct)


def _mem_kernel(q_ref, g_ref, k_ref, v_ref, o_ref):
    scale = MEM_HEAD_DIM ** -0.5
    nm = k_ref.shape[0]
    k = k_ref[...].reshape(nm, D_MODEL).astype(BF16)
    v = v_ref[...].reshape(nm, D_MODEL).astype(BF16)
    for h in range(MEM_HEADS):
        sl = slice(h * MEM_HEAD_DIM, (h + 1) * MEM_HEAD_DIM)
        s = lax.dot_general(q_ref[:, sl], k[:, sl], NT_DIMS, preferred_element_type=F32) * scale
        e = jnp.exp(s - jnp.max(s, axis=1, keepdims=True))
        p = e * (1.0 / jnp.sum(e, axis=1, keepdims=True))
        o = jnp.dot(p.astype(BF16), v[:, sl], preferred_element_type=F32)
        o_ref[:, sl] = (o * _silu(g_ref[:, sl].astype(F32))).astype(o_ref.dtype)


def _mem_attend(u3, mk, mv, nbatch, t, layer=None):
    tq = min(512, t)
    nq = t // tq
    if layer is None:
        nm = mk.shape[1]
        kv = pl.BlockSpec((None, nm, D_MODEL), lambda b, i: (b, 0, 0))
    else:
        nm = mk.shape[2]
        kv = pl.BlockSpec((None, None, nm, MEM_HEADS, MEM_HEAD_DIM), lambda b, i: (layer, b, 0, 0, 0))
    return pl.pallas_call(
        _mem_kernel,
        out_shape=jax.ShapeDtypeStruct((nbatch * t, D_MODEL), BF16),
        grid=(nbatch, nq),
        in_specs=[pl.BlockSpec((tq, D_MODEL), lambda b, i: (b * nq + i, U3_Q)),
                  pl.BlockSpec((tq, D_MODEL), lambda b, i: (b * nq + i, U3_G)), kv, kv],
        out_specs=pl.BlockSpec((tq, D_MODEL), lambda b, i: (b * nq + i, 0)),
        compiler_params=_params("parallel", "arbitrary"),
        name="mem_attend",
    )(u3, u3, mk, mv)


def _merge_kernel(ys_ref, yf_ref, ym_ref, ws_ref, wf_ref, wm_ref, gs_ref, gf_ref, gm_ref, o_ref):
    def branch(y_ref, w_ref, g_ref):
        return jax.nn.sigmoid(g_ref[...].astype(F32)) * jnp.dot(y_ref[...], w_ref[...], preferred_element_type=F32)
    o_ref[...] = (branch(ys_ref, ws_ref, gs_ref) + branch(yf_ref, wf_ref, gf_ref)
                  + branch(ym_ref, wm_ref, gm_ref)).astype(o_ref.dtype)


def _merge(u, y_ssd, y_fox, y_mem, w_s, w_f, w_m, tm, tn):
    m = y_ssd.shape[0]
    nj = D_MODEL // tn
    yspec = pl.BlockSpec((tm, D_MODEL), lambda i, j: (i, 0))
    wspec = pl.BlockSpec((D_MODEL, tn), lambda i, j: (0, j))
    gspec = lambda col: pl.BlockSpec((tm, tn), lambda i, j: (i, col * nj + j))
    return pl.pallas_call(
        _merge_kernel,
        out_shape=jax.ShapeDtypeStruct((m, D_MODEL), BF16),
        grid=(m // tm, nj),
        in_specs=[yspec, yspec, yspec, wspec, wspec, wspec, gspec(U3_GS), gspec(U3_GF), gspec(U3_GM)],
        out_specs=pl.BlockSpec((tm, tn), lambda i, j: (i, j)),
        compiler_params=_params("parallel", "arbitrary"),
        name="gated_merge",
    )(y_ssd, y_fox, y_mem, w_s, w_f, w_m, u, u, u)


def _final_kernel(mg_ref, w_ref, x_ref, g_ref, o_ref, *, normalize):
    xo = x_ref[...] + jnp.dot(mg_ref[...], w_ref[...], preferred_element_type=F32)
    if normalize:
        ms = jnp.mean(xo * xo, axis=-1, keepdims=True)
        xo = xo * lax.rsqrt(ms + EPS) * g_ref[...]
    o_ref[...] = xo


def _final(merged, w_out, x, g_final, tm, normalize):
    m = x.shape[0]
    return pl.pallas_call(
        functools.partial(_final_kernel, normalize=normalize),
        out_shape=jax.ShapeDtypeStruct((m, D_MODEL), F32),
        grid=(m // tm,),
        in_specs=[pl.BlockSpec((tm, D_MODEL), lambda i: (i, 0)),
                  pl.BlockSpec((D_MODEL, D_MODEL), lambda i: (0, 0)),
                  pl.BlockSpec((tm, D_MODEL), lambda i: (i, 0)),
                  pl.BlockSpec((1, D_MODEL), lambda i: (0, 0))],
        out_specs=pl.BlockSpec((tm, D_MODEL), lambda i: (i, 0)),
        compiler_params=_params("parallel"),
        name="out_proj_norm",
    )(merged, w_out, x, g_final.reshape(1, D_MODEL))


def _row_tile(m, pref):
    t = pref
    while m % t:
        t //= 2
    return t


def _layer(x, conv0, h0, caches, logf_past, mem_k, mem_v, wd, g_final, last):
    nbatch, t, d = x.shape
    m = nbatch * t
    x2 = x.reshape(m, d)
    tm = _row_tile(m, 1024)

    h = _rmsnorm(x2, wd["g_norm"], _row_tile(m, 512))
    wt = wd["w_in_t"]
    _, (r2, _), (r3, n3) = W_IN_SEGMENTS
    u1, small = _mm_ssd(h, wt, tm, 1024)
    uq = _mm_nt(h, wt, r2, D_MODEL, BF16, tm, 1024, "in_proj_fox_q")
    uk, k_new = _mm_nt_heads(h, wt, r2 + D_MODEL, tm, 1024, "in_proj_fox_k")
    uv, v_new = _mm_nt_heads(h, wt, r2 + 2 * D_MODEL, tm, 1024, "in_proj_fox_v")
    ug = _mm_nt(h, wt, r2 + 3 * D_MODEL, D_MODEL, BF16, tm, 1024, "in_proj_fox_gate")
    u3 = _mm_nt(h, wt, r3, n3, BF16, _row_tile(m, 2048), 1024, "in_proj_mem_gates")

    past_t = None if logf_past is None else jnp.transpose(logf_past, (0, 2, 1))
    logf_t, ct = _logf(small, wd["b_forget"], past_t, nbatch, t)
    ct = ct.reshape(nbatch, FOX_HEADS, 1, ct.shape[-1])

    conv0p = jnp.pad(conv0, ((0, 0), (8 - (SSD_CONV - 1), 0), (0, 0)))
    y_ssd, h_final, new_conv = _ssd(u1, small, conv0p, h0, wd["w_conv"], wd["b_conv"], wd["dt_bias"], wd["a_log"],
                                    wd["d_skip"], wd["g_ssd_out"], nbatch, t)

    if caches is None:
        y_fox = _fox_prompt(uq, uk, uv, ug, ct, nbatch, t)
        y_mem = _mem_attend(u3, mem_k, mem_v, nbatch, t)
    else:
        layer, fox_k, fox_v = caches
        y_fox = _fox_sample(uq, uk, uv, ug, fox_k, fox_v, layer, ct, nbatch, t)
        y_mem = _mem_attend(u3, mem_k, mem_v, nbatch, t, layer)
    merged = _merge(u3, y_ssd, y_fox, y_mem, wd["w_o_ssd"], wd["w_o_fox"], wd["w_o_mem"], tm, 512)
    y = _final(merged, wd["w_out"], x2, g_final, _row_tile(m, 512), last)

    return (y.reshape(nbatch, t, d), new_conv, h_final,
            k_new.reshape(nbatch, t, FOX_HEADS, FOX_HEAD_DIM), v_new.reshape(nbatch, t, FOX_HEADS, FOX_HEAD_DIM),
            jnp.transpose(logf_t, (0, 2, 1)))


def kernel(x_prompt, x_sample, mem_prompt, cache_fox_k, cache_fox_v, cache_fox_logf, state_ssd, state_ssd_conv,
           cache_mem_k, cache_mem_v, g_norm, w_in, w_conv, b_conv, dt_bias, a_log, d_skip, g_ssd_out, b_forget,
           g_mem, w_mem_kv, w_o_ssd, w_o_fox, w_o_mem, w_out, g_final):
    depth = w_in.shape[0]
    xp, xs = x_prompt, x_sample
    bp = xp.shape[0]
    n_mem = mem_prompt.shape[1]
    outs = [[] for _ in range(12)]
    for l in range(depth):
        wkv = _cast_bf16(w_mem_kv[l])
        wd = {
            "g_norm": g_norm[l],
            "w_in_t": _cast_bf16(jnp.transpose(w_in[l])),
            "w_conv": w_conv[l], "b_conv": b_conv[l], "dt_bias": dt_bias[l], "a_log": a_log[l], "d_skip": d_skip[l],
            "g_ssd_out": g_ssd_out[l], "b_forget": b_forget[l],
            "w_o_ssd": _cast_bf16(w_o_ssd[l]), "w_o_fox": _cast_bf16(w_o_fox[l]),
            "w_o_mem": _cast_bf16(w_o_mem[l]), "w_out": _cast_bf16(w_out[l]),
        }
        hm = _rmsnorm(mem_prompt.reshape(bp * n_mem, D_MODEL), g_mem[l], _row_tile(bp * n_mem, 256))
        tmm = _row_tile(bp * n_mem, 512)
        mk, mk_b = _mm_heads(hm, wkv, 0, MEM_HEADS, MEM_HEAD_DIM, tmm, "mem_k_proj")
        mv, mv_b = _mm_heads(hm, wkv, D_MODEL, MEM_HEADS, MEM_HEAD_DIM, tmm, "mem_v_proj")

        xp, c_p, h_p, k_p, v_p, lf_p = _layer(
            xp, jnp.zeros((bp, SSD_CONV - 1, SSD_CONV_CH), F32),
            jnp.zeros((bp, SSD_HEADS, SSD_HEAD_DIM, SSD_D_STATE), F32), None, None,
            mk_b.reshape(bp, n_mem, D_MODEL), mv_b.reshape(bp, n_mem, D_MODEL), wd, g_final, l == depth - 1)
        xs, c_s, h_s, k_s, v_s, lf_s = _layer(
            xs, state_ssd_conv[l], state_ssd[l], (l, cache_fox_k, cache_fox_v), cache_fox_logf[l],
            cache_mem_k, cache_mem_v, wd, g_final, l == depth - 1)
        for lst, val in zip(outs, (k_p, v_p, lf_p, h_p, c_p,
                                   mk.reshape(bp, n_mem, MEM_HEADS, MEM_HEAD_DIM),
                                   mv.reshape(bp, n_mem, MEM_HEADS, MEM_HEAD_DIM),
                                   k_s, v_s, lf_s, h_s, c_s)):
            lst.append(val)
    return (xp, xs) + tuple(jnp.stack(o) for o in outs)
```

```python
import functools

import numpy as np
import jax
import jax.numpy as jnp
from jax import lax
from jax.experimental import pallas as pl
from jax.experimental.pallas import tpu as pltpu

F32 = jnp.float32
BF16 = jnp.bfloat16

EPS = 1e-6
D_MODEL = 2048
SSD_HEAD_DIM = 64
SSD_HEADS = 32
SSD_GROUPS = 4
SSD_D_STATE = 128
SSD_CONV = 4
SSD_BC = SSD_GROUPS * SSD_D_STATE
SSD_CONV_CH = D_MODEL + 2 * SSD_BC
FOX_HEADS = 16
FOX_HEAD_DIM = 128
MEM_HEADS = 4
MEM_HEAD_DIM = 512
SSD_CHUNK = 64
SSD_NB = 2
NEG_BIG = -1e30

VMEM_LIMIT = 56 * 1024 * 1024

W_IN_SEGMENTS = ((0, D_MODEL + SSD_CONV_CH),
                 (D_MODEL + SSD_CONV_CH + SSD_HEADS, 4 * D_MODEL),
                 (D_MODEL + SSD_CONV_CH + SSD_HEADS + 4 * D_MODEL + FOX_HEADS, 5 * D_MODEL))
W_IN_DT = D_MODEL + SSD_CONV_CH
W_IN_FF = W_IN_SEGMENTS[1][0] + 4 * D_MODEL
U1_Z, U1_XS = 0, 1
U3_Q, U3_G, U3_GS, U3_GF, U3_GM = 0, 1, 2, 3, 4
LOG2E = 1.4426950408889634


def _params(*sem):
    return pltpu.CompilerParams(dimension_semantics=sem, vmem_limit_bytes=VMEM_LIMIT)


def _split3(x):
    hi = x.astype(BF16)
    r1 = x - hi.astype(F32)
    mid = r1.astype(BF16)
    lo = (r1 - mid.astype(F32)).astype(BF16)
    return hi, mid, lo


def _softplus(x):
    return jnp.maximum(x, 0.0) + jnp.log1p(jnp.exp(-jnp.abs(x)))


def _sigmoid(x):
    return 0.5 * jnp.tanh(0.5 * x) + 0.5


def _silu(x):
    return x * _sigmoid(x)


def _rmsnorm_kernel(x_ref, g_ref, o_ref):
    x = x_ref[...]
    ms = jnp.mean(x * x, axis=-1, keepdims=True)
    o_ref[...] = (x * lax.rsqrt(ms + EPS) * g_ref[...]).astype(o_ref.dtype)


def _rmsnorm(x, g, tm):
    m, d = x.shape
    return pl.pallas_call(
        _rmsnorm_kernel,
        out_shape=jax.ShapeDtypeStruct((m, d), BF16),
        grid=(m // tm,),
        in_specs=[pl.BlockSpec((tm, d), lambda i: (i, 0)), pl.BlockSpec((1, d), lambda i: (0, 0))],
        out_specs=pl.BlockSpec((tm, d), lambda i: (i, 0)),
        compiler_params=_params("parallel"),
        name="rmsnorm",
    )(x, g.reshape(1, d))


def _mm_kernel(h_ref, w_ref, o_ref):
    o_ref[...] = jnp.dot(h_ref[...], w_ref[...], preferred_element_type=F32).astype(o_ref.dtype)


def _mm(h, w, out_dtype, tm, tn, name, w_col0=0, n=None):
    m, k = h.shape
    n = w.shape[1] if n is None else n
    j0 = w_col0 // tn
    return pl.pallas_call(
        _mm_kernel,
        out_shape=jax.ShapeDtypeStruct((m, n), out_dtype),
        grid=(m // tm, n // tn),
        in_specs=[pl.BlockSpec((tm, k), lambda i, j: (i, 0)), pl.BlockSpec((k, tn), lambda i, j: (0, j0 + j))],
        out_specs=pl.BlockSpec((tm, tn), lambda i, j: (i, j)),
        compiler_params=_params("parallel", "arbitrary"),
        name=name,
    )(h, w)


def _mm_heads_kernel(h_ref, w_ref, o_ref, ob_ref):
    acc = jnp.dot(h_ref[...], w_ref[...], preferred_element_type=F32)
    o_ref[...] = acc.reshape(o_ref.shape)
    ob_ref[...] = acc.astype(ob_ref.dtype)


def _mm_heads(h, w, w_col0, heads, head_dim, tm, name):
    m, k = h.shape
    n = heads * head_dim
    return pl.pallas_call(
        _mm_heads_kernel,
        out_shape=(jax.ShapeDtypeStruct((m, heads, head_dim), F32), jax.ShapeDtypeStruct((m, n), BF16)),
        grid=(m // tm,),
        in_specs=[pl.BlockSpec((tm, k), lambda i: (i, 0)), pl.BlockSpec((k, n), lambda i: (0, w_col0 // n))],
        out_specs=(pl.BlockSpec((tm, heads, head_dim), lambda i: (i, 0, 0)), pl.BlockSpec((tm, n), lambda i: (i, 0))),
        compiler_params=_params("parallel"),
        name=name,
    )(h, w)


def _mm_nt_kernel(h_ref, wt_ref, o_ref):
    o_ref[...] = lax.dot_general(h_ref[...], wt_ref[...], NT_DIMS, preferred_element_type=F32).astype(o_ref.dtype)


def _wt_rows_spec(tn, k, row0, grid_rank):
    g = 16
    assert row0 % g == 0 and tn % g == 0
    if grid_rank == 1:
        return pl.BlockSpec((pl.Element(tn), pl.Element(k)), lambda i: (row0, 0))
    return pl.BlockSpec((pl.Element(tn), pl.Element(k)), lambda i, j: ((row0 // g + j * (tn // g)) * g, 0))


def _mm_nt(h, wt, row0, n, out_dtype, tm, tn, name):
    m, k = h.shape
    return pl.pallas_call(
        _mm_nt_kernel,
        out_shape=jax.ShapeDtypeStruct((m, n), out_dtype),
        grid=(m // tm, n // tn),
        in_specs=[pl.BlockSpec((tm, k), lambda i, j: (i, 0)), _wt_rows_spec(tn, k, row0, 2)],
        out_specs=pl.BlockSpec((tm, tn), lambda i, j: (i, j)),
        compiler_params=_params("parallel", "arbitrary"),
        name=name,
    )(h, wt)


def _mm_nt_norm_kernel(x_ref, g_ref, wt_ref, o_ref, h_ref, hn_ref):
    @pl.when(pl.program_id(1) == 0)
    def _norm():
        x = x_ref[...]
        ms = jnp.mean(x * x, axis=-1, keepdims=True)
        hn = (x * lax.rsqrt(ms + EPS) * g_ref[...]).astype(BF16)
        hn_ref[...] = hn
        h_ref[...] = hn

    o_ref[...] = lax.dot_general(hn_ref[...], wt_ref[...], NT_DIMS, preferred_element_type=F32).astype(o_ref.dtype)


def _mm_nt_norm(x, g, wt, row0, n, tm, tn, name):
    m, k = x.shape
    return pl.pallas_call(
        _mm_nt_norm_kernel,
        out_shape=(jax.ShapeDtypeStruct((m, n), BF16), jax.ShapeDtypeStruct((m, k), BF16)),
        grid=(m // tm, n // tn),
        in_specs=[pl.BlockSpec((tm, k), lambda i, j: (i, 0)), pl.BlockSpec((1, k), lambda i, j: (0, 0)),
                  _wt_rows_spec(tn, k, row0, 2)],
        out_specs=(pl.BlockSpec((tm, tn), lambda i, j: (i, j)), pl.BlockSpec((tm, k), lambda i, j: (i, 0))),
        scratch_shapes=[pltpu.VMEM((tm, k), BF16)],
        compiler_params=_params("parallel", "arbitrary"),
        name=name,
    )(x, g.reshape(1, k), wt)


def _mm_ssd_kernel(h_ref, wt_ref, wdt_ref, wff_ref, o_ref, small_ref):
    o_ref[...] = lax.dot_general(h_ref[...], wt_ref[...], NT_DIMS, preferred_element_type=F32).astype(o_ref.dtype)

    @pl.when(pl.program_id(1) == 0)
    def _narrow():
        w = jnp.concatenate([wdt_ref[...], wff_ref[...]], axis=0)
        small_ref[...] = lax.dot_general(h_ref[...], w, NT_DIMS, preferred_element_type=F32)


def _mm_ssd(h, wt, tm, tn):
    m, k = h.shape
    row0, n = W_IN_SEGMENTS[0]
    fixed = lambda r0: pl.BlockSpec((pl.Element(128), pl.Element(k)), lambda i, j: (r0, 0))
    return pl.pallas_call(
        _mm_ssd_kernel,
        out_shape=(jax.ShapeDtypeStruct((m, n), BF16), jax.ShapeDtypeStruct((m, 256), F32)),
        grid=(m // tm, n // tn),
        in_specs=[pl.BlockSpec((tm, k), lambda i, j: (i, 0)), _wt_rows_spec(tn, k, row0, 2),
                  fixed(W_IN_DT), fixed(W_IN_FF)],
        out_specs=(pl.BlockSpec((tm, tn), lambda i, j: (i, j)), pl.BlockSpec((tm, 256), lambda i, j: (i, 0))),
        compiler_params=_params("parallel", "arbitrary"),
        name="in_proj_ssd",
    )(h, wt, wt, wt)


def _mm_nt_heads_kernel(h_ref, wt_ref, o_ref, oh_ref):
    acc = lax.dot_general(h_ref[...], wt_ref[...], NT_DIMS, preferred_element_type=F32)
    o_ref[...] = acc.astype(o_ref.dtype)
    oh_ref[...] = acc.reshape(oh_ref.shape)


def _mm_nt_heads(h, wt, row0, tm, tn, name):
    m, k = h.shape
    hb = tn // FOX_HEAD_DIM
    return pl.pallas_call(
        _mm_nt_heads_kernel,
        out_shape=(jax.ShapeDtypeStruct((m, D_MODEL), BF16),
                   jax.ShapeDtypeStruct((m, FOX_HEADS, FOX_HEAD_DIM), F32)),
        grid=(m // tm, D_MODEL // tn),
        in_specs=[pl.BlockSpec((tm, k), lambda i, j: (i, 0)), _wt_rows_spec(tn, k, row0, 2)],
        out_specs=(pl.BlockSpec((tm, tn), lambda i, j: (i, j)),
                   pl.BlockSpec((tm, hb, FOX_HEAD_DIM), lambda i, j: (i, j, 0))),
        compiler_params=_params("parallel", "arbitrary"),
        name=name,
    )(h, wt)


def _cast_kernel(a_ref, o_ref):
    o_ref[...] = a_ref[...].astype(o_ref.dtype)


def _cast_bf16(w):
    rows, cols = w.shape
    tr = max(d for d in range(16, 1025, 16) if rows % d == 0)
    return pl.pallas_call(
        _cast_kernel,
        out_shape=jax.ShapeDtypeStruct((rows, cols), BF16),
        grid=(rows // tr,),
        in_specs=[pl.BlockSpec((tr, cols), lambda i: (i, 0))],
        out_specs=pl.BlockSpec((tr, cols), lambda i: (i, 0)),
        compiler_params=_params("parallel"),
        name="weight_cast",
    )(w)


def _upper3(tb):
    k = lax.broadcasted_iota(jnp.int32, (3 * tb, tb), 0) % tb
    t = lax.broadcasted_iota(jnp.int32, (3 * tb, tb), 1)
    return jnp.where(k <= t, 1.0, 0.0).astype(BF16)


def _cumsum_lanes(x, carry, tb):
    n = x.shape[1]
    u3 = _upper3(tb)
    out = []
    for s in range(0, n, tb):
        hi, mid, lo = _split3(x[:, s:s + tb])
        c = jnp.dot(jnp.concatenate([hi, mid, lo], axis=1), u3, preferred_element_type=F32) + carry
        carry = c[:, tb - 1:tb]
        out.append(c)
    return out, carry


def _logf_kernel(*refs, nb, t, p):
    if p:
        ffc_ref, bias_ref, past_ref, lf_ref, c_ref = refs
    else:
        ffc_ref, bias_ref, lf_ref, c_ref = refs
    fft = ffc_ref[...].T[0:FOX_HEADS, :]
    lf_all = -_softplus(-(fft + bias_ref[...]))
    for b in range(nb):
        lf = lf_all[:, b * t:(b + 1) * t]
        lf_ref[b] = lf
        carry = jnp.zeros((FOX_HEADS, 1), F32)
        if p:
            tbp = min(256, p)
            blocks, carry = _cumsum_lanes(past_ref[b], carry, tbp)
            for i, c in enumerate(blocks):
                c_ref[b, :, i * tbp:(i + 1) * tbp] = c
        tb = min(256, t)
        blocks, carry = _cumsum_lanes(lf, carry, tb)
        for i, c in enumerate(blocks):
            c_ref[b, :, p + i * tb:p + (i + 1) * tb] = c


def _logf(small, b_forget, past_t, nbatch, t):
    p = 0 if past_t is None else past_t.shape[2]
    nb = 1 if t % 128 == 0 else nbatch
    grid = (nbatch // nb,)
    in_specs = [pl.BlockSpec((nb * t, 128), lambda i: (i, 1)),
                pl.BlockSpec((FOX_HEADS, 1), lambda i: (0, 0))]
    args = [small, b_forget.reshape(FOX_HEADS, 1)]
    if p:
        in_specs.append(pl.BlockSpec((nb, FOX_HEADS, p), lambda i: (i, 0, 0)))
        args.append(past_t)
    return pl.pallas_call(
        functools.partial(_logf_kernel, nb=nb, t=t, p=p),
        out_shape=(jax.ShapeDtypeStruct((nbatch, FOX_HEADS, t), F32),
                   jax.ShapeDtypeStruct((nbatch, FOX_HEADS, p + t), F32)),
        grid=grid,
        in_specs=in_specs,
        out_specs=(pl.BlockSpec((nb, FOX_HEADS, t), lambda i: (i, 0, 0)),
                   pl.BlockSpec((nb, FOX_HEADS, p + t), lambda i: (i, 0, 0))),
        compiler_params=_params("parallel"),
        name="logf_cumsum",
    )(*args)


def _expand_matrix():
    r = np.arange(128)[:, None]
    c = np.arange(D_MODEL)[None, :]
    return jnp.asarray(((r < 96) & ((r % SSD_HEADS) == (c // SSD_HEAD_DIM))).astype(np.float32), dtype=BF16)


def _pack3(x):
    lane = lax.broadcasted_iota(jnp.int32, x.shape, 1)
    x = jnp.where(lane < SSD_HEADS, x, 0.0)
    hi, mid, lo = _split3(x)
    packed = hi.astype(F32) + pltpu.roll(mid.astype(F32), SSD_HEADS, 1) + pltpu.roll(lo.astype(F32), 2 * SSD_HEADS, 1)
    return packed.astype(BF16)


HIST = 16


def _shift_matrix(L):
    sh = np.zeros((SSD_CONV * L, 2 * HIST + L), np.float32)
    for d in range(SSD_CONV):
        for t in range(L):
            sh[d * L + t, 2 * HIST + t - d] = 1.0
            if t - d < 0:
                sh[d * L + t, HIST + t - d] = 1.0
    return jnp.asarray(sh, dtype=BF16)


def _ssd_kernel(xs_ref, b_ref, c_ref, z_ref, sm_ref, conv0_ref, h0_ref,
                wconv_ref, bconv_ref, dtb_ref, alog_ref, dskip_ref, gout_ref, e3_ref, sh_ref,
                y_ref, hfin_ref, convout_ref,
                st_ref, xq_ref):
    ci = pl.program_id(1)
    nb, L = xs_ref.shape[0], xs_ref.shape[1]
    P = SSD_HEAD_DIM
    W = D_MODEL
    gw = W // SSD_GROUPS
    hpm = 256 // L

    @pl.when(ci == 0)
    def _init():
        for bi in range(nb):
            st_ref[bi] = h0_ref[bi].T
            hist = jnp.concatenate([jnp.zeros((HIST - 8, SSD_CONV_CH), F32), conv0_ref[bi]], axis=0)
            hist_hi = hist.astype(BF16)
            xq_ref[bi, 0:HIST, :] = hist_hi
            xq_ref[bi, HIST:2 * HIST, :] = (hist - hist_hi.astype(F32)).astype(BF16)

    lane = lax.broadcasted_iota(jnp.int32, (L, 128), 1)
    kk = lax.broadcasted_iota(jnp.int32, (L, 3 * L), 1) % L
    ll = lax.broadcasted_iota(jnp.int32, (L, 3 * L), 0)
    tri3 = jnp.where(kk <= ll, 1.0, 0.0).astype(BF16)
    row = lax.broadcasted_iota(jnp.int32, (L, W), 0)
    sidx = lax.broadcasted_iota(jnp.int32, (L, W), 1) % L
    rb = lax.broadcasted_iota(jnp.int32, (256, hpm * P), 0) // L
    cb = lax.broadcasted_iota(jnp.int32, (256, hpm * P), 1) // P
    blockmask = rb == cb
    a_small = -jnp.exp(alog_ref[...])
    e3 = e3_ref[...]

    def chunk(bi):
        xq_ref[bi, 2 * HIST:, 0:W] = xs_ref[bi]
        xq_ref[bi, 2 * HIST:, W:W + SSD_BC] = b_ref[bi]
        xq_ref[bi, 2 * HIST:, W + SSD_BC:] = c_ref[bi]

        def conv(lo, hi):
            sh = jnp.dot(sh_ref[...], xq_ref[bi, :, lo:hi], preferred_element_type=F32)
            acc = bconv_ref[:, lo:hi]
            for d in range(SSD_CONV):
                acc = acc + sh[d * L:(d + 1) * L] * wconv_ref[SSD_CONV - 1 - d:SSD_CONV - d, lo:hi]
            return _silu(acc)

        xs = conv(0, W)
        bm = conv(W, W + SSD_BC).astype(BF16)
        cm = conv(W + SSD_BC, W + 2 * SSD_BC).astype(BF16)
        xq_ref[bi, 0:HIST, :] = jnp.zeros((HIST, SSD_CONV_CH), BF16)
        xq_ref[bi, HIST:2 * HIST, :] = xq_ref[bi, HIST + L:2 * HIST + L, :]

        dt = _softplus(sm_ref[bi] + dtb_ref[...])
        dta = jnp.where(lane < SSD_HEADS, dt * a_small, 0.0)
        hi, mid, lo = _split3(dta)
        acum = jnp.dot(tri3, jnp.concatenate([hi, mid, lo], axis=0), preferred_element_type=F32)
        a_x = jnp.dot(_pack3(acum), e3, preferred_element_type=F32)
        dt_x = jnp.dot(_pack3(dt), e3, preferred_element_type=F32)

        a_row = jnp.sum(jnp.where(row == sidx, a_x, 0.0), axis=0, keepdims=True)
        ldec = jnp.exp(jnp.where(row >= sidx, a_x - a_row, NEG_BIG))
        cbx = []
        for g in range(SSD_GROUPS):
            cg = cm[:, g * SSD_D_STATE:(g + 1) * SSD_D_STATE]
            bg = bm[:, g * SSD_D_STATE:(g + 1) * SSD_D_STATE]
            brep = jnp.concatenate([bg] * (SSD_HEADS // SSD_GROUPS), axis=0)
            cbx.append(lax.dot_general(cg, brep, NT_DIMS, preferred_element_type=F32))
        m = (jnp.concatenate(cbx, axis=1) * ldec).astype(BF16)

        xdt = xs * dt_x
        xdt_b = xdt.astype(BF16)
        y_parts = []
        for j in range(SSD_HEADS // hpm):
            xj = xdt_b[:, j * hpm * P:(j + 1) * hpm * P]
            bd = jnp.where(blockmask, jnp.concatenate([xj] * hpm, axis=0), jnp.zeros((), BF16))
            y_parts.append(jnp.dot(m[:, j * 256:(j + 1) * 256], bd, preferred_element_type=F32))
        y = jnp.concatenate(y_parts, axis=1)

        st = st_ref[bi]
        st_b = st.astype(BF16)
        yo = [jnp.dot(cm[:, g * SSD_D_STATE:(g + 1) * SSD_D_STATE], st_b[:, g * gw:(g + 1) * gw],
                      preferred_element_type=F32) for g in range(SSD_GROUPS)]
        y = y + jnp.concatenate(yo, axis=1) * jnp.exp(a_x)

        a_last = a_x[L - 1:L, :]
        xw = (xdt * jnp.exp(a_last - a_x)).astype(BF16)
        upd = [lax.dot_general(bm[:, g * SSD_D_STATE:(g + 1) * SSD_D_STATE], xw[:, g * gw:(g + 1) * gw],
                               (((0,), (0,)), ((), ())), preferred_element_type=F32) for g in range(SSD_GROUPS)]
        st_ref[bi] = st * jnp.exp(a_last) + jnp.concatenate(upd, axis=1)

        y = y + dskip_ref[...] * xs
        gz = y * _silu(z_ref[bi].astype(F32))
        ms = jnp.mean(gz * gz, axis=-1, keepdims=True)
        y_ref[bi] = (gz * lax.rsqrt(ms + EPS) * gout_ref[...]).astype(y_ref.dtype)

    for bi in range(nb):
        chunk(bi)

    @pl.when(ci == pl.num_programs(1) - 1)
    def _final():
        for bi in range(nb):
            hfin_ref[bi] = st_ref[bi].T
            convout_ref[bi] = xq_ref[bi, HIST:2 * HIST, :].astype(F32)[HIST - 8:, :]


def _ssd(u, small, conv0, h0, w_conv, b_conv, dt_bias, a_log, d_skip, g_ssd_out, nbatch, t):
    L = SSD_CHUNK
    nc = t // L
    nb = SSD_NB if nbatch % SSD_NB == 0 else 1
    pad128 = lambda v: jnp.pad(v.astype(F32), (0, 128 - v.shape[0])).reshape(1, 128)
    rep = lambda v: jnp.repeat(v.astype(F32), SSD_HEAD_DIM).reshape(1, D_MODEL)
    const2 = lambda shape: pl.BlockSpec(shape, lambda b, c: (0, 0))
    seq = lambda width, col: pl.BlockSpec((nb, L, width), lambda b, c: (b, c, col))
    per_seq = lambda rows, width: pl.BlockSpec((nb, rows, width), lambda b, c: (b, 0, 0))
    u3d = u.reshape(nbatch, t, u.shape[1])
    bc0 = (U1_XS + 1) * D_MODEL // SSD_BC
    y, hfin, convout = pl.pallas_call(
        _ssd_kernel,
        out_shape=(jax.ShapeDtypeStruct((nbatch, t, D_MODEL), BF16),
                   jax.ShapeDtypeStruct((nbatch, D_MODEL, SSD_D_STATE), F32),
                   jax.ShapeDtypeStruct((nbatch, 8, SSD_CONV_CH), F32)),
        grid=(nbatch // nb, nc),
        in_specs=[
            seq(D_MODEL, U1_XS), seq(SSD_BC, bc0), seq(SSD_BC, bc0 + 1), seq(D_MODEL, U1_Z), seq(128, 0),
            per_seq(8, SSD_CONV_CH), per_seq(D_MODEL, SSD_D_STATE),
            const2((SSD_CONV, SSD_CONV_CH)), const2((1, SSD_CONV_CH)),
            const2((1, 128)), const2((1, 128)), const2((1, D_MODEL)), const2((1, D_MODEL)),
            const2((128, D_MODEL)), const2((SSD_CONV * L, 2 * HIST + L)),
        ],
        out_specs=(seq(D_MODEL, 0), per_seq(D_MODEL, SSD_D_STATE), per_seq(8, SSD_CONV_CH)),
        scratch_shapes=[pltpu.VMEM((nb, SSD_D_STATE, D_MODEL), F32),
                        pltpu.VMEM((nb, 2 * HIST + L, SSD_CONV_CH), BF16)],
        compiler_params=_params("parallel", "arbitrary"),
        name="ssd_scan",
    )(u3d, u3d, u3d, u3d, small.reshape(nbatch, t, small.shape[1]), conv0, h0.reshape(nbatch, D_MODEL, SSD_D_STATE),
      w_conv, b_conv.reshape(1, SSD_CONV_CH), pad128(dt_bias), pad128(a_log), rep(d_skip),
      g_ssd_out.reshape(1, D_MODEL), _expand_matrix(), _shift_matrix(L))
    return (y.reshape(nbatch * t, D_MODEL), hfin.reshape(nbatch, SSD_HEADS, SSD_HEAD_DIM, SSD_D_STATE),
            convout[:, 8 - (SSD_CONV - 1):])


def _col_from_row(row_vals):
    n = row_vals.shape[1]
    r = lax.broadcasted_iota(jnp.int32, (n, n), 0)
    c = lax.broadcasted_iota(jnp.int32, (n, n), 1)
    return jnp.sum(jnp.where(r == c, row_vals, 0.0), axis=1, keepdims=True)


FOX_PAIR = 4
NT_DIMS = (((1,), (1,)), ((), ()))
FOX_TQ = 512


def _fox_prompt_kernel(q_ref, k_ref, v_ref, fg_ref, c_ref, o_ref, va_ref, *, tq):
    qi = pl.program_id(2)
    hd = FOX_HEAD_DIM
    t = k_ref.shape[0]
    c1 = hd ** -0.5 * LOG2E
    q0 = pl.multiple_of(qi * tq, tq)
    lanes = [slice(j * hd, (j + 1) * hd) for j in range(FOX_PAIR)]

    @pl.when(qi == 0)
    def _stage_values():
        for j in range(FOX_PAIR):
            va_ref[j, :, 0:hd] = v_ref[:, lanes[j]]
            va_ref[j, :, hd:2 * hd] = jnp.ones((t, hd), BF16)

    cq2_rep = [jnp.broadcast_to(c_ref[j, :, pl.ds(q0, tq)], (hd, tq)).T * LOG2E for j in range(FOX_PAIR)]
    cq2 = [cr[:, 0:1] for cr in cq2_rep]
    row = lax.broadcasted_iota(jnp.int32, (tq, tq), 0)
    col = lax.broadcasted_iota(jnp.int32, (tq, tq), 1)

    def scores(j, k0, masked):
        s = lax.dot_general(q_ref[:, lanes[j]], k_ref[pl.ds(k0, tq), lanes[j]], NT_DIMS, preferred_element_type=F32)
        t2 = s * c1 - c_ref[j, :, pl.ds(k0, tq)] * LOG2E
        return jnp.where(col <= row, t2, NEG_BIG) if masked else t2

    shift = []
    ones = jnp.ones((hd, hd), BF16)
    for j in range(FOX_PAIR):
        qk = q_ref[:, lanes[j]] * k_ref[pl.ds(q0, tq), lanes[j]]
        shift.append(jnp.dot(qk, ones, preferred_element_type=F32) * c1 - cq2_rep[j])

    def fast_tile(k0, accs, masked):
        out = []
        for j in range(FOX_PAIR):
            t2 = scores(j, k0, masked)
            e = jnp.concatenate([t2[:, i:i + hd] - shift[j] for i in range(0, tq, hd)], axis=1)
            out.append(accs[j] + jnp.dot(jnp.exp2(e).astype(BF16), va_ref[j, pl.ds(k0, tq), :],
                                         preferred_element_type=F32))
        return tuple(out)

    def fast_part(j, r0, nr, kk, nk):
        s = lax.dot_general(q_ref[r0:r0 + nr, lanes[j]], k_ref[pl.ds(q0 + kk, nk), lanes[j]], NT_DIMS,
                            preferred_element_type=F32)
        t2 = s * c1 - c_ref[j, :, pl.ds(q0 + kk, nk)] * LOG2E
        rr = lax.broadcasted_iota(jnp.int32, (nr, nk), 0) + r0
        cc = lax.broadcasted_iota(jnp.int32, (nr, nk), 1) + kk
        t2 = jnp.where(cc <= rr, t2, NEG_BIG)
        e = jnp.concatenate([t2[:, i:i + hd] - shift[j][r0:r0 + nr] for i in range(0, nk, hd)], axis=1)
        return jnp.dot(jnp.exp2(e).astype(BF16), va_ref[j, pl.ds(q0 + kk, nk), :], preferred_element_type=F32)

    def fast_diag(accs):
        half = tq // 2
        out = []
        for j in range(FOX_PAIR):
            left = fast_part(j, 0, tq, 0, half)
            right = fast_part(j, half, half, half, half)
            out.append(accs[j] + left + jnp.concatenate([jnp.zeros((half, 2 * hd), F32), right], axis=0))
        return tuple(out)

    accs = tuple(jnp.zeros((tq, 2 * hd), F32) for _ in range(FOX_PAIR))
    accs = lax.fori_loop(0, qi, lambda i, a: fast_tile(pl.multiple_of(i * tq, tq), a, False), accs)
    accs = fast_diag(accs)
    bad = jnp.float32(0.0)
    for j in range(FOX_PAIR):
        num, den = accs[j][:, 0:hd], accs[j][:, hd:2 * hd]
        o_ref[:, lanes[j]] = (num / den * _silu(fg_ref[:, lanes[j]].astype(F32))).astype(o_ref.dtype)
        bad = jnp.maximum(bad, jnp.max(jnp.where(jnp.isfinite(accs[j]), 0.0, 1.0)))

    @pl.when(bad > 0.0)
    def _running_max():
        def tile(k0, carry, masked):
            out = []
            for j in range(FOX_PAIR):
                m_i, l_i, acc = carry[j]
                t2 = scores(j, k0, masked)
                m_new = jnp.maximum(m_i, jnp.max(t2, axis=1, keepdims=True) + cq2[j])
                p = jnp.exp2(t2 - (m_new - cq2[j]))
                alpha = jnp.exp2(m_i - m_new)
                l_new = alpha * l_i + jnp.sum(p, axis=1, keepdims=True)
                acc = alpha * acc + jnp.dot(p.astype(BF16), v_ref[pl.ds(k0, tq), lanes[j]],
                                            preferred_element_type=F32)
                out.append((m_new, l_new, acc))
            return tuple(out)

        init = tuple((jnp.full((tq, 1), NEG_BIG, F32), jnp.zeros((tq, 1), F32), jnp.zeros((tq, hd), F32))
                     for _ in range(FOX_PAIR))
        carry = lax.fori_loop(0, qi, lambda i, cr: tile(pl.multiple_of(i * tq, tq), cr, False), init)
        carry = tile(q0, carry, True)
        for j in range(FOX_PAIR):
            _, l_i, acc = carry[j]
            o_ref[:, lanes[j]] = (acc / l_i * _silu(fg_ref[:, lanes[j]].astype(F32))).astype(o_ref.dtype)


def _fox_prompt(uq, uk, uv, ug, ct, nbatch, t):
    tq = min(FOX_TQ, t)
    nq = t // tq
    w = FOX_PAIR * FOX_HEAD_DIM
    nb = D_MODEL // w
    return pl.pallas_call(
        functools.partial(_fox_prompt_kernel, tq=tq),
        out_shape=jax.ShapeDtypeStruct((nbatch * t, D_MODEL), BF16),
        grid=(nbatch, nb, nq),
        in_specs=[
            pl.BlockSpec((tq, w), lambda b, h, i: (b * nq + i, h)),
            pl.BlockSpec((t, w), lambda b, h, i: (b, h)),
            pl.BlockSpec((t, w), lambda b, h, i: (b, h)),
            pl.BlockSpec((tq, w), lambda b, h, i: (b * nq + i, h)),
            pl.BlockSpec((None, FOX_PAIR, 1, t), lambda b, h, i: (b, h, 0, 0)),
        ],
        out_specs=pl.BlockSpec((tq, w), lambda b, h, i: (b * nq + i, h)),
        scratch_shapes=[pltpu.VMEM((FOX_PAIR, t, 2 * FOX_HEAD_DIM), BF16)],
        compiler_params=_params("parallel", "parallel", "arbitrary"),
        name="fox_prompt",
    )(uq, uk, uv, ug, ct)


def _fox_sample_kernel(q_ref, kp_ref, vp_ref, kn_ref, vn_ref, fg_ref, c_ref, o_ref, *, p, t):
    c1 = FOX_HEAD_DIM ** -0.5 * LOG2E
    kp = kp_ref[...].reshape(p, D_MODEL).astype(BF16)
    vp = vp_ref[...].reshape(p, D_MODEL).astype(BF16)
    r = lax.broadcasted_iota(jnp.int32, (t, t), 0)
    c = lax.broadcasted_iota(jnp.int32, (t, t), 1)
    for j in range(FOX_HEADS):
        sl = slice(j * FOX_HEAD_DIM, (j + 1) * FOX_HEAD_DIM)
        q = q_ref[:, sl]
        cq2 = _col_from_row(c_ref[j, :, p:p + t]) * LOG2E
        s_p = lax.dot_general(q, kp[:, sl], NT_DIMS, preferred_element_type=F32) * c1 - c_ref[j, :, 0:p] * LOG2E
        s_n = (lax.dot_general(q, kn_ref[:, sl], NT_DIMS, preferred_element_type=F32) * c1
               - c_ref[j, :, p:p + t] * LOG2E)
        s_n = jnp.where(c <= r, s_n, NEG_BIG)
        m = jnp.maximum(jnp.max(s_p, axis=1, keepdims=True), jnp.max(s_n, axis=1, keepdims=True)) + cq2
        e_p = jnp.exp2(s_p - (m - cq2))
        e_n = jnp.exp2(s_n - (m - cq2))
        inv = 1.0 / (jnp.sum(e_p, axis=1, keepdims=True) + jnp.sum(e_n, axis=1, keepdims=True))
        o = (jnp.dot((e_p * inv).astype(BF16), vp[:, sl], preferred_element_type=F32)
             + jnp.dot((e_n * inv).astype(BF16), vn_ref[:, sl], preferred_element_type=F32))
        o_ref[:, sl] = (o * _silu(fg_ref[:, sl].astype(F32))).astype(o_ref.dtype)


def _fox_sample(uq, uk, uv, ug, k_cache, v_cache, layer, ct, nbatch, t):
    p = k_cache.shape[2]
    past = pl.BlockSpec((None, None, p, FOX_HEADS, FOX_HEAD_DIM), lambda b: (layer, b, 0, 0, 0))
    col = pl.BlockSpec((t, D_MODEL), lambda b: (b, 0))
    return pl.pallas_call(
        functools.partial(_fox_sample_kernel, p=p, t=t),
        out_shape=jax.ShapeDtypeStruct((nbatch * t, D_MODEL), BF16),
        grid=(nbatch,),
        in_specs=[col, past, past, col, col, col,
                  pl.BlockSpec((None, FOX_HEADS, 1, p + t), lambda b: (b, 0, 0, 0))],
        out_specs=pl.BlockSpec((t, D_MODEL), lambda b: (b, 0)),
        compiler_params=_params("parallel"),
        name="fox_sample",
    )(uq, k_cache, v_cache, uk, uv, ug, ct)


def _mem_kernel(q_ref, g_ref, k_ref, v_ref, o_ref):
    scale = MEM_HEAD_DIM ** -0.5
    nm = k_ref.shape[0]
    k = k_ref[...].reshape(nm, D_MODEL).astype(BF16)
    v = v_ref[...].reshape(nm, D_MODEL).astype(BF16)
    for h in range(MEM_HEADS):
        sl = slice(h * MEM_HEAD_DIM, (h + 1) * MEM_HEAD_DIM)
        s = lax.dot_general(q_ref[:, sl], k[:, sl], NT_DIMS, preferred_element_type=F32) * scale
        e = jnp.exp(s - jnp.max(s, axis=1, keepdims=True))
        p = e * (1.0 / jnp.sum(e, axis=1, keepdims=True))
        o = jnp.dot(p.astype(BF16), v[:, sl], preferred_element_type=F32)
        o_ref[:, sl] = (o * _silu(g_ref[:, sl].astype(F32))).astype(o_ref.dtype)


def _mem_attend(u3, mk, mv, nbatch, t, layer=None):
    tq = min(512, t)
    nq = t // tq
    if layer is None:
        nm = mk.shape[1]
        kv = pl.BlockSpec((None, nm, D_MODEL), lambda b, i: (b, 0, 0))
    else:
        nm = mk.shape[2]
        kv = pl.BlockSpec((None, None, nm, MEM_HEADS, MEM_HEAD_DIM), lambda b, i: (layer, b, 0, 0, 0))
    return pl.pallas_call(
        _mem_kernel,
        out_shape=jax.ShapeDtypeStruct((nbatch * t, D_MODEL), BF16),
        grid=(nbatch, nq),
        in_specs=[pl.BlockSpec((tq, D_MODEL), lambda b, i: (b * nq + i, U3_Q)),
                  pl.BlockSpec((tq, D_MODEL), lambda b, i: (b * nq + i, U3_G)), kv, kv],
        out_specs=pl.BlockSpec((tq, D_MODEL), lambda b, i: (b * nq + i, 0)),
        compiler_params=_params("parallel", "arbitrary"),
        name="mem_attend",
    )(u3, u3, mk, mv)


def _merge_kernel(ys_ref, yf_ref, ym_ref, ws_ref, wf_ref, wm_ref, gs_ref, gf_ref, gm_ref, o_ref):
    def branch(y_ref, w_ref, g_ref):
        return _sigmoid(g_ref[...].astype(F32)) * jnp.dot(y_ref[...], w_ref[...], preferred_element_type=F32)
    o_ref[...] = (branch(ys_ref, ws_ref, gs_ref) + branch(yf_ref, wf_ref, gf_ref)
                  + branch(ym_ref, wm_ref, gm_ref)).astype(o_ref.dtype)


def _merge(u, y_ssd, y_fox, y_mem, w_s, w_f, w_m, tm, tn):
    m = y_ssd.shape[0]
    nj = D_MODEL // tn
    yspec = pl.BlockSpec((tm, D_MODEL), lambda i, j: (i, 0))
    wspec = pl.BlockSpec((D_MODEL, tn), lambda i, j: (0, j))
    gspec = lambda col: pl.BlockSpec((tm, tn), lambda i, j: (i, col * nj + j))
    return pl.pallas_call(
        _merge_kernel,
        out_shape=jax.ShapeDtypeStruct((m, D_MODEL), BF16),
        grid=(m // tm, nj),
        in_specs=[yspec, yspec, yspec, wspec, wspec, wspec, gspec(U3_GS), gspec(U3_GF), gspec(U3_GM)],
        out_specs=pl.BlockSpec((tm, tn), lambda i, j: (i, j)),
        compiler_params=_params("parallel", "arbitrary"),
        name="gated_merge",
    )(y_ssd, y_fox, y_mem, w_s, w_f, w_m, u, u, u)


def _final_kernel(mg_ref, w_ref, x_ref, g_ref, o_ref, *, normalize):
    xo = x_ref[...] + jnp.dot(mg_ref[...], w_ref[...], preferred_element_type=F32)
    if normalize:
        ms = jnp.mean(xo * xo, axis=-1, keepdims=True)
        xo = xo * lax.rsqrt(ms + EPS) * g_ref[...]
    o_ref[...] = xo


def _final(merged, w_out, x, g_final, tm, normalize):
    m = x.shape[0]
    return pl.pallas_call(
        functools.partial(_final_kernel, normalize=normalize),
        out_shape=jax.ShapeDtypeStruct((m, D_MODEL), F32),
        grid=(m // tm,),
        in_specs=[pl.BlockSpec((tm, D_MODEL), lambda i: (i, 0)),
                  pl.BlockSpec((D_MODEL, D_MODEL), lambda i: (0, 0)),
                  pl.BlockSpec((tm, D_MODEL), lambda i: (i, 0)),
                  pl.BlockSpec((1, D_MODEL), lambda i: (0, 0))],
        out_specs=pl.BlockSpec((tm, D_MODEL), lambda i: (i, 0)),
        compiler_params=_params("parallel"),
        name="out_proj_norm",
    )(merged, w_out, x, g_final.reshape(1, D_MODEL))


def _row_tile(m, pref):
    t = pref
    while m % t:
        t //= 2
    return t


def _layer(x, conv0, h0, caches, logf_past, mem_k, mem_v, wd, g_final, last):
    nbatch, t, d = x.shape
    m = nbatch * t
    x2 = x.reshape(m, d)
    tm = _row_tile(m, 1024)

    wt = wd["w_in_t"]
    _, (r2, _), (r3, n3) = W_IN_SEGMENTS
    tm2 = _row_tile(m, 2048)
    uq, h = _mm_nt_norm(x2, wd["g_norm"], wt, r2, D_MODEL, tm, 1024, "norm_in_proj_fox_q")
    u1, small = _mm_ssd(h, wt, tm2, 1024)
    uk, k_new = _mm_nt_heads(h, wt, r2 + D_MODEL, tm, 1024, "in_proj_fox_k")
    uv, v_new = _mm_nt_heads(h, wt, r2 + 2 * D_MODEL, tm, 1024, "in_proj_fox_v")
    ug = _mm_nt(h, wt, r2 + 3 * D_MODEL, D_MODEL, BF16, tm2, 1024, "in_proj_fox_gate")
    u3 = _mm_nt(h, wt, r3, n3, BF16, tm2, 1024, "in_proj_mem_gates")

    past_t = None if logf_past is None else jnp.transpose(logf_past, (0, 2, 1))
    logf_t, ct = _logf(small, wd["b_forget"], past_t, nbatch, t)
    ct = ct.reshape(nbatch, FOX_HEADS, 1, ct.shape[-1])

    conv0p = jnp.pad(conv0, ((0, 0), (8 - (SSD_CONV - 1), 0), (0, 0)))
    y_ssd, h_final, new_conv = _ssd(u1, small, conv0p, h0, wd["w_conv"], wd["b_conv"], wd["dt_bias"], wd["a_log"],
                                    wd["d_skip"], wd["g_ssd_out"], nbatch, t)

    if caches is None:
        y_fox = _fox_prompt(uq, uk, uv, ug, ct, nbatch, t)
        y_mem = _mem_attend(u3, mem_k, mem_v, nbatch, t)
    else:
        layer, fox_k, fox_v = caches
        y_fox = _fox_sample(uq, uk, uv, ug, fox_k, fox_v, layer, ct, nbatch, t)
        y_mem = _mem_attend(u3, mem_k, mem_v, nbatch, t, layer)
    merged = _merge(u3, y_ssd, y_fox, y_mem, wd["w_o_ssd"], wd["w_o_fox"], wd["w_o_mem"], tm, 512)
    y = _final(merged, wd["w_out"], x2, g_final, _row_tile(m, 512), last)

    return (y.reshape(nbatch, t, d), new_conv, h_final,
            k_new.reshape(nbatch, t, FOX_HEADS, FOX_HEAD_DIM), v_new.reshape(nbatch, t, FOX_HEADS, FOX_HEAD_DIM),
            jnp.transpose(logf_t, (0, 2, 1)))


def kernel(x_prompt, x_sample, mem_prompt, cache_fox_k, cache_fox_v, cache_fox_logf, state_ssd, state_ssd_conv,
           cache_mem_k, cache_mem_v, g_norm, w_in, w_conv, b_conv, dt_bias, a_log, d_skip, g_ssd_out, b_forget,
           g_mem, w_mem_kv, w_o_ssd, w_o_fox, w_o_mem, w_out, g_final):
    depth = w_in.shape[0]
    xp, xs = x_prompt, x_sample
    bp = xp.shape[0]
    n_mem = mem_prompt.shape[1]
    outs = [[] for _ in range(12)]
    for l in range(depth):
        wkv = _cast_bf16(w_mem_kv[l])
        wd = {
            "g_norm": g_norm[l],
            "w_in_t": _cast_bf16(jnp.transpose(w_in[l])),
            "w_conv": w_conv[l], "b_conv": b_conv[l], "dt_bias": dt_bias[l], "a_log": a_log[l], "d_skip": d_skip[l],
            "g_ssd_out": g_ssd_out[l], "b_forget": b_forget[l],
            "w_o_ssd": _cast_bf16(w_o_ssd[l]), "w_o_fox": _cast_bf16(w_o_fox[l]),
            "w_o_mem": _cast_bf16(w_o_mem[l]), "w_out": _cast_bf16(w_out[l]),
        }
        hm = _rmsnorm(mem_prompt.reshape(bp * n_mem, D_MODEL), g_mem[l], _row_tile(bp * n_mem, 256))
        tmm = _row_tile(bp * n_mem, 512)
        mk, mk_b = _mm_heads(hm, wkv, 0, MEM_HEADS, MEM_HEAD_DIM, tmm, "mem_k_proj")
        mv, mv_b = _mm_heads(hm, wkv, D_MODEL, MEM_HEADS, MEM_HEAD_DIM, tmm, "mem_v_proj")

        xp, c_p, h_p, k_p, v_p, lf_p = _layer(
            xp, jnp.zeros((bp, SSD_CONV - 1, SSD_CONV_CH), F32),
            jnp.zeros((bp, SSD_HEADS, SSD_HEAD_DIM, SSD_D_STATE), F32), None, None,
            mk_b.reshape(bp, n_mem, D_MODEL), mv_b.reshape(bp, n_mem, D_MODEL), wd, g_final, l == depth - 1)
        xs, c_s, h_s, k_s, v_s, lf_s = _layer(
            xs, state_ssd_conv[l], state_ssd[l], (l, cache_fox_k, cache_fox_v), cache_fox_logf[l],
            cache_mem_k, cache_mem_v, wd, g_final, l == depth - 1)
        for lst, val in zip(outs, (k_p, v_p, lf_p, h_p, c_p,
                                   mk.reshape(bp, n_mem, MEM_HEADS, MEM_HEAD_DIM),
                                   mv.reshape(bp, n_mem, MEM_HEADS, MEM_HEAD_DIM),
                                   k_s, v_s, lf_s, h_s, c_s)):
            lst.append(val)
    return (xp, xs) + tuple(jnp.stack(o) for o in outs)
```

```python
import functools

import numpy as np
import jax
import jax.numpy as jnp
from jax import lax
from jax.experimental import pallas as pl
from jax.experimental.pallas import tpu as pltpu

F32 = jnp.float32
BF16 = jnp.bfloat16

EPS = 1e-6
D_MODEL = 2048
SSD_HEAD_DIM = 64
SSD_HEADS = 32
SSD_GROUPS = 4
SSD_D_STATE = 128
SSD_CONV = 4
SSD_BC = SSD_GROUPS * SSD_D_STATE
SSD_CONV_CH = D_MODEL + 2 * SSD_BC
FOX_HEADS = 16
FOX_HEAD_DIM = 128
MEM_HEADS = 4
MEM_HEAD_DIM = 512
SSD_CHUNK = 64
SSD_NB = 2
NEG_BIG = -1e30

VMEM_LIMIT = 56 * 1024 * 1024

W_IN_SEGMENTS = ((0, D_MODEL + SSD_CONV_CH),
                 (D_MODEL + SSD_CONV_CH + SSD_HEADS, 4 * D_MODEL),
                 (D_MODEL + SSD_CONV_CH + SSD_HEADS + 4 * D_MODEL + FOX_HEADS, 5 * D_MODEL))
W_IN_DT = D_MODEL + SSD_CONV_CH
W_IN_FF = W_IN_SEGMENTS[1][0] + 4 * D_MODEL
U1_Z, U1_XS = 0, 1
U3_Q, U3_G, U3_GS, U3_GF, U3_GM = 0, 1, 2, 3, 4
LOG2E = 1.4426950408889634


def _params(*sem):
    return pltpu.CompilerParams(dimension_semantics=sem, vmem_limit_bytes=VMEM_LIMIT)


def _split3(x):
    hi = x.astype(BF16)
    r1 = x - hi.astype(F32)
    mid = r1.astype(BF16)
    lo = (r1 - mid.astype(F32)).astype(BF16)
    return hi, mid, lo


def _softplus(x):
    return jnp.maximum(x, 0.0) + jnp.log1p(jnp.exp(-jnp.abs(x)))


def _sigmoid(x):
    return 0.5 * jnp.tanh(0.5 * x) + 0.5


def _silu(x):
    return x * _sigmoid(x)


def _rmsnorm_kernel(x_ref, g_ref, o_ref):
    x = x_ref[...]
    ms = jnp.mean(x * x, axis=-1, keepdims=True)
    o_ref[...] = (x * lax.rsqrt(ms + EPS) * g_ref[...]).astype(o_ref.dtype)


def _rmsnorm(x, g, tm):
    m, d = x.shape
    return pl.pallas_call(
        _rmsnorm_kernel,
        out_shape=jax.ShapeDtypeStruct((m, d), BF16),
        grid=(m // tm,),
        in_specs=[pl.BlockSpec((tm, d), lambda i: (i, 0)), pl.BlockSpec((1, d), lambda i: (0, 0))],
        out_specs=pl.BlockSpec((tm, d), lambda i: (i, 0)),
        compiler_params=_params("parallel"),
        name="rmsnorm",
    )(x, g.reshape(1, d))


def _mm_kernel(h_ref, w_ref, o_ref):
    o_ref[...] = jnp.dot(h_ref[...], w_ref[...], preferred_element_type=F32).astype(o_ref.dtype)


def _mm(h, w, out_dtype, tm, tn, name, w_col0=0, n=None):
    m, k = h.shape
    n = w.shape[1] if n is None else n
    j0 = w_col0 // tn
    return pl.pallas_call(
        _mm_kernel,
        out_shape=jax.ShapeDtypeStruct((m, n), out_dtype),
        grid=(m // tm, n // tn),
        in_specs=[pl.BlockSpec((tm, k), lambda i, j: (i, 0)), pl.BlockSpec((k, tn), lambda i, j: (0, j0 + j))],
        out_specs=pl.BlockSpec((tm, tn), lambda i, j: (i, j)),
        compiler_params=_params("parallel", "arbitrary"),
        name=name,
    )(h, w)


def _mm_heads_kernel(h_ref, w_ref, o_ref, ob_ref):
    acc = jnp.dot(h_ref[...], w_ref[...], preferred_element_type=F32)
    o_ref[...] = acc.reshape(o_ref.shape)
    ob_ref[...] = acc.astype(ob_ref.dtype)


def _mm_heads(h, w, w_col0, heads, head_dim, tm, name):
    m, k = h.shape
    n = heads * head_dim
    return pl.pallas_call(
        _mm_heads_kernel,
        out_shape=(jax.ShapeDtypeStruct((m, heads, head_dim), F32), jax.ShapeDtypeStruct((m, n), BF16)),
        grid=(m // tm,),
        in_specs=[pl.BlockSpec((tm, k), lambda i: (i, 0)), pl.BlockSpec((k, n), lambda i: (0, w_col0 // n))],
        out_specs=(pl.BlockSpec((tm, heads, head_dim), lambda i: (i, 0, 0)), pl.BlockSpec((tm, n), lambda i: (i, 0))),
        compiler_params=_params("parallel"),
        name=name,
    )(h, w)


def _mm_nt_kernel(h_ref, wt_ref, o_ref):
    o_ref[...] = lax.dot_general(h_ref[...], wt_ref[...], NT_DIMS, preferred_element_type=F32).astype(o_ref.dtype)


def _wt_rows_spec(tn, k, row0, grid_rank):
    g = 16
    assert row0 % g == 0 and tn % g == 0
    if grid_rank == 1:
        return pl.BlockSpec((pl.Element(tn), pl.Element(k)), lambda i: (row0, 0))
    return pl.BlockSpec((pl.Element(tn), pl.Element(k)), lambda i, j: ((row0 // g + j * (tn // g)) * g, 0))


def _mm_nt(h, wt, row0, n, out_dtype, tm, tn, name):
    m, k = h.shape
    return pl.pallas_call(
        _mm_nt_kernel,
        out_shape=jax.ShapeDtypeStruct((m, n), out_dtype),
        grid=(m // tm, n // tn),
        in_specs=[pl.BlockSpec((tm, k), lambda i, j: (i, 0)), _wt_rows_spec(tn, k, row0, 2)],
        out_specs=pl.BlockSpec((tm, tn), lambda i, j: (i, j)),
        compiler_params=_params("parallel", "arbitrary"),
        name=name,
    )(h, wt)


def _mm_nt_norm_kernel(x_ref, g_ref, wt_ref, o_ref, h_ref):
    x = x_ref[...]
    ms = jnp.mean(x * x, axis=-1, keepdims=True)
    hn = (x * lax.rsqrt(ms + EPS) * g_ref[...]).astype(BF16)
    h_ref[...] = hn
    o_ref[...] = lax.dot_general(hn, wt_ref[...], NT_DIMS, preferred_element_type=F32).astype(o_ref.dtype)


def _mm_nt_norm(x, g, wt, row0, n, tm, tn, name):
    m, k = x.shape
    return pl.pallas_call(
        _mm_nt_norm_kernel,
        out_shape=(jax.ShapeDtypeStruct((m, n), BF16), jax.ShapeDtypeStruct((m, k), BF16)),
        grid=(m // tm, n // tn),
        in_specs=[pl.BlockSpec((tm, k), lambda i, j: (i, 0)), pl.BlockSpec((1, k), lambda i, j: (0, 0)),
                  _wt_rows_spec(tn, k, row0, 2)],
        out_specs=(pl.BlockSpec((tm, tn), lambda i, j: (i, j)), pl.BlockSpec((tm, k), lambda i, j: (i, 0))),
        compiler_params=_params("parallel", "arbitrary"),
        name=name,
    )(x, g.reshape(1, k), wt)


def _mm_ssd_kernel(h_ref, wt_ref, wdt_ref, wff_ref, o_ref, small_ref):
    o_ref[...] = lax.dot_general(h_ref[...], wt_ref[...], NT_DIMS, preferred_element_type=F32).astype(o_ref.dtype)

    @pl.when(pl.program_id(1) == 0)
    def _narrow():
        w = jnp.concatenate([wdt_ref[...], wff_ref[...]], axis=0)
        small_ref[...] = lax.dot_general(h_ref[...], w, NT_DIMS, preferred_element_type=F32)


def _mm_ssd(h, wt, tm, tn):
    m, k = h.shape
    row0, n = W_IN_SEGMENTS[0]
    fixed = lambda r0: pl.BlockSpec((pl.Element(128), pl.Element(k)), lambda i, j: (r0, 0))
    return pl.pallas_call(
        _mm_ssd_kernel,
        out_shape=(jax.ShapeDtypeStruct((m, n), BF16), jax.ShapeDtypeStruct((m, 256), F32)),
        grid=(m // tm, n // tn),
        in_specs=[pl.BlockSpec((tm, k), lambda i, j: (i, 0)), _wt_rows_spec(tn, k, row0, 2),
                  fixed(W_IN_DT), fixed(W_IN_FF)],
        out_specs=(pl.BlockSpec((tm, tn), lambda i, j: (i, j)), pl.BlockSpec((tm, 256), lambda i, j: (i, 0))),
        compiler_params=_params("parallel", "arbitrary"),
        name="in_proj_ssd",
    )(h, wt, wt, wt)


def _mm_nt_heads_kernel(h_ref, wt_ref, o_ref, oh_ref):
    acc = lax.dot_general(h_ref[...], wt_ref[...], NT_DIMS, preferred_element_type=F32)
    o_ref[...] = acc.astype(o_ref.dtype)
    oh_ref[...] = acc.reshape(oh_ref.shape)


def _mm_nt_heads(h, wt, row0, tm, tn, name):
    m, k = h.shape
    hb = tn // FOX_HEAD_DIM
    return pl.pallas_call(
        _mm_nt_heads_kernel,
        out_shape=(jax.ShapeDtypeStruct((m, D_MODEL), BF16),
                   jax.ShapeDtypeStruct((m, FOX_HEADS, FOX_HEAD_DIM), F32)),
        grid=(m // tm, D_MODEL // tn),
        in_specs=[pl.BlockSpec((tm, k), lambda i, j: (i, 0)), _wt_rows_spec(tn, k, row0, 2)],
        out_specs=(pl.BlockSpec((tm, tn), lambda i, j: (i, j)),
                   pl.BlockSpec((tm, hb, FOX_HEAD_DIM), lambda i, j: (i, j, 0))),
        compiler_params=_params("parallel", "arbitrary"),
        name=name,
    )(h, wt)


def _cast_kernel(a_ref, o_ref):
    o_ref[...] = a_ref[...].astype(o_ref.dtype)


def _cast_bf16(w):
    rows, cols = w.shape
    tr = max(d for d in range(16, 1025, 16) if rows % d == 0)
    return pl.pallas_call(
        _cast_kernel,
        out_shape=jax.ShapeDtypeStruct((rows, cols), BF16),
        grid=(rows // tr,),
        in_specs=[pl.BlockSpec((tr, cols), lambda i: (i, 0))],
        out_specs=pl.BlockSpec((tr, cols), lambda i: (i, 0)),
        compiler_params=_params("parallel"),
        name="weight_cast",
    )(w)


def _upper3(tb):
    k = lax.broadcasted_iota(jnp.int32, (3 * tb, tb), 0) % tb
    t = lax.broadcasted_iota(jnp.int32, (3 * tb, tb), 1)
    return jnp.where(k <= t, 1.0, 0.0).astype(BF16)


def _cumsum_lanes(x, carry, tb):
    n = x.shape[1]
    u3 = _upper3(tb)
    out = []
    for s in range(0, n, tb):
        hi, mid, lo = _split3(x[:, s:s + tb])
        c = jnp.dot(jnp.concatenate([hi, mid, lo], axis=1), u3, preferred_element_type=F32) + carry
        carry = c[:, tb - 1:tb]
        out.append(c)
    return out, carry


def _logf_kernel(*refs, nb, t, p):
    if p:
        ffc_ref, bias_ref, past_ref, lf_ref, c_ref = refs
    else:
        ffc_ref, bias_ref, lf_ref, c_ref = refs
    fft = ffc_ref[...].T[0:FOX_HEADS, :]
    lf_all = -_softplus(-(fft + bias_ref[...]))
    for b in range(nb):
        lf = lf_all[:, b * t:(b + 1) * t]
        lf_ref[b] = lf
        carry = jnp.zeros((FOX_HEADS, 1), F32)
        if p:
            tbp = min(256, p)
            blocks, carry = _cumsum_lanes(past_ref[b], carry, tbp)
            for i, c in enumerate(blocks):
                c_ref[b, :, i * tbp:(i + 1) * tbp] = c
        tb = min(256, t)
        blocks, carry = _cumsum_lanes(lf, carry, tb)
        for i, c in enumerate(blocks):
            c_ref[b, :, p + i * tb:p + (i + 1) * tb] = c


def _logf(small, b_forget, past_t, nbatch, t):
    p = 0 if past_t is None else past_t.shape[2]
    nb = 1 if t % 128 == 0 else nbatch
    grid = (nbatch // nb,)
    in_specs = [pl.BlockSpec((nb * t, 128), lambda i: (i, 1)),
                pl.BlockSpec((FOX_HEADS, 1), lambda i: (0, 0))]
    args = [small, b_forget.reshape(FOX_HEADS, 1)]
    if p:
        in_specs.append(pl.BlockSpec((nb, FOX_HEADS, p), lambda i: (i, 0, 0)))
        args.append(past_t)
    return pl.pallas_call(
        functools.partial(_logf_kernel, nb=nb, t=t, p=p),
        out_shape=(jax.ShapeDtypeStruct((nbatch, FOX_HEADS, t), F32),
                   jax.ShapeDtypeStruct((nbatch, FOX_HEADS, p + t), F32)),
        grid=grid,
        in_specs=in_specs,
        out_specs=(pl.BlockSpec((nb, FOX_HEADS, t), lambda i: (i, 0, 0)),
                   pl.BlockSpec((nb, FOX_HEADS, p + t), lambda i: (i, 0, 0))),
        compiler_params=_params("parallel"),
        name="logf_cumsum",
    )(*args)


def _expand_matrix():
    r = np.arange(128)[:, None]
    c = np.arange(D_MODEL)[None, :]
    return jnp.asarray(((r < 96) & ((r % SSD_HEADS) == (c // SSD_HEAD_DIM))).astype(np.float32), dtype=BF16)


def _pack3(x):
    lane = lax.broadcasted_iota(jnp.int32, x.shape, 1)
    x = jnp.where(lane < SSD_HEADS, x, 0.0)
    hi, mid, lo = _split3(x)
    packed = hi.astype(F32) + pltpu.roll(mid.astype(F32), SSD_HEADS, 1) + pltpu.roll(lo.astype(F32), 2 * SSD_HEADS, 1)
    return packed.astype(BF16)


HIST = 16


def _shift_matrix(L):
    sh = np.zeros((SSD_CONV * L, 2 * HIST + L), np.float32)
    for d in range(SSD_CONV):
        for t in range(L):
            sh[d * L + t, 2 * HIST + t - d] = 1.0
            if t - d < 0:
                sh[d * L + t, HIST + t - d] = 1.0
    return jnp.asarray(sh, dtype=BF16)


def _ssd_kernel(xs_ref, b_ref, c_ref, z_ref, sm_ref, conv0_ref, h0_ref,
                wconv_ref, bconv_ref, dtb_ref, alog_ref, dskip_ref, gout_ref, e3_ref, sh_ref,
                y_ref, hfin_ref, convout_ref,
                st_ref, xq_ref):
    ci = pl.program_id(1)
    nb, L = xs_ref.shape[0], xs_ref.shape[1]
    P = SSD_HEAD_DIM
    W = D_MODEL
    gw = W // SSD_GROUPS
    hpm = 256 // L

    @pl.when(ci == 0)
    def _init():
        for bi in range(nb):
            st_ref[bi] = h0_ref[bi].T
            hist = jnp.concatenate([jnp.zeros((HIST - 8, SSD_CONV_CH), F32), conv0_ref[bi]], axis=0)
            hist_hi = hist.astype(BF16)
            xq_ref[bi, 0:HIST, :] = hist_hi
            xq_ref[bi, HIST:2 * HIST, :] = (hist - hist_hi.astype(F32)).astype(BF16)

    lane = lax.broadcasted_iota(jnp.int32, (L, 128), 1)
    kk = lax.broadcasted_iota(jnp.int32, (L, 3 * L), 1) % L
    ll = lax.broadcasted_iota(jnp.int32, (L, 3 * L), 0)
    tri3 = jnp.where(kk <= ll, 1.0, 0.0).astype(BF16)
    row = lax.broadcasted_iota(jnp.int32, (L, W), 0)
    sidx = lax.broadcasted_iota(jnp.int32, (L, W), 1) % L
    rb = lax.broadcasted_iota(jnp.int32, (256, hpm * P), 0) // L
    cb = lax.broadcasted_iota(jnp.int32, (256, hpm * P), 1) // P
    blockmask = rb == cb
    a_small = -jnp.exp(alog_ref[...])
    e3 = e3_ref[...]

    def chunk(bi):
        xq_ref[bi, 2 * HIST:, 0:W] = xs_ref[bi]
        xq_ref[bi, 2 * HIST:, W:W + SSD_BC] = b_ref[bi]
        xq_ref[bi, 2 * HIST:, W + SSD_BC:] = c_ref[bi]

        def conv(lo, hi):
            sh = jnp.dot(sh_ref[...], xq_ref[bi, :, lo:hi], preferred_element_type=F32)
            acc = bconv_ref[:, lo:hi]
            for d in range(SSD_CONV):
                acc = acc + sh[d * L:(d + 1) * L] * wconv_ref[SSD_CONV - 1 - d:SSD_CONV - d, lo:hi]
            return _silu(acc)

        xs = conv(0, W)
        bm = conv(W, W + SSD_BC).astype(BF16)
        cm = conv(W + SSD_BC, W + 2 * SSD_BC).astype(BF16)
        xq_ref[bi, 0:HIST, :] = jnp.zeros((HIST, SSD_CONV_CH), BF16)
        xq_ref[bi, HIST:2 * HIST, :] = xq_ref[bi, HIST + L:2 * HIST + L, :]

        dt = _softplus(sm_ref[bi] + dtb_ref[...])
        dta = jnp.where(lane < SSD_HEADS, dt * a_small, 0.0)
        hi, mid, lo = _split3(dta)
        acum = jnp.dot(tri3, jnp.concatenate([hi, mid, lo], axis=0), preferred_element_type=F32)
        a_x = jnp.dot(_pack3(acum), e3, preferred_element_type=F32)
        dt_x = jnp.dot(_pack3(dt), e3, preferred_element_type=F32)

        a_row = jnp.sum(jnp.where(row == sidx, a_x, 0.0), axis=0, keepdims=True)
        ldec = jnp.exp(jnp.where(row >= sidx, a_x - a_row, NEG_BIG))
        cbx = []
        for g in range(SSD_GROUPS):
            cg = cm[:, g * SSD_D_STATE:(g + 1) * SSD_D_STATE]
            bg = bm[:, g * SSD_D_STATE:(g + 1) * SSD_D_STATE]
            brep = jnp.concatenate([bg] * (SSD_HEADS // SSD_GROUPS), axis=0)
            cbx.append(lax.dot_general(cg, brep, NT_DIMS, preferred_element_type=F32))
        m = (jnp.concatenate(cbx, axis=1) * ldec).astype(BF16)

        xdt = xs * dt_x
        xdt_b = xdt.astype(BF16)
        y_parts = []
        for j in range(SSD_HEADS // hpm):
            xj = xdt_b[:, j * hpm * P:(j + 1) * hpm * P]
            bd = jnp.where(blockmask, jnp.concatenate([xj] * hpm, axis=0), jnp.zeros((), BF16))
            y_parts.append(jnp.dot(m[:, j * 256:(j + 1) * 256], bd, preferred_element_type=F32))
        y = jnp.concatenate(y_parts, axis=1)

        st = st_ref[bi]
        st_b = st.astype(BF16)
        yo = [jnp.dot(cm[:, g * SSD_D_STATE:(g + 1) * SSD_D_STATE], st_b[:, g * gw:(g + 1) * gw],
                      preferred_element_type=F32) for g in range(SSD_GROUPS)]
        y = y + jnp.concatenate(yo, axis=1) * jnp.exp(a_x)

        a_last = a_x[L - 1:L, :]
        xw = (xdt * jnp.exp(a_last - a_x)).astype(BF16)
        upd = [lax.dot_general(bm[:, g * SSD_D_STATE:(g + 1) * SSD_D_STATE], xw[:, g * gw:(g + 1) * gw],
                               (((0,), (0,)), ((), ())), preferred_element_type=F32) for g in range(SSD_GROUPS)]
        st_ref[bi] = st * jnp.exp(a_last) + jnp.concatenate(upd, axis=1)

        y = y + dskip_ref[...] * xs
        gz = y * _silu(z_ref[bi].astype(F32))
        ms = jnp.mean(gz * gz, axis=-1, keepdims=True)
        y_ref[bi] = (gz * lax.rsqrt(ms + EPS) * gout_ref[...]).astype(y_ref.dtype)

    for bi in range(nb):
        chunk(bi)

    @pl.when(ci == pl.num_programs(1) - 1)
    def _final():
        for bi in range(nb):
            hfin_ref[bi] = st_ref[bi].T
            convout_ref[bi] = xq_ref[bi, HIST:2 * HIST, :].astype(F32)[HIST - 8:, :]


def _ssd(u, small, conv0, h0, w_conv, b_conv, dt_bias, a_log, d_skip, g_ssd_out, nbatch, t):
    L = SSD_CHUNK
    nc = t // L
    nb = SSD_NB if nbatch % SSD_NB == 0 else 1
    pad128 = lambda v: jnp.pad(v.astype(F32), (0, 128 - v.shape[0])).reshape(1, 128)
    rep = lambda v: jnp.repeat(v.astype(F32), SSD_HEAD_DIM).reshape(1, D_MODEL)
    const2 = lambda shape: pl.BlockSpec(shape, lambda b, c: (0, 0))
    seq = lambda width, col: pl.BlockSpec((nb, L, width), lambda b, c: (b, c, col))
    per_seq = lambda rows, width: pl.BlockSpec((nb, rows, width), lambda b, c: (b, 0, 0))
    u3d = u.reshape(nbatch, t, u.shape[1])
    bc0 = (U1_XS + 1) * D_MODEL // SSD_BC
    y, hfin, convout = pl.pallas_call(
        _ssd_kernel,
        out_shape=(jax.ShapeDtypeStruct((nbatch, t, D_MODEL), BF16),
                   jax.ShapeDtypeStruct((nbatch, D_MODEL, SSD_D_STATE), F32),
                   jax.ShapeDtypeStruct((nbatch, 8, SSD_CONV_CH), F32)),
        grid=(nbatch // nb, nc),
        in_specs=[
            seq(D_MODEL, U1_XS), seq(SSD_BC, bc0), seq(SSD_BC, bc0 + 1), seq(D_MODEL, U1_Z), seq(128, 0),
            per_seq(8, SSD_CONV_CH), per_seq(D_MODEL, SSD_D_STATE),
            const2((SSD_CONV, SSD_CONV_CH)), const2((1, SSD_CONV_CH)),
            const2((1, 128)), const2((1, 128)), const2((1, D_MODEL)), const2((1, D_MODEL)),
            const2((128, D_MODEL)), const2((SSD_CONV * L, 2 * HIST + L)),
        ],
        out_specs=(seq(D_MODEL, 0), per_seq(D_MODEL, SSD_D_STATE), per_seq(8, SSD_CONV_CH)),
        scratch_shapes=[pltpu.VMEM((nb, SSD_D_STATE, D_MODEL), F32),
                        pltpu.VMEM((nb, 2 * HIST + L, SSD_CONV_CH), BF16)],
        compiler_params=_params("parallel", "arbitrary"),
        name="ssd_scan",
    )(u3d, u3d, u3d, u3d, small.reshape(nbatch, t, small.shape[1]), conv0, h0.reshape(nbatch, D_MODEL, SSD_D_STATE),
      w_conv, b_conv.reshape(1, SSD_CONV_CH), pad128(dt_bias), pad128(a_log), rep(d_skip),
      g_ssd_out.reshape(1, D_MODEL), _expand_matrix(), _shift_matrix(L))
    return (y.reshape(nbatch * t, D_MODEL), hfin.reshape(nbatch, SSD_HEADS, SSD_HEAD_DIM, SSD_D_STATE),
            convout[:, 8 - (SSD_CONV - 1):])


def _col_from_row(row_vals):
    n = row_vals.shape[1]
    r = lax.broadcasted_iota(jnp.int32, (n, n), 0)
    c = lax.broadcasted_iota(jnp.int32, (n, n), 1)
    return jnp.sum(jnp.where(r == c, row_vals, 0.0), axis=1, keepdims=True)


FOX_PAIR = 4
NT_DIMS = (((1,), (1,)), ((), ()))
FOX_TQ = 512


def _fox_prompt_kernel(q_ref, k_ref, v_ref, fg_ref, c_ref, o_ref, va_ref, *, tq):
    qi = pl.program_id(2)
    hd = FOX_HEAD_DIM
    t = k_ref.shape[0]
    c1 = hd ** -0.5 * LOG2E
    q0 = pl.multiple_of(qi * tq, tq)
    lanes = [slice(j * hd, (j + 1) * hd) for j in range(FOX_PAIR)]

    @pl.when(qi == 0)
    def _stage_values():
        for j in range(FOX_PAIR):
            va_ref[j, :, 0:hd] = v_ref[:, lanes[j]]
            va_ref[j, :, hd:2 * hd] = jnp.ones((t, hd), BF16)

    cq2_rep = [jnp.broadcast_to(c_ref[j, :, pl.ds(q0, tq)], (hd, tq)).T * LOG2E for j in range(FOX_PAIR)]
    cq2 = [cr[:, 0:1] for cr in cq2_rep]
    row = lax.broadcasted_iota(jnp.int32, (tq, tq), 0)
    col = lax.broadcasted_iota(jnp.int32, (tq, tq), 1)

    def scores(j, k0, masked):
        s = lax.dot_general(q_ref[:, lanes[j]], k_ref[pl.ds(k0, tq), lanes[j]], NT_DIMS, preferred_element_type=F32)
        t2 = s * c1 - c_ref[j, :, pl.ds(k0, tq)] * LOG2E
        return jnp.where(col <= row, t2, NEG_BIG) if masked else t2

    shift = []
    ones = jnp.ones((hd, hd), BF16)
    for j in range(FOX_PAIR):
        qk = q_ref[:, lanes[j]] * k_ref[pl.ds(q0, tq), lanes[j]]
        shift.append(jnp.dot(qk, ones, preferred_element_type=F32) * c1 - cq2_rep[j])

    def fast_tile(k0, accs, masked):
        out = []
        for j in range(FOX_PAIR):
            t2 = scores(j, k0, masked)
            e = jnp.concatenate([t2[:, i:i + hd] - shift[j] for i in range(0, tq, hd)], axis=1)
            out.append(accs[j] + jnp.dot(jnp.exp2(e).astype(BF16), va_ref[j, pl.ds(k0, tq), :],
                                         preferred_element_type=F32))
        return tuple(out)

    def fast_part(j, r0, nr, kk, nk):
        s = lax.dot_general(q_ref[r0:r0 + nr, lanes[j]], k_ref[pl.ds(q0 + kk, nk), lanes[j]], NT_DIMS,
                            preferred_element_type=F32)
        t2 = s * c1 - c_ref[j, :, pl.ds(q0 + kk, nk)] * LOG2E
        rr = lax.broadcasted_iota(jnp.int32, (nr, nk), 0) + r0
        cc = lax.broadcasted_iota(jnp.int32, (nr, nk), 1) + kk
        t2 = jnp.where(cc <= rr, t2, NEG_BIG)
        e = jnp.concatenate([t2[:, i:i + hd] - shift[j][r0:r0 + nr] for i in range(0, nk, hd)], axis=1)
        return jnp.dot(jnp.exp2(e).astype(BF16), va_ref[j, pl.ds(q0 + kk, nk), :], preferred_element_type=F32)

    def fast_diag(accs):
        half = tq // 2
        out = []
        for j in range(FOX_PAIR):
            left = fast_part(j, 0, tq, 0, half)
            right = fast_part(j, half, half, half, half)
            out.append(accs[j] + left + jnp.concatenate([jnp.zeros((half, 2 * hd), F32), right], axis=0))
        return tuple(out)

    accs = tuple(jnp.zeros((tq, 2 * hd), F32) for _ in range(FOX_PAIR))
    accs = lax.fori_loop(0, qi, lambda i, a: fast_tile(pl.multiple_of(i * tq, tq), a, False), accs)
    accs = fast_diag(accs)
    bad = jnp.float32(0.0)
    for j in range(FOX_PAIR):
        num, den = accs[j][:, 0:hd], accs[j][:, hd:2 * hd]
        o_ref[:, lanes[j]] = (num / den * _silu(fg_ref[:, lanes[j]].astype(F32))).astype(o_ref.dtype)
        bad = jnp.maximum(bad, jnp.max(jnp.where(jnp.isfinite(accs[j]), 0.0, 1.0)))

    @pl.when(bad > 0.0)
    def _running_max():
        def tile(k0, carry, masked):
            out = []
            for j in range(FOX_PAIR):
                m_i, l_i, acc = carry[j]
                t2 = scores(j, k0, masked)
                m_new = jnp.maximum(m_i, jnp.max(t2, axis=1, keepdims=True) + cq2[j])
                p = jnp.exp2(t2 - (m_new - cq2[j]))
                alpha = jnp.exp2(m_i - m_new)
                l_new = alpha * l_i + jnp.sum(p, axis=1, keepdims=True)
                acc = alpha * acc + jnp.dot(p.astype(BF16), v_ref[pl.ds(k0, tq), lanes[j]],
                                            preferred_element_type=F32)
                out.append((m_new, l_new, acc))
            return tuple(out)

        init = tuple((jnp.full((tq, 1), NEG_BIG, F32), jnp.zeros((tq, 1), F32), jnp.zeros((tq, hd), F32))
                     for _ in range(FOX_PAIR))
        carry = lax.fori_loop(0, qi, lambda i, cr: tile(pl.multiple_of(i * tq, tq), cr, False), init)
        carry = tile(q0, carry, True)
        for j in range(FOX_PAIR):
            _, l_i, acc = carry[j]
            o_ref[:, lanes[j]] = (acc / l_i * _silu(fg_ref[:, lanes[j]].astype(F32))).astype(o_ref.dtype)


def _fox_prompt(uq, uk, uv, ug, ct, nbatch, t):
    tq = min(FOX_TQ, t)
    nq = t // tq
    w = FOX_PAIR * FOX_HEAD_DIM
    nb = D_MODEL // w
    return pl.pallas_call(
        functools.partial(_fox_prompt_kernel, tq=tq),
        out_shape=jax.ShapeDtypeStruct((nbatch * t, D_MODEL), BF16),
        grid=(nbatch, nb, nq),
        in_specs=[
            pl.BlockSpec((tq, w), lambda b, h, i: (b * nq + i, h)),
            pl.BlockSpec((t, w), lambda b, h, i: (b, h)),
            pl.BlockSpec((t, w), lambda b, h, i: (b, h)),
            pl.BlockSpec((tq, w), lambda b, h, i: (b * nq + i, h)),
            pl.BlockSpec((None, FOX_PAIR, 1, t), lambda b, h, i: (b, h, 0, 0)),
        ],
        out_specs=pl.BlockSpec((tq, w), lambda b, h, i: (b * nq + i, h)),
        scratch_shapes=[pltpu.VMEM((FOX_PAIR, t, 2 * FOX_HEAD_DIM), BF16)],
        compiler_params=_params("parallel", "parallel", "arbitrary"),
        name="fox_prompt",
    )(uq, uk, uv, ug, ct)


def _fox_sample_kernel(q_ref, kp_ref, vp_ref, kn_ref, vn_ref, fg_ref, c_ref, o_ref, *, p, t):
    c1 = FOX_HEAD_DIM ** -0.5 * LOG2E
    kp = kp_ref[...].reshape(p, D_MODEL).astype(BF16)
    vp = vp_ref[...].reshape(p, D_MODEL).astype(BF16)
    r = lax.broadcasted_iota(jnp.int32, (t, t), 0)
    c = lax.broadcasted_iota(jnp.int32, (t, t), 1)
    for j in range(FOX_HEADS):
        sl = slice(j * FOX_HEAD_DIM, (j + 1) * FOX_HEAD_DIM)
        q = q_ref[:, sl]
        cq2 = _col_from_row(c_ref[j, :, p:p + t]) * LOG2E
        s_p = lax.dot_general(q, kp[:, sl], NT_DIMS, preferred_element_type=F32) * c1 - c_ref[j, :, 0:p] * LOG2E
        s_n = (lax.dot_general(q, kn_ref[:, sl], NT_DIMS, preferred_element_type=F32) * c1
               - c_ref[j, :, p:p + t] * LOG2E)
        s_n = jnp.where(c <= r, s_n, NEG_BIG)
        m = jnp.maximum(jnp.max(s_p, axis=1, keepdims=True), jnp.max(s_n, axis=1, keepdims=True)) + cq2
        e_p = jnp.exp2(s_p - (m - cq2))
        e_n = jnp.exp2(s_n - (m - cq2))
        inv = 1.0 / (jnp.sum(e_p, axis=1, keepdims=True) + jnp.sum(e_n, axis=1, keepdims=True))
        o = (jnp.dot((e_p * inv).astype(BF16), vp[:, sl], preferred_element_type=F32)
             + jnp.dot((e_n * inv).astype(BF16), vn_ref[:, sl], preferred_element_type=F32))
        o_ref[:, sl] = (o * _silu(fg_ref[:, sl].astype(F32))).astype(o_ref.dtype)


def _fox_sample(uq, uk, uv, ug, k_cache, v_cache, layer, ct, nbatch, t):
    p = k_cache.shape[2]
    past = pl.BlockSpec((None, None, p, FOX_HEADS, FOX_HEAD_DIM), lambda b: (layer, b, 0, 0, 0))
    col = pl.BlockSpec((t, D_MODEL), lambda b: (b, 0))
    return pl.pallas_call(
        functools.partial(_fox_sample_kernel, p=p, t=t),
        out_shape=jax.ShapeDtypeStruct((nbatch * t, D_MODEL), BF16),
        grid=(nbatch,),
        in_specs=[col, past, past, col, col, col,
                  pl.BlockSpec((None, FOX_HEADS, 1, p + t), lambda b: (b, 0, 0, 0))],
        out_specs=pl.BlockSpec((t, D_MODEL), lambda b: (b, 0)),
        compiler_params=_params("parallel"),
        name="fox_sample",
    )(uq, k_cache, v_cache, uk, uv, ug, ct)


def _mem_kernel(q_ref, g_ref, k_ref, v_ref, o_ref):
    scale = MEM_HEAD_DIM ** -0.5
    nm = k_ref.shape[0]
    k = k_ref[...].reshape(nm, D_MODEL).astype(BF16)
    v = v_ref[...].reshape(nm, D_MODEL).astype(BF16)
    for h in range(MEM_HEADS):
        sl = slice(h * MEM_HEAD_DIM, (h + 1) * MEM_HEAD_DIM)
        s = lax.dot_general(q_ref[:, sl], k[:, sl], NT_DIMS, preferred_element_type=F32) * scale
        e = jnp.exp(s - jnp.max(s, axis=1, keepdims=True))
        p = e * (1.0 / jnp.sum(e, axis=1, keepdims=True))
        o = jnp.dot(p.astype(BF16), v[:, sl], preferred_element_type=F32)
        o_ref[:, sl] = (o * _silu(g_ref[:, sl].astype(F32))).astype(o_ref.dtype)


def _mem_attend(u3, mk, mv, nbatch, t, layer=None):
    tq = min(512, t)
    nq = t // tq
    if layer is None:
        nm = mk.shape[1]
        kv = pl.BlockSpec((None, nm, D_MODEL), lambda b, i: (b, 0, 0))
    else:
        nm = mk.shape[2]
        kv = pl.BlockSpec((None, None, nm, MEM_HEADS, MEM_HEAD_DIM), lambda b, i: (layer, b, 0, 0, 0))
    return pl.pallas_call(
        _mem_kernel,
        out_shape=jax.ShapeDtypeStruct((nbatch * t, D_MODEL), BF16),
        grid=(nbatch, nq),
        in_specs=[pl.BlockSpec((tq, D_MODEL), lambda b, i: (b * nq + i, U3_Q)),
                  pl.BlockSpec((tq, D_MODEL), lambda b, i: (b * nq + i, U3_G)), kv, kv],
        out_specs=pl.BlockSpec((tq, D_MODEL), lambda b, i: (b * nq + i, 0)),
        compiler_params=_params("parallel", "arbitrary"),
        name="mem_attend",
    )(u3, u3, mk, mv)


def _merge_kernel(ys_ref, yf_ref, ym_ref, ws_ref, wf_ref, wm_ref, gs_ref, gf_ref, gm_ref, o_ref):
    def branch(y_ref, w_ref, g_ref):
        return _sigmoid(g_ref[...].astype(F32)) * jnp.dot(y_ref[...], w_ref[...], preferred_element_type=F32)
    o_ref[...] = (branch(ys_ref, ws_ref, gs_ref) + branch(yf_ref, wf_ref, gf_ref)
                  + branch(ym_ref, wm_ref, gm_ref)).astype(o_ref.dtype)


def _merge(u, y_ssd, y_fox, y_mem, w_s, w_f, w_m, tm, tn):
    m = y_ssd.shape[0]
    nj = D_MODEL // tn
    yspec = pl.BlockSpec((tm, D_MODEL), lambda i, j: (i, 0))
    wspec = pl.BlockSpec((D_MODEL, tn), lambda i, j: (0, j))
    gspec = lambda col: pl.BlockSpec((tm, tn), lambda i, j: (i, col * nj + j))
    return pl.pallas_call(
        _merge_kernel,
        out_shape=jax.ShapeDtypeStruct((m, D_MODEL), BF16),
        grid=(m // tm, nj),
        in_specs=[yspec, yspec, yspec, wspec, wspec, wspec, gspec(U3_GS), gspec(U3_GF), gspec(U3_GM)],
        out_specs=pl.BlockSpec((tm, tn), lambda i, j: (i, j)),
        compiler_params=_params("parallel", "arbitrary"),
        name="gated_merge",
    )(y_ssd, y_fox, y_mem, w_s, w_f, w_m, u, u, u)


def _final_kernel(mg_ref, w_ref, x_ref, g_ref, o_ref, *, normalize):
    xo = x_ref[...] + jnp.dot(mg_ref[...], w_ref[...], preferred_element_type=F32)
    if normalize:
        ms = jnp.mean(xo * xo, axis=-1, keepdims=True)
        xo = xo * lax.rsqrt(ms + EPS) * g_ref[...]
    o_ref[...] = xo


def _final(merged, w_out, x, g_final, tm, normalize):
    m = x.shape[0]
    return pl.pallas_call(
        functools.partial(_final_kernel, normalize=normalize),
        out_shape=jax.ShapeDtypeStruct((m, D_MODEL), F32),
        grid=(m // tm,),
        in_specs=[pl.BlockSpec((tm, D_MODEL), lambda i: (i, 0)),
                  pl.BlockSpec((D_MODEL, D_MODEL), lambda i: (0, 0)),
                  pl.BlockSpec((tm, D_MODEL), lambda i: (i, 0)),
                  pl.BlockSpec((1, D_MODEL), lambda i: (0, 0))],
        out_specs=pl.BlockSpec((tm, D_MODEL), lambda i: (i, 0)),
        compiler_params=_params("parallel"),
        name="out_proj_norm",
    )(merged, w_out, x, g_final.reshape(1, D_MODEL))


def _row_tile(m, pref):
    t = pref
    while m % t:
        t //= 2
    return t


def _layer(x, conv0, h0, caches, logf_past, mem_k, mem_v, wd, g_final, last):
    nbatch, t, d = x.shape
    m = nbatch * t
    x2 = x.reshape(m, d)
    tm = _row_tile(m, 1024)

    wt = wd["w_in_t"]
    _, (r2, _), (r3, n3) = W_IN_SEGMENTS
    tm2 = _row_tile(m, 2048)
    uq, h = _mm_nt_norm(x2, wd["g_norm"], wt, r2, D_MODEL, tm, 1024, "norm_in_proj_fox_q")
    u1, small = _mm_ssd(h, wt, tm2, 1024)
    uk, k_new = _mm_nt_heads(h, wt, r2 + D_MODEL, tm, 1024, "in_proj_fox_k")
    uv, v_new = _mm_nt_heads(h, wt, r2 + 2 * D_MODEL, tm, 1024, "in_proj_fox_v")
    ug = _mm_nt(h, wt, r2 + 3 * D_MODEL, D_MODEL, BF16, tm2, 1024, "in_proj_fox_gate")
    u3 = _mm_nt(h, wt, r3, n3, BF16, tm2, 1024, "in_proj_mem_gates")

    past_t = None if logf_past is None else jnp.transpose(logf_past, (0, 2, 1))
    logf_t, ct = _logf(small, wd["b_forget"], past_t, nbatch, t)
    ct = ct.reshape(nbatch, FOX_HEADS, 1, ct.shape[-1])

    conv0p = jnp.pad(conv0, ((0, 0), (8 - (SSD_CONV - 1), 0), (0, 0)))
    y_ssd, h_final, new_conv = _ssd(u1, small, conv0p, h0, wd["w_conv"], wd["b_conv"], wd["dt_bias"], wd["a_log"],
                                    wd["d_skip"], wd["g_ssd_out"], nbatch, t)

    if caches is None:
        y_fox = _fox_prompt(uq, uk, uv, ug, ct, nbatch, t)
        y_mem = _mem_attend(u3, mem_k, mem_v, nbatch, t)
    else:
        layer, fox_k, fox_v = caches
        y_fox = _fox_sample(uq, uk, uv, ug, fox_k, fox_v, layer, ct, nbatch, t)
        y_mem = _mem_attend(u3, mem_k, mem_v, nbatch, t, layer)
    merged = _merge(u3, y_ssd, y_fox, y_mem, wd["w_o_ssd"], wd["w_o_fox"], wd["w_o_mem"], tm, 512)
    y = _final(merged, wd["w_out"], x2, g_final, _row_tile(m, 512), last)

    return (y.reshape(nbatch, t, d), new_conv, h_final,
            k_new.reshape(nbatch, t, FOX_HEADS, FOX_HEAD_DIM), v_new.reshape(nbatch, t, FOX_HEADS, FOX_HEAD_DIM),
            jnp.transpose(logf_t, (0, 2, 1)))


def kernel(x_prompt, x_sample, mem_prompt, cache_fox_k, cache_fox_v, cache_fox_logf, state_ssd, state_ssd_conv,
           cache_mem_k, cache_mem_v, g_norm, w_in, w_conv, b_conv, dt_bias, a_log, d_skip, g_ssd_out, b_forget,
           g_mem, w_mem_kv, w_o_ssd, w_o_fox, w_o_mem, w_out, g_final):
    depth = w_in.shape[0]
    xp, xs = x_prompt, x_sample
    bp = xp.shape[0]
    n_mem = mem_prompt.shape[1]
    outs = [[] for _ in range(12)]
    for l in range(depth):
        wkv = _cast_bf16(w_mem_kv[l])
        wd = {
            "g_norm": g_norm[l],
            "w_in_t": _cast_bf16(jnp.transpose(w_in[l])),
            "w_conv": w_conv[l], "b_conv": b_conv[l], "dt_bias": dt_bias[l], "a_log": a_log[l], "d_skip": d_skip[l],
            "g_ssd_out": g_ssd_out[l], "b_forget": b_forget[l],
            "w_o_ssd": _cast_bf16(w_o_ssd[l]), "w_o_fox": _cast_bf16(w_o_fox[l]),
            "w_o_mem": _cast_bf16(w_o_mem[l]), "w_out": _cast_bf16(w_out[l]),
        }
        hm = _rmsnorm(mem_prompt.reshape(bp * n_mem, D_MODEL), g_mem[l], _row_tile(bp * n_mem, 256))
        tmm = _row_tile(bp * n_mem, 512)
        mk, mk_b = _mm_heads(hm, wkv, 0, MEM_HEADS, MEM_HEAD_DIM, tmm, "mem_k_proj")
        mv, mv_b = _mm_heads(hm, wkv, D_MODEL, MEM_HEADS, MEM_HEAD_DIM, tmm, "mem_v_proj")

        xp, c_p, h_p, k_p, v_p, lf_p = _layer(
            xp, jnp.zeros((bp, SSD_CONV - 1, SSD_CONV_CH), F32),
            jnp.zeros((bp, SSD_HEADS, SSD_HEAD_DIM, SSD_D_STATE), F32), None, None,
            mk_b.reshape(bp, n_mem, D_MODEL), mv_b.reshape(bp, n_mem, D_MODEL), wd, g_final, l == depth - 1)
        xs, c_s, h_s, k_s, v_s, lf_s = _layer(
            xs, state_ssd_conv[l], state_ssd[l], (l, cache_fox_k, cache_fox_v), cache_fox_logf[l],
            cache_mem_k, cache_mem_v, wd, g_final, l == depth - 1)
        for lst, val in zip(outs, (k_p, v_p, lf_p, h_p, c_p,
                                   mk.reshape(bp, n_mem, MEM_HEADS, MEM_HEAD_DIM),
                                   mv.reshape(bp, n_mem, MEM_HEADS, MEM_HEAD_DIM),
                                   k_s, v_s, lf_s, h_s, c_s)):
            lst.append(val)
    return (xp, xs) + tuple(jnp.stack(o) for o in outs)
```

```python
import functools

import numpy as np
import jax
import jax.numpy as jnp
from jax import lax
from jax.experimental import pallas as pl
from jax.experimental.pallas import tpu as pltpu

F32 = jnp.float32
BF16 = jnp.bfloat16

EPS = 1e-6
D_MODEL = 2048
SSD_HEAD_DIM = 64
SSD_HEADS = 32
SSD_GROUPS = 4
SSD_D_STATE = 128
SSD_CONV = 4
SSD_BC = SSD_GROUPS * SSD_D_STATE
SSD_CONV_CH = D_MODEL + 2 * SSD_BC
FOX_HEADS = 16
FOX_HEAD_DIM = 128
MEM_HEADS = 4
MEM_HEAD_DIM = 512
SSD_CHUNK = 64
SSD_NB = 2
NEG_BIG = -1e30

VMEM_LIMIT = 56 * 1024 * 1024

W_IN_SEGMENTS = ((0, D_MODEL + SSD_CONV_CH),
                 (D_MODEL + SSD_CONV_CH + SSD_HEADS, 4 * D_MODEL),
                 (D_MODEL + SSD_CONV_CH + SSD_HEADS + 4 * D_MODEL + FOX_HEADS, 5 * D_MODEL))
W_IN_DT = D_MODEL + SSD_CONV_CH
W_IN_FF = W_IN_SEGMENTS[1][0] + 4 * D_MODEL
U1_Z, U1_XS = 0, 1
U3_Q, U3_G, U3_GS, U3_GF, U3_GM = 0, 1, 2, 3, 4
LOG2E = 1.4426950408889634


def _params(*sem):
    return pltpu.CompilerParams(dimension_semantics=sem, vmem_limit_bytes=VMEM_LIMIT)


def _split3(x):
    hi = x.astype(BF16)
    r1 = x - hi.astype(F32)
    mid = r1.astype(BF16)
    lo = (r1 - mid.astype(F32)).astype(BF16)
    return hi, mid, lo


def _softplus(x):
    return jnp.maximum(x, 0.0) + jnp.log1p(jnp.exp(-jnp.abs(x)))


def _sigmoid(x):
    return 0.5 * jnp.tanh(0.5 * x) + 0.5


def _silu(x):
    return x * _sigmoid(x)


def _rmsnorm_kernel(x_ref, g_ref, o_ref):
    x = x_ref[...]
    ms = jnp.mean(x * x, axis=-1, keepdims=True)
    o_ref[...] = (x * lax.rsqrt(ms + EPS) * g_ref[...]).astype(o_ref.dtype)


def _rmsnorm(x, g, tm):
    m, d = x.shape
    return pl.pallas_call(
        _rmsnorm_kernel,
        out_shape=jax.ShapeDtypeStruct((m, d), BF16),
        grid=(m // tm,),
        in_specs=[pl.BlockSpec((tm, d), lambda i: (i, 0)), pl.BlockSpec((1, d), lambda i: (0, 0))],
        out_specs=pl.BlockSpec((tm, d), lambda i: (i, 0)),
        compiler_params=_params("parallel"),
        name="rmsnorm",
    )(x, g.reshape(1, d))


def _mm_kernel(h_ref, w_ref, o_ref):
    o_ref[...] = jnp.dot(h_ref[...], w_ref[...], preferred_element_type=F32).astype(o_ref.dtype)


def _mm(h, w, out_dtype, tm, tn, name, w_col0=0, n=None):
    m, k = h.shape
    n = w.shape[1] if n is None else n
    j0 = w_col0 // tn
    return pl.pallas_call(
        _mm_kernel,
        out_shape=jax.ShapeDtypeStruct((m, n), out_dtype),
        grid=(m // tm, n // tn),
        in_specs=[pl.BlockSpec((tm, k), lambda i, j: (i, 0)), pl.BlockSpec((k, tn), lambda i, j: (0, j0 + j))],
        out_specs=pl.BlockSpec((tm, tn), lambda i, j: (i, j)),
        compiler_params=_params("parallel", "arbitrary"),
        name=name,
    )(h, w)


def _mm_heads_kernel(h_ref, w_ref, o_ref, ob_ref):
    acc = jnp.dot(h_ref[...], w_ref[...], preferred_element_type=F32)
    o_ref[...] = acc.reshape(o_ref.shape)
    ob_ref[...] = acc.astype(ob_ref.dtype)


def _mm_heads(h, w, w_col0, heads, head_dim, tm, name):
    m, k = h.shape
    n = heads * head_dim
    return pl.pallas_call(
        _mm_heads_kernel,
        out_shape=(jax.ShapeDtypeStruct((m, heads, head_dim), F32), jax.ShapeDtypeStruct((m, n), BF16)),
        grid=(m // tm,),
        in_specs=[pl.BlockSpec((tm, k), lambda i: (i, 0)), pl.BlockSpec((k, n), lambda i: (0, w_col0 // n))],
        out_specs=(pl.BlockSpec((tm, heads, head_dim), lambda i: (i, 0, 0)), pl.BlockSpec((tm, n), lambda i: (i, 0))),
        compiler_params=_params("parallel"),
        name=name,
    )(h, w)


def _mm_nt_kernel(h_ref, wt_ref, o_ref):
    o_ref[...] = lax.dot_general(h_ref[...], wt_ref[...], NT_DIMS, preferred_element_type=F32).astype(o_ref.dtype)


def _wt_rows_spec(tn, k, row0, grid_rank):
    g = 16
    assert row0 % g == 0 and tn % g == 0
    if grid_rank == 1:
        return pl.BlockSpec((pl.Element(tn), pl.Element(k)), lambda i: (row0, 0))
    return pl.BlockSpec((pl.Element(tn), pl.Element(k)), lambda i, j: ((row0 // g + j * (tn // g)) * g, 0))


def _mm_nt(h, wt, row0, n, out_dtype, tm, tn, name):
    m, k = h.shape
    return pl.pallas_call(
        _mm_nt_kernel,
        out_shape=jax.ShapeDtypeStruct((m, n), out_dtype),
        grid=(m // tm, n // tn),
        in_specs=[pl.BlockSpec((tm, k), lambda i, j: (i, 0)), _wt_rows_spec(tn, k, row0, 2)],
        out_specs=pl.BlockSpec((tm, tn), lambda i, j: (i, j)),
        compiler_params=_params("parallel", "arbitrary"),
        name=name,
    )(h, wt)


def _mm_nt_norm_kernel(x_ref, g_ref, wt_ref, o_ref, h_ref):
    x = x_ref[...]
    ms = jnp.mean(x * x, axis=-1, keepdims=True)
    hn = (x * lax.rsqrt(ms + EPS) * g_ref[...]).astype(BF16)
    h_ref[...] = hn
    o_ref[...] = lax.dot_general(hn, wt_ref[...], NT_DIMS, preferred_element_type=F32).astype(o_ref.dtype)


def _mm_nt_norm(x, g, wt, row0, n, tm, tn, name):
    m, k = x.shape
    return pl.pallas_call(
        _mm_nt_norm_kernel,
        out_shape=(jax.ShapeDtypeStruct((m, n), BF16), jax.ShapeDtypeStruct((m, k), BF16)),
        grid=(m // tm, n // tn),
        in_specs=[pl.BlockSpec((tm, k), lambda i, j: (i, 0)), pl.BlockSpec((1, k), lambda i, j: (0, 0)),
                  _wt_rows_spec(tn, k, row0, 2)],
        out_specs=(pl.BlockSpec((tm, tn), lambda i, j: (i, j)), pl.BlockSpec((tm, k), lambda i, j: (i, 0))),
        compiler_params=_params("parallel", "arbitrary"),
        name=name,
    )(x, g.reshape(1, k), wt)


def _mm_ssd_kernel(h_ref, wt_ref, wdt_ref, wff_ref, o_ref, small_ref):
    o_ref[...] = lax.dot_general(h_ref[...], wt_ref[...], NT_DIMS, preferred_element_type=F32).astype(o_ref.dtype)

    @pl.when(pl.program_id(1) == 0)
    def _narrow():
        w = jnp.concatenate([wdt_ref[...], wff_ref[...]], axis=0)
        small_ref[...] = lax.dot_general(h_ref[...], w, NT_DIMS, preferred_element_type=F32)


def _mm_ssd(h, wt, tm, tn):
    m, k = h.shape
    row0, n = W_IN_SEGMENTS[0]
    fixed = lambda r0: pl.BlockSpec((pl.Element(128), pl.Element(k)), lambda i, j: (r0, 0))
    return pl.pallas_call(
        _mm_ssd_kernel,
        out_shape=(jax.ShapeDtypeStruct((m, n), BF16), jax.ShapeDtypeStruct((m, 256), F32)),
        grid=(m // tm, n // tn),
        in_specs=[pl.BlockSpec((tm, k), lambda i, j: (i, 0)), _wt_rows_spec(tn, k, row0, 2),
                  fixed(W_IN_DT), fixed(W_IN_FF)],
        out_specs=(pl.BlockSpec((tm, tn), lambda i, j: (i, j)), pl.BlockSpec((tm, 256), lambda i, j: (i, 0))),
        compiler_params=_params("parallel", "arbitrary"),
        name="in_proj_ssd",
    )(h, wt, wt, wt)


def _mm_nt_heads_kernel(h_ref, wt_ref, o_ref, oh_ref):
    acc = lax.dot_general(h_ref[...], wt_ref[...], NT_DIMS, preferred_element_type=F32)
    o_ref[...] = acc.astype(o_ref.dtype)
    oh_ref[...] = acc.reshape(oh_ref.shape)


def _mm_nt_heads(h, wt, row0, tm, tn, name):
    m, k = h.shape
    hb = tn // FOX_HEAD_DIM
    return pl.pallas_call(
        _mm_nt_heads_kernel,
        out_shape=(jax.ShapeDtypeStruct((m, D_MODEL), BF16),
                   jax.ShapeDtypeStruct((m, FOX_HEADS, FOX_HEAD_DIM), F32)),
        grid=(m // tm, D_MODEL // tn),
        in_specs=[pl.BlockSpec((tm, k), lambda i, j: (i, 0)), _wt_rows_spec(tn, k, row0, 2)],
        out_specs=(pl.BlockSpec((tm, tn), lambda i, j: (i, j)),
                   pl.BlockSpec((tm, hb, FOX_HEAD_DIM), lambda i, j: (i, j, 0))),
        compiler_params=_params("parallel", "arbitrary"),
        name=name,
    )(h, wt)


def _cast_kernel(a_ref, o_ref):
    o_ref[...] = a_ref[...].astype(o_ref.dtype)


def _cast_bf16(w):
    rows, cols = w.shape
    tr = max(d for d in range(16, 1025, 16) if rows % d == 0)
    return pl.pallas_call(
        _cast_kernel,
        out_shape=jax.ShapeDtypeStruct((rows, cols), BF16),
        grid=(rows // tr,),
        in_specs=[pl.BlockSpec((tr, cols), lambda i: (i, 0))],
        out_specs=pl.BlockSpec((tr, cols), lambda i: (i, 0)),
        compiler_params=_params("parallel"),
        name="weight_cast",
    )(w)


def _upper3(tb):
    k = lax.broadcasted_iota(jnp.int32, (3 * tb, tb), 0) % tb
    t = lax.broadcasted_iota(jnp.int32, (3 * tb, tb), 1)
    return jnp.where(k <= t, 1.0, 0.0).astype(BF16)


def _cumsum_lanes(x, carry, tb):
    n = x.shape[1]
    u3 = _upper3(tb)
    out = []
    for s in range(0, n, tb):
        hi, mid, lo = _split3(x[:, s:s + tb])
        c = jnp.dot(jnp.concatenate([hi, mid, lo], axis=1), u3, preferred_element_type=F32) + carry
        carry = c[:, tb - 1:tb]
        out.append(c)
    return out, carry


def _logf_kernel(*refs, nb, t, p):
    if p:
        ffc_ref, bias_ref, past_ref, lf_ref, c_ref = refs
    else:
        ffc_ref, bias_ref, lf_ref, c_ref = refs
    fft = ffc_ref[...].T[0:FOX_HEADS, :]
    lf_all = -_softplus(-(fft + bias_ref[...]))
    for b in range(nb):
        lf = lf_all[:, b * t:(b + 1) * t]
        lf_ref[b] = lf
        carry = jnp.zeros((FOX_HEADS, 1), F32)
        if p:
            tbp = min(256, p)
            blocks, carry = _cumsum_lanes(past_ref[b], carry, tbp)
            for i, c in enumerate(blocks):
                c_ref[b, :, i * tbp:(i + 1) * tbp] = c
        tb = min(256, t)
        blocks, carry = _cumsum_lanes(lf, carry, tb)
        for i, c in enumerate(blocks):
            c_ref[b, :, p + i * tb:p + (i + 1) * tb] = c


def _logf(small, b_forget, past_t, nbatch, t):
    p = 0 if past_t is None else past_t.shape[2]
    nb = 1 if t % 128 == 0 else nbatch
    grid = (nbatch // nb,)
    in_specs = [pl.BlockSpec((nb * t, 128), lambda i: (i, 1)),
                pl.BlockSpec((FOX_HEADS, 1), lambda i: (0, 0))]
    args = [small, b_forget.reshape(FOX_HEADS, 1)]
    if p:
        in_specs.append(pl.BlockSpec((nb, FOX_HEADS, p), lambda i: (i, 0, 0)))
        args.append(past_t)
    return pl.pallas_call(
        functools.partial(_logf_kernel, nb=nb, t=t, p=p),
        out_shape=(jax.ShapeDtypeStruct((nbatch, FOX_HEADS, t), F32),
                   jax.ShapeDtypeStruct((nbatch, FOX_HEADS, p + t), F32)),
        grid=grid,
        in_specs=in_specs,
        out_specs=(pl.BlockSpec((nb, FOX_HEADS, t), lambda i: (i, 0, 0)),
                   pl.BlockSpec((nb, FOX_HEADS, p + t), lambda i: (i, 0, 0))),
        compiler_params=_params("parallel"),
        name="logf_cumsum",
    )(*args)


def _expand_matrix():
    r = np.arange(128)[:, None]
    c = np.arange(D_MODEL)[None, :]
    return jnp.asarray(((r < 96) & ((r % SSD_HEADS) == (c // SSD_HEAD_DIM))).astype(np.float32), dtype=BF16)


def _pack3(x):
    lane = lax.broadcasted_iota(jnp.int32, x.shape, 1)
    x = jnp.where(lane < SSD_HEADS, x, 0.0)
    hi, mid, lo = _split3(x)
    packed = hi.astype(F32) + pltpu.roll(mid.astype(F32), SSD_HEADS, 1) + pltpu.roll(lo.astype(F32), 2 * SSD_HEADS, 1)
    return packed.astype(BF16)


HIST = 16


def _shift_matrix(L):
    sh = np.zeros((SSD_CONV * L, 2 * HIST + L), np.float32)
    for d in range(SSD_CONV):
        for t in range(L):
            sh[d * L + t, 2 * HIST + t - d] = 1.0
            if t - d < 0:
                sh[d * L + t, HIST + t - d] = 1.0
    return jnp.asarray(sh, dtype=BF16)


def _ssd_kernel(xs_ref, b_ref, c_ref, z_ref, sm_ref, conv0_ref, h0_ref,
                wconv_ref, bconv_ref, dtb_ref, alog_ref, dskip_ref, gout_ref, e3_ref, sh_ref,
                y_ref, hfin_ref, convout_ref,
                st_ref, xq_ref):
    ci = pl.program_id(1)
    nb, L = xs_ref.shape[0], xs_ref.shape[1]
    P = SSD_HEAD_DIM
    W = D_MODEL
    gw = W // SSD_GROUPS
    hpm = 256 // L

    @pl.when(ci == 0)
    def _init():
        for bi in range(nb):
            st_ref[bi] = h0_ref[bi].T
            hist = jnp.concatenate([jnp.zeros((HIST - 8, SSD_CONV_CH), F32), conv0_ref[bi]], axis=0)
            hist_hi = hist.astype(BF16)
            xq_ref[bi, 0:HIST, :] = hist_hi
            xq_ref[bi, HIST:2 * HIST, :] = (hist - hist_hi.astype(F32)).astype(BF16)

    lane = lax.broadcasted_iota(jnp.int32, (L, 128), 1)
    kk = lax.broadcasted_iota(jnp.int32, (L, 3 * L), 1) % L
    ll = lax.broadcasted_iota(jnp.int32, (L, 3 * L), 0)
    tri3 = jnp.where(kk <= ll, 1.0, 0.0).astype(BF16)
    row = lax.broadcasted_iota(jnp.int32, (L, W), 0)
    sidx = lax.broadcasted_iota(jnp.int32, (L, W), 1) % L
    rb = lax.broadcasted_iota(jnp.int32, (256, hpm * P), 0) // L
    cb = lax.broadcasted_iota(jnp.int32, (256, hpm * P), 1) // P
    blockmask = rb == cb
    a_small = -jnp.exp(alog_ref[...])
    e3 = e3_ref[...]

    def chunk(bi):
        xq_ref[bi, 2 * HIST:, 0:W] = xs_ref[bi]
        xq_ref[bi, 2 * HIST:, W:W + SSD_BC] = b_ref[bi]
        xq_ref[bi, 2 * HIST:, W + SSD_BC:] = c_ref[bi]

        def conv(lo, hi):
            sh = jnp.dot(sh_ref[...], xq_ref[bi, :, lo:hi], preferred_element_type=F32)
            acc = bconv_ref[:, lo:hi]
            for d in range(SSD_CONV):
                acc = acc + sh[d * L:(d + 1) * L] * wconv_ref[SSD_CONV - 1 - d:SSD_CONV - d, lo:hi]
            return _silu(acc)

        xs = conv(0, W)
        bm = conv(W, W + SSD_BC).astype(BF16)
        cm = conv(W + SSD_BC, W + 2 * SSD_BC).astype(BF16)
        xq_ref[bi, 0:HIST, :] = jnp.zeros((HIST, SSD_CONV_CH), BF16)
        xq_ref[bi, HIST:2 * HIST, :] = xq_ref[bi, HIST + L:2 * HIST + L, :]

        dt = _softplus(sm_ref[bi] + dtb_ref[...])
        dta = jnp.where(lane < SSD_HEADS, dt * a_small, 0.0)
        hi, mid, lo = _split3(dta)
        acum = jnp.dot(tri3, jnp.concatenate([hi, mid, lo], axis=0), preferred_element_type=F32)
        a_x = jnp.dot(_pack3(acum), e3, preferred_element_type=F32)
        dt_x = jnp.dot(_pack3(dt), e3, preferred_element_type=F32)

        a_row = jnp.sum(jnp.where(row == sidx, a_x, 0.0), axis=0, keepdims=True)
        ldec = jnp.exp(jnp.where(row >= sidx, a_x - a_row, NEG_BIG))
        cbx = []
        for g in range(SSD_GROUPS):
            cg = cm[:, g * SSD_D_STATE:(g + 1) * SSD_D_STATE]
            bg = bm[:, g * SSD_D_STATE:(g + 1) * SSD_D_STATE]
            brep = jnp.concatenate([bg] * (SSD_HEADS // SSD_GROUPS), axis=0)
            cbx.append(lax.dot_general(cg, brep, NT_DIMS, preferred_element_type=F32))
        m = (jnp.concatenate(cbx, axis=1) * ldec).astype(BF16)

        xdt = xs * dt_x
        xdt_b = xdt.astype(BF16)
        y_parts = []
        for j in range(SSD_HEADS // hpm):
            xj = xdt_b[:, j * hpm * P:(j + 1) * hpm * P]
            bd = jnp.where(blockmask, jnp.concatenate([xj] * hpm, axis=0), jnp.zeros((), BF16))
            y_parts.append(jnp.dot(m[:, j * 256:(j + 1) * 256], bd, preferred_element_type=F32))
        y = jnp.concatenate(y_parts, axis=1)

        st = st_ref[bi]
        st_b = st.astype(BF16)
        yo = [jnp.dot(cm[:, g * SSD_D_STATE:(g + 1) * SSD_D_STATE], st_b[:, g * gw:(g + 1) * gw],
                      preferred_element_type=F32) for g in range(SSD_GROUPS)]
        y = y + jnp.concatenate(yo, axis=1) * jnp.exp(a_x)

        a_last = a_x[L - 1:L, :]
        xw = (xdt * jnp.exp(a_last - a_x)).astype(BF16)
        upd = [lax.dot_general(bm[:, g * SSD_D_STATE:(g + 1) * SSD_D_STATE], xw[:, g * gw:(g + 1) * gw],
                               (((0,), (0,)), ((), ())), preferred_element_type=F32) for g in range(SSD_GROUPS)]
        st_ref[bi] = st * jnp.exp(a_last) + jnp.concatenate(upd, axis=1)

        y = y + dskip_ref[...] * xs
        gz = y * _silu(z_ref[bi].astype(F32))
        ms = jnp.mean(gz * gz, axis=-1, keepdims=True)
        y_ref[bi] = (gz * lax.rsqrt(ms + EPS) * gout_ref[...]).astype(y_ref.dtype)

    for bi in range(nb):
        chunk(bi)

    @pl.when(ci == pl.num_programs(1) - 1)
    def _final():
        for bi in range(nb):
            hfin_ref[bi] = st_ref[bi].T
            convout_ref[bi] = xq_ref[bi, HIST:2 * HIST, :].astype(F32)[HIST - 8:, :]


def _ssd(u, small, conv0, h0, w_conv, b_conv, dt_bias, a_log, d_skip, g_ssd_out, nbatch, t):
    L = SSD_CHUNK
    nc = t // L
    nb = SSD_NB if nbatch % SSD_NB == 0 else 1
    pad128 = lambda v: jnp.pad(v.astype(F32), (0, 128 - v.shape[0])).reshape(1, 128)
    rep = lambda v: jnp.repeat(v.astype(F32), SSD_HEAD_DIM).reshape(1, D_MODEL)
    const2 = lambda shape: pl.BlockSpec(shape, lambda b, c: (0, 0))
    seq = lambda width, col: pl.BlockSpec((nb, L, width), lambda b, c: (b, c, col))
    per_seq = lambda rows, width: pl.BlockSpec((nb, rows, width), lambda b, c: (b, 0, 0))
    u3d = u.reshape(nbatch, t, u.shape[1])
    bc0 = (U1_XS + 1) * D_MODEL // SSD_BC
    y, hfin, convout = pl.pallas_call(
        _ssd_kernel,
        out_shape=(jax.ShapeDtypeStruct((nbatch, t, D_MODEL), BF16),
                   jax.ShapeDtypeStruct((nbatch, D_MODEL, SSD_D_STATE), F32),
                   jax.ShapeDtypeStruct((nbatch, 8, SSD_CONV_CH), F32)),
        grid=(nbatch // nb, nc),
        in_specs=[
            seq(D_MODEL, U1_XS), seq(SSD_BC, bc0), seq(SSD_BC, bc0 + 1), seq(D_MODEL, U1_Z), seq(128, 0),
            per_seq(8, SSD_CONV_CH), per_seq(D_MODEL, SSD_D_STATE),
            const2((SSD_CONV, SSD_CONV_CH)), const2((1, SSD_CONV_CH)),
            const2((1, 128)), const2((1, 128)), const2((1, D_MODEL)), const2((1, D_MODEL)),
            const2((128, D_MODEL)), const2((SSD_CONV * L, 2 * HIST + L)),
        ],
        out_specs=(seq(D_MODEL, 0), per_seq(D_MODEL, SSD_D_STATE), per_seq(8, SSD_CONV_CH)),
        scratch_shapes=[pltpu.VMEM((nb, SSD_D_STATE, D_MODEL), F32),
                        pltpu.VMEM((nb, 2 * HIST + L, SSD_CONV_CH), BF16)],
        compiler_params=_params("parallel", "arbitrary"),
        name="ssd_scan",
    )(u3d, u3d, u3d, u3d, small.reshape(nbatch, t, small.shape[1]), conv0, h0.reshape(nbatch, D_MODEL, SSD_D_STATE),
      w_conv, b_conv.reshape(1, SSD_CONV_CH), pad128(dt_bias), pad128(a_log), rep(d_skip),
      g_ssd_out.reshape(1, D_MODEL), _expand_matrix(), _shift_matrix(L))
    return (y.reshape(nbatch * t, D_MODEL), hfin.reshape(nbatch, SSD_HEADS, SSD_HEAD_DIM, SSD_D_STATE),
            convout[:, 8 - (SSD_CONV - 1):])


def _col_from_row(row_vals):
    n = row_vals.shape[1]
    r = lax.broadcasted_iota(jnp.int32, (n, n), 0)
    c = lax.broadcasted_iota(jnp.int32, (n, n), 1)
    return jnp.sum(jnp.where(r == c, row_vals, 0.0), axis=1, keepdims=True)


FOX_PAIR = 4
NT_DIMS = (((1,), (1,)), ((), ()))
FOX_TQ = 512


def _fox_prompt_kernel(q_ref, k_ref, v_ref, fg_ref, c_ref, o_ref, va_ref, acc_ref, *, tq):
    qi = pl.program_id(2)
    hd = FOX_HEAD_DIM
    t = k_ref.shape[0]
    c1 = hd ** -0.5 * LOG2E
    q0 = pl.multiple_of(qi * tq, tq)
    lanes = [slice(j * hd, (j + 1) * hd) for j in range(FOX_PAIR)]

    @pl.when(qi == 0)
    def _stage_values():
        for j in range(FOX_PAIR):
            va_ref[j, :, 0:hd] = v_ref[:, lanes[j]]
            va_ref[j, :, hd:2 * hd] = jnp.ones((t, hd), BF16)

    cq2_rep = [jnp.broadcast_to(c_ref[j, :, pl.ds(q0, tq)], (hd, tq)).T * LOG2E for j in range(FOX_PAIR)]
    cq2 = [cr[:, 0:1] for cr in cq2_rep]
    row = lax.broadcasted_iota(jnp.int32, (tq, tq), 0)
    col = lax.broadcasted_iota(jnp.int32, (tq, tq), 1)

    def scores(j, k0, masked):
        s = lax.dot_general(q_ref[:, lanes[j]], k_ref[pl.ds(k0, tq), lanes[j]], NT_DIMS, preferred_element_type=F32)
        t2 = s * c1 - c_ref[j, :, pl.ds(k0, tq)] * LOG2E
        return jnp.where(col <= row, t2, NEG_BIG) if masked else t2

    shift = []
    ones = jnp.ones((hd, hd), BF16)
    for j in range(FOX_PAIR):
        qk = q_ref[:, lanes[j]] * k_ref[pl.ds(q0, tq), lanes[j]]
        shift.append(jnp.dot(qk, ones, preferred_element_type=F32) * c1 - cq2_rep[j])

    def fast_tile(k0, accs, masked):
        out = []
        for j in range(FOX_PAIR):
            t2 = scores(j, k0, masked)
            e = jnp.concatenate([t2[:, i:i + hd] - shift[j] for i in range(0, tq, hd)], axis=1)
            out.append(accs[j] + jnp.dot(jnp.exp2(e).astype(BF16), va_ref[j, pl.ds(k0, tq), :],
                                         preferred_element_type=F32))
        return tuple(out)

    def fast_part(j, r0, nr, kk, nk):
        s = lax.dot_general(q_ref[r0:r0 + nr, lanes[j]], k_ref[pl.ds(q0 + kk, nk), lanes[j]], NT_DIMS,
                            preferred_element_type=F32)
        t2 = s * c1 - c_ref[j, :, pl.ds(q0 + kk, nk)] * LOG2E
        rr = lax.broadcasted_iota(jnp.int32, (nr, nk), 0) + r0
        cc = lax.broadcasted_iota(jnp.int32, (nr, nk), 1) + kk
        t2 = jnp.where(cc <= rr, t2, NEG_BIG)
        e = jnp.concatenate([t2[:, i:i + hd] - shift[j][r0:r0 + nr] for i in range(0, nk, hd)], axis=1)
        return jnp.dot(jnp.exp2(e).astype(BF16), va_ref[j, pl.ds(q0 + kk, nk), :], preferred_element_type=F32)

    def fast_diag(accs):
        half = tq // 2
        out = []
        for j in range(FOX_PAIR):
            left = fast_part(j, 0, tq, 0, half)
            right = fast_part(j, half, half, half, half)
            out.append(accs[j] + left + jnp.concatenate([jnp.zeros((half, 2 * hd), F32), right], axis=0))
        return tuple(out)

    for qv in range(t // tq):
        @pl.when(qi == qv)
        def _fast(qv=qv):
            accs = tuple(jnp.zeros((tq, 2 * hd), F32) for _ in range(FOX_PAIR))
            for i in range(qv):
                accs = fast_tile(i * tq, accs, False)
            accs = fast_diag(accs)
            for j in range(FOX_PAIR):
                acc_ref[j] = accs[j]

    accs = [acc_ref[j] for j in range(FOX_PAIR)]
    bad = jnp.float32(0.0)
    for j in range(FOX_PAIR):
        num, den = accs[j][:, 0:hd], accs[j][:, hd:2 * hd]
        o_ref[:, lanes[j]] = (num / den * _silu(fg_ref[:, lanes[j]].astype(F32))).astype(o_ref.dtype)
        bad = jnp.maximum(bad, jnp.max(jnp.where(jnp.isfinite(accs[j]), 0.0, 1.0)))

    @pl.when(bad > 0.0)
    def _running_max():
        def tile(k0, carry, masked):
            out = []
            for j in range(FOX_PAIR):
                m_i, l_i, acc = carry[j]
                t2 = scores(j, k0, masked)
                m_new = jnp.maximum(m_i, jnp.max(t2, axis=1, keepdims=True) + cq2[j])
                p = jnp.exp2(t2 - (m_new - cq2[j]))
                alpha = jnp.exp2(m_i - m_new)
                l_new = alpha * l_i + jnp.sum(p, axis=1, keepdims=True)
                acc = alpha * acc + jnp.dot(p.astype(BF16), v_ref[pl.ds(k0, tq), lanes[j]],
                                            preferred_element_type=F32)
                out.append((m_new, l_new, acc))
            return tuple(out)

        init = tuple((jnp.full((tq, 1), NEG_BIG, F32), jnp.zeros((tq, 1), F32), jnp.zeros((tq, hd), F32))
                     for _ in range(FOX_PAIR))
        carry = lax.fori_loop(0, qi, lambda i, cr: tile(pl.multiple_of(i * tq, tq), cr, False), init)
        carry = tile(q0, carry, True)
        for j in range(FOX_PAIR):
            _, l_i, acc = carry[j]
            o_ref[:, lanes[j]] = (acc / l_i * _silu(fg_ref[:, lanes[j]].astype(F32))).astype(o_ref.dtype)


def _fox_prompt(uq, uk, uv, ug, ct, nbatch, t):
    tq = min(FOX_TQ, t)
    nq = t // tq
    w = FOX_PAIR * FOX_HEAD_DIM
    nb = D_MODEL // w
    return pl.pallas_call(
        functools.partial(_fox_prompt_kernel, tq=tq),
        out_shape=jax.ShapeDtypeStruct((nbatch * t, D_MODEL), BF16),
        grid=(nbatch, nb, nq),
        in_specs=[
            pl.BlockSpec((tq, w), lambda b, h, i: (b * nq + i, h)),
            pl.BlockSpec((t, w), lambda b, h, i: (b, h)),
            pl.BlockSpec((t, w), lambda b, h, i: (b, h)),
            pl.BlockSpec((tq, w), lambda b, h, i: (b * nq + i, h)),
            pl.BlockSpec((None, FOX_PAIR, 1, t), lambda b, h, i: (b, h, 0, 0)),
        ],
        out_specs=pl.BlockSpec((tq, w), lambda b, h, i: (b * nq + i, h)),
        scratch_shapes=[pltpu.VMEM((FOX_PAIR, t, 2 * FOX_HEAD_DIM), BF16),
                        pltpu.VMEM((FOX_PAIR, tq, 2 * FOX_HEAD_DIM), F32)],
        compiler_params=_params("parallel", "parallel", "arbitrary"),
        name="fox_prompt",
    )(uq, uk, uv, ug, ct)


def _fox_sample_kernel(q_ref, kp_ref, vp_ref, kn_ref, vn_ref, fg_ref, c_ref, o_ref, *, p, t):
    c1 = FOX_HEAD_DIM ** -0.5 * LOG2E
    kp = kp_ref[...].reshape(p, D_MODEL).astype(BF16)
    vp = vp_ref[...].reshape(p, D_MODEL).astype(BF16)
    r = lax.broadcasted_iota(jnp.int32, (t, t), 0)
    c = lax.broadcasted_iota(jnp.int32, (t, t), 1)
    for j in range(FOX_HEADS):
        sl = slice(j * FOX_HEAD_DIM, (j + 1) * FOX_HEAD_DIM)
        q = q_ref[:, sl]
        cq2 = _col_from_row(c_ref[j, :, p:p + t]) * LOG2E
        s_p = lax.dot_general(q, kp[:, sl], NT_DIMS, preferred_element_type=F32) * c1 - c_ref[j, :, 0:p] * LOG2E
        s_n = (lax.dot_general(q, kn_ref[:, sl], NT_DIMS, preferred_element_type=F32) * c1
               - c_ref[j, :, p:p + t] * LOG2E)
        s_n = jnp.where(c <= r, s_n, NEG_BIG)
        m = jnp.maximum(jnp.max(s_p, axis=1, keepdims=True), jnp.max(s_n, axis=1, keepdims=True)) + cq2
        e_p = jnp.exp2(s_p - (m - cq2))
        e_n = jnp.exp2(s_n - (m - cq2))
        inv = 1.0 / (jnp.sum(e_p, axis=1, keepdims=True) + jnp.sum(e_n, axis=1, keepdims=True))
        o = (jnp.dot((e_p * inv).astype(BF16), vp[:, sl], preferred_element_type=F32)
             + jnp.dot((e_n * inv).astype(BF16), vn_ref[:, sl], preferred_element_type=F32))
        o_ref[:, sl] = (o * _silu(fg_ref[:, sl].astype(F32))).astype(o_ref.dtype)


def _fox_sample(uq, uk, uv, ug, k_cache, v_cache, layer, ct, nbatch, t):
    p = k_cache.shape[2]
    past = pl.BlockSpec((None, None, p, FOX_HEADS, FOX_HEAD_DIM), lambda b: (layer, b, 0, 0, 0))
    col = pl.BlockSpec((t, D_MODEL), lambda b: (b, 0))
    return pl.pallas_call(
        functools.partial(_fox_sample_kernel, p=p, t=t),
        out_shape=jax.ShapeDtypeStruct((nbatch * t, D_MODEL), BF16),
        grid=(nbatch,),
        in_specs=[col, past, past, col, col, col,
                  pl.BlockSpec((None, FOX_HEADS, 1, p + t), lambda b: (b, 0, 0, 0))],
        out_specs=pl.BlockSpec((t, D_MODEL), lambda b: (b, 0)),
        compiler_params=_params("parallel"),
        name="fox_sample",
    )(uq, k_cache, v_cache, uk, uv, ug, ct)


def _mem_kernel(q_ref, g_ref, k_ref, v_ref, o_ref):
    scale = MEM_HEAD_DIM ** -0.5
    nm = k_ref.shape[0]
    k = k_ref[...].reshape(nm, D_MODEL).astype(BF16)
    v = v_ref[...].reshape(nm, D_MODEL).astype(BF16)
    for h in range(MEM_HEADS):
        sl = slice(h * MEM_HEAD_DIM, (h + 1) * MEM_HEAD_DIM)
        s = lax.dot_general(q_ref[:, sl], k[:, sl], NT_DIMS, preferred_element_type=F32) * scale
        e = jnp.exp(s - jnp.max(s, axis=1, keepdims=True))
        p = e * (1.0 / jnp.sum(e, axis=1, keepdims=True))
        o = jnp.dot(p.astype(BF16), v[:, sl], preferred_element_type=F32)
        o_ref[:, sl] = (o * _silu(g_ref[:, sl].astype(F32))).astype(o_ref.dtype)


def _mem_attend(u3, mk, mv, nbatch, t, layer=None):
    tq = min(512, t)
    nq = t // tq
    if layer is None:
        nm = mk.shape[1]
        kv = pl.BlockSpec((None, nm, D_MODEL), lambda b, i: (b, 0, 0))
    else:
        nm = mk.shape[2]
        kv = pl.BlockSpec((None, None, nm, MEM_HEADS, MEM_HEAD_DIM), lambda b, i: (layer, b, 0, 0, 0))
    return pl.pallas_call(
        _mem_kernel,
        out_shape=jax.ShapeDtypeStruct((nbatch * t, D_MODEL), BF16),
        grid=(nbatch, nq),
        in_specs=[pl.BlockSpec((tq, D_MODEL), lambda b, i: (b * nq + i, U3_Q)),
                  pl.BlockSpec((tq, D_MODEL), lambda b, i: (b * nq + i, U3_G)), kv, kv],
        out_specs=pl.BlockSpec((tq, D_MODEL), lambda b, i: (b * nq + i, 0)),
        compiler_params=_params("parallel", "arbitrary"),
        name="mem_attend",
    )(u3, u3, mk, mv)


def _merge_kernel(ys_ref, yf_ref, ym_ref, ws_ref, wf_ref, wm_ref, gs_ref, gf_ref, gm_ref, o_ref):
    def branch(y_ref, w_ref, g_ref):
        return _sigmoid(g_ref[...].astype(F32)) * jnp.dot(y_ref[...], w_ref[...], preferred_element_type=F32)
    o_ref[...] = (branch(ys_ref, ws_ref, gs_ref) + branch(yf_ref, wf_ref, gf_ref)
                  + branch(ym_ref, wm_ref, gm_ref)).astype(o_ref.dtype)


def _merge(u, y_ssd, y_fox, y_mem, w_s, w_f, w_m, tm, tn):
    m = y_ssd.shape[0]
    nj = D_MODEL // tn
    yspec = pl.BlockSpec((tm, D_MODEL), lambda i, j: (i, 0))
    wspec = pl.BlockSpec((D_MODEL, tn), lambda i, j: (0, j))
    gspec = lambda col: pl.BlockSpec((tm, tn), lambda i, j: (i, col * nj + j))
    return pl.pallas_call(
        _merge_kernel,
        out_shape=jax.ShapeDtypeStruct((m, D_MODEL), BF16),
        grid=(m // tm, nj),
        in_specs=[yspec, yspec, yspec, wspec, wspec, wspec, gspec(U3_GS), gspec(U3_GF), gspec(U3_GM)],
        out_specs=pl.BlockSpec((tm, tn), lambda i, j: (i, j)),
        compiler_params=_params("parallel", "arbitrary"),
        name="gated_merge",
    )(y_ssd, y_fox, y_mem, w_s, w_f, w_m, u, u, u)


def _final_kernel(mg_ref, w_ref, x_ref, g_ref, o_ref, *, normalize):
    xo = x_ref[...] + jnp.dot(mg_ref[...], w_ref[...], preferred_element_type=F32)
    if normalize:
        ms = jnp.mean(xo * xo, axis=-1, keepdims=True)
        xo = xo * lax.rsqrt(ms + EPS) * g_ref[...]
    o_ref[...] = xo


def _final(merged, w_out, x, g_final, tm, normalize):
    m = x.shape[0]
    return pl.pallas_call(
        functools.partial(_final_kernel, normalize=normalize),
        out_shape=jax.ShapeDtypeStruct((m, D_MODEL), F32),
        grid=(m // tm,),
        in_specs=[pl.BlockSpec((tm, D_MODEL), lambda i: (i, 0)),
                  pl.BlockSpec((D_MODEL, D_MODEL), lambda i: (0, 0)),
                  pl.BlockSpec((tm, D_MODEL), lambda i: (i, 0)),
                  pl.BlockSpec((1, D_MODEL), lambda i: (0, 0))],
        out_specs=pl.BlockSpec((tm, D_MODEL), lambda i: (i, 0)),
        compiler_params=_params("parallel"),
        name="out_proj_norm",
    )(merged, w_out, x, g_final.reshape(1, D_MODEL))


def _row_tile(m, pref):
    t = pref
    while m % t:
        t //= 2
    return t


def _layer(x, conv0, h0, caches, logf_past, mem_k, mem_v, wd, g_final, last):
    nbatch, t, d = x.shape
    m = nbatch * t
    x2 = x.reshape(m, d)
    tm = _row_tile(m, 1024)

    wt = wd["w_in_t"]
    _, (r2, _), (r3, n3) = W_IN_SEGMENTS
    tm2 = _row_tile(m, 2048)
    uq, h = _mm_nt_norm(x2, wd["g_norm"], wt, r2, D_MODEL, tm, 1024, "norm_in_proj_fox_q")
    u1, small = _mm_ssd(h, wt, tm2, 1024)
    uk, k_new = _mm_nt_heads(h, wt, r2 + D_MODEL, tm, 1024, "in_proj_fox_k")
    uv, v_new = _mm_nt_heads(h, wt, r2 + 2 * D_MODEL, tm, 1024, "in_proj_fox_v")
    ug = _mm_nt(h, wt, r2 + 3 * D_MODEL, D_MODEL, BF16, tm2, 1024, "in_proj_fox_gate")
    u3 = _mm_nt(h, wt, r3, n3, BF16, tm2, 1024, "in_proj_mem_gates")

    past_t = None if logf_past is None else jnp.transpose(logf_past, (0, 2, 1))
    logf_t, ct = _logf(small, wd["b_forget"], past_t, nbatch, t)
    ct = ct.reshape(nbatch, FOX_HEADS, 1, ct.shape[-1])

    conv0p = jnp.pad(conv0, ((0, 0), (8 - (SSD_CONV - 1), 0), (0, 0)))
    y_ssd, h_final, new_conv = _ssd(u1, small, conv0p, h0, wd["w_conv"], wd["b_conv"], wd["dt_bias"], wd["a_log"],
                                    wd["d_skip"], wd["g_ssd_out"], nbatch, t)

    if caches is None:
        y_fox = _fox_prompt(uq, uk, uv, ug, ct, nbatch, t)
        y_mem = _mem_attend(u3, mem_k, mem_v, nbatch, t)
    else:
        layer, fox_k, fox_v = caches
        y_fox = _fox_sample(uq, uk, uv, ug, fox_k, fox_v, layer, ct, nbatch, t)
        y_mem = _mem_attend(u3, mem_k, mem_v, nbatch, t, layer)
    merged = _merge(u3, y_ssd, y_fox, y_mem, wd["w_o_ssd"], wd["w_o_fox"], wd["w_o_mem"], tm, 512)
    y = _final(merged, wd["w_out"], x2, g_final, _row_tile(m, 512), last)

    return (y.reshape(nbatch, t, d), new_conv, h_final,
            k_new.reshape(nbatch, t, FOX_HEADS, FOX_HEAD_DIM), v_new.reshape(nbatch, t, FOX_HEADS, FOX_HEAD_DIM),
            jnp.transpose(logf_t, (0, 2, 1)))


def kernel(x_prompt, x_sample, mem_prompt, cache_fox_k, cache_fox_v, cache_fox_logf, state_ssd, state_ssd_conv,
           cache_mem_k, cache_mem_v, g_norm, w_in, w_conv, b_conv, dt_bias, a_log, d_skip, g_ssd_out, b_forget,
           g_mem, w_mem_kv, w_o_ssd, w_o_fox, w_o_mem, w_out, g_final):
    depth = w_in.shape[0]
    xp, xs = x_prompt, x_sample
    bp = xp.shape[0]
    n_mem = mem_prompt.shape[1]
    outs = [[] for _ in range(12)]
    for l in range(depth):
        wkv = _cast_bf16(w_mem_kv[l])
        wd = {
            "g_norm": g_norm[l],
            "w_in_t": _cast_bf16(jnp.transpose(w_in[l])),
            "w_conv": w_conv[l], "b_conv": b_conv[l], "dt_bias": dt_bias[l], "a_log": a_log[l], "d_skip": d_skip[l],
            "g_ssd_out": g_ssd_out[l], "b_forget": b_forget[l],
            "w_o_ssd": _cast_bf16(w_o_ssd[l]), "w_o_fox": _cast_bf16(w_o_fox[l]),
            "w_o_mem": _cast_bf16(w_o_mem[l]), "w_out": _cast_bf16(w_out[l]),
        }
        hm = _rmsnorm(mem_prompt.reshape(bp * n_mem, D_MODEL), g_mem[l], _row_tile(bp * n_mem, 256))
        tmm = _row_tile(bp * n_mem, 512)
        mk, mk_b = _mm_heads(hm, wkv, 0, MEM_HEADS, MEM_HEAD_DIM, tmm, "mem_k_proj")
        mv, mv_b = _mm_heads(hm, wkv, D_MODEL, MEM_HEADS, MEM_HEAD_DIM, tmm, "mem_v_proj")

        xp, c_p, h_p, k_p, v_p, lf_p = _layer(
            xp, jnp.zeros((bp, SSD_CONV - 1, SSD_CONV_CH), F32),
            jnp.zeros((bp, SSD_HEADS, SSD_HEAD_DIM, SSD_D_STATE), F32), None, None,
            mk_b.reshape(bp, n_mem, D_MODEL), mv_b.reshape(bp, n_mem, D_MODEL), wd, g_final, l == depth - 1)
        xs, c_s, h_s, k_s, v_s, lf_s = _layer(
            xs, state_ssd_conv[l], state_ssd[l], (l, cache_fox_k, cache_fox_v), cache_fox_logf[l],
            cache_mem_k, cache_mem_v, wd, g_final, l == depth - 1)
        for lst, val in zip(outs, (k_p, v_p, lf_p, h_p, c_p,
                                   mk.reshape(bp, n_mem, MEM_HEADS, MEM_HEAD_DIM),
                                   mv.reshape(bp, n_mem, MEM_HEADS, MEM_HEAD_DIM),
                                   k_s, v_s, lf_s, h_s, c_s)):
            lst.append(val)
    return (xp, xs) + tuple(jnp.stack(o) for o in outs)
```

```python
import functools

import numpy as np
import jax
import jax.numpy as jnp
from jax import lax
from jax.experimental import pallas as pl
from jax.experimental.pallas import tpu as pltpu

F32 = jnp.float32
BF16 = jnp.bfloat16

EPS = 1e-6
D_MODEL = 2048
SSD_HEAD_DIM = 64
SSD_HEADS = 32
SSD_GROUPS = 4
SSD_D_STATE = 128
SSD_CONV = 4
SSD_BC = SSD_GROUPS * SSD_D_STATE
SSD_CONV_CH = D_MODEL + 2 * SSD_BC
FOX_HEADS = 16
FOX_HEAD_DIM = 128
MEM_HEADS = 4
MEM_HEAD_DIM = 512
SSD_CHUNK = 64
SSD_NB = 2
NEG_BIG = -1e30

VMEM_LIMIT = 56 * 1024 * 1024

W_IN_SEGMENTS = ((0, D_MODEL + SSD_CONV_CH),
                 (D_MODEL + SSD_CONV_CH + SSD_HEADS, 4 * D_MODEL),
                 (D_MODEL + SSD_CONV_CH + SSD_HEADS + 4 * D_MODEL + FOX_HEADS, 5 * D_MODEL))
W_IN_DT = D_MODEL + SSD_CONV_CH
W_IN_FF = W_IN_SEGMENTS[1][0] + 4 * D_MODEL
U1_Z, U1_XS = 0, 1
U3_Q, U3_G, U3_GS, U3_GF, U3_GM = 0, 1, 2, 3, 4
LOG2E = 1.4426950408889634


def _params(*sem):
    return pltpu.CompilerParams(dimension_semantics=sem, vmem_limit_bytes=VMEM_LIMIT)


def _split3(x):
    hi = x.astype(BF16)
    r1 = x - hi.astype(F32)
    mid = r1.astype(BF16)
    lo = (r1 - mid.astype(F32)).astype(BF16)
    return hi, mid, lo


def _softplus(x):
    return jnp.maximum(x, 0.0) + jnp.log1p(jnp.exp(-jnp.abs(x)))


def _sigmoid(x):
    return 0.5 * jnp.tanh(0.5 * x) + 0.5


def _silu(x):
    return x * _sigmoid(x)


def _rmsnorm_kernel(x_ref, g_ref, o_ref):
    x = x_ref[...]
    ms = jnp.mean(x * x, axis=-1, keepdims=True)
    o_ref[...] = (x * lax.rsqrt(ms + EPS) * g_ref[...]).astype(o_ref.dtype)


def _rmsnorm(x, g, tm):
    m, d = x.shape
    return pl.pallas_call(
        _rmsnorm_kernel,
        out_shape=jax.ShapeDtypeStruct((m, d), BF16),
        grid=(m // tm,),
        in_specs=[pl.BlockSpec((tm, d), lambda i: (i, 0)), pl.BlockSpec((1, d), lambda i: (0, 0))],
        out_specs=pl.BlockSpec((tm, d), lambda i: (i, 0)),
        compiler_params=_params("parallel"),
        name="rmsnorm",
    )(x, g.reshape(1, d))


def _mm_kernel(h_ref, w_ref, o_ref):
    o_ref[...] = jnp.dot(h_ref[...], w_ref[...], preferred_element_type=F32).astype(o_ref.dtype)


def _mm(h, w, out_dtype, tm, tn, name, w_col0=0, n=None):
    m, k = h.shape
    n = w.shape[1] if n is None else n
    j0 = w_col0 // tn
    return pl.pallas_call(
        _mm_kernel,
        out_shape=jax.ShapeDtypeStruct((m, n), out_dtype),
        grid=(m // tm, n // tn),
        in_specs=[pl.BlockSpec((tm, k), lambda i, j: (i, 0)), pl.BlockSpec((k, tn), lambda i, j: (0, j0 + j))],
        out_specs=pl.BlockSpec((tm, tn), lambda i, j: (i, j)),
        compiler_params=_params("parallel", "arbitrary"),
        name=name,
    )(h, w)


def _mm_heads_kernel(h_ref, w_ref, o_ref, ob_ref):
    acc = jnp.dot(h_ref[...], w_ref[...], preferred_element_type=F32)
    o_ref[...] = acc.reshape(o_ref.shape)
    ob_ref[...] = acc.astype(ob_ref.dtype)


def _mm_heads(h, w, w_col0, heads, head_dim, tm, name):
    m, k = h.shape
    n = heads * head_dim
    return pl.pallas_call(
        _mm_heads_kernel,
        out_shape=(jax.ShapeDtypeStruct((m, heads, head_dim), F32), jax.ShapeDtypeStruct((m, n), BF16)),
        grid=(m // tm,),
        in_specs=[pl.BlockSpec((tm, k), lambda i: (i, 0)), pl.BlockSpec((k, n), lambda i: (0, w_col0 // n))],
        out_specs=(pl.BlockSpec((tm, heads, head_dim), lambda i: (i, 0, 0)), pl.BlockSpec((tm, n), lambda i: (i, 0))),
        compiler_params=_params("parallel"),
        name=name,
    )(h, w)


def _mm_nt_kernel(h_ref, wt_ref, o_ref):
    o_ref[...] = lax.dot_general(h_ref[...], wt_ref[...], NT_DIMS, preferred_element_type=F32).astype(o_ref.dtype)


def _wt_rows_spec(tn, k, row0, grid_rank):
    g = 16
    assert row0 % g == 0 and tn % g == 0
    if grid_rank == 1:
        return pl.BlockSpec((pl.Element(tn), pl.Element(k)), lambda i: (row0, 0))
    return pl.BlockSpec((pl.Element(tn), pl.Element(k)), lambda i, j: ((row0 // g + j * (tn // g)) * g, 0))


def _mm_nt(h, wt, row0, n, out_dtype, tm, tn, name):
    m, k = h.shape
    return pl.pallas_call(
        _mm_nt_kernel,
        out_shape=jax.ShapeDtypeStruct((m, n), out_dtype),
        grid=(m // tm, n // tn),
        in_specs=[pl.BlockSpec((tm, k), lambda i, j: (i, 0)), _wt_rows_spec(tn, k, row0, 2)],
        out_specs=pl.BlockSpec((tm, tn), lambda i, j: (i, j)),
        compiler_params=_params("parallel", "arbitrary"),
        name=name,
    )(h, wt)


def _mm_nt_norm_kernel(x_ref, g_ref, wt_ref, o_ref, h_ref):
    x = x_ref[...]
    ms = jnp.mean(x * x, axis=-1, keepdims=True)
    hn = (x * lax.rsqrt(ms + EPS) * g_ref[...]).astype(BF16)
    h_ref[...] = hn
    o_ref[...] = lax.dot_general(hn, wt_ref[...], NT_DIMS, preferred_element_type=F32).astype(o_ref.dtype)


def _mm_nt_norm(x, g, wt, row0, n, tm, tn, name):
    m, k = x.shape
    return pl.pallas_call(
        _mm_nt_norm_kernel,
        out_shape=(jax.ShapeDtypeStruct((m, n), BF16), jax.ShapeDtypeStruct((m, k), BF16)),
        grid=(m // tm, n // tn),
        in_specs=[pl.BlockSpec((tm, k), lambda i, j: (i, 0)), pl.BlockSpec((1, k), lambda i, j: (0, 0)),
                  _wt_rows_spec(tn, k, row0, 2)],
        out_specs=(pl.BlockSpec((tm, tn), lambda i, j: (i, j)), pl.BlockSpec((tm, k), lambda i, j: (i, 0))),
        compiler_params=_params("parallel", "arbitrary"),
        name=name,
    )(x, g.reshape(1, k), wt)


def _mm_ssd_kernel(h_ref, wt_ref, wdt_ref, wff_ref, o_ref, small_ref):
    o_ref[...] = lax.dot_general(h_ref[...], wt_ref[...], NT_DIMS, preferred_element_type=F32).astype(o_ref.dtype)

    @pl.when(pl.program_id(1) == 0)
    def _narrow():
        w = jnp.concatenate([wdt_ref[...], wff_ref[...]], axis=0)
        small_ref[...] = lax.dot_general(h_ref[...], w, NT_DIMS, preferred_element_type=F32)


def _mm_ssd(h, wt, tm, tn):
    m, k = h.shape
    row0, n = W_IN_SEGMENTS[0]
    fixed = lambda r0: pl.BlockSpec((pl.Element(128), pl.Element(k)), lambda i, j: (r0, 0))
    return pl.pallas_call(
        _mm_ssd_kernel,
        out_shape=(jax.ShapeDtypeStruct((m, n), BF16), jax.ShapeDtypeStruct((m, 256), F32)),
        grid=(m // tm, n // tn),
        in_specs=[pl.BlockSpec((tm, k), lambda i, j: (i, 0)), _wt_rows_spec(tn, k, row0, 2),
                  fixed(W_IN_DT), fixed(W_IN_FF)],
        out_specs=(pl.BlockSpec((tm, tn), lambda i, j: (i, j)), pl.BlockSpec((tm, 256), lambda i, j: (i, 0))),
        compiler_params=_params("parallel", "arbitrary"),
        name="in_proj_ssd",
    )(h, wt, wt, wt)


def _mm_nt_heads_kernel(h_ref, wt_ref, o_ref, oh_ref):
    acc = lax.dot_general(h_ref[...], wt_ref[...], NT_DIMS, preferred_element_type=F32)
    o_ref[...] = acc.astype(o_ref.dtype)
    oh_ref[...] = acc.reshape(oh_ref.shape)


def _mm_nt_heads(h, wt, row0, tm, tn, name):
    m, k = h.shape
    hb = tn // FOX_HEAD_DIM
    return pl.pallas_call(
        _mm_nt_heads_kernel,
        out_shape=(jax.ShapeDtypeStruct((m, D_MODEL), BF16),
                   jax.ShapeDtypeStruct((m, FOX_HEADS, FOX_HEAD_DIM), F32)),
        grid=(m // tm, D_MODEL // tn),
        in_specs=[pl.BlockSpec((tm, k), lambda i, j: (i, 0)), _wt_rows_spec(tn, k, row0, 2)],
        out_specs=(pl.BlockSpec((tm, tn), lambda i, j: (i, j)),
                   pl.BlockSpec((tm, hb, FOX_HEAD_DIM), lambda i, j: (i, j, 0))),
        compiler_params=_params("parallel", "arbitrary"),
        name=name,
    )(h, wt)


def _cast_kernel(a_ref, o_ref):
    o_ref[...] = a_ref[...].astype(o_ref.dtype)


def _cast_bf16(w):
    rows, cols = w.shape
    tr = max(d for d in range(16, 1025, 16) if rows % d == 0)
    return pl.pallas_call(
        _cast_kernel,
        out_shape=jax.ShapeDtypeStruct((rows, cols), BF16),
        grid=(rows // tr,),
        in_specs=[pl.BlockSpec((tr, cols), lambda i: (i, 0))],
        out_specs=pl.BlockSpec((tr, cols), lambda i: (i, 0)),
        compiler_params=_params("parallel"),
        name="weight_cast",
    )(w)


def _upper3(tb):
    k = lax.broadcasted_iota(jnp.int32, (3 * tb, tb), 0) % tb
    t = lax.broadcasted_iota(jnp.int32, (3 * tb, tb), 1)
    return jnp.where(k <= t, 1.0, 0.0).astype(BF16)


def _cumsum_lanes(x, carry, tb):
    n = x.shape[1]
    u3 = _upper3(tb)
    out = []
    for s in range(0, n, tb):
        hi, mid, lo = _split3(x[:, s:s + tb])
        c = jnp.dot(jnp.concatenate([hi, mid, lo], axis=1), u3, preferred_element_type=F32) + carry
        carry = c[:, tb - 1:tb]
        out.append(c)
    return out, carry


def _logf_kernel(*refs, nb, t, p):
    if p:
        ffc_ref, bias_ref, past_ref, lf_ref, c_ref = refs
    else:
        ffc_ref, bias_ref, lf_ref, c_ref = refs
    fft = ffc_ref[...].T[0:FOX_HEADS, :]
    lf_all = -_softplus(-(fft + bias_ref[...]))
    for b in range(nb):
        lf = lf_all[:, b * t:(b + 1) * t]
        lf_ref[b] = lf
        carry = jnp.zeros((FOX_HEADS, 1), F32)
        if p:
            tbp = min(256, p)
            blocks, carry = _cumsum_lanes(past_ref[b], carry, tbp)
            for i, c in enumerate(blocks):
                c_ref[b, :, i * tbp:(i + 1) * tbp] = c
        tb = min(256, t)
        blocks, carry = _cumsum_lanes(lf, carry, tb)
        for i, c in enumerate(blocks):
            c_ref[b, :, p + i * tb:p + (i + 1) * tb] = c


def _logf(small, b_forget, past_t, nbatch, t):
    p = 0 if past_t is None else past_t.shape[2]
    nb = 1 if t % 128 == 0 else nbatch
    grid = (nbatch // nb,)
    in_specs = [pl.BlockSpec((nb * t, 128), lambda i: (i, 1)),
                pl.BlockSpec((FOX_HEADS, 1), lambda i: (0, 0))]
    args = [small, b_forget.reshape(FOX_HEADS, 1)]
    if p:
        in_specs.append(pl.BlockSpec((nb, FOX_HEADS, p), lambda i: (i, 0, 0)))
        args.append(past_t)
    return pl.pallas_call(
        functools.partial(_logf_kernel, nb=nb, t=t, p=p),
        out_shape=(jax.ShapeDtypeStruct((nbatch, FOX_HEADS, t), F32),
                   jax.ShapeDtypeStruct((nbatch, FOX_HEADS, p + t), F32)),
        grid=grid,
        in_specs=in_specs,
        out_specs=(pl.BlockSpec((nb, FOX_HEADS, t), lambda i: (i, 0, 0)),
                   pl.BlockSpec((nb, FOX_HEADS, p + t), lambda i: (i, 0, 0))),
        compiler_params=_params("parallel"),
        name="logf_cumsum",
    )(*args)


def _expand_matrix():
    r = np.arange(128)[:, None]
    c = np.arange(D_MODEL)[None, :]
    return jnp.asarray(((r < 96) & ((r % SSD_HEADS) == (c // SSD_HEAD_DIM))).astype(np.float32), dtype=BF16)


def _pack3(x):
    lane = lax.broadcasted_iota(jnp.int32, x.shape, 1)
    x = jnp.where(lane < SSD_HEADS, x, 0.0)
    hi, mid, lo = _split3(x)
    packed = hi.astype(F32) + pltpu.roll(mid.astype(F32), SSD_HEADS, 1) + pltpu.roll(lo.astype(F32), 2 * SSD_HEADS, 1)
    return packed.astype(BF16)


HIST = 16


def _shift_matrix(L):
    sh = np.zeros((SSD_CONV * L, 2 * HIST + L), np.float32)
    for d in range(SSD_CONV):
        for t in range(L):
            sh[d * L + t, 2 * HIST + t - d] = 1.0
            if t - d < 0:
                sh[d * L + t, HIST + t - d] = 1.0
    return jnp.asarray(sh, dtype=BF16)


def _ssd_kernel(xs_ref, b_ref, c_ref, z_ref, sm_ref, conv0_ref, h0_ref,
                wconv_ref, bconv_ref, dtb_ref, alog_ref, dskip_ref, gout_ref, e3_ref, sh_ref,
                y_ref, hfin_ref, convout_ref,
                st_ref, xq_ref):
    ci = pl.program_id(1)
    nb, L = xs_ref.shape[0], xs_ref.shape[1]
    P = SSD_HEAD_DIM
    W = D_MODEL
    gw = W // SSD_GROUPS
    hpm = 256 // L

    @pl.when(ci == 0)
    def _init():
        for bi in range(nb):
            st_ref[bi] = h0_ref[bi].T
            hist = jnp.concatenate([jnp.zeros((HIST - 8, SSD_CONV_CH), F32), conv0_ref[bi]], axis=0)
            hist_hi = hist.astype(BF16)
            xq_ref[bi, 0:HIST, :] = hist_hi
            xq_ref[bi, HIST:2 * HIST, :] = (hist - hist_hi.astype(F32)).astype(BF16)

    lane = lax.broadcasted_iota(jnp.int32, (L, 128), 1)
    kk = lax.broadcasted_iota(jnp.int32, (L, 3 * L), 1) % L
    ll = lax.broadcasted_iota(jnp.int32, (L, 3 * L), 0)
    tri3 = jnp.where(kk <= ll, 1.0, 0.0).astype(BF16)
    row = lax.broadcasted_iota(jnp.int32, (L, W), 0)
    sidx = lax.broadcasted_iota(jnp.int32, (L, W), 1) % L
    rb = lax.broadcasted_iota(jnp.int32, (256, hpm * P), 0) // L
    cb = lax.broadcasted_iota(jnp.int32, (256, hpm * P), 1) // P
    blockmask = rb == cb
    a_small = -jnp.exp(alog_ref[...])
    e3 = e3_ref[...]

    def chunk(bi):
        xq_ref[bi, 2 * HIST:, 0:W] = xs_ref[bi]
        xq_ref[bi, 2 * HIST:, W:W + SSD_BC] = b_ref[bi]
        xq_ref[bi, 2 * HIST:, W + SSD_BC:] = c_ref[bi]

        def conv(lo, hi):
            sh = jnp.dot(sh_ref[...], xq_ref[bi, :, lo:hi], preferred_element_type=F32)
            acc = bconv_ref[:, lo:hi]
            for d in range(SSD_CONV):
                acc = acc + sh[d * L:(d + 1) * L] * wconv_ref[SSD_CONV - 1 - d:SSD_CONV - d, lo:hi]
            return _silu(acc)

        xs = conv(0, W)
        bm = conv(W, W + SSD_BC).astype(BF16)
        cm = conv(W + SSD_BC, W + 2 * SSD_BC).astype(BF16)
        xq_ref[bi, 0:HIST, :] = jnp.zeros((HIST, SSD_CONV_CH), BF16)
        xq_ref[bi, HIST:2 * HIST, :] = xq_ref[bi, HIST + L:2 * HIST + L, :]

        dt = _softplus(sm_ref[bi] + dtb_ref[...])
        dta = jnp.where(lane < SSD_HEADS, dt * a_small, 0.0)
        hi, mid, lo = _split3(dta)
        acum = jnp.dot(tri3, jnp.concatenate([hi, mid, lo], axis=0), preferred_element_type=F32)
        a_x = jnp.dot(_pack3(acum), e3, preferred_element_type=F32)
        dt_x = jnp.dot(_pack3(dt), e3, preferred_element_type=F32)

        a_row = jnp.sum(jnp.where(row == sidx, a_x, 0.0), axis=0, keepdims=True)
        ldec = jnp.exp(jnp.where(row >= sidx, a_x - a_row, NEG_BIG))
        cbx = []
        for g in range(SSD_GROUPS):
            cg = cm[:, g * SSD_D_STATE:(g + 1) * SSD_D_STATE]
            bg = bm[:, g * SSD_D_STATE:(g + 1) * SSD_D_STATE]
            brep = jnp.concatenate([bg] * (SSD_HEADS // SSD_GROUPS), axis=0)
            cbx.append(lax.dot_general(cg, brep, NT_DIMS, preferred_element_type=F32))
        m = (jnp.concatenate(cbx, axis=1) * ldec).astype(BF16)

        xdt = xs * dt_x
        xdt_b = xdt.astype(BF16)
        y_parts = []
        for j in range(SSD_HEADS // hpm):
            xj = xdt_b[:, j * hpm * P:(j + 1) * hpm * P]
            bd = jnp.where(blockmask, jnp.concatenate([xj] * hpm, axis=0), jnp.zeros((), BF16))
            y_parts.append(jnp.dot(m[:, j * 256:(j + 1) * 256], bd, preferred_element_type=F32))
        y = jnp.concatenate(y_parts, axis=1)

        st = st_ref[bi]
        st_b = st.astype(BF16)
        yo = [jnp.dot(cm[:, g * SSD_D_STATE:(g + 1) * SSD_D_STATE], st_b[:, g * gw:(g + 1) * gw],
                      preferred_element_type=F32) for g in range(SSD_GROUPS)]
        y = y + jnp.concatenate(yo, axis=1) * jnp.exp(a_x)

        a_last = a_x[L - 1:L, :]
        xw = (xdt * jnp.exp(a_last - a_x)).astype(BF16)
        upd = [lax.dot_general(bm[:, g * SSD_D_STATE:(g + 1) * SSD_D_STATE], xw[:, g * gw:(g + 1) * gw],
                               (((0,), (0,)), ((), ())), preferred_element_type=F32) for g in range(SSD_GROUPS)]
        st_ref[bi] = st * jnp.exp(a_last) + jnp.concatenate(upd, axis=1)

        y = y + dskip_ref[...] * xs
        gz = y * _silu(z_ref[bi].astype(F32))
        ms = jnp.mean(gz * gz, axis=-1, keepdims=True)
        y_ref[bi] = (gz * lax.rsqrt(ms + EPS) * gout_ref[...]).astype(y_ref.dtype)

    for bi in range(nb):
        chunk(bi)

    @pl.when(ci == pl.num_programs(1) - 1)
    def _final():
        for bi in range(nb):
            hfin_ref[bi] = st_ref[bi].T
            convout_ref[bi] = xq_ref[bi, HIST:2 * HIST, :].astype(F32)[HIST - 8:, :]


def _ssd(u, small, conv0, h0, w_conv, b_conv, dt_bias, a_log, d_skip, g_ssd_out, nbatch, t):
    L = SSD_CHUNK
    nc = t // L
    nb = SSD_NB if nbatch % SSD_NB == 0 else 1
    pad128 = lambda v: jnp.pad(v.astype(F32), (0, 128 - v.shape[0])).reshape(1, 128)
    rep = lambda v: jnp.repeat(v.astype(F32), SSD_HEAD_DIM).reshape(1, D_MODEL)
    const2 = lambda shape: pl.BlockSpec(shape, lambda b, c: (0, 0))
    seq = lambda width, col: pl.BlockSpec((nb, L, width), lambda b, c: (b, c, col))
    per_seq = lambda rows, width: pl.BlockSpec((nb, rows, width), lambda b, c: (b, 0, 0))
    u3d = u.reshape(nbatch, t, u.shape[1])
    bc0 = (U1_XS + 1) * D_MODEL // SSD_BC
    y, hfin, convout = pl.pallas_call(
        _ssd_kernel,
        out_shape=(jax.ShapeDtypeStruct((nbatch, t, D_MODEL), BF16),
                   jax.ShapeDtypeStruct((nbatch, D_MODEL, SSD_D_STATE), F32),
                   jax.ShapeDtypeStruct((nbatch, 8, SSD_CONV_CH), F32)),
        grid=(nbatch // nb, nc),
        in_specs=[
            seq(D_MODEL, U1_XS), seq(SSD_BC, bc0), seq(SSD_BC, bc0 + 1), seq(D_MODEL, U1_Z), seq(128, 0),
            per_seq(8, SSD_CONV_CH), per_seq(D_MODEL, SSD_D_STATE),
            const2((SSD_CONV, SSD_CONV_CH)), const2((1, SSD_CONV_CH)),
            const2((1, 128)), const2((1, 128)), const2((1, D_MODEL)), const2((1, D_MODEL)),
            const2((128, D_MODEL)), const2((SSD_CONV * L, 2 * HIST + L)),
        ],
        out_specs=(seq(D_MODEL, 0), per_seq(D_MODEL, SSD_D_STATE), per_seq(8, SSD_CONV_CH)),
        scratch_shapes=[pltpu.VMEM((nb, SSD_D_STATE, D_MODEL), F32),
                        pltpu.VMEM((nb, 2 * HIST + L, SSD_CONV_CH), BF16)],
        compiler_params=_params("parallel", "arbitrary"),
        name="ssd_scan",
    )(u3d, u3d, u3d, u3d, small.reshape(nbatch, t, small.shape[1]), conv0, h0.reshape(nbatch, D_MODEL, SSD_D_STATE),
      w_conv, b_conv.reshape(1, SSD_CONV_CH), pad128(dt_bias), pad128(a_log), rep(d_skip),
      g_ssd_out.reshape(1, D_MODEL), _expand_matrix(), _shift_matrix(L))
    return (y.reshape(nbatch * t, D_MODEL), hfin.reshape(nbatch, SSD_HEADS, SSD_HEAD_DIM, SSD_D_STATE),
            convout[:, 8 - (SSD_CONV - 1):])


def _col_from_row(row_vals):
    n = row_vals.shape[1]
    r = lax.broadcasted_iota(jnp.int32, (n, n), 0)
    c = lax.broadcasted_iota(jnp.int32, (n, n), 1)
    return jnp.sum(jnp.where(r == c, row_vals, 0.0), axis=1, keepdims=True)


FOX_PAIR = 4
NT_DIMS = (((1,), (1,)), ((), ()))
FOX_TQ = 512


def _fox_prompt_kernel(q_ref, k_ref, v_ref, fg_ref, c_ref, o_ref, va_ref, ok_ref, *, tq):
    qi = pl.program_id(2)
    hd = FOX_HEAD_DIM
    t = k_ref.shape[0]
    c1 = hd ** -0.5 * LOG2E
    q0 = pl.multiple_of(qi * tq, tq)
    lanes = [slice(j * hd, (j + 1) * hd) for j in range(FOX_PAIR)]

    @pl.when(qi == 0)
    def _stage_values():
        for j in range(FOX_PAIR):
            va_ref[j, :, 0:hd] = v_ref[:, lanes[j]]
            va_ref[j, :, hd:2 * hd] = jnp.ones((t, hd), BF16)

    cq2_rep = [jnp.broadcast_to(c_ref[j, :, pl.ds(q0, tq)], (hd, tq)).T * LOG2E for j in range(FOX_PAIR)]
    cq2 = [cr[:, 0:1] for cr in cq2_rep]
    row = lax.broadcasted_iota(jnp.int32, (tq, tq), 0)
    col = lax.broadcasted_iota(jnp.int32, (tq, tq), 1)

    def scores(j, k0, masked):
        s = lax.dot_general(q_ref[:, lanes[j]], k_ref[pl.ds(k0, tq), lanes[j]], NT_DIMS, preferred_element_type=F32)
        t2 = s * c1 - c_ref[j, :, pl.ds(k0, tq)] * LOG2E
        return jnp.where(col <= row, t2, NEG_BIG) if masked else t2

    shift = []
    ones = jnp.ones((hd, hd), BF16)
    for j in range(FOX_PAIR):
        qk = q_ref[:, lanes[j]] * k_ref[pl.ds(q0, tq), lanes[j]]
        shift.append(jnp.dot(qk, ones, preferred_element_type=F32) * c1 - cq2_rep[j])

    def fast_tile(k0, accs, masked):
        out = []
        for j in range(FOX_PAIR):
            t2 = scores(j, k0, masked)
            e = jnp.concatenate([t2[:, i:i + hd] - shift[j] for i in range(0, tq, hd)], axis=1)
            out.append(accs[j] + jnp.dot(jnp.exp2(e).astype(BF16), va_ref[j, pl.ds(k0, tq), :],
                                         preferred_element_type=F32))
        return tuple(out)

    def fast_part(j, r0, nr, kk, nk):
        s = lax.dot_general(q_ref[r0:r0 + nr, lanes[j]], k_ref[pl.ds(q0 + kk, nk), lanes[j]], NT_DIMS,
                            preferred_element_type=F32)
        t2 = s * c1 - c_ref[j, :, pl.ds(q0 + kk, nk)] * LOG2E
        rr = lax.broadcasted_iota(jnp.int32, (nr, nk), 0) + r0
        cc = lax.broadcasted_iota(jnp.int32, (nr, nk), 1) + kk
        t2 = jnp.where(cc <= rr, t2, NEG_BIG)
        e = jnp.concatenate([t2[:, i:i + hd] - shift[j][r0:r0 + nr] for i in range(0, nk, hd)], axis=1)
        return jnp.dot(jnp.exp2(e).astype(BF16), va_ref[j, pl.ds(q0 + kk, nk), :], preferred_element_type=F32)

    def fast_diag(accs):
        half = tq // 2
        out = []
        for j in range(FOX_PAIR):
            left = fast_part(j, 0, tq, 0, half)
            right = fast_part(j, half, half, half, half)
            out.append(accs[j] + left + jnp.concatenate([jnp.zeros((half, 2 * hd), F32), right], axis=0))
        return tuple(out)

    for qv in range(t // tq):
        @pl.when(qi == qv)
        def _fast(qv=qv):
            accs = tuple(jnp.zeros((tq, 2 * hd), F32) for _ in range(FOX_PAIR))
            for i in range(qv):
                accs = fast_tile(i * tq, accs, False)
            accs = fast_diag(accs)
            probe = jnp.float32(0.0)
            for j in range(FOX_PAIR):
                num, den = accs[j][:, 0:hd], accs[j][:, hd:2 * hd]
                o_ref[:, lanes[j]] = (num / den * _silu(fg_ref[:, lanes[j]].astype(F32))).astype(o_ref.dtype)
                probe = probe + jnp.sum(accs[j] * 0.0)
            ok_ref[0] = jnp.where(probe == 0.0, 1, 0)

    @pl.when(ok_ref[0] == 0)
    def _running_max():
        def tile(k0, carry, masked):
            out = []
            for j in range(FOX_PAIR):
                m_i, l_i, acc = carry[j]
                t2 = scores(j, k0, masked)
                m_new = jnp.maximum(m_i, jnp.max(t2, axis=1, keepdims=True) + cq2[j])
                p = jnp.exp2(t2 - (m_new - cq2[j]))
                alpha = jnp.exp2(m_i - m_new)
                l_new = alpha * l_i + jnp.sum(p, axis=1, keepdims=True)
                acc = alpha * acc + jnp.dot(p.astype(BF16), v_ref[pl.ds(k0, tq), lanes[j]],
                                            preferred_element_type=F32)
                out.append((m_new, l_new, acc))
            return tuple(out)

        init = tuple((jnp.full((tq, 1), NEG_BIG, F32), jnp.zeros((tq, 1), F32), jnp.zeros((tq, hd), F32))
                     for _ in range(FOX_PAIR))
        carry = lax.fori_loop(0, qi, lambda i, cr: tile(pl.multiple_of(i * tq, tq), cr, False), init)
        carry = tile(q0, carry, True)
        for j in range(FOX_PAIR):
            _, l_i, acc = carry[j]
            o_ref[:, lanes[j]] = (acc / l_i * _silu(fg_ref[:, lanes[j]].astype(F32))).astype(o_ref.dtype)


def _fox_prompt(uq, uk, uv, ug, ct, nbatch, t):
    tq = min(FOX_TQ, t)
    nq = t // tq
    w = FOX_PAIR * FOX_HEAD_DIM
    nb = D_MODEL // w
    return pl.pallas_call(
        functools.partial(_fox_prompt_kernel, tq=tq),
        out_shape=jax.ShapeDtypeStruct((nbatch * t, D_MODEL), BF16),
        grid=(nbatch, nb, nq),
        in_specs=[
            pl.BlockSpec((tq, w), lambda b, h, i: (b * nq + i, h)),
            pl.BlockSpec((t, w), lambda b, h, i: (b, h)),
            pl.BlockSpec((t, w), lambda b, h, i: (b, h)),
            pl.BlockSpec((tq, w), lambda b, h, i: (b * nq + i, h)),
            pl.BlockSpec((None, FOX_PAIR, 1, t), lambda b, h, i: (b, h, 0, 0)),
        ],
        out_specs=pl.BlockSpec((tq, w), lambda b, h, i: (b * nq + i, h)),
        scratch_shapes=[pltpu.VMEM((FOX_PAIR, t, 2 * FOX_HEAD_DIM), BF16), pltpu.SMEM((1,), jnp.int32)],
        compiler_params=_params("parallel", "parallel", "arbitrary"),
        name="fox_prompt",
    )(uq, uk, uv, ug, ct)


def _fox_sample_kernel(q_ref, kp_ref, vp_ref, kn_ref, vn_ref, fg_ref, c_ref, o_ref, *, p, t):
    c1 = FOX_HEAD_DIM ** -0.5 * LOG2E
    kp = kp_ref[...].reshape(p, D_MODEL).astype(BF16)
    vp = vp_ref[...].reshape(p, D_MODEL).astype(BF16)
    r = lax.broadcasted_iota(jnp.int32, (t, t), 0)
    c = lax.broadcasted_iota(jnp.int32, (t, t), 1)
    for j in range(FOX_HEADS):
        sl = slice(j * FOX_HEAD_DIM, (j + 1) * FOX_HEAD_DIM)
        q = q_ref[:, sl]
        cq2 = _col_from_row(c_ref[j, :, p:p + t]) * LOG2E
        s_p = lax.dot_general(q, kp[:, sl], NT_DIMS, preferred_element_type=F32) * c1 - c_ref[j, :, 0:p] * LOG2E
        s_n = (lax.dot_general(q, kn_ref[:, sl], NT_DIMS, preferred_element_type=F32) * c1
               - c_ref[j, :, p:p + t] * LOG2E)
        s_n = jnp.where(c <= r, s_n, NEG_BIG)
        m = jnp.maximum(jnp.max(s_p, axis=1, keepdims=True), jnp.max(s_n, axis=1, keepdims=True)) + cq2
        e_p = jnp.exp2(s_p - (m - cq2))
        e_n = jnp.exp2(s_n - (m - cq2))
        inv = 1.0 / (jnp.sum(e_p, axis=1, keepdims=True) + jnp.sum(e_n, axis=1, keepdims=True))
        o = (jnp.dot((e_p * inv).astype(BF16), vp[:, sl], preferred_element_type=F32)
             + jnp.dot((e_n * inv).astype(BF16), vn_ref[:, sl], preferred_element_type=F32))
        o_ref[:, sl] = (o * _silu(fg_ref[:, sl].astype(F32))).astype(o_ref.dtype)


def _fox_sample(uq, uk, uv, ug, k_cache, v_cache, layer, ct, nbatch, t):
    p = k_cache.shape[2]
    past = pl.BlockSpec((None, None, p, FOX_HEADS, FOX_HEAD_DIM), lambda b: (layer, b, 0, 0, 0))
    col = pl.BlockSpec((t, D_MODEL), lambda b: (b, 0))
    return pl.pallas_call(
        functools.partial(_fox_sample_kernel, p=p, t=t),
        out_shape=jax.ShapeDtypeStruct((nbatch * t, D_MODEL), BF16),
        grid=(nbatch,),
        in_specs=[col, past, past, col, col, col,
                  pl.BlockSpec((None, FOX_HEADS, 1, p + t), lambda b: (b, 0, 0, 0))],
        out_specs=pl.BlockSpec((t, D_MODEL), lambda b: (b, 0)),
        compiler_params=_params("parallel"),
        name="fox_sample",
    )(uq, k_cache, v_cache, uk, uv, ug, ct)


def _mem_kernel(q_ref, g_ref, k_ref, v_ref, o_ref):
    scale = MEM_HEAD_DIM ** -0.5
    nm = k_ref.shape[0]
    k = k_ref[...].reshape(nm, D_MODEL).astype(BF16)
    v = v_ref[...].reshape(nm, D_MODEL).astype(BF16)
    for h in range(MEM_HEADS):
        sl = slice(h * MEM_HEAD_DIM, (h + 1) * MEM_HEAD_DIM)
        s = lax.dot_general(q_ref[:, sl], k[:, sl], NT_DIMS, preferred_element_type=F32) * scale
        e = jnp.exp(s - jnp.max(s, axis=1, keepdims=True))
        p = e * (1.0 / jnp.sum(e, axis=1, keepdims=True))
        o = jnp.dot(p.astype(BF16), v[:, sl], preferred_element_type=F32)
        o_ref[:, sl] = (o * _silu(g_ref[:, sl].astype(F32))).astype(o_ref.dtype)


def _mem_attend(u3, mk, mv, nbatch, t, layer=None):
    tq = min(512, t)
    nq = t // tq
    if layer is None:
        nm = mk.shape[1]
        kv = pl.BlockSpec((None, nm, D_MODEL), lambda b, i: (b, 0, 0))
    else:
        nm = mk.shape[2]
        kv = pl.BlockSpec((None, None, nm, MEM_HEADS, MEM_HEAD_DIM), lambda b, i: (layer, b, 0, 0, 0))
    return pl.pallas_call(
        _mem_kernel,
        out_shape=jax.ShapeDtypeStruct((nbatch * t, D_MODEL), BF16),
        grid=(nbatch, nq),
        in_specs=[pl.BlockSpec((tq, D_MODEL), lambda b, i: (b * nq + i, U3_Q)),
                  pl.BlockSpec((tq, D_MODEL), lambda b, i: (b * nq + i, U3_G)), kv, kv],
        out_specs=pl.BlockSpec((tq, D_MODEL), lambda b, i: (b * nq + i, 0)),
        compiler_params=_params("parallel", "arbitrary"),
        name="mem_attend",
    )(u3, u3, mk, mv)


def _merge_kernel(ys_ref, yf_ref, ym_ref, ws_ref, wf_ref, wm_ref, gs_ref, gf_ref, gm_ref, o_ref):
    def branch(y_ref, w_ref, g_ref):
        return _sigmoid(g_ref[...].astype(F32)) * jnp.dot(y_ref[...], w_ref[...], preferred_element_type=F32)
    o_ref[...] = (branch(ys_ref, ws_ref, gs_ref) + branch(yf_ref, wf_ref, gf_ref)
                  + branch(ym_ref, wm_ref, gm_ref)).astype(o_ref.dtype)


def _merge(u, y_ssd, y_fox, y_mem, w_s, w_f, w_m, tm, tn):
    m = y_ssd.shape[0]
    nj = D_MODEL // tn
    yspec = pl.BlockSpec((tm, D_MODEL), lambda i, j: (i, 0))
    wspec = pl.BlockSpec((D_MODEL, tn), lambda i, j: (0, j))
    gspec = lambda col: pl.BlockSpec((tm, tn), lambda i, j: (i, col * nj + j))
    return pl.pallas_call(
        _merge_kernel,
        out_shape=jax.ShapeDtypeStruct((m, D_MODEL), BF16),
        grid=(m // tm, nj),
        in_specs=[yspec, yspec, yspec, wspec, wspec, wspec, gspec(U3_GS), gspec(U3_GF), gspec(U3_GM)],
        out_specs=pl.BlockSpec((tm, tn), lambda i, j: (i, j)),
        compiler_params=_params("parallel", "arbitrary"),
        name="gated_merge",
    )(y_ssd, y_fox, y_mem, w_s, w_f, w_m, u, u, u)


def _final_kernel(mg_ref, w_ref, x_ref, g_ref, o_ref, *, normalize):
    xo = x_ref[...] + jnp.dot(mg_ref[...], w_ref[...], preferred_element_type=F32)
    if normalize:
        ms = jnp.mean(xo * xo, axis=-1, keepdims=True)
        xo = xo * lax.rsqrt(ms + EPS) * g_ref[...]
    o_ref[...] = xo


def _final(merged, w_out, x, g_final, tm, normalize):
    m = x.shape[0]
    return pl.pallas_call(
        functools.partial(_final_kernel, normalize=normalize),
        out_shape=jax.ShapeDtypeStruct((m, D_MODEL), F32),
        grid=(m // tm,),
        in_specs=[pl.BlockSpec((tm, D_MODEL), lambda i: (i, 0)),
                  pl.BlockSpec((D_MODEL, D_MODEL), lambda i: (0, 0)),
                  pl.BlockSpec((tm, D_MODEL), lambda i: (i, 0)),
                  pl.BlockSpec((1, D_MODEL), lambda i: (0, 0))],
        out_specs=pl.BlockSpec((tm, D_MODEL), lambda i: (i, 0)),
        compiler_params=_params("parallel"),
        name="out_proj_norm",
    )(merged, w_out, x, g_final.reshape(1, D_MODEL))


def _row_tile(m, pref):
    t = pref
    while m % t:
        t //= 2
    return t


def _layer(x, conv0, h0, caches, logf_past, mem_k, mem_v, wd, g_final, last):
    nbatch, t, d = x.shape
    m = nbatch * t
    x2 = x.reshape(m, d)
    tm = _row_tile(m, 1024)

    wt = wd["w_in_t"]
    _, (r2, _), (r3, n3) = W_IN_SEGMENTS
    tm2 = _row_tile(m, 2048)
    uq, h = _mm_nt_norm(x2, wd["g_norm"], wt, r2, D_MODEL, tm, 1024, "norm_in_proj_fox_q")
    u1, small = _mm_ssd(h, wt, tm2, 1024)
    uk, k_new = _mm_nt_heads(h, wt, r2 + D_MODEL, tm, 1024, "in_proj_fox_k")
    uv, v_new = _mm_nt_heads(h, wt, r2 + 2 * D_MODEL, tm, 1024, "in_proj_fox_v")
    ug = _mm_nt(h, wt, r2 + 3 * D_MODEL, D_MODEL, BF16, tm2, 1024, "in_proj_fox_gate")
    u3 = _mm_nt(h, wt, r3, n3, BF16, tm2, 1024, "in_proj_mem_gates")

    past_t = None if logf_past is None else jnp.transpose(logf_past, (0, 2, 1))
    logf_t, ct = _logf(small, wd["b_forget"], past_t, nbatch, t)
    ct = ct.reshape(nbatch, FOX_HEADS, 1, ct.shape[-1])

    conv0p = jnp.pad(conv0, ((0, 0), (8 - (SSD_CONV - 1), 0), (0, 0)))
    y_ssd, h_final, new_conv = _ssd(u1, small, conv0p, h0, wd["w_conv"], wd["b_conv"], wd["dt_bias"], wd["a_log"],
                                    wd["d_skip"], wd["g_ssd_out"], nbatch, t)

    if caches is None:
        y_fox = _fox_prompt(uq, uk, uv, ug, ct, nbatch, t)
        y_mem = _mem_attend(u3, mem_k, mem_v, nbatch, t)
    else:
        layer, fox_k, fox_v = caches
        y_fox = _fox_sample(uq, uk, uv, ug, fox_k, fox_v, layer, ct, nbatch, t)
        y_mem = _mem_attend(u3, mem_k, mem_v, nbatch, t, layer)
    merged = _merge(u3, y_ssd, y_fox, y_mem, wd["w_o_ssd"], wd["w_o_fox"], wd["w_o_mem"], tm, 512)
    y = _final(merged, wd["w_out"], x2, g_final, _row_tile(m, 512), last)

    return (y.reshape(nbatch, t, d), new_conv, h_final,
            k_new.reshape(nbatch, t, FOX_HEADS, FOX_HEAD_DIM), v_new.reshape(nbatch, t, FOX_HEADS, FOX_HEAD_DIM),
            jnp.transpose(logf_t, (0, 2, 1)))


def kernel(x_prompt, x_sample, mem_prompt, cache_fox_k, cache_fox_v, cache_fox_logf, state_ssd, state_ssd_conv,
           cache_mem_k, cache_mem_v, g_norm, w_in, w_conv, b_conv, dt_bias, a_log, d_skip, g_ssd_out, b_forget,
           g_mem, w_mem_kv, w_o_ssd, w_o_fox, w_o_mem, w_out, g_final):
    depth = w_in.shape[0]
    xp, xs = x_prompt, x_sample
    bp = xp.shape[0]
    n_mem = mem_prompt.shape[1]
    outs = [[] for _ in range(12)]
    for l in range(depth):
        wkv = _cast_bf16(w_mem_kv[l])
        wd = {
            "g_norm": g_norm[l],
            "w_in_t": _cast_bf16(jnp.transpose(w_in[l])),
            "w_conv": w_conv[l], "b_conv": b_conv[l], "dt_bias": dt_bias[l], "a_log": a_log[l], "d_skip": d_skip[l],
            "g_ssd_out": g_ssd_out[l], "b_forget": b_forget[l],
            "w_o_ssd": _cast_bf16(w_o_ssd[l]), "w_o_fox": _cast_bf16(w_o_fox[l]),
            "w_o_mem": _cast_bf16(w_o_mem[l]), "w_out": _cast_bf16(w_out[l]),
        }
        hm = _rmsnorm(mem_prompt.reshape(bp * n_mem, D_MODEL), g_mem[l], _row_tile(bp * n_mem, 256))
        tmm = _row_tile(bp * n_mem, 512)
        mk, mk_b = _mm_heads(hm, wkv, 0, MEM_HEADS, MEM_HEAD_DIM, tmm, "mem_k_proj")
        mv, mv_b = _mm_heads(hm, wkv, D_MODEL, MEM_HEADS, MEM_HEAD_DIM, tmm, "mem_v_proj")

        xp, c_p, h_p, k_p, v_p, lf_p = _layer(
            xp, jnp.zeros((bp, SSD_CONV - 1, SSD_CONV_CH), F32),
            jnp.zeros((bp, SSD_HEADS, SSD_HEAD_DIM, SSD_D_STATE), F32), None, None,
            mk_b.reshape(bp, n_mem, D_MODEL), mv_b.reshape(bp, n_mem, D_MODEL), wd, g_final, l == depth - 1)
        xs, c_s, h_s, k_s, v_s, lf_s = _layer(
            xs, state_ssd_conv[l], state_ssd[l], (l, cache_fox_k, cache_fox_v), cache_fox_logf[l],
            cache_mem_k, cache_mem_v, wd, g_final, l == depth - 1)
        for lst, val in zip(outs, (k_p, v_p, lf_p, h_p, c_p,
                                   mk.reshape(bp, n_mem, MEM_HEADS, MEM_HEAD_DIM),
                                   mv.reshape(bp, n_mem, MEM_HEADS, MEM_HEAD_DIM),
                                   k_s, v_s, lf_s, h_s, c_s)):
            lst.append(val)
    return (xp, xs) + tuple(jnp.stack(o) for o in outs)
```

```python
import functools

import numpy as np
import jax
import jax.numpy as jnp
from jax import lax
from jax.experimental import pallas as pl
from jax.experimental.pallas import tpu as pltpu

F32 = jnp.float32
BF16 = jnp.bfloat16

EPS = 1e-6
D_MODEL = 2048
SSD_HEAD_DIM = 64
SSD_HEADS = 32
SSD_GROUPS = 4
SSD_D_STATE = 128
SSD_CONV = 4
SSD_BC = SSD_GROUPS * SSD_D_STATE
SSD_CONV_CH = D_MODEL + 2 * SSD_BC
FOX_HEADS = 16
FOX_HEAD_DIM = 128
MEM_HEADS = 4
MEM_HEAD_DIM = 512
SSD_CHUNK = 64
SSD_NB = 4
NEG_BIG = -1e30

VMEM_LIMIT = 56 * 1024 * 1024

W_IN_SEGMENTS = ((0, D_MODEL + SSD_CONV_CH),
                 (D_MODEL + SSD_CONV_CH + SSD_HEADS, 4 * D_MODEL),
                 (D_MODEL + SSD_CONV_CH + SSD_HEADS + 4 * D_MODEL + FOX_HEADS, 5 * D_MODEL))
W_IN_DT = D_MODEL + SSD_CONV_CH
W_IN_FF = W_IN_SEGMENTS[1][0] + 4 * D_MODEL
U1_Z, U1_XS = 0, 1
U3_Q, U3_G, U3_GS, U3_GF, U3_GM = 0, 1, 2, 3, 4
LOG2E = 1.4426950408889634


def _params(*sem):
    return pltpu.CompilerParams(dimension_semantics=sem, vmem_limit_bytes=VMEM_LIMIT)


def _split3(x):
    hi = x.astype(BF16)
    r1 = x - hi.astype(F32)
    mid = r1.astype(BF16)
    lo = (r1 - mid.astype(F32)).astype(BF16)
    return hi, mid, lo


def _softplus(x):
    return jnp.maximum(x, 0.0) + jnp.log1p(jnp.exp(-jnp.abs(x)))


def _sigmoid(x):
    return 0.5 * jnp.tanh(0.5 * x) + 0.5


def _silu(x):
    return x * _sigmoid(x)


def _rmsnorm_kernel(x_ref, g_ref, o_ref):
    x = x_ref[...]
    ms = jnp.mean(x * x, axis=-1, keepdims=True)
    o_ref[...] = (x * lax.rsqrt(ms + EPS) * g_ref[...]).astype(o_ref.dtype)


def _rmsnorm(x, g, tm):
    m, d = x.shape
    return pl.pallas_call(
        _rmsnorm_kernel,
        out_shape=jax.ShapeDtypeStruct((m, d), BF16),
        grid=(m // tm,),
        in_specs=[pl.BlockSpec((tm, d), lambda i: (i, 0)), pl.BlockSpec((1, d), lambda i: (0, 0))],
        out_specs=pl.BlockSpec((tm, d), lambda i: (i, 0)),
        compiler_params=_params("parallel"),
        name="rmsnorm",
    )(x, g.reshape(1, d))


def _mm_kernel(h_ref, w_ref, o_ref):
    o_ref[...] = jnp.dot(h_ref[...], w_ref[...], preferred_element_type=F32).astype(o_ref.dtype)


def _mm(h, w, out_dtype, tm, tn, name, w_col0=0, n=None):
    m, k = h.shape
    n = w.shape[1] if n is None else n
    j0 = w_col0 // tn
    return pl.pallas_call(
        _mm_kernel,
        out_shape=jax.ShapeDtypeStruct((m, n), out_dtype),
        grid=(m // tm, n // tn),
        in_specs=[pl.BlockSpec((tm, k), lambda i, j: (i, 0)), pl.BlockSpec((k, tn), lambda i, j: (0, j0 + j))],
        out_specs=pl.BlockSpec((tm, tn), lambda i, j: (i, j)),
        compiler_params=_params("parallel", "arbitrary"),
        name=name,
    )(h, w)


def _mm_heads_kernel(h_ref, w_ref, o_ref, ob_ref):
    acc = jnp.dot(h_ref[...], w_ref[...], preferred_element_type=F32)
    o_ref[...] = acc.reshape(o_ref.shape)
    ob_ref[...] = acc.astype(ob_ref.dtype)


def _mm_heads(h, w, w_col0, heads, head_dim, tm, name):
    m, k = h.shape
    n = heads * head_dim
    return pl.pallas_call(
        _mm_heads_kernel,
        out_shape=(jax.ShapeDtypeStruct((m, heads, head_dim), F32), jax.ShapeDtypeStruct((m, n), BF16)),
        grid=(m // tm,),
        in_specs=[pl.BlockSpec((tm, k), lambda i: (i, 0)), pl.BlockSpec((k, n), lambda i: (0, w_col0 // n))],
        out_specs=(pl.BlockSpec((tm, heads, head_dim), lambda i: (i, 0, 0)), pl.BlockSpec((tm, n), lambda i: (i, 0))),
        compiler_params=_params("parallel"),
        name=name,
    )(h, w)


def _mm_nt_kernel(h_ref, wt_ref, o_ref):
    o_ref[...] = lax.dot_general(h_ref[...], wt_ref[...], NT_DIMS, preferred_element_type=F32).astype(o_ref.dtype)


def _wt_rows_spec(tn, k, row0, grid_rank):
    g = 16
    assert row0 % g == 0 and tn % g == 0
    if grid_rank == 1:
        return pl.BlockSpec((pl.Element(tn), pl.Element(k)), lambda i: (row0, 0))
    return pl.BlockSpec((pl.Element(tn), pl.Element(k)), lambda i, j: ((row0 // g + j * (tn // g)) * g, 0))


def _mm_nt(h, wt, row0, n, out_dtype, tm, tn, name):
    m, k = h.shape
    return pl.pallas_call(
        _mm_nt_kernel,
        out_shape=jax.ShapeDtypeStruct((m, n), out_dtype),
        grid=(m // tm, n // tn),
        in_specs=[pl.BlockSpec((tm, k), lambda i, j: (i, 0)), _wt_rows_spec(tn, k, row0, 2)],
        out_specs=pl.BlockSpec((tm, tn), lambda i, j: (i, j)),
        compiler_params=_params("parallel", "arbitrary"),
        name=name,
    )(h, wt)


def _mm_nt_norm_kernel(x_ref, g_ref, wt_ref, o_ref, h_ref):
    x = x_ref[...]
    ms = jnp.mean(x * x, axis=-1, keepdims=True)
    hn = (x * lax.rsqrt(ms + EPS) * g_ref[...]).astype(BF16)
    h_ref[...] = hn
    o_ref[...] = lax.dot_general(hn, wt_ref[...], NT_DIMS, preferred_element_type=F32).astype(o_ref.dtype)


def _mm_nt_norm(x, g, wt, row0, n, tm, tn, name):
    m, k = x.shape
    return pl.pallas_call(
        _mm_nt_norm_kernel,
        out_shape=(jax.ShapeDtypeStruct((m, n), BF16), jax.ShapeDtypeStruct((m, k), BF16)),
        grid=(m // tm, n // tn),
        in_specs=[pl.BlockSpec((tm, k), lambda i, j: (i, 0)), pl.BlockSpec((1, k), lambda i, j: (0, 0)),
                  _wt_rows_spec(tn, k, row0, 2)],
        out_specs=(pl.BlockSpec((tm, tn), lambda i, j: (i, j)), pl.BlockSpec((tm, k), lambda i, j: (i, 0))),
        compiler_params=_params("parallel", "arbitrary"),
        name=name,
    )(x, g.reshape(1, k), wt)


def _mm_ssd_kernel(h_ref, wt_ref, wdt_ref, wff_ref, o_ref, small_ref):
    o_ref[...] = lax.dot_general(h_ref[...], wt_ref[...], NT_DIMS, preferred_element_type=F32).astype(o_ref.dtype)

    @pl.when(pl.program_id(1) == 0)
    def _narrow():
        w = jnp.concatenate([wdt_ref[...], wff_ref[...]], axis=0)
        small_ref[...] = lax.dot_general(h_ref[...], w, NT_DIMS, preferred_element_type=F32)


def _mm_ssd(h, wt, tm, tn):
    m, k = h.shape
    row0, n = W_IN_SEGMENTS[0]
    fixed = lambda r0: pl.BlockSpec((pl.Element(128), pl.Element(k)), lambda i, j: (r0, 0))
    return pl.pallas_call(
        _mm_ssd_kernel,
        out_shape=(jax.ShapeDtypeStruct((m, n), BF16), jax.ShapeDtypeStruct((m, 256), F32)),
        grid=(m // tm, n // tn),
        in_specs=[pl.BlockSpec((tm, k), lambda i, j: (i, 0)), _wt_rows_spec(tn, k, row0, 2),
                  fixed(W_IN_DT), fixed(W_IN_FF)],
        out_specs=(pl.BlockSpec((tm, tn), lambda i, j: (i, j)), pl.BlockSpec((tm, 256), lambda i, j: (i, 0))),
        compiler_params=_params("parallel", "arbitrary"),
        name="in_proj_ssd",
    )(h, wt, wt, wt)


def _mm_nt_heads_kernel(h_ref, wt_ref, o_ref, oh_ref):
    acc = lax.dot_general(h_ref[...], wt_ref[...], NT_DIMS, preferred_element_type=F32)
    o_ref[...] = acc.astype(o_ref.dtype)
    oh_ref[...] = acc.reshape(oh_ref.shape)


def _mm_nt_heads(h, wt, row0, tm, tn, name):
    m, k = h.shape
    hb = tn // FOX_HEAD_DIM
    return pl.pallas_call(
        _mm_nt_heads_kernel,
        out_shape=(jax.ShapeDtypeStruct((m, D_MODEL), BF16),
                   jax.ShapeDtypeStruct((m, FOX_HEADS, FOX_HEAD_DIM), F32)),
        grid=(m // tm, D_MODEL // tn),
        in_specs=[pl.BlockSpec((tm, k), lambda i, j: (i, 0)), _wt_rows_spec(tn, k, row0, 2)],
        out_specs=(pl.BlockSpec((tm, tn), lambda i, j: (i, j)),
                   pl.BlockSpec((tm, hb, FOX_HEAD_DIM), lambda i, j: (i, j, 0))),
        compiler_params=_params("parallel", "arbitrary"),
        name=name,
    )(h, wt)


def _cast_kernel(a_ref, o_ref):
    o_ref[...] = a_ref[...].astype(o_ref.dtype)


def _cast_bf16(w):
    rows, cols = w.shape
    tr = max(d for d in range(16, 1025, 16) if rows % d == 0)
    return pl.pallas_call(
        _cast_kernel,
        out_shape=jax.ShapeDtypeStruct((rows, cols), BF16),
        grid=(rows // tr,),
        in_specs=[pl.BlockSpec((tr, cols), lambda i: (i, 0))],
        out_specs=pl.BlockSpec((tr, cols), lambda i: (i, 0)),
        compiler_params=_params("parallel"),
        name="weight_cast",
    )(w)


def _upper3(tb):
    k = lax.broadcasted_iota(jnp.int32, (3 * tb, tb), 0) % tb
    t = lax.broadcasted_iota(jnp.int32, (3 * tb, tb), 1)
    return jnp.where(k <= t, 1.0, 0.0).astype(BF16)


def _cumsum_lanes(x, carry, tb):
    n = x.shape[1]
    u3 = _upper3(tb)
    out = []
    for s in range(0, n, tb):
        hi, mid, lo = _split3(x[:, s:s + tb])
        c = jnp.dot(jnp.concatenate([hi, mid, lo], axis=1), u3, preferred_element_type=F32) + carry
        carry = c[:, tb - 1:tb]
        out.append(c)
    return out, carry


def _logf_kernel(*refs, nb, t, p):
    if p:
        ffc_ref, bias_ref, past_ref, lf_ref, c_ref = refs
    else:
        ffc_ref, bias_ref, lf_ref, c_ref = refs
    fft = ffc_ref[...].T[0:FOX_HEADS, :]
    lf_all = -_softplus(-(fft + bias_ref[...]))
    for b in range(nb):
        lf = lf_all[:, b * t:(b + 1) * t]
        lf_ref[b] = lf
        carry = jnp.zeros((FOX_HEADS, 1), F32)
        if p:
            tbp = min(256, p)
            blocks, carry = _cumsum_lanes(past_ref[b], carry, tbp)
            for i, c in enumerate(blocks):
                c_ref[b, :, i * tbp:(i + 1) * tbp] = c
        tb = min(256, t)
        blocks, carry = _cumsum_lanes(lf, carry, tb)
        for i, c in enumerate(blocks):
            c_ref[b, :, p + i * tb:p + (i + 1) * tb] = c


def _logf(small, b_forget, past_t, nbatch, t):
    p = 0 if past_t is None else past_t.shape[2]
    nb = 1 if t % 128 == 0 else nbatch
    grid = (nbatch // nb,)
    in_specs = [pl.BlockSpec((nb * t, 128), lambda i: (i, 1)),
                pl.BlockSpec((FOX_HEADS, 1), lambda i: (0, 0))]
    args = [small, b_forget.reshape(FOX_HEADS, 1)]
    if p:
        in_specs.append(pl.BlockSpec((nb, FOX_HEADS, p), lambda i: (i, 0, 0)))
        args.append(past_t)
    return pl.pallas_call(
        functools.partial(_logf_kernel, nb=nb, t=t, p=p),
        out_shape=(jax.ShapeDtypeStruct((nbatch, FOX_HEADS, t), F32),
                   jax.ShapeDtypeStruct((nbatch, FOX_HEADS, p + t), F32)),
        grid=grid,
        in_specs=in_specs,
        out_specs=(pl.BlockSpec((nb, FOX_HEADS, t), lambda i: (i, 0, 0)),
                   pl.BlockSpec((nb, FOX_HEADS, p + t), lambda i: (i, 0, 0))),
        compiler_params=_params("parallel"),
        name="logf_cumsum",
    )(*args)


def _expand_matrix():
    r = np.arange(128)[:, None]
    c = np.arange(D_MODEL)[None, :]
    return jnp.asarray(((r < 96) & ((r % SSD_HEADS) == (c // SSD_HEAD_DIM))).astype(np.float32), dtype=BF16)


def _pack3(x):
    lane = lax.broadcasted_iota(jnp.int32, x.shape, 1)
    x = jnp.where(lane < SSD_HEADS, x, 0.0)
    hi, mid, lo = _split3(x)
    packed = hi.astype(F32) + pltpu.roll(mid.astype(F32), SSD_HEADS, 1) + pltpu.roll(lo.astype(F32), 2 * SSD_HEADS, 1)
    return packed.astype(BF16)


HIST = 16


def _shift_matrix(L):
    sh = np.zeros((SSD_CONV * L, 2 * HIST + L), np.float32)
    for d in range(SSD_CONV):
        for t in range(L):
            sh[d * L + t, 2 * HIST + t - d] = 1.0
            if t - d < 0:
                sh[d * L + t, HIST + t - d] = 1.0
    return jnp.asarray(sh, dtype=BF16)


def _ssd_kernel(xs_ref, b_ref, c_ref, z_ref, sm_ref, conv0_ref, h0_ref,
                wconv_ref, bconv_ref, dtb_ref, alog_ref, dskip_ref, gout_ref, e3_ref, sh_ref,
                y_ref, hfin_ref, convout_ref,
                st_ref, xq_ref):
    ci = pl.program_id(1)
    nb, L = xs_ref.shape[0], xs_ref.shape[1]
    P = SSD_HEAD_DIM
    W = D_MODEL
    gw = W // SSD_GROUPS
    hpm = 256 // L

    @pl.when(ci == 0)
    def _init():
        for bi in range(nb):
            st_ref[bi] = h0_ref[bi].T
            hist = jnp.concatenate([jnp.zeros((HIST - 8, SSD_CONV_CH), F32), conv0_ref[bi]], axis=0)
            hist_hi = hist.astype(BF16)
            xq_ref[bi, 0:HIST, :] = hist_hi
            xq_ref[bi, HIST:2 * HIST, :] = (hist - hist_hi.astype(F32)).astype(BF16)

    lane = lax.broadcasted_iota(jnp.int32, (L, 128), 1)
    kk = lax.broadcasted_iota(jnp.int32, (L, 3 * L), 1) % L
    ll = lax.broadcasted_iota(jnp.int32, (L, 3 * L), 0)
    tri3 = jnp.where(kk <= ll, 1.0, 0.0).astype(BF16)
    row = lax.broadcasted_iota(jnp.int32, (L, W), 0)
    sidx = lax.broadcasted_iota(jnp.int32, (L, W), 1) % L
    rb = lax.broadcasted_iota(jnp.int32, (256, hpm * P), 0) // L
    cb = lax.broadcasted_iota(jnp.int32, (256, hpm * P), 1) // P
    blockmask = rb == cb
    a_small = -jnp.exp(alog_ref[...])
    e3 = e3_ref[...]

    def chunk(bi):
        xq_ref[bi, 2 * HIST:, 0:W] = xs_ref[bi]
        xq_ref[bi, 2 * HIST:, W:W + SSD_BC] = b_ref[bi]
        xq_ref[bi, 2 * HIST:, W + SSD_BC:] = c_ref[bi]

        def conv(lo, hi):
            sh = jnp.dot(sh_ref[...], xq_ref[bi, :, lo:hi], preferred_element_type=F32)
            acc = bconv_ref[:, lo:hi]
            for d in range(SSD_CONV):
                acc = acc + sh[d * L:(d + 1) * L] * wconv_ref[SSD_CONV - 1 - d:SSD_CONV - d, lo:hi]
            return _silu(acc)

        xs = conv(0, W)
        bm = conv(W, W + SSD_BC).astype(BF16)
        cm = conv(W + SSD_BC, W + 2 * SSD_BC).astype(BF16)
        xq_ref[bi, 0:HIST, :] = jnp.zeros((HIST, SSD_CONV_CH), BF16)
        xq_ref[bi, HIST:2 * HIST, :] = xq_ref[bi, HIST + L:2 * HIST + L, :]

        dt = _softplus(sm_ref[bi] + dtb_ref[...])
        dta = jnp.where(lane < SSD_HEADS, dt * a_small, 0.0)
        hi, mid, lo = _split3(dta)
        acum = jnp.dot(tri3, jnp.concatenate([hi, mid, lo], axis=0), preferred_element_type=F32)
        a_x = jnp.dot(_pack3(acum), e3, preferred_element_type=F32)
        dt_x = jnp.dot(_pack3(dt), e3, preferred_element_type=F32)

        a_row = jnp.sum(jnp.where(row == sidx, a_x, 0.0), axis=0, keepdims=True)
        ldec = jnp.exp(jnp.where(row >= sidx, a_x - a_row, NEG_BIG))
        cbx = []
        for g in range(SSD_GROUPS):
            cg = cm[:, g * SSD_D_STATE:(g + 1) * SSD_D_STATE]
            bg = bm[:, g * SSD_D_STATE:(g + 1) * SSD_D_STATE]
            brep = jnp.concatenate([bg] * (SSD_HEADS // SSD_GROUPS), axis=0)
            cbx.append(lax.dot_general(cg, brep, NT_DIMS, preferred_element_type=F32))
        m = (jnp.concatenate(cbx, axis=1) * ldec).astype(BF16)

        xdt = xs * dt_x
        xdt_b = xdt.astype(BF16)
        y_parts = []
        for j in range(SSD_HEADS // hpm):
            xj = xdt_b[:, j * hpm * P:(j + 1) * hpm * P]
            bd = jnp.where(blockmask, jnp.concatenate([xj] * hpm, axis=0), jnp.zeros((), BF16))
            y_parts.append(jnp.dot(m[:, j * 256:(j + 1) * 256], bd, preferred_element_type=F32))
        y = jnp.concatenate(y_parts, axis=1)

        st = st_ref[bi]
        st_b = st.astype(BF16)
        yo = [jnp.dot(cm[:, g * SSD_D_STATE:(g + 1) * SSD_D_STATE], st_b[:, g * gw:(g + 1) * gw],
                      preferred_element_type=F32) for g in range(SSD_GROUPS)]
        y = y + jnp.concatenate(yo, axis=1) * jnp.exp(a_x)

        a_last = a_x[L - 1:L, :]
        xw = (xdt * jnp.exp(a_last - a_x)).astype(BF16)
        upd = [lax.dot_general(bm[:, g * SSD_D_STATE:(g + 1) * SSD_D_STATE], xw[:, g * gw:(g + 1) * gw],
                               (((0,), (0,)), ((), ())), preferred_element_type=F32) for g in range(SSD_GROUPS)]
        st_ref[bi] = st * jnp.exp(a_last) + jnp.concatenate(upd, axis=1)

        y = y + dskip_ref[...] * xs
        gz = y * _silu(z_ref[bi].astype(F32))
        ms = jnp.mean(gz * gz, axis=-1, keepdims=True)
        y_ref[bi] = (gz * lax.rsqrt(ms + EPS) * gout_ref[...]).astype(y_ref.dtype)

    for bi in range(nb):
        chunk(bi)

    @pl.when(ci == pl.num_programs(1) - 1)
    def _final():
        for bi in range(nb):
            hfin_ref[bi] = st_ref[bi].T
            convout_ref[bi] = xq_ref[bi, HIST:2 * HIST, :].astype(F32)[HIST - 8:, :]


def _ssd(u, small, conv0, h0, w_conv, b_conv, dt_bias, a_log, d_skip, g_ssd_out, nbatch, t):
    L = SSD_CHUNK
    nc = t // L
    nb = SSD_NB if nbatch % SSD_NB == 0 else 1
    pad128 = lambda v: jnp.pad(v.astype(F32), (0, 128 - v.shape[0])).reshape(1, 128)
    rep = lambda v: jnp.repeat(v.astype(F32), SSD_HEAD_DIM).reshape(1, D_MODEL)
    const2 = lambda shape: pl.BlockSpec(shape, lambda b, c: (0, 0))
    seq = lambda width, col: pl.BlockSpec((nb, L, width), lambda b, c: (b, c, col))
    per_seq = lambda rows, width: pl.BlockSpec((nb, rows, width), lambda b, c: (b, 0, 0))
    u3d = u.reshape(nbatch, t, u.shape[1])
    bc0 = (U1_XS + 1) * D_MODEL // SSD_BC
    y, hfin, convout = pl.pallas_call(
        _ssd_kernel,
        out_shape=(jax.ShapeDtypeStruct((nbatch, t, D_MODEL), BF16),
                   jax.ShapeDtypeStruct((nbatch, D_MODEL, SSD_D_STATE), F32),
                   jax.ShapeDtypeStruct((nbatch, 8, SSD_CONV_CH), F32)),
        grid=(nbatch // nb, nc),
        in_specs=[
            seq(D_MODEL, U1_XS), seq(SSD_BC, bc0), seq(SSD_BC, bc0 + 1), seq(D_MODEL, U1_Z), seq(128, 0),
            per_seq(8, SSD_CONV_CH), per_seq(D_MODEL, SSD_D_STATE),
            const2((SSD_CONV, SSD_CONV_CH)), const2((1, SSD_CONV_CH)),
            const2((1, 128)), const2((1, 128)), const2((1, D_MODEL)), const2((1, D_MODEL)),
            const2((128, D_MODEL)), const2((SSD_CONV * L, 2 * HIST + L)),
        ],
        out_specs=(seq(D_MODEL, 0), per_seq(D_MODEL, SSD_D_STATE), per_seq(8, SSD_CONV_CH)),
        scratch_shapes=[pltpu.VMEM((nb, SSD_D_STATE, D_MODEL), F32),
                        pltpu.VMEM((nb, 2 * HIST + L, SSD_CONV_CH), BF16)],
        compiler_params=_params("parallel", "arbitrary"),
        name="ssd_scan",
    )(u3d, u3d, u3d, u3d, small.reshape(nbatch, t, small.shape[1]), conv0, h0.reshape(nbatch, D_MODEL, SSD_D_STATE),
      w_conv, b_conv.reshape(1, SSD_CONV_CH), pad128(dt_bias), pad128(a_log), rep(d_skip),
      g_ssd_out.reshape(1, D_MODEL), _expand_matrix(), _shift_matrix(L))
    return (y.reshape(nbatch * t, D_MODEL), hfin.reshape(nbatch, SSD_HEADS, SSD_HEAD_DIM, SSD_D_STATE),
            convout[:, 8 - (SSD_CONV - 1):])


def _col_from_row(row_vals):
    n = row_vals.shape[1]
    r = lax.broadcasted_iota(jnp.int32, (n, n), 0)
    c = lax.broadcasted_iota(jnp.int32, (n, n), 1)
    return jnp.sum(jnp.where(r == c, row_vals, 0.0), axis=1, keepdims=True)


FOX_PAIR = 4
NT_DIMS = (((1,), (1,)), ((), ()))
FOX_TQ = 512


def _fox_prompt_kernel(q_ref, k_ref, v_ref, fg_ref, c_ref, o_ref, va_ref, ok_ref, *, tq):
    qi = pl.program_id(2)
    hd = FOX_HEAD_DIM
    t = k_ref.shape[0]
    c1 = hd ** -0.5 * LOG2E
    q0 = pl.multiple_of(qi * tq, tq)
    lanes = [slice(j * hd, (j + 1) * hd) for j in range(FOX_PAIR)]

    @pl.when(qi == 0)
    def _stage_values():
        for j in range(FOX_PAIR):
            va_ref[j, :, 0:hd] = v_ref[:, lanes[j]]
            va_ref[j, :, hd:2 * hd] = jnp.ones((t, hd), BF16)

    cq2_rep = [jnp.broadcast_to(c_ref[j, :, pl.ds(q0, tq)], (hd, tq)).T * LOG2E for j in range(FOX_PAIR)]
    cq2 = [cr[:, 0:1] for cr in cq2_rep]
    row = lax.broadcasted_iota(jnp.int32, (tq, tq), 0)
    col = lax.broadcasted_iota(jnp.int32, (tq, tq), 1)

    def scores(j, k0, masked):
        s = lax.dot_general(q_ref[:, lanes[j]], k_ref[pl.ds(k0, tq), lanes[j]], NT_DIMS, preferred_element_type=F32)
        t2 = s * c1 - c_ref[j, :, pl.ds(k0, tq)] * LOG2E
        return jnp.where(col <= row, t2, NEG_BIG) if masked else t2

    shift = []
    ones = jnp.ones((hd, hd), BF16)
    for j in range(FOX_PAIR):
        qk = q_ref[:, lanes[j]] * k_ref[pl.ds(q0, tq), lanes[j]]
        shift.append(jnp.dot(qk, ones, preferred_element_type=F32) * c1 - cq2_rep[j])

    def fast_tile(k0, accs, masked):
        out = []
        for j in range(FOX_PAIR):
            t2 = scores(j, k0, masked)
            e = jnp.concatenate([t2[:, i:i + hd] - shift[j] for i in range(0, tq, hd)], axis=1)
            out.append(accs[j] + jnp.dot(jnp.exp2(e).astype(BF16), va_ref[j, pl.ds(k0, tq), :],
                                         preferred_element_type=F32))
        return tuple(out)

    def fast_part(j, r0, nr, kk, nk):
        s = lax.dot_general(q_ref[r0:r0 + nr, lanes[j]], k_ref[pl.ds(q0 + kk, nk), lanes[j]], NT_DIMS,
                            preferred_element_type=F32)
        t2 = s * c1 - c_ref[j, :, pl.ds(q0 + kk, nk)] * LOG2E
        rr = lax.broadcasted_iota(jnp.int32, (nr, nk), 0) + r0
        cc = lax.broadcasted_iota(jnp.int32, (nr, nk), 1) + kk
        t2 = jnp.where(cc <= rr, t2, NEG_BIG)
        e = jnp.concatenate([t2[:, i:i + hd] - shift[j][r0:r0 + nr] for i in range(0, nk, hd)], axis=1)
        return jnp.dot(jnp.exp2(e).astype(BF16), va_ref[j, pl.ds(q0 + kk, nk), :], preferred_element_type=F32)

    def fast_diag(accs):
        half = tq // 2
        out = []
        for j in range(FOX_PAIR):
            left = fast_part(j, 0, tq, 0, half)
            right = fast_part(j, half, half, half, half)
            out.append(accs[j] + left + jnp.concatenate([jnp.zeros((half, 2 * hd), F32), right], axis=0))
        return tuple(out)

    for qv in range(t // tq):
        @pl.when(qi == qv)
        def _fast(qv=qv):
            accs = tuple(jnp.zeros((tq, 2 * hd), F32) for _ in range(FOX_PAIR))
            for i in range(qv):
                accs = fast_tile(i * tq, accs, False)
            accs = fast_diag(accs)
            probe = jnp.float32(0.0)
            for j in range(FOX_PAIR):
                num, den = accs[j][:, 0:hd], accs[j][:, hd:2 * hd]
                o_ref[:, lanes[j]] = (num / den * _silu(fg_ref[:, lanes[j]].astype(F32))).astype(o_ref.dtype)
                probe = probe + jnp.sum(accs[j] * 0.0)
            ok_ref[0] = jnp.where(probe == 0.0, 1, 0)

    @pl.when(ok_ref[0] == 0)
    def _running_max():
        def tile(k0, carry, masked):
            out = []
            for j in range(FOX_PAIR):
                m_i, l_i, acc = carry[j]
                t2 = scores(j, k0, masked)
                m_new = jnp.maximum(m_i, jnp.max(t2, axis=1, keepdims=True) + cq2[j])
                p = jnp.exp2(t2 - (m_new - cq2[j]))
                alpha = jnp.exp2(m_i - m_new)
                l_new = alpha * l_i + jnp.sum(p, axis=1, keepdims=True)
                acc = alpha * acc + jnp.dot(p.astype(BF16), v_ref[pl.ds(k0, tq), lanes[j]],
                                            preferred_element_type=F32)
                out.append((m_new, l_new, acc))
            return tuple(out)

        init = tuple((jnp.full((tq, 1), NEG_BIG, F32), jnp.zeros((tq, 1), F32), jnp.zeros((tq, hd), F32))
                     for _ in range(FOX_PAIR))
        carry = lax.fori_loop(0, qi, lambda i, cr: tile(pl.multiple_of(i * tq, tq), cr, False), init)
        carry = tile(q0, carry, True)
        for j in range(FOX_PAIR):
            _, l_i, acc = carry[j]
            o_ref[:, lanes[j]] = (acc / l_i * _silu(fg_ref[:, lanes[j]].astype(F32))).astype(o_ref.dtype)


def _fox_prompt(uq, uk, uv, ug, ct, nbatch, t):
    tq = min(FOX_TQ, t)
    nq = t // tq
    w = FOX_PAIR * FOX_HEAD_DIM
    nb = D_MODEL // w
    return pl.pallas_call(
        functools.partial(_fox_prompt_kernel, tq=tq),
        out_shape=jax.ShapeDtypeStruct((nbatch * t, D_MODEL), BF16),
        grid=(nbatch, nb, nq),
        in_specs=[
            pl.BlockSpec((tq, w), lambda b, h, i: (b * nq + i, h)),
            pl.BlockSpec((t, w), lambda b, h, i: (b, h)),
            pl.BlockSpec((t, w), lambda b, h, i: (b, h)),
            pl.BlockSpec((tq, w), lambda b, h, i: (b * nq + i, h)),
            pl.BlockSpec((None, FOX_PAIR, 1, t), lambda b, h, i: (b, h, 0, 0)),
        ],
        out_specs=pl.BlockSpec((tq, w), lambda b, h, i: (b * nq + i, h)),
        scratch_shapes=[pltpu.VMEM((FOX_PAIR, t, 2 * FOX_HEAD_DIM), BF16), pltpu.SMEM((1,), jnp.int32)],
        compiler_params=_params("parallel", "parallel", "arbitrary"),
        name="fox_prompt",
    )(uq, uk, uv, ug, ct)


def _fox_sample_kernel(q_ref, kp_ref, vp_ref, kn_ref, vn_ref, fg_ref, c_ref, o_ref, *, p, t):
    c1 = FOX_HEAD_DIM ** -0.5 * LOG2E
    kp = kp_ref[...].reshape(p, D_MODEL).astype(BF16)
    vp = vp_ref[...].reshape(p, D_MODEL).astype(BF16)
    r = lax.broadcasted_iota(jnp.int32, (t, t), 0)
    c = lax.broadcasted_iota(jnp.int32, (t, t), 1)
    for j in range(FOX_HEADS):
        sl = slice(j * FOX_HEAD_DIM, (j + 1) * FOX_HEAD_DIM)
        q = q_ref[:, sl]
        cq2 = _col_from_row(c_ref[j, :, p:p + t]) * LOG2E
        s_p = lax.dot_general(q, kp[:, sl], NT_DIMS, preferred_element_type=F32) * c1 - c_ref[j, :, 0:p] * LOG2E
        s_n = (lax.dot_general(q, kn_ref[:, sl], NT_DIMS, preferred_element_type=F32) * c1
               - c_ref[j, :, p:p + t] * LOG2E)
        s_n = jnp.where(c <= r, s_n, NEG_BIG)
        m = jnp.maximum(jnp.max(s_p, axis=1, keepdims=True), jnp.max(s_n, axis=1, keepdims=True)) + cq2
        e_p = jnp.exp2(s_p - (m - cq2))
        e_n = jnp.exp2(s_n - (m - cq2))
        inv = 1.0 / (jnp.sum(e_p, axis=1, keepdims=True) + jnp.sum(e_n, axis=1, keepdims=True))
        o = (jnp.dot((e_p * inv).astype(BF16), vp[:, sl], preferred_element_type=F32)
             + jnp.dot((e_n * inv).astype(BF16), vn_ref[:, sl], preferred_element_type=F32))
        o_ref[:, sl] = (o * _silu(fg_ref[:, sl].astype(F32))).astype(o_ref.dtype)


def _fox_sample(uq, uk, uv, ug, k_cache, v_cache, layer, ct, nbatch, t):
    p = k_cache.shape[2]
    past = pl.BlockSpec((None, None, p, FOX_HEADS, FOX_HEAD_DIM), lambda b: (layer, b, 0, 0, 0))
    col = pl.BlockSpec((t, D_MODEL), lambda b: (b, 0))
    return pl.pallas_call(
        functools.partial(_fox_sample_kernel, p=p, t=t),
        out_shape=jax.ShapeDtypeStruct((nbatch * t, D_MODEL), BF16),
        grid=(nbatch,),
        in_specs=[col, past, past, col, col, col,
                  pl.BlockSpec((None, FOX_HEADS, 1, p + t), lambda b: (b, 0, 0, 0))],
        out_specs=pl.BlockSpec((t, D_MODEL), lambda b: (b, 0)),
        compiler_params=_params("parallel"),
        name="fox_sample",
    )(uq, k_cache, v_cache, uk, uv, ug, ct)


def _mem_kernel(q_ref, g_ref, k_ref, v_ref, o_ref):
    scale = MEM_HEAD_DIM ** -0.5
    nm = k_ref.shape[0]
    k = k_ref[...].reshape(nm, D_MODEL).astype(BF16)
    v = v_ref[...].reshape(nm, D_MODEL).astype(BF16)
    for h in range(MEM_HEADS):
        sl = slice(h * MEM_HEAD_DIM, (h + 1) * MEM_HEAD_DIM)
        s = lax.dot_general(q_ref[:, sl], k[:, sl], NT_DIMS, preferred_element_type=F32) * scale
        e = jnp.exp(s - jnp.max(s, axis=1, keepdims=True))
        p = e * (1.0 / jnp.sum(e, axis=1, keepdims=True))
        o = jnp.dot(p.astype(BF16), v[:, sl], preferred_element_type=F32)
        o_ref[:, sl] = (o * _silu(g_ref[:, sl].astype(F32))).astype(o_ref.dtype)


def _mem_attend(u3, mk, mv, nbatch, t, layer=None):
    tq = min(512, t)
    nq = t // tq
    if layer is None:
        nm = mk.shape[1]
        kv = pl.BlockSpec((None, nm, D_MODEL), lambda b, i: (b, 0, 0))
    else:
        nm = mk.shape[2]
        kv = pl.BlockSpec((None, None, nm, MEM_HEADS, MEM_HEAD_DIM), lambda b, i: (layer, b, 0, 0, 0))
    return pl.pallas_call(
        _mem_kernel,
        out_shape=jax.ShapeDtypeStruct((nbatch * t, D_MODEL), BF16),
        grid=(nbatch, nq),
        in_specs=[pl.BlockSpec((tq, D_MODEL), lambda b, i: (b * nq + i, U3_Q)),
                  pl.BlockSpec((tq, D_MODEL), lambda b, i: (b * nq + i, U3_G)), kv, kv],
        out_specs=pl.BlockSpec((tq, D_MODEL), lambda b, i: (b * nq + i, 0)),
        compiler_params=_params("parallel", "arbitrary"),
        name="mem_attend",
    )(u3, u3, mk, mv)


def _merge_kernel(ys_ref, yf_ref, ym_ref, ws_ref, wf_ref, wm_ref, gs_ref, gf_ref, gm_ref, o_ref):
    def branch(y_ref, w_ref, g_ref):
        return _sigmoid(g_ref[...].astype(F32)) * jnp.dot(y_ref[...], w_ref[...], preferred_element_type=F32)
    o_ref[...] = (branch(ys_ref, ws_ref, gs_ref) + branch(yf_ref, wf_ref, gf_ref)
                  + branch(ym_ref, wm_ref, gm_ref)).astype(o_ref.dtype)


def _merge(u, y_ssd, y_fox, y_mem, w_s, w_f, w_m, tm, tn):
    m = y_ssd.shape[0]
    nj = D_MODEL // tn
    yspec = pl.BlockSpec((tm, D_MODEL), lambda i, j: (i, 0))
    wspec = pl.BlockSpec((D_MODEL, tn), lambda i, j: (0, j))
    gspec = lambda col: pl.BlockSpec((tm, tn), lambda i, j: (i, col * nj + j))
    return pl.pallas_call(
        _merge_kernel,
        out_shape=jax.ShapeDtypeStruct((m, D_MODEL), BF16),
        grid=(m // tm, nj),
        in_specs=[yspec, yspec, yspec, wspec, wspec, wspec, gspec(U3_GS), gspec(U3_GF), gspec(U3_GM)],
        out_specs=pl.BlockSpec((tm, tn), lambda i, j: (i, j)),
        compiler_params=_params("parallel", "arbitrary"),
        name="gated_merge",
    )(y_ssd, y_fox, y_mem, w_s, w_f, w_m, u, u, u)


def _final_kernel(mg_ref, w_ref, x_ref, g_ref, o_ref, *, normalize):
    xo = x_ref[...] + jnp.dot(mg_ref[...], w_ref[...], preferred_element_type=F32)
    if normalize:
        ms = jnp.mean(xo * xo, axis=-1, keepdims=True)
        xo = xo * lax.rsqrt(ms + EPS) * g_ref[...]
    o_ref[...] = xo


def _final(merged, w_out, x, g_final, tm, normalize):
    m = x.shape[0]
    return pl.pallas_call(
        functools.partial(_final_kernel, normalize=normalize),
        out_shape=jax.ShapeDtypeStruct((m, D_MODEL), F32),
        grid=(m // tm,),
        in_specs=[pl.BlockSpec((tm, D_MODEL), lambda i: (i, 0)),
                  pl.BlockSpec((D_MODEL, D_MODEL), lambda i: (0, 0)),
                  pl.BlockSpec((tm, D_MODEL), lambda i: (i, 0)),
                  pl.BlockSpec((1, D_MODEL), lambda i: (0, 0))],
        out_specs=pl.BlockSpec((tm, D_MODEL), lambda i: (i, 0)),
        compiler_params=_params("parallel"),
        name="out_proj_norm",
    )(merged, w_out, x, g_final.reshape(1, D_MODEL))


def _row_tile(m, pref):
    t = pref
    while m % t:
        t //= 2
    return t


def _layer(x, conv0, h0, caches, logf_past, mem_k, mem_v, wd, g_final, last):
    nbatch, t, d = x.shape
    m = nbatch * t
    x2 = x.reshape(m, d)
    tm = _row_tile(m, 1024)

    wt = wd["w_in_t"]
    _, (r2, _), (r3, n3) = W_IN_SEGMENTS
    tm2 = _row_tile(m, 2048)
    uq, h = _mm_nt_norm(x2, wd["g_norm"], wt, r2, D_MODEL, tm, 1024, "norm_in_proj_fox_q")
    u1, small = _mm_ssd(h, wt, tm2, 1024)
    uk, k_new = _mm_nt_heads(h, wt, r2 + D_MODEL, tm, 1024, "in_proj_fox_k")
    uv, v_new = _mm_nt_heads(h, wt, r2 + 2 * D_MODEL, tm, 1024, "in_proj_fox_v")
    ug = _mm_nt(h, wt, r2 + 3 * D_MODEL, D_MODEL, BF16, tm2, 1024, "in_proj_fox_gate")
    u3 = _mm_nt(h, wt, r3, n3, BF16, tm2, 1024, "in_proj_mem_gates")

    past_t = None if logf_past is None else jnp.transpose(logf_past, (0, 2, 1))
    logf_t, ct = _logf(small, wd["b_forget"], past_t, nbatch, t)
    ct = ct.reshape(nbatch, FOX_HEADS, 1, ct.shape[-1])

    conv0p = jnp.pad(conv0, ((0, 0), (8 - (SSD_CONV - 1), 0), (0, 0)))
    y_ssd, h_final, new_conv = _ssd(u1, small, conv0p, h0, wd["w_conv"], wd["b_conv"], wd["dt_bias"], wd["a_log"],
                                    wd["d_skip"], wd["g_ssd_out"], nbatch, t)

    if caches is None:
        y_fox = _fox_prompt(uq, uk, uv, ug, ct, nbatch, t)
        y_mem = _mem_attend(u3, mem_k, mem_v, nbatch, t)
    else:
        layer, fox_k, fox_v = caches
        y_fox = _fox_sample(uq, uk, uv, ug, fox_k, fox_v, layer, ct, nbatch, t)
        y_mem = _mem_attend(u3, mem_k, mem_v, nbatch, t, layer)
    merged = _merge(u3, y_ssd, y_fox, y_mem, wd["w_o_ssd"], wd["w_o_fox"], wd["w_o_mem"], tm, 512)
    y = _final(merged, wd["w_out"], x2, g_final, _row_tile(m, 512), last)

    return (y.reshape(nbatch, t, d), new_conv, h_final,
            k_new.reshape(nbatch, t, FOX_HEADS, FOX_HEAD_DIM), v_new.reshape(nbatch, t, FOX_HEADS, FOX_HEAD_DIM),
            jnp.transpose(logf_t, (0, 2, 1)))


def kernel(x_prompt, x_sample, mem_prompt, cache_fox_k, cache_fox_v, cache_fox_logf, state_ssd, state_ssd_conv,
           cache_mem_k, cache_mem_v, g_norm, w_in, w_conv, b_conv, dt_bias, a_log, d_skip, g_ssd_out, b_forget,
           g_mem, w_mem_kv, w_o_ssd, w_o_fox, w_o_mem, w_out, g_final):
    depth = w_in.shape[0]
    xp, xs = x_prompt, x_sample
    bp = xp.shape[0]
    n_mem = mem_prompt.shape[1]
    outs = [[] for _ in range(12)]
    for l in range(depth):
        wkv = _cast_bf16(w_mem_kv[l])
        wd = {
            "g_norm": g_norm[l],
            "w_in_t": _cast_bf16(jnp.transpose(w_in[l])),
            "w_conv": w_conv[l], "b_conv": b_conv[l], "dt_bias": dt_bias[l], "a_log": a_log[l], "d_skip": d_skip[l],
            "g_ssd_out": g_ssd_out[l], "b_forget": b_forget[l],
            "w_o_ssd": _cast_bf16(w_o_ssd[l]), "w_o_fox": _cast_bf16(w_o_fox[l]),
            "w_o_mem": _cast_bf16(w_o_mem[l]), "w_out": _cast_bf16(w_out[l]),
        }
        hm = _rmsnorm(mem_prompt.reshape(bp * n_mem, D_MODEL), g_mem[l], _row_tile(bp * n_mem, 256))
        tmm = _row_tile(bp * n_mem, 512)
        mk, mk_b = _mm_heads(hm, wkv, 0, MEM_HEADS, MEM_HEAD_DIM, tmm, "mem_k_proj")
        mv, mv_b = _mm_heads(hm, wkv, D_MODEL, MEM_HEADS, MEM_HEAD_DIM, tmm, "mem_v_proj")

        xp, c_p, h_p, k_p, v_p, lf_p = _layer(
            xp, jnp.zeros((bp, SSD_CONV - 1, SSD_CONV_CH), F32),
            jnp.zeros((bp, SSD_HEADS, SSD_HEAD_DIM, SSD_D_STATE), F32), None, None,
            mk_b.reshape(bp, n_mem, D_MODEL), mv_b.reshape(bp, n_mem, D_MODEL), wd, g_final, l == depth - 1)
        xs, c_s, h_s, k_s, v_s, lf_s = _layer(
            xs, state_ssd_conv[l], state_ssd[l], (l, cache_fox_k, cache_fox_v), cache_fox_logf[l],
            cache_mem_k, cache_mem_v, wd, g_final, l == depth - 1)
        for lst, val in zip(outs, (k_p, v_p, lf_p, h_p, c_p,
                                   mk.reshape(bp, n_mem, MEM_HEADS, MEM_HEAD_DIM),
                                   mv.reshape(bp, n_mem, MEM_HEADS, MEM_HEAD_DIM),
                                   k_s, v_s, lf_s, h_s, c_s)):
            lst.append(val)
    return (xp, xs) + tuple(jnp.stack(o) for o in outs)
```

```python
import functools

import numpy as np
import jax
import jax.numpy as jnp
from jax import lax
from jax.experimental import pallas as pl
from jax.experimental.pallas import tpu as pltpu

F32 = jnp.float32
BF16 = jnp.bfloat16

EPS = 1e-6
D_MODEL = 2048
SSD_HEAD_DIM = 64
SSD_HEADS = 32
SSD_GROUPS = 4
SSD_D_STATE = 128
SSD_CONV = 4
SSD_BC = SSD_GROUPS * SSD_D_STATE
SSD_CONV_CH = D_MODEL + 2 * SSD_BC
FOX_HEADS = 16
FOX_HEAD_DIM = 128
MEM_HEADS = 4
MEM_HEAD_DIM = 512
SSD_CHUNK = 64
SSD_NB = 2
NEG_BIG = -1e30

VMEM_LIMIT = 56 * 1024 * 1024
PROJ_TN = 1024
PROJ_TM = 1024
PROJ_TM_WIDE = 2048

W_IN_SEGMENTS = ((0, D_MODEL + SSD_CONV_CH),
                 (D_MODEL + SSD_CONV_CH + SSD_HEADS, 4 * D_MODEL),
                 (D_MODEL + SSD_CONV_CH + SSD_HEADS + 4 * D_MODEL + FOX_HEADS, 5 * D_MODEL))
W_IN_DT = D_MODEL + SSD_CONV_CH
W_IN_FF = W_IN_SEGMENTS[1][0] + 4 * D_MODEL
U1_Z, U1_XS = 0, 1
U3_Q, U3_G, U3_GS, U3_GF, U3_GM = 0, 1, 2, 3, 4
LOG2E = 1.4426950408889634


def _params(*sem):
    return pltpu.CompilerParams(dimension_semantics=sem, vmem_limit_bytes=VMEM_LIMIT)


def _split3(x):
    hi = x.astype(BF16)
    r1 = x - hi.astype(F32)
    mid = r1.astype(BF16)
    lo = (r1 - mid.astype(F32)).astype(BF16)
    return hi, mid, lo


def _softplus(x):
    return jnp.maximum(x, 0.0) + jnp.log1p(jnp.exp(-jnp.abs(x)))


def _sigmoid(x):
    return 0.5 * jnp.tanh(0.5 * x) + 0.5


def _silu(x):
    return x * _sigmoid(x)


def _rmsnorm_kernel(x_ref, g_ref, o_ref):
    x = x_ref[...]
    ms = jnp.mean(x * x, axis=-1, keepdims=True)
    o_ref[...] = (x * lax.rsqrt(ms + EPS) * g_ref[...]).astype(o_ref.dtype)


def _rmsnorm(x, g, tm):
    m, d = x.shape
    return pl.pallas_call(
        _rmsnorm_kernel,
        out_shape=jax.ShapeDtypeStruct((m, d), BF16),
        grid=(m // tm,),
        in_specs=[pl.BlockSpec((tm, d), lambda i: (i, 0)), pl.BlockSpec((1, d), lambda i: (0, 0))],
        out_specs=pl.BlockSpec((tm, d), lambda i: (i, 0)),
        compiler_params=_params("parallel"),
        name="rmsnorm",
    )(x, g.reshape(1, d))


def _mm_heads_kernel(h_ref, w_ref, o_ref, ob_ref):
    acc = jnp.dot(h_ref[...], w_ref[...], preferred_element_type=F32)
    o_ref[...] = acc.reshape(o_ref.shape)
    ob_ref[...] = acc.astype(ob_ref.dtype)


def _mm_heads(h, w, w_col0, heads, head_dim, tm, name):
    m, k = h.shape
    n = heads * head_dim
    return pl.pallas_call(
        _mm_heads_kernel,
        out_shape=(jax.ShapeDtypeStruct((m, heads, head_dim), F32), jax.ShapeDtypeStruct((m, n), BF16)),
        grid=(m // tm,),
        in_specs=[pl.BlockSpec((tm, k), lambda i: (i, 0)), pl.BlockSpec((k, n), lambda i: (0, w_col0 // n))],
        out_specs=(pl.BlockSpec((tm, heads, head_dim), lambda i: (i, 0, 0)), pl.BlockSpec((tm, n), lambda i: (i, 0))),
        compiler_params=_params("parallel"),
        name=name,
    )(h, w)


def _mm_nt_kernel(h_ref, wt_ref, o_ref):
    o_ref[...] = lax.dot_general(h_ref[...], wt_ref[...], NT_DIMS, preferred_element_type=F32).astype(o_ref.dtype)


def _wt_rows_spec(tn, k, row0, grid_rank):
    g = 16
    assert row0 % g == 0 and tn % g == 0
    if grid_rank == 1:
        return pl.BlockSpec((pl.Element(tn), pl.Element(k)), lambda i: (row0, 0))
    return pl.BlockSpec((pl.Element(tn), pl.Element(k)), lambda i, j: ((row0 // g + j * (tn // g)) * g, 0))


def _mm_nt(h, wt, row0, n, out_dtype, tm, tn, name):
    m, k = h.shape
    return pl.pallas_call(
        _mm_nt_kernel,
        out_shape=jax.ShapeDtypeStruct((m, n), out_dtype),
        grid=(m // tm, n // tn),
        in_specs=[pl.BlockSpec((tm, k), lambda i, j: (i, 0)), _wt_rows_spec(tn, k, row0, 2)],
        out_specs=pl.BlockSpec((tm, tn), lambda i, j: (i, j)),
        compiler_params=_params("parallel", "arbitrary"),
        name=name,
    )(h, wt)


def _mm_nt_norm_kernel(x_ref, g_ref, wt_ref, o_ref, h_ref):
    x = x_ref[...]
    ms = jnp.mean(x * x, axis=-1, keepdims=True)
    hn = (x * lax.rsqrt(ms + EPS) * g_ref[...]).astype(BF16)
    h_ref[...] = hn
    o_ref[...] = lax.dot_general(hn, wt_ref[...], NT_DIMS, preferred_element_type=F32).astype(o_ref.dtype)


def _mm_nt_norm(x, g, wt, row0, n, tm, tn, name):
    m, k = x.shape
    return pl.pallas_call(
        _mm_nt_norm_kernel,
        out_shape=(jax.ShapeDtypeStruct((m, n), BF16), jax.ShapeDtypeStruct((m, k), BF16)),
        grid=(m // tm, n // tn),
        in_specs=[pl.BlockSpec((tm, k), lambda i, j: (i, 0)), pl.BlockSpec((1, k), lambda i, j: (0, 0)),
                  _wt_rows_spec(tn, k, row0, 2)],
        out_specs=(pl.BlockSpec((tm, tn), lambda i, j: (i, j)), pl.BlockSpec((tm, k), lambda i, j: (i, 0))),
        compiler_params=_params("parallel", "arbitrary"),
        name=name,
    )(x, g.reshape(1, k), wt)


def _mm_ssd_kernel(h_ref, wt_ref, wdt_ref, wff_ref, o_ref, small_ref):
    o_ref[...] = lax.dot_general(h_ref[...], wt_ref[...], NT_DIMS, preferred_element_type=F32).astype(o_ref.dtype)

    @pl.when(pl.program_id(1) == 0)
    def _narrow():
        w = jnp.concatenate([wdt_ref[...], wff_ref[...]], axis=0)
        small_ref[...] = lax.dot_general(h_ref[...], w, NT_DIMS, preferred_element_type=F32)


def _mm_ssd(h, wt, tm, tn):
    m, k = h.shape
    row0, n = W_IN_SEGMENTS[0]
    fixed = lambda r0: pl.BlockSpec((pl.Element(128), pl.Element(k)), lambda i, j: (r0, 0))
    return pl.pallas_call(
        _mm_ssd_kernel,
        out_shape=(jax.ShapeDtypeStruct((m, n), BF16), jax.ShapeDtypeStruct((m, 256), F32)),
        grid=(m // tm, n // tn),
        in_specs=[pl.BlockSpec((tm, k), lambda i, j: (i, 0)), _wt_rows_spec(tn, k, row0, 2),
                  fixed(W_IN_DT), fixed(W_IN_FF)],
        out_specs=(pl.BlockSpec((tm, tn), lambda i, j: (i, j)), pl.BlockSpec((tm, 256), lambda i, j: (i, 0))),
        compiler_params=_params("parallel", "arbitrary"),
        name="in_proj_ssd",
    )(h, wt, wt, wt)


def _mm_nt_heads_kernel(h_ref, wt_ref, o_ref, oh_ref):
    acc = lax.dot_general(h_ref[...], wt_ref[...], NT_DIMS, preferred_element_type=F32)
    o_ref[...] = acc.astype(o_ref.dtype)
    oh_ref[...] = acc.reshape(oh_ref.shape)


def _mm_nt_heads(h, wt, row0, tm, tn, name):
    m, k = h.shape
    hb = tn // FOX_HEAD_DIM
    return pl.pallas_call(
        _mm_nt_heads_kernel,
        out_shape=(jax.ShapeDtypeStruct((m, D_MODEL), BF16),
                   jax.ShapeDtypeStruct((m, FOX_HEADS, FOX_HEAD_DIM), F32)),
        grid=(m // tm, D_MODEL // tn),
        in_specs=[pl.BlockSpec((tm, k), lambda i, j: (i, 0)), _wt_rows_spec(tn, k, row0, 2)],
        out_specs=(pl.BlockSpec((tm, tn), lambda i, j: (i, j)),
                   pl.BlockSpec((tm, hb, FOX_HEAD_DIM), lambda i, j: (i, j, 0))),
        compiler_params=_params("parallel", "arbitrary"),
        name=name,
    )(h, wt)


def _cast_kernel(a_ref, o_ref):
    o_ref[...] = a_ref[...].astype(o_ref.dtype)


def _cast_bf16(w):
    rows, cols = w.shape
    tr = max(d for d in range(16, 1025, 16) if rows % d == 0)
    return pl.pallas_call(
        _cast_kernel,
        out_shape=jax.ShapeDtypeStruct((rows, cols), BF16),
        grid=(rows // tr,),
        in_specs=[pl.BlockSpec((tr, cols), lambda i: (i, 0))],
        out_specs=pl.BlockSpec((tr, cols), lambda i: (i, 0)),
        compiler_params=_params("parallel"),
        name="weight_cast",
    )(w)


def _upper3(tb):
    k = lax.broadcasted_iota(jnp.int32, (3 * tb, tb), 0) % tb
    t = lax.broadcasted_iota(jnp.int32, (3 * tb, tb), 1)
    return jnp.where(k <= t, 1.0, 0.0).astype(BF16)


def _cumsum_lanes(x, carry, tb):
    n = x.shape[1]
    u3 = _upper3(tb)
    out = []
    for s in range(0, n, tb):
        hi, mid, lo = _split3(x[:, s:s + tb])
        c = jnp.dot(jnp.concatenate([hi, mid, lo], axis=1), u3, preferred_element_type=F32) + carry
        carry = c[:, tb - 1:tb]
        out.append(c)
    return out, carry


def _logf_kernel(*refs, nb, t, p):
    if p:
        ffc_ref, bias_ref, past_ref, lf_ref, c_ref = refs
    else:
        ffc_ref, bias_ref, lf_ref, c_ref = refs
    fft = ffc_ref[...].T[0:FOX_HEADS, :]
    lf_all = -_softplus(-(fft + bias_ref[...]))
    for b in range(nb):
        lf = lf_all[:, b * t:(b + 1) * t]
        lf_ref[b] = lf
        carry = jnp.zeros((FOX_HEADS, 1), F32)
        if p:
            tbp = min(256, p)
            blocks, carry = _cumsum_lanes(past_ref[b], carry, tbp)
            for i, c in enumerate(blocks):
                c_ref[b, :, i * tbp:(i + 1) * tbp] = c
        tb = min(256, t)
        blocks, carry = _cumsum_lanes(lf, carry, tb)
        for i, c in enumerate(blocks):
            c_ref[b, :, p + i * tb:p + (i + 1) * tb] = c


def _logf(small, b_forget, past_t, nbatch, t):
    p = 0 if past_t is None else past_t.shape[2]
    nb = 1 if t % 128 == 0 else nbatch
    grid = (nbatch // nb,)
    in_specs = [pl.BlockSpec((nb * t, 128), lambda i: (i, 1)),
                pl.BlockSpec((FOX_HEADS, 1), lambda i: (0, 0))]
    args = [small, b_forget.reshape(FOX_HEADS, 1)]
    if p:
        in_specs.append(pl.BlockSpec((nb, FOX_HEADS, p), lambda i: (i, 0, 0)))
        args.append(past_t)
    return pl.pallas_call(
        functools.partial(_logf_kernel, nb=nb, t=t, p=p),
        out_shape=(jax.ShapeDtypeStruct((nbatch, FOX_HEADS, t), F32),
                   jax.ShapeDtypeStruct((nbatch, FOX_HEADS, p + t), F32)),
        grid=grid,
        in_specs=in_specs,
        out_specs=(pl.BlockSpec((nb, FOX_HEADS, t), lambda i: (i, 0, 0)),
                   pl.BlockSpec((nb, FOX_HEADS, p + t), lambda i: (i, 0, 0))),
        compiler_params=_params("parallel"),
        name="logf_cumsum",
    )(*args)


def _expand_matrix():
    r = np.arange(128)[:, None]
    c = np.arange(D_MODEL)[None, :]
    return jnp.asarray(((r < 96) & ((r % SSD_HEADS) == (c // SSD_HEAD_DIM))).astype(np.float32), dtype=BF16)


def _pack3(x):
    lane = lax.broadcasted_iota(jnp.int32, x.shape, 1)
    x = jnp.where(lane < SSD_HEADS, x, 0.0)
    hi, mid, lo = _split3(x)
    packed = hi.astype(F32) + pltpu.roll(mid.astype(F32), SSD_HEADS, 1) + pltpu.roll(lo.astype(F32), 2 * SSD_HEADS, 1)
    return packed.astype(BF16)


HIST = 16


def _shift_matrix(L):
    sh = np.zeros((SSD_CONV * L, 2 * HIST + L), np.float32)
    for d in range(SSD_CONV):
        for t in range(L):
            sh[d * L + t, 2 * HIST + t - d] = 1.0
            if t - d < 0:
                sh[d * L + t, HIST + t - d] = 1.0
    return jnp.asarray(sh, dtype=BF16)


def _ssd_kernel(xs_ref, b_ref, c_ref, z_ref, sm_ref, conv0_ref, h0_ref,
                wconv_ref, bconv_ref, dtb_ref, alog_ref, dskip_ref, gout_ref, e3_ref, sh_ref,
                y_ref, hfin_ref, convout_ref,
                st_ref, xq_ref):
    ci = pl.program_id(1)
    nb, L = xs_ref.shape[0], xs_ref.shape[1]
    P = SSD_HEAD_DIM
    W = D_MODEL
    gw = W // SSD_GROUPS
    hpm = 256 // L

    @pl.when(ci == 0)
    def _init():
        for bi in range(nb):
            st_ref[bi] = h0_ref[bi].T
            hist = jnp.concatenate([jnp.zeros((HIST - 8, SSD_CONV_CH), F32), conv0_ref[bi]], axis=0)
            hist_hi = hist.astype(BF16)
            xq_ref[bi, 0:HIST, :] = hist_hi
            xq_ref[bi, HIST:2 * HIST, :] = (hist - hist_hi.astype(F32)).astype(BF16)

    lane = lax.broadcasted_iota(jnp.int32, (L, 128), 1)
    kk = lax.broadcasted_iota(jnp.int32, (L, 3 * L), 1) % L
    ll = lax.broadcasted_iota(jnp.int32, (L, 3 * L), 0)
    tri3 = jnp.where(kk <= ll, 1.0, 0.0).astype(BF16)
    row = lax.broadcasted_iota(jnp.int32, (L, W), 0)
    sidx = lax.broadcasted_iota(jnp.int32, (L, W), 1) % L
    rb = lax.broadcasted_iota(jnp.int32, (256, hpm * P), 0) // L
    cb = lax.broadcasted_iota(jnp.int32, (256, hpm * P), 1) // P
    blockmask = rb == cb
    a_small = -jnp.exp(alog_ref[...])
    e3 = e3_ref[...]

    def chunk(bi):
        xq_ref[bi, 2 * HIST:, 0:W] = xs_ref[bi]
        xq_ref[bi, 2 * HIST:, W:W + SSD_BC] = b_ref[bi]
        xq_ref[bi, 2 * HIST:, W + SSD_BC:] = c_ref[bi]

        def conv(lo, hi):
            sh = jnp.dot(sh_ref[...], xq_ref[bi, :, lo:hi], preferred_element_type=F32)
            acc = bconv_ref[:, lo:hi]
            for d in range(SSD_CONV):
                acc = acc + sh[d * L:(d + 1) * L] * wconv_ref[SSD_CONV - 1 - d:SSD_CONV - d, lo:hi]
            return _silu(acc)

        xs = conv(0, W)
        bm = conv(W, W + SSD_BC).astype(BF16)
        cm = conv(W + SSD_BC, W + 2 * SSD_BC).astype(BF16)
        xq_ref[bi, 0:HIST, :] = jnp.zeros((HIST, SSD_CONV_CH), BF16)
        xq_ref[bi, HIST:2 * HIST, :] = xq_ref[bi, HIST + L:2 * HIST + L, :]

        dt = _softplus(sm_ref[bi] + dtb_ref[...])
        dta = jnp.where(lane < SSD_HEADS, dt * a_small, 0.0)
        hi, mid, lo = _split3(dta)
        acum = jnp.dot(tri3, jnp.concatenate([hi, mid, lo], axis=0), preferred_element_type=F32)
        a_x = jnp.dot(_pack3(acum), e3, preferred_element_type=F32)
        dt_x = jnp.dot(_pack3(dt), e3, preferred_element_type=F32)

        a_row = jnp.sum(jnp.where(row == sidx, a_x, 0.0), axis=0, keepdims=True)
        ldec = jnp.exp(jnp.where(row >= sidx, a_x - a_row, NEG_BIG))
        cbx = []
        for g in range(SSD_GROUPS):
            cg = cm[:, g * SSD_D_STATE:(g + 1) * SSD_D_STATE]
            bg = bm[:, g * SSD_D_STATE:(g + 1) * SSD_D_STATE]
            brep = jnp.concatenate([bg] * (SSD_HEADS // SSD_GROUPS), axis=0)
            cbx.append(lax.dot_general(cg, brep, NT_DIMS, preferred_element_type=F32))
        m = (jnp.concatenate(cbx, axis=1) * ldec).astype(BF16)

        xdt = xs * dt_x
        xdt_b = xdt.astype(BF16)
        y_parts = []
        for j in range(SSD_HEADS // hpm):
            xj = xdt_b[:, j * hpm * P:(j + 1) * hpm * P]
            bd = jnp.where(blockmask, jnp.concatenate([xj] * hpm, axis=0), jnp.zeros((), BF16))
            y_parts.append(jnp.dot(m[:, j * 256:(j + 1) * 256], bd, preferred_element_type=F32))
        y = jnp.concatenate(y_parts, axis=1)

        st = st_ref[bi]
        st_b = st.astype(BF16)
        yo = [jnp.dot(cm[:, g * SSD_D_STATE:(g + 1) * SSD_D_STATE], st_b[:, g * gw:(g + 1) * gw],
                      preferred_element_type=F32) for g in range(SSD_GROUPS)]
        y = y + jnp.concatenate(yo, axis=1) * jnp.exp(a_x)

        a_last = a_x[L - 1:L, :]
        xw = (xdt * jnp.exp(a_last - a_x)).astype(BF16)
        upd = [lax.dot_general(bm[:, g * SSD_D_STATE:(g + 1) * SSD_D_STATE], xw[:, g * gw:(g + 1) * gw],
                               (((0,), (0,)), ((), ())), preferred_element_type=F32) for g in range(SSD_GROUPS)]
        st_ref[bi] = st * jnp.exp(a_last) + jnp.concatenate(upd, axis=1)

        y = y + dskip_ref[...] * xs
        gz = y * _silu(z_ref[bi].astype(F32))
        ms = jnp.mean(gz * gz, axis=-1, keepdims=True)
        y_ref[bi] = (gz * lax.rsqrt(ms + EPS) * gout_ref[...]).astype(y_ref.dtype)

    for bi in range(nb):
        chunk(bi)

    @pl.when(ci == pl.num_programs(1) - 1)
    def _final():
        for bi in range(nb):
            hfin_ref[bi] = st_ref[bi].T
            convout_ref[bi] = xq_ref[bi, HIST:2 * HIST, :].astype(F32)[HIST - 8:, :]


def _ssd(u, small, conv0, h0, w_conv, b_conv, dt_bias, a_log, d_skip, g_ssd_out, nbatch, t):
    L = SSD_CHUNK
    nc = t // L
    nb = SSD_NB if nbatch % SSD_NB == 0 else 1
    pad128 = lambda v: jnp.pad(v.astype(F32), (0, 128 - v.shape[0])).reshape(1, 128)
    rep = lambda v: jnp.repeat(v.astype(F32), SSD_HEAD_DIM).reshape(1, D_MODEL)
    const2 = lambda shape: pl.BlockSpec(shape, lambda b, c: (0, 0))
    seq = lambda width, col: pl.BlockSpec((nb, L, width), lambda b, c: (b, c, col))
    per_seq = lambda rows, width: pl.BlockSpec((nb, rows, width), lambda b, c: (b, 0, 0))
    u3d = u.reshape(nbatch, t, u.shape[1])
    bc0 = (U1_XS + 1) * D_MODEL // SSD_BC
    y, hfin, convout = pl.pallas_call(
        _ssd_kernel,
        out_shape=(jax.ShapeDtypeStruct((nbatch, t, D_MODEL), BF16),
                   jax.ShapeDtypeStruct((nbatch, D_MODEL, SSD_D_STATE), F32),
                   jax.ShapeDtypeStruct((nbatch, 8, SSD_CONV_CH), F32)),
        grid=(nbatch // nb, nc),
        in_specs=[
            seq(D_MODEL, U1_XS), seq(SSD_BC, bc0), seq(SSD_BC, bc0 + 1), seq(D_MODEL, U1_Z), seq(128, 0),
            per_seq(8, SSD_CONV_CH), per_seq(D_MODEL, SSD_D_STATE),
            const2((SSD_CONV, SSD_CONV_CH)), const2((1, SSD_CONV_CH)),
            const2((1, 128)), const2((1, 128)), const2((1, D_MODEL)), const2((1, D_MODEL)),
            const2((128, D_MODEL)), const2((SSD_CONV * L, 2 * HIST + L)),
        ],
        out_specs=(seq(D_MODEL, 0), per_seq(D_MODEL, SSD_D_STATE), per_seq(8, SSD_CONV_CH)),
        scratch_shapes=[pltpu.VMEM((nb, SSD_D_STATE, D_MODEL), F32),
                        pltpu.VMEM((nb, 2 * HIST + L, SSD_CONV_CH), BF16)],
        compiler_params=_params("parallel", "arbitrary"),
        name="ssd_scan",
    )(u3d, u3d, u3d, u3d, small.reshape(nbatch, t, small.shape[1]), conv0, h0.reshape(nbatch, D_MODEL, SSD_D_STATE),
      w_conv, b_conv.reshape(1, SSD_CONV_CH), pad128(dt_bias), pad128(a_log), rep(d_skip),
      g_ssd_out.reshape(1, D_MODEL), _expand_matrix(), _shift_matrix(L))
    return (y.reshape(nbatch * t, D_MODEL), hfin.reshape(nbatch, SSD_HEADS, SSD_HEAD_DIM, SSD_D_STATE),
            convout[:, 8 - (SSD_CONV - 1):])


def _col_from_row(row_vals):
    n = row_vals.shape[1]
    r = lax.broadcasted_iota(jnp.int32, (n, n), 0)
    c = lax.broadcasted_iota(jnp.int32, (n, n), 1)
    return jnp.sum(jnp.where(r == c, row_vals, 0.0), axis=1, keepdims=True)


FOX_PAIR = 4
NT_DIMS = (((1,), (1,)), ((), ()))
FOX_TQ = 512


def _fox_prompt_kernel(q_ref, k_ref, v_ref, fg_ref, c_ref, o_ref, va_ref, ok_ref, *, tq):
    qi = pl.program_id(2)
    hd = FOX_HEAD_DIM
    t = k_ref.shape[0]
    c1 = hd ** -0.5 * LOG2E
    q0 = pl.multiple_of(qi * tq, tq)
    lanes = [slice(j * hd, (j + 1) * hd) for j in range(FOX_PAIR)]

    @pl.when(qi == 0)
    def _stage_values():
        for j in range(FOX_PAIR):
            va_ref[j, :, 0:hd] = v_ref[:, lanes[j]]
            va_ref[j, :, hd:2 * hd] = jnp.ones((t, hd), BF16)

    cq2_rep = [jnp.broadcast_to(c_ref[j, :, pl.ds(q0, tq)], (hd, tq)).T * LOG2E for j in range(FOX_PAIR)]
    cq2 = [cr[:, 0:1] for cr in cq2_rep]
    row = lax.broadcasted_iota(jnp.int32, (tq, tq), 0)
    col = lax.broadcasted_iota(jnp.int32, (tq, tq), 1)

    def scores(j, k0, masked):
        s = lax.dot_general(q_ref[:, lanes[j]], k_ref[pl.ds(k0, tq), lanes[j]], NT_DIMS, preferred_element_type=F32)
        t2 = s * c1 - c_ref[j, :, pl.ds(k0, tq)] * LOG2E
        return jnp.where(col <= row, t2, NEG_BIG) if masked else t2

    shift = []
    ones = jnp.ones((hd, hd), BF16)
    for j in range(FOX_PAIR):
        qk = q_ref[:, lanes[j]] * k_ref[pl.ds(q0, tq), lanes[j]]
        shift.append(jnp.dot(qk, ones, preferred_element_type=F32) * c1 - cq2_rep[j])

    def fast_tile(k0, accs):
        out = []
        for j in range(FOX_PAIR):
            t2 = scores(j, k0, False)
            e = jnp.concatenate([t2[:, i:i + hd] - shift[j] for i in range(0, tq, hd)], axis=1)
            out.append(accs[j] + jnp.dot(jnp.exp2(e).astype(BF16), va_ref[j, pl.ds(k0, tq), :],
                                         preferred_element_type=F32))
        return tuple(out)

    def fast_part(j, r0, nr, kk, nk):
        s = lax.dot_general(q_ref[r0:r0 + nr, lanes[j]], k_ref[pl.ds(q0 + kk, nk), lanes[j]], NT_DIMS,
                            preferred_element_type=F32)
        t2 = s * c1 - c_ref[j, :, pl.ds(q0 + kk, nk)] * LOG2E
        rr = lax.broadcasted_iota(jnp.int32, (nr, nk), 0) + r0
        cc = lax.broadcasted_iota(jnp.int32, (nr, nk), 1) + kk
        t2 = jnp.where(cc <= rr, t2, NEG_BIG)
        e = jnp.concatenate([t2[:, i:i + hd] - shift[j][r0:r0 + nr] for i in range(0, nk, hd)], axis=1)
        return jnp.dot(jnp.exp2(e).astype(BF16), va_ref[j, pl.ds(q0 + kk, nk), :], preferred_element_type=F32)

    def fast_diag(accs):
        half = tq // 2
        out = []
        for j in range(FOX_PAIR):
            left = fast_part(j, 0, tq, 0, half)
            right = fast_part(j, half, half, half, half)
            out.append(accs[j] + left + jnp.concatenate([jnp.zeros((half, 2 * hd), F32), right], axis=0))
        return tuple(out)

    for qv in range(t // tq):
        @pl.when(qi == qv)
        def _fast(qv=qv):
            accs = tuple(jnp.zeros((tq, 2 * hd), F32) for _ in range(FOX_PAIR))
            for i in range(qv):
                accs = fast_tile(i * tq, accs)
            accs = fast_diag(accs)
            probe = jnp.float32(0.0)
            for j in range(FOX_PAIR):
                num, den = accs[j][:, 0:hd], accs[j][:, hd:2 * hd]
                o_ref[:, lanes[j]] = (num / den * _silu(fg_ref[:, lanes[j]].astype(F32))).astype(o_ref.dtype)
                probe = probe + jnp.sum(accs[j] * 0.0)
            ok_ref[0] = jnp.where(probe == 0.0, 1, 0)

    @pl.when(ok_ref[0] == 0)
    def _running_max():
        def tile(k0, carry, masked):
            out = []
            for j in range(FOX_PAIR):
                m_i, l_i, acc = carry[j]
                t2 = scores(j, k0, masked)
                m_new = jnp.maximum(m_i, jnp.max(t2, axis=1, keepdims=True) + cq2[j])
                p = jnp.exp2(t2 - (m_new - cq2[j]))
                alpha = jnp.exp2(m_i - m_new)
                l_new = alpha * l_i + jnp.sum(p, axis=1, keepdims=True)
                acc = alpha * acc + jnp.dot(p.astype(BF16), v_ref[pl.ds(k0, tq), lanes[j]],
                                            preferred_element_type=F32)
                out.append((m_new, l_new, acc))
            return tuple(out)

        init = tuple((jnp.full((tq, 1), NEG_BIG, F32), jnp.zeros((tq, 1), F32), jnp.zeros((tq, hd), F32))
                     for _ in range(FOX_PAIR))
        carry = lax.fori_loop(0, qi, lambda i, cr: tile(pl.multiple_of(i * tq, tq), cr, False), init)
        carry = tile(q0, carry, True)
        for j in range(FOX_PAIR):
            _, l_i, acc = carry[j]
            o_ref[:, lanes[j]] = (acc / l_i * _silu(fg_ref[:, lanes[j]].astype(F32))).astype(o_ref.dtype)


def _fox_prompt(uq, uk, uv, ug, ct, nbatch, t):
    tq = min(FOX_TQ, t)
    nq = t // tq
    w = FOX_PAIR * FOX_HEAD_DIM
    nb = D_MODEL // w
    return pl.pallas_call(
        functools.partial(_fox_prompt_kernel, tq=tq),
        out_shape=jax.ShapeDtypeStruct((nbatch * t, D_MODEL), BF16),
        grid=(nbatch, nb, nq),
        in_specs=[
            pl.BlockSpec((tq, w), lambda b, h, i: (b * nq + i, h)),
            pl.BlockSpec((t, w), lambda b, h, i: (b, h)),
            pl.BlockSpec((t, w), lambda b, h, i: (b, h)),
            pl.BlockSpec((tq, w), lambda b, h, i: (b * nq + i, h)),
            pl.BlockSpec((None, FOX_PAIR, 1, t), lambda b, h, i: (b, h, 0, 0)),
        ],
        out_specs=pl.BlockSpec((tq, w), lambda b, h, i: (b * nq + i, h)),
        scratch_shapes=[pltpu.VMEM((FOX_PAIR, t, 2 * FOX_HEAD_DIM), BF16), pltpu.SMEM((1,), jnp.int32)],
        compiler_params=_params("parallel", "parallel", "arbitrary"),
        name="fox_prompt",
    )(uq, uk, uv, ug, ct)


def _fox_sample_kernel(q_ref, kp_ref, vp_ref, kn_ref, vn_ref, fg_ref, c_ref, o_ref, *, p, t):
    c1 = FOX_HEAD_DIM ** -0.5 * LOG2E
    kp = kp_ref[...].reshape(p, D_MODEL).astype(BF16)
    vp = vp_ref[...].reshape(p, D_MODEL).astype(BF16)
    r = lax.broadcasted_iota(jnp.int32, (t, t), 0)
    c = lax.broadcasted_iota(jnp.int32, (t, t), 1)
    for j in range(FOX_HEADS):
        sl = slice(j * FOX_HEAD_DIM, (j + 1) * FOX_HEAD_DIM)
        q = q_ref[:, sl]
        cq2 = _col_from_row(c_ref[j, :, p:p + t]) * LOG2E
        s_p = lax.dot_general(q, kp[:, sl], NT_DIMS, preferred_element_type=F32) * c1 - c_ref[j, :, 0:p] * LOG2E
        s_n = (lax.dot_general(q, kn_ref[:, sl], NT_DIMS, preferred_element_type=F32) * c1
               - c_ref[j, :, p:p + t] * LOG2E)
        s_n = jnp.where(c <= r, s_n, NEG_BIG)
        m = jnp.maximum(jnp.max(s_p, axis=1, keepdims=True), jnp.max(s_n, axis=1, keepdims=True)) + cq2
        e_p = jnp.exp2(s_p - (m - cq2))
        e_n = jnp.exp2(s_n - (m - cq2))
        inv = 1.0 / (jnp.sum(e_p, axis=1, keepdims=True) + jnp.sum(e_n, axis=1, keepdims=True))
        o = (jnp.dot((e_p * inv).astype(BF16), vp[:, sl], preferred_element_type=F32)
             + jnp.dot((e_n * inv).astype(BF16), vn_ref[:, sl], preferred_element_type=F32))
        o_ref[:, sl] = (o * _silu(fg_ref[:, sl].astype(F32))).astype(o_ref.dtype)


def _fox_sample(uq, uk, uv, ug, k_cache, v_cache, layer, ct, nbatch, t):
    p = k_cache.shape[2]
    past = pl.BlockSpec((None, None, p, FOX_HEADS, FOX_HEAD_DIM), lambda b: (layer, b, 0, 0, 0))
    col = pl.BlockSpec((t, D_MODEL), lambda b: (b, 0))
    return pl.pallas_call(
        functools.partial(_fox_sample_kernel, p=p, t=t),
        out_shape=jax.ShapeDtypeStruct((nbatch * t, D_MODEL), BF16),
        grid=(nbatch,),
        in_specs=[col, past, past, col, col, col,
                  pl.BlockSpec((None, FOX_HEADS, 1, p + t), lambda b: (b, 0, 0, 0))],
        out_specs=pl.BlockSpec((t, D_MODEL), lambda b: (b, 0)),
        compiler_params=_params("parallel"),
        name="fox_sample",
    )(uq, k_cache, v_cache, uk, uv, ug, ct)


def _mem_kernel(q_ref, g_ref, k_ref, v_ref, o_ref):
    c1 = MEM_HEAD_DIM ** -0.5 * LOG2E
    nm = k_ref.shape[0]
    k = k_ref[...].reshape(nm, D_MODEL).astype(BF16)
    v = v_ref[...].reshape(nm, D_MODEL).astype(BF16)
    for h in range(MEM_HEADS):
        sl = slice(h * MEM_HEAD_DIM, (h + 1) * MEM_HEAD_DIM)
        s = lax.dot_general(q_ref[:, sl], k[:, sl], NT_DIMS, preferred_element_type=F32) * c1
        e = jnp.exp2(s - jnp.max(s, axis=1, keepdims=True))
        p = e * (1.0 / jnp.sum(e, axis=1, keepdims=True))
        o = jnp.dot(p.astype(BF16), v[:, sl], preferred_element_type=F32)
        o_ref[:, sl] = (o * _silu(g_ref[:, sl].astype(F32))).astype(o_ref.dtype)


def _mem_attend(u3, mk, mv, nbatch, t, layer=None):
    tq = min(512, t)
    nq = t // tq
    if layer is None:
        nm = mk.shape[1]
        kv = pl.BlockSpec((None, nm, D_MODEL), lambda b, i: (b, 0, 0))
    else:
        nm = mk.shape[2]
        kv = pl.BlockSpec((None, None, nm, MEM_HEADS, MEM_HEAD_DIM), lambda b, i: (layer, b, 0, 0, 0))
    return pl.pallas_call(
        _mem_kernel,
        out_shape=jax.ShapeDtypeStruct((nbatch * t, D_MODEL), BF16),
        grid=(nbatch, nq),
        in_specs=[pl.BlockSpec((tq, D_MODEL), lambda b, i: (b * nq + i, U3_Q)),
                  pl.BlockSpec((tq, D_MODEL), lambda b, i: (b * nq + i, U3_G)), kv, kv],
        out_specs=pl.BlockSpec((tq, D_MODEL), lambda b, i: (b * nq + i, 0)),
        compiler_params=_params("parallel", "arbitrary"),
        name="mem_attend",
    )(u3, u3, mk, mv)


def _merge_kernel(ys_ref, yf_ref, ym_ref, ws_ref, wf_ref, wm_ref, gs_ref, gf_ref, gm_ref, o_ref):
    def branch(y_ref, w_ref, g_ref):
        return _sigmoid(g_ref[...].astype(F32)) * jnp.dot(y_ref[...], w_ref[...], preferred_element_type=F32)
    o_ref[...] = (branch(ys_ref, ws_ref, gs_ref) + branch(yf_ref, wf_ref, gf_ref)
                  + branch(ym_ref, wm_ref, gm_ref)).astype(o_ref.dtype)


def _merge(u, y_ssd, y_fox, y_mem, w_s, w_f, w_m, tm, tn):
    m = y_ssd.shape[0]
    nj = D_MODEL // tn
    yspec = pl.BlockSpec((tm, D_MODEL), lambda i, j: (i, 0))
    wspec = pl.BlockSpec((D_MODEL, tn), lambda i, j: (0, j))
    gspec = lambda col: pl.BlockSpec((tm, tn), lambda i, j: (i, col * nj + j))
    return pl.pallas_call(
        _merge_kernel,
        out_shape=jax.ShapeDtypeStruct((m, D_MODEL), BF16),
        grid=(m // tm, nj),
        in_specs=[yspec, yspec, yspec, wspec, wspec, wspec, gspec(U3_GS), gspec(U3_GF), gspec(U3_GM)],
        out_specs=pl.BlockSpec((tm, tn), lambda i, j: (i, j)),
        compiler_params=_params("parallel", "arbitrary"),
        name="gated_merge",
    )(y_ssd, y_fox, y_mem, w_s, w_f, w_m, u, u, u)


def _final_kernel(mg_ref, w_ref, x_ref, g_ref, o_ref, *, normalize):
    xo = x_ref[...] + jnp.dot(mg_ref[...], w_ref[...], preferred_element_type=F32)
    if normalize:
        ms = jnp.mean(xo * xo, axis=-1, keepdims=True)
        xo = xo * lax.rsqrt(ms + EPS) * g_ref[...]
    o_ref[...] = xo


def _final(merged, w_out, x, g_final, tm, normalize):
    m = x.shape[0]
    return pl.pallas_call(
        functools.partial(_final_kernel, normalize=normalize),
        out_shape=jax.ShapeDtypeStruct((m, D_MODEL), F32),
        grid=(m // tm,),
        in_specs=[pl.BlockSpec((tm, D_MODEL), lambda i: (i, 0)),
                  pl.BlockSpec((D_MODEL, D_MODEL), lambda i: (0, 0)),
                  pl.BlockSpec((tm, D_MODEL), lambda i: (i, 0)),
                  pl.BlockSpec((1, D_MODEL), lambda i: (0, 0))],
        out_specs=pl.BlockSpec((tm, D_MODEL), lambda i: (i, 0)),
        compiler_params=_params("parallel"),
        name="out_proj_norm",
    )(merged, w_out, x, g_final.reshape(1, D_MODEL))


def _row_tile(m, pref):
    t = pref
    while m % t:
        t //= 2
    return t


def _layer(x, conv0, h0, caches, logf_past, mem_k, mem_v, wd, g_final, last):
    nbatch, t, d = x.shape
    m = nbatch * t
    x2 = x.reshape(m, d)
    tm = _row_tile(m, PROJ_TM)
    tm2 = _row_tile(m, PROJ_TM_WIDE)
    tn = PROJ_TN

    wt = wd["w_in_t"]
    _, (r2, _), (r3, n3) = W_IN_SEGMENTS
    uq, h = _mm_nt_norm(x2, wd["g_norm"], wt, r2, D_MODEL, tm, tn, "norm_in_proj_fox_q")
    u1, small = _mm_ssd(h, wt, tm2, tn)
    uk, k_new = _mm_nt_heads(h, wt, r2 + D_MODEL, tm, tn, "in_proj_fox_k")
    uv, v_new = _mm_nt_heads(h, wt, r2 + 2 * D_MODEL, tm, tn, "in_proj_fox_v")
    ug = _mm_nt(h, wt, r2 + 3 * D_MODEL, D_MODEL, BF16, tm2, tn, "in_proj_fox_gate")
    u3 = _mm_nt(h, wt, r3, n3, BF16, tm2, tn, "in_proj_mem_gates")

    past_t = None if logf_past is None else jnp.transpose(logf_past, (0, 2, 1))
    logf_t, ct = _logf(small, wd["b_forget"], past_t, nbatch, t)
    ct = ct.reshape(nbatch, FOX_HEADS, 1, ct.shape[-1])

    conv0p = jnp.pad(conv0, ((0, 0), (8 - (SSD_CONV - 1), 0), (0, 0)))
    y_ssd, h_final, new_conv = _ssd(u1, small, conv0p, h0, wd["w_conv"], wd["b_conv"], wd["dt_bias"], wd["a_log"],
                                    wd["d_skip"], wd["g_ssd_out"], nbatch, t)

    if caches is None:
        y_fox = _fox_prompt(uq, uk, uv, ug, ct, nbatch, t)
        y_mem = _mem_attend(u3, mem_k, mem_v, nbatch, t)
    else:
        layer, fox_k, fox_v = caches
        y_fox = _fox_sample(uq, uk, uv, ug, fox_k, fox_v, layer, ct, nbatch, t)
        y_mem = _mem_attend(u3, mem_k, mem_v, nbatch, t, layer)
    merged = _merge(u3, y_ssd, y_fox, y_mem, wd["w_o_ssd"], wd["w_o_fox"], wd["w_o_mem"], tm, tn // 2)
    y = _final(merged, wd["w_out"], x2, g_final, _row_tile(m, PROJ_TM // 2), last)

    return (y.reshape(nbatch, t, d), new_conv, h_final,
            k_new.reshape(nbatch, t, FOX_HEADS, FOX_HEAD_DIM), v_new.reshape(nbatch, t, FOX_HEADS, FOX_HEAD_DIM),
            jnp.transpose(logf_t, (0, 2, 1)))


def kernel(x_prompt, x_sample, mem_prompt, cache_fox_k, cache_fox_v, cache_fox_logf, state_ssd, state_ssd_conv,
           cache_mem_k, cache_mem_v, g_norm, w_in, w_conv, b_conv, dt_bias, a_log, d_skip, g_ssd_out, b_forget,
           g_mem, w_mem_kv, w_o_ssd, w_o_fox, w_o_mem, w_out, g_final):
    depth = w_in.shape[0]
    xp, xs = x_prompt, x_sample
    bp = xp.shape[0]
    n_mem = mem_prompt.shape[1]
    outs = [[] for _ in range(12)]
    for l in range(depth):
        wkv = _cast_bf16(w_mem_kv[l])
        wd = {
            "g_norm": g_norm[l],
            "w_in_t": _cast_bf16(jnp.transpose(w_in[l])),
            "w_conv": w_conv[l], "b_conv": b_conv[l], "dt_bias": dt_bias[l], "a_log": a_log[l], "d_skip": d_skip[l],
            "g_ssd_out": g_ssd_out[l], "b_forget": b_forget[l],
            "w_o_ssd": _cast_bf16(w_o_ssd[l]), "w_o_fox": _cast_bf16(w_o_fox[l]),
            "w_o_mem": _cast_bf16(w_o_mem[l]), "w_out": _cast_bf16(w_out[l]),
        }
        hm = _rmsnorm(mem_prompt.reshape(bp * n_mem, D_MODEL), g_mem[l], _row_tile(bp * n_mem, 256))
        tmm = _row_tile(bp * n_mem, 512)
        mk, mk_b = _mm_heads(hm, wkv, 0, MEM_HEADS, MEM_HEAD_DIM, tmm, "mem_k_proj")
        mv, mv_b = _mm_heads(hm, wkv, D_MODEL, MEM_HEADS, MEM_HEAD_DIM, tmm, "mem_v_proj")

        xp, c_p, h_p, k_p, v_p, lf_p = _layer(
            xp, jnp.zeros((bp, SSD_CONV - 1, SSD_CONV_CH), F32),
            jnp.zeros((bp, SSD_HEADS, SSD_HEAD_DIM, SSD_D_STATE), F32), None, None,
            mk_b.reshape(bp, n_mem, D_MODEL), mv_b.reshape(bp, n_mem, D_MODEL), wd, g_final, l == depth - 1)
        xs, c_s, h_s, k_s, v_s, lf_s = _layer(
            xs, state_ssd_conv[l], state_ssd[l], (l, cache_fox_k, cache_fox_v), cache_fox_logf[l],
            cache_mem_k, cache_mem_v, wd, g_final, l == depth - 1)
        for lst, val in zip(outs, (k_p, v_p, lf_p, h_p, c_p,
                                   mk.reshape(bp, n_mem, MEM_HEADS, MEM_HEAD_DIM),
                                   mv.reshape(bp, n_mem, MEM_HEADS, MEM_HEAD_DIM),
                                   k_s, v_s, lf_s, h_s, c_s)):
            lst.append(val)
    return (xp, xs) + tuple(jnp.stack(o) for o in outs)
```

```python
import functools

import numpy as np
import jax
import jax.numpy as jnp
from jax import lax
from jax.experimental import pallas as pl
from jax.experimental.pallas import tpu as pltpu

F32 = jnp.float32
BF16 = jnp.bfloat16

EPS = 1e-6
D_MODEL = 2048
SSD_HEAD_DIM = 64
SSD_HEADS = 32
SSD_GROUPS = 4
SSD_D_STATE = 128
SSD_CONV = 4
SSD_BC = SSD_GROUPS * SSD_D_STATE
SSD_CONV_CH = D_MODEL + 2 * SSD_BC
FOX_HEADS = 16
FOX_HEAD_DIM = 128
MEM_HEADS = 4
MEM_HEAD_DIM = 512
SSD_CHUNK = 64
SSD_NB = 2
NEG_BIG = -1e30

VMEM_LIMIT = 56 * 1024 * 1024
PROJ_TN = 1024
PROJ_TM = 1024
PROJ_TM_WIDE = 2048

W_IN_SEGMENTS = ((0, D_MODEL + SSD_CONV_CH),
                 (D_MODEL + SSD_CONV_CH + SSD_HEADS, 4 * D_MODEL),
                 (D_MODEL + SSD_CONV_CH + SSD_HEADS + 4 * D_MODEL + FOX_HEADS, 5 * D_MODEL))
W_IN_DT = D_MODEL + SSD_CONV_CH
W_IN_FF = W_IN_SEGMENTS[1][0] + 4 * D_MODEL
U1_Z, U1_XS = 0, 1
U3_Q, U3_G, U3_GS, U3_GF, U3_GM = 0, 1, 2, 3, 4
LOG2E = 1.4426950408889634


def _params(*sem):
    return pltpu.CompilerParams(dimension_semantics=sem, vmem_limit_bytes=VMEM_LIMIT)


def _split3(x):
    hi = x.astype(BF16)
    r1 = x - hi.astype(F32)
    mid = r1.astype(BF16)
    lo = (r1 - mid.astype(F32)).astype(BF16)
    return hi, mid, lo


def _softplus(x):
    return jnp.maximum(x, 0.0) + jnp.log1p(jnp.exp(-jnp.abs(x)))


def _sigmoid(x):
    return 0.5 * jnp.tanh(0.5 * x) + 0.5


def _silu(x):
    return x * _sigmoid(x)


def _rmsnorm_kernel(x_ref, g_ref, o_ref):
    x = x_ref[...]
    ms = jnp.mean(x * x, axis=-1, keepdims=True)
    o_ref[...] = (x * lax.rsqrt(ms + EPS) * g_ref[...]).astype(o_ref.dtype)


def _rmsnorm(x, g, tm):
    m, d = x.shape
    return pl.pallas_call(
        _rmsnorm_kernel,
        out_shape=jax.ShapeDtypeStruct((m, d), BF16),
        grid=(m // tm,),
        in_specs=[pl.BlockSpec((tm, d), lambda i: (i, 0)), pl.BlockSpec((1, d), lambda i: (0, 0))],
        out_specs=pl.BlockSpec((tm, d), lambda i: (i, 0)),
        compiler_params=_params("parallel"),
        name="rmsnorm",
    )(x, g.reshape(1, d))


def _mm_heads_kernel(h_ref, w_ref, o_ref, ob_ref):
    acc = jnp.dot(h_ref[...], w_ref[...], preferred_element_type=F32)
    o_ref[...] = acc.reshape(o_ref.shape)
    ob_ref[...] = acc.astype(ob_ref.dtype)


def _mm_heads(h, w, w_col0, heads, head_dim, tm, name):
    m, k = h.shape
    n = heads * head_dim
    return pl.pallas_call(
        _mm_heads_kernel,
        out_shape=(jax.ShapeDtypeStruct((m, heads, head_dim), F32), jax.ShapeDtypeStruct((m, n), BF16)),
        grid=(m // tm,),
        in_specs=[pl.BlockSpec((tm, k), lambda i: (i, 0)), pl.BlockSpec((k, n), lambda i: (0, w_col0 // n))],
        out_specs=(pl.BlockSpec((tm, heads, head_dim), lambda i: (i, 0, 0)), pl.BlockSpec((tm, n), lambda i: (i, 0))),
        compiler_params=_params("parallel"),
        name=name,
    )(h, w)


def _mm_nt_kernel(h_ref, wt_ref, o_ref):
    o_ref[...] = lax.dot_general(h_ref[...], wt_ref[...], NT_DIMS, preferred_element_type=F32).astype(o_ref.dtype)


def _wt_rows_spec(tn, k, row0, grid_rank):
    g = 16
    assert row0 % g == 0 and tn % g == 0
    if grid_rank == 1:
        return pl.BlockSpec((pl.Element(tn), pl.Element(k)), lambda i: (row0, 0))
    return pl.BlockSpec((pl.Element(tn), pl.Element(k)), lambda i, j: ((row0 // g + j * (tn // g)) * g, 0))


def _mm_nt(h, wt, row0, n, out_dtype, tm, tn, name):
    m, k = h.shape
    return pl.pallas_call(
        _mm_nt_kernel,
        out_shape=jax.ShapeDtypeStruct((m, n), out_dtype),
        grid=(m // tm, n // tn),
        in_specs=[pl.BlockSpec((tm, k), lambda i, j: (i, 0)), _wt_rows_spec(tn, k, row0, 2)],
        out_specs=pl.BlockSpec((tm, tn), lambda i, j: (i, j)),
        compiler_params=_params("parallel", "arbitrary"),
        name=name,
    )(h, wt)


def _mm_nt_norm_kernel(x_ref, g_ref, wt_ref, o_ref, h_ref):
    x = x_ref[...]
    ms = jnp.mean(x * x, axis=-1, keepdims=True)
    hn = (x * lax.rsqrt(ms + EPS) * g_ref[...]).astype(BF16)
    h_ref[...] = hn
    o_ref[...] = lax.dot_general(hn, wt_ref[...], NT_DIMS, preferred_element_type=F32).astype(o_ref.dtype)


def _mm_nt_norm(x, g, wt, row0, n, tm, tn, name):
    m, k = x.shape
    return pl.pallas_call(
        _mm_nt_norm_kernel,
        out_shape=(jax.ShapeDtypeStruct((m, n), BF16), jax.ShapeDtypeStruct((m, k), BF16)),
        grid=(m // tm, n // tn),
        in_specs=[pl.BlockSpec((tm, k), lambda i, j: (i, 0)), pl.BlockSpec((1, k), lambda i, j: (0, 0)),
                  _wt_rows_spec(tn, k, row0, 2)],
        out_specs=(pl.BlockSpec((tm, tn), lambda i, j: (i, j)), pl.BlockSpec((tm, k), lambda i, j: (i, 0))),
        compiler_params=_params("parallel", "arbitrary"),
        name=name,
    )(x, g.reshape(1, k), wt)


def _mm_ssd_kernel(h_ref, wt_ref, wdt_ref, wff_ref, o_ref, small_ref):
    o_ref[...] = lax.dot_general(h_ref[...], wt_ref[...], NT_DIMS, preferred_element_type=F32).astype(o_ref.dtype)

    @pl.when(pl.program_id(1) == 0)
    def _narrow():
        w = jnp.concatenate([wdt_ref[...], wff_ref[...]], axis=0)
        small_ref[...] = lax.dot_general(h_ref[...], w, NT_DIMS, preferred_element_type=F32)


def _mm_ssd(h, wt, tm, tn):
    m, k = h.shape
    row0, n = W_IN_SEGMENTS[0]
    fixed = lambda r0: pl.BlockSpec((pl.Element(128), pl.Element(k)), lambda i, j: (r0, 0))
    return pl.pallas_call(
        _mm_ssd_kernel,
        out_shape=(jax.ShapeDtypeStruct((m, n), BF16), jax.ShapeDtypeStruct((m, 256), F32)),
        grid=(m // tm, n // tn),
        in_specs=[pl.BlockSpec((tm, k), lambda i, j: (i, 0)), _wt_rows_spec(tn, k, row0, 2),
                  fixed(W_IN_DT), fixed(W_IN_FF)],
        out_specs=(pl.BlockSpec((tm, tn), lambda i, j: (i, j)), pl.BlockSpec((tm, 256), lambda i, j: (i, 0))),
        compiler_params=_params("parallel", "arbitrary"),
        name="in_proj_ssd",
    )(h, wt, wt, wt)


def _mm_nt_heads_kernel(h_ref, wt_ref, o_ref, oh_ref):
    acc = lax.dot_general(h_ref[...], wt_ref[...], NT_DIMS, preferred_element_type=F32)
    o_ref[...] = acc.astype(o_ref.dtype)
    oh_ref[...] = acc.reshape(oh_ref.shape)


def _mm_nt_heads(h, wt, row0, tm, tn, name):
    m, k = h.shape
    hb = tn // FOX_HEAD_DIM
    return pl.pallas_call(
        _mm_nt_heads_kernel,
        out_shape=(jax.ShapeDtypeStruct((m, D_MODEL), BF16),
                   jax.ShapeDtypeStruct((m, FOX_HEADS, FOX_HEAD_DIM), F32)),
        grid=(m // tm, D_MODEL // tn),
        in_specs=[pl.BlockSpec((tm, k), lambda i, j: (i, 0)), _wt_rows_spec(tn, k, row0, 2)],
        out_specs=(pl.BlockSpec((tm, tn), lambda i, j: (i, j)),
                   pl.BlockSpec((tm, hb, FOX_HEAD_DIM), lambda i, j: (i, j, 0))),
        compiler_params=_params("parallel", "arbitrary"),
        name=name,
    )(h, wt)


def _cast_kernel(a_ref, o_ref):
    o_ref[...] = a_ref[...].astype(o_ref.dtype)


def _cast_bf16(w):
    rows, cols = w.shape
    tr = max(d for d in range(16, 1025, 16) if rows % d == 0)
    return pl.pallas_call(
        _cast_kernel,
        out_shape=jax.ShapeDtypeStruct((rows, cols), BF16),
        grid=(rows // tr,),
        in_specs=[pl.BlockSpec((tr, cols), lambda i: (i, 0))],
        out_specs=pl.BlockSpec((tr, cols), lambda i: (i, 0)),
        compiler_params=_params("parallel"),
        name="weight_cast",
    )(w)


def _upper3(tb):
    k = lax.broadcasted_iota(jnp.int32, (3 * tb, tb), 0) % tb
    t = lax.broadcasted_iota(jnp.int32, (3 * tb, tb), 1)
    return jnp.where(k <= t, 1.0, 0.0).astype(BF16)


def _cumsum_lanes(x, carry, tb):
    n = x.shape[1]
    u3 = _upper3(tb)
    out = []
    for s in range(0, n, tb):
        hi, mid, lo = _split3(x[:, s:s + tb])
        c = jnp.dot(jnp.concatenate([hi, mid, lo], axis=1), u3, preferred_element_type=F32) + carry
        carry = c[:, tb - 1:tb]
        out.append(c)
    return out, carry


def _logf_kernel(*refs, nb, t, p):
    if p:
        ffc_ref, bias_ref, past_ref, lf_ref, c_ref = refs
    else:
        ffc_ref, bias_ref, lf_ref, c_ref = refs
    fft = ffc_ref[...].T[0:FOX_HEADS, :]
    lf_all = -_softplus(-(fft + bias_ref[...]))
    for b in range(nb):
        lf = lf_all[:, b * t:(b + 1) * t]
        lf_ref[b] = lf
        carry = jnp.zeros((FOX_HEADS, 1), F32)
        if p:
            tbp = min(256, p)
            blocks, carry = _cumsum_lanes(past_ref[b], carry, tbp)
            for i, c in enumerate(blocks):
                c_ref[b, :, i * tbp:(i + 1) * tbp] = c
        tb = min(256, t)
        blocks, carry = _cumsum_lanes(lf, carry, tb)
        for i, c in enumerate(blocks):
            c_ref[b, :, p + i * tb:p + (i + 1) * tb] = c


def _logf(small, b_forget, past_t, nbatch, t):
    p = 0 if past_t is None else past_t.shape[2]
    nb = 1 if t % 128 == 0 else nbatch
    grid = (nbatch // nb,)
    in_specs = [pl.BlockSpec((nb * t, 128), lambda i: (i, 1)),
                pl.BlockSpec((FOX_HEADS, 1), lambda i: (0, 0))]
    args = [small, b_forget.reshape(FOX_HEADS, 1)]
    if p:
        in_specs.append(pl.BlockSpec((nb, FOX_HEADS, p), lambda i: (i, 0, 0)))
        args.append(past_t)
    return pl.pallas_call(
        functools.partial(_logf_kernel, nb=nb, t=t, p=p),
        out_shape=(jax.ShapeDtypeStruct((nbatch, FOX_HEADS, t), F32),
                   jax.ShapeDtypeStruct((nbatch, FOX_HEADS, p + t), F32)),
        grid=grid,
        in_specs=in_specs,
        out_specs=(pl.BlockSpec((nb, FOX_HEADS, t), lambda i: (i, 0, 0)),
                   pl.BlockSpec((nb, FOX_HEADS, p + t), lambda i: (i, 0, 0))),
        compiler_params=_params("parallel"),
        name="logf_cumsum",
    )(*args)


def _expand_matrix():
    r = np.arange(128)[:, None]
    c = np.arange(D_MODEL)[None, :]
    return jnp.asarray(((r < 96) & ((r % SSD_HEADS) == (c // SSD_HEAD_DIM))).astype(np.float32), dtype=BF16)


def _pack3(x):
    lane = lax.broadcasted_iota(jnp.int32, x.shape, 1)
    x = jnp.where(lane < SSD_HEADS, x, 0.0)
    hi, mid, lo = _split3(x)
    packed = hi.astype(F32) + pltpu.roll(mid.astype(F32), SSD_HEADS, 1) + pltpu.roll(lo.astype(F32), 2 * SSD_HEADS, 1)
    return packed.astype(BF16)


HIST = 16


def _shift_matrix(L):
    sh = np.zeros((SSD_CONV * L, 2 * HIST + L), np.float32)
    for d in range(SSD_CONV):
        for t in range(L):
            sh[d * L + t, 2 * HIST + t - d] = 1.0
            if t - d < 0:
                sh[d * L + t, HIST + t - d] = 1.0
    return jnp.asarray(sh, dtype=BF16)


def _ssd_kernel(xs_ref, b_ref, c_ref, z_ref, sm_ref, conv0_ref, h0_ref,
                wconv_ref, bconv_ref, dtb_ref, alog_ref, dskip_ref, gout_ref, e3_ref, sh_ref,
                y_ref, hfin_ref, convout_ref,
                st_ref, xq_ref, *, nc):
    ci = pl.program_id(1)
    nb, L = xs_ref.shape[0], xs_ref.shape[1]
    P = SSD_HEAD_DIM
    W = D_MODEL
    gw = W // SSD_GROUPS
    hpm = 256 // L

    @pl.when(ci == 0)
    def _init():
        for bi in range(nb):
            st_ref[bi] = h0_ref[bi].T
            hist = jnp.concatenate([jnp.zeros((HIST - 8, SSD_CONV_CH), F32), conv0_ref[bi]], axis=0)
            hist_hi = hist.astype(BF16)
            xq_ref[bi, 0:HIST, :] = hist_hi
            xq_ref[bi, HIST:2 * HIST, :] = (hist - hist_hi.astype(F32)).astype(BF16)

    lane = lax.broadcasted_iota(jnp.int32, (L, 128), 1)
    kk = lax.broadcasted_iota(jnp.int32, (L, 3 * L), 1) % L
    ll = lax.broadcasted_iota(jnp.int32, (L, 3 * L), 0)
    tri3 = jnp.where(kk <= ll, 1.0, 0.0).astype(BF16)
    row = lax.broadcasted_iota(jnp.int32, (L, W), 0)
    sidx = lax.broadcasted_iota(jnp.int32, (L, W), 1) % L
    rb = lax.broadcasted_iota(jnp.int32, (256, hpm * P), 0) // L
    cb = lax.broadcasted_iota(jnp.int32, (256, hpm * P), 1) // P
    blockmask = rb == cb
    a_small = -jnp.exp(alog_ref[...])
    e3 = e3_ref[...]

    def chunk(bi):
        xq_ref[bi, 2 * HIST:, 0:W] = xs_ref[bi]
        xq_ref[bi, 2 * HIST:, W:W + SSD_BC] = b_ref[bi]
        xq_ref[bi, 2 * HIST:, W + SSD_BC:] = c_ref[bi]

        def conv(lo, hi):
            sh = jnp.dot(sh_ref[...], xq_ref[bi, :, lo:hi], preferred_element_type=F32)
            acc = bconv_ref[:, lo:hi]
            for d in range(SSD_CONV):
                acc = acc + sh[d * L:(d + 1) * L] * wconv_ref[SSD_CONV - 1 - d:SSD_CONV - d, lo:hi]
            return _silu(acc)

        xs = conv(0, W)
        bm = conv(W, W + SSD_BC).astype(BF16)
        cm = conv(W + SSD_BC, W + 2 * SSD_BC).astype(BF16)
        xq_ref[bi, 0:HIST, :] = jnp.zeros((HIST, SSD_CONV_CH), BF16)
        xq_ref[bi, HIST:2 * HIST, :] = xq_ref[bi, HIST + L:2 * HIST + L, :]

        dt = _softplus(sm_ref[bi] + dtb_ref[...])
        dta = jnp.where(lane < SSD_HEADS, dt * a_small, 0.0)
        hi, mid, lo = _split3(dta)
        acum = jnp.dot(tri3, jnp.concatenate([hi, mid, lo], axis=0), preferred_element_type=F32)
        a_x = jnp.dot(_pack3(acum), e3, preferred_element_type=F32)
        dt_x = jnp.dot(_pack3(dt), e3, preferred_element_type=F32)

        a_row = jnp.sum(jnp.where(row == sidx, a_x, 0.0), axis=0, keepdims=True)
        ldec = jnp.exp(jnp.where(row >= sidx, a_x - a_row, NEG_BIG))
        cbx = []
        for g in range(SSD_GROUPS):
            cg = cm[:, g * SSD_D_STATE:(g + 1) * SSD_D_STATE]
            bg = bm[:, g * SSD_D_STATE:(g + 1) * SSD_D_STATE]
            brep = jnp.concatenate([bg] * (SSD_HEADS // SSD_GROUPS), axis=0)
            cbx.append(lax.dot_general(cg, brep, NT_DIMS, preferred_element_type=F32))
        m = (jnp.concatenate(cbx, axis=1) * ldec).astype(BF16)

        xdt = xs * dt_x
        xdt_b = xdt.astype(BF16)
        y_parts = []
        for j in range(SSD_HEADS // hpm):
            xj = xdt_b[:, j * hpm * P:(j + 1) * hpm * P]
            bd = jnp.where(blockmask, jnp.concatenate([xj] * hpm, axis=0), jnp.zeros((), BF16))
            y_parts.append(jnp.dot(m[:, j * 256:(j + 1) * 256], bd, preferred_element_type=F32))
        y = jnp.concatenate(y_parts, axis=1)

        st = st_ref[bi]
        st_b = st.astype(BF16)
        yo = [jnp.dot(cm[:, g * SSD_D_STATE:(g + 1) * SSD_D_STATE], st_b[:, g * gw:(g + 1) * gw],
                      preferred_element_type=F32) for g in range(SSD_GROUPS)]
        y = y + jnp.concatenate(yo, axis=1) * jnp.exp(a_x)

        a_last = a_x[L - 1:L, :]
        xw = (xdt * jnp.exp(a_last - a_x)).astype(BF16)
        upd = [lax.dot_general(bm[:, g * SSD_D_STATE:(g + 1) * SSD_D_STATE], xw[:, g * gw:(g + 1) * gw],
                               (((0,), (0,)), ((), ())), preferred_element_type=F32) for g in range(SSD_GROUPS)]
        st_ref[bi] = st * jnp.exp(a_last) + jnp.concatenate(upd, axis=1)

        y = y + dskip_ref[...] * xs
        gz = y * _silu(z_ref[bi].astype(F32))
        ms = jnp.mean(gz * gz, axis=-1, keepdims=True)
        y_ref[bi] = (gz * lax.rsqrt(ms + EPS) * gout_ref[...]).astype(y_ref.dtype)

    for bi in range(nb):
        chunk(bi)

    @pl.when(ci == nc - 1)
    def _final():
        for bi in range(nb):
            hfin_ref[bi] = st_ref[bi].T
            convout_ref[bi] = xq_ref[bi, HIST:2 * HIST, :].astype(F32)[HIST - 8:, :]


def _ssd(u, small, conv0, h0, w_conv, b_conv, dt_bias, a_log, d_skip, g_ssd_out, nbatch, t):
    L = SSD_CHUNK
    nc = t // L
    nb = SSD_NB if nbatch % SSD_NB == 0 else 1
    pad128 = lambda v: jnp.pad(v.astype(F32), (0, 128 - v.shape[0])).reshape(1, 128)
    rep = lambda v: jnp.repeat(v.astype(F32), SSD_HEAD_DIM).reshape(1, D_MODEL)
    const2 = lambda shape: pl.BlockSpec(shape, lambda b, c: (0, 0))
    seq = lambda width, col: pl.BlockSpec((nb, L, width), lambda b, c: (b, c, col))
    per_seq = lambda rows, width: pl.BlockSpec((nb, rows, width), lambda b, c: (b, 0, 0))
    u3d = u.reshape(nbatch, t, u.shape[1])
    bc0 = (U1_XS + 1) * D_MODEL // SSD_BC
    y, hfin, convout = pl.pallas_call(
        functools.partial(_ssd_kernel, nc=nc),
        out_shape=(jax.ShapeDtypeStruct((nbatch, t, D_MODEL), BF16),
                   jax.ShapeDtypeStruct((nbatch, D_MODEL, SSD_D_STATE), F32),
                   jax.ShapeDtypeStruct((nbatch, 8, SSD_CONV_CH), F32)),
        grid=(nbatch // nb, nc),
        in_specs=[
            seq(D_MODEL, U1_XS), seq(SSD_BC, bc0), seq(SSD_BC, bc0 + 1), seq(D_MODEL, U1_Z), seq(128, 0),
            per_seq(8, SSD_CONV_CH), per_seq(D_MODEL, SSD_D_STATE),
            const2((SSD_CONV, SSD_CONV_CH)), const2((1, SSD_CONV_CH)),
            const2((1, 128)), const2((1, 128)), const2((1, D_MODEL)), const2((1, D_MODEL)),
            const2((128, D_MODEL)), const2((SSD_CONV * L, 2 * HIST + L)),
        ],
        out_specs=(seq(D_MODEL, 0), per_seq(D_MODEL, SSD_D_STATE), per_seq(8, SSD_CONV_CH)),
        scratch_shapes=[pltpu.VMEM((nb, SSD_D_STATE, D_MODEL), F32),
                        pltpu.VMEM((nb, 2 * HIST + L, SSD_CONV_CH), BF16)],
        compiler_params=_params("parallel", "arbitrary"),
        name="ssd_scan",
    )(u3d, u3d, u3d, u3d, small.reshape(nbatch, t, small.shape[1]), conv0, h0.reshape(nbatch, D_MODEL, SSD_D_STATE),
      w_conv, b_conv.reshape(1, SSD_CONV_CH), pad128(dt_bias), pad128(a_log), rep(d_skip),
      g_ssd_out.reshape(1, D_MODEL), _expand_matrix(), _shift_matrix(L))
    return (y.reshape(nbatch * t, D_MODEL), hfin.reshape(nbatch, SSD_HEADS, SSD_HEAD_DIM, SSD_D_STATE),
            convout[:, 8 - (SSD_CONV - 1):])


def _col_from_row(row_vals):
    n = row_vals.shape[1]
    r = lax.broadcasted_iota(jnp.int32, (n, n), 0)
    c = lax.broadcasted_iota(jnp.int32, (n, n), 1)
    return jnp.sum(jnp.where(r == c, row_vals, 0.0), axis=1, keepdims=True)


FOX_PAIR = 4
NT_DIMS = (((1,), (1,)), ((), ()))
FOX_TQ = 512


def _fox_prompt_kernel(q_ref, k_ref, v_ref, fg_ref, c_ref, o_ref, va_ref, ok_ref, *, tq):
    qi = pl.program_id(2)
    hd = FOX_HEAD_DIM
    t = k_ref.shape[0]
    c1 = hd ** -0.5 * LOG2E
    q0 = pl.multiple_of(qi * tq, tq)
    lanes = [slice(j * hd, (j + 1) * hd) for j in range(FOX_PAIR)]

    @pl.when(qi == 0)
    def _stage_values():
        for j in range(FOX_PAIR):
            va_ref[j, :, 0:hd] = v_ref[:, lanes[j]]
            va_ref[j, :, hd:2 * hd] = jnp.ones((t, hd), BF16)

    cq2_rep = [jnp.broadcast_to(c_ref[j, :, pl.ds(q0, tq)], (hd, tq)).T * LOG2E for j in range(FOX_PAIR)]
    cq2 = [cr[:, 0:1] for cr in cq2_rep]
    row = lax.broadcasted_iota(jnp.int32, (tq, tq), 0)
    col = lax.broadcasted_iota(jnp.int32, (tq, tq), 1)

    def scores(j, k0, masked):
        s = lax.dot_general(q_ref[:, lanes[j]], k_ref[pl.ds(k0, tq), lanes[j]], NT_DIMS, preferred_element_type=F32)
        t2 = s * c1 - c_ref[j, :, pl.ds(k0, tq)] * LOG2E
        return jnp.where(col <= row, t2, NEG_BIG) if masked else t2

    shift = []
    ones = jnp.ones((hd, hd), BF16)
    for j in range(FOX_PAIR):
        qk = q_ref[:, lanes[j]] * k_ref[pl.ds(q0, tq), lanes[j]]
        shift.append(jnp.dot(qk, ones, preferred_element_type=F32) * c1 - cq2_rep[j])

    def fast_tile(k0, accs):
        out = []
        for j in range(FOX_PAIR):
            t2 = scores(j, k0, False)
            e = jnp.concatenate([t2[:, i:i + hd] - shift[j] for i in range(0, tq, hd)], axis=1)
            out.append(accs[j] + jnp.dot(jnp.exp2(e).astype(BF16), va_ref[j, pl.ds(k0, tq), :],
                                         preferred_element_type=F32))
        return tuple(out)

    def fast_part(j, r0, nr, kk, nk):
        s = lax.dot_general(q_ref[r0:r0 + nr, lanes[j]], k_ref[pl.ds(q0 + kk, nk), lanes[j]], NT_DIMS,
                            preferred_element_type=F32)
        t2 = s * c1 - c_ref[j, :, pl.ds(q0 + kk, nk)] * LOG2E
        rr = lax.broadcasted_iota(jnp.int32, (nr, nk), 0) + r0
        cc = lax.broadcasted_iota(jnp.int32, (nr, nk), 1) + kk
        t2 = jnp.where(cc <= rr, t2, NEG_BIG)
        e = jnp.concatenate([t2[:, i:i + hd] - shift[j][r0:r0 + nr] for i in range(0, nk, hd)], axis=1)
        return jnp.dot(jnp.exp2(e).astype(BF16), va_ref[j, pl.ds(q0 + kk, nk), :], preferred_element_type=F32)

    def fast_diag(accs):
        half = tq // 2
        out = []
        for j in range(FOX_PAIR):
            left = fast_part(j, 0, tq, 0, half)
            right = fast_part(j, half, half, half, half)
            out.append(accs[j] + left + jnp.concatenate([jnp.zeros((half, 2 * hd), F32), right], axis=0))
        return tuple(out)

    for qv in range(t // tq):
        @pl.when(qi == qv)
        def _fast(qv=qv):
            accs = tuple(jnp.zeros((tq, 2 * hd), F32) for _ in range(FOX_PAIR))
            for i in range(qv):
                accs = fast_tile(i * tq, accs)
            accs = fast_diag(accs)
            probe = jnp.float32(0.0)
            for j in range(FOX_PAIR):
                num, den = accs[j][:, 0:hd], accs[j][:, hd:2 * hd]
                o_ref[:, lanes[j]] = (num / den * _silu(fg_ref[:, lanes[j]].astype(F32))).astype(o_ref.dtype)
                probe = probe + jnp.sum(accs[j] * 0.0)
            ok_ref[0] = jnp.where(probe == 0.0, 1, 0)

    @pl.when(ok_ref[0] == 0)
    def _running_max():
        def tile(k0, carry, masked):
            out = []
            for j in range(FOX_PAIR):
                m_i, l_i, acc = carry[j]
                t2 = scores(j, k0, masked)
                m_new = jnp.maximum(m_i, jnp.max(t2, axis=1, keepdims=True) + cq2[j])
                p = jnp.exp2(t2 - (m_new - cq2[j]))
                alpha = jnp.exp2(m_i - m_new)
                l_new = alpha * l_i + jnp.sum(p, axis=1, keepdims=True)
                acc = alpha * acc + jnp.dot(p.astype(BF16), v_ref[pl.ds(k0, tq), lanes[j]],
                                            preferred_element_type=F32)
                out.append((m_new, l_new, acc))
            return tuple(out)

        init = tuple((jnp.full((tq, 1), NEG_BIG, F32), jnp.zeros((tq, 1), F32), jnp.zeros((tq, hd), F32))
                     for _ in range(FOX_PAIR))
        carry = lax.fori_loop(0, qi, lambda i, cr: tile(pl.multiple_of(i * tq, tq), cr, False), init)
        carry = tile(q0, carry, True)
        for j in range(FOX_PAIR):
            _, l_i, acc = carry[j]
            o_ref[:, lanes[j]] = (acc / l_i * _silu(fg_ref[:, lanes[j]].astype(F32))).astype(o_ref.dtype)


def _fox_prompt(uq, uk, uv, ug, ct, nbatch, t):
    tq = min(FOX_TQ, t)
    nq = t // tq
    w = FOX_PAIR * FOX_HEAD_DIM
    nb = D_MODEL // w
    return pl.pallas_call(
        functools.partial(_fox_prompt_kernel, tq=tq),
        out_shape=jax.ShapeDtypeStruct((nbatch * t, D_MODEL), BF16),
        grid=(nbatch, nb, nq),
        in_specs=[
            pl.BlockSpec((tq, w), lambda b, h, i: (b * nq + i, h)),
            pl.BlockSpec((t, w), lambda b, h, i: (b, h)),
            pl.BlockSpec((t, w), lambda b, h, i: (b, h)),
            pl.BlockSpec((tq, w), lambda b, h, i: (b * nq + i, h)),
            pl.BlockSpec((None, FOX_PAIR, 1, t), lambda b, h, i: (b, h, 0, 0)),
        ],
        out_specs=pl.BlockSpec((tq, w), lambda b, h, i: (b * nq + i, h)),
        scratch_shapes=[pltpu.VMEM((FOX_PAIR, t, 2 * FOX_HEAD_DIM), BF16), pltpu.SMEM((1,), jnp.int32)],
        compiler_params=_params("parallel", "parallel", "arbitrary"),
        name="fox_prompt",
    )(uq, uk, uv, ug, ct)


def _fox_sample_kernel(q_ref, kp_ref, vp_ref, kn_ref, vn_ref, fg_ref, c_ref, o_ref, *, p, t):
    c1 = FOX_HEAD_DIM ** -0.5 * LOG2E
    kp = kp_ref[...].reshape(p, D_MODEL).astype(BF16)
    vp = vp_ref[...].reshape(p, D_MODEL).astype(BF16)
    r = lax.broadcasted_iota(jnp.int32, (t, t), 0)
    c = lax.broadcasted_iota(jnp.int32, (t, t), 1)
    for j in range(FOX_HEADS):
        sl = slice(j * FOX_HEAD_DIM, (j + 1) * FOX_HEAD_DIM)
        q = q_ref[:, sl]
        cq2 = _col_from_row(c_ref[j, :, p:p + t]) * LOG2E
        s_p = lax.dot_general(q, kp[:, sl], NT_DIMS, preferred_element_type=F32) * c1 - c_ref[j, :, 0:p] * LOG2E
        s_n = (lax.dot_general(q, kn_ref[:, sl], NT_DIMS, preferred_element_type=F32) * c1
               - c_ref[j, :, p:p + t] * LOG2E)
        s_n = jnp.where(c <= r, s_n, NEG_BIG)
        m = jnp.maximum(jnp.max(s_p, axis=1, keepdims=True), jnp.max(s_n, axis=1, keepdims=True)) + cq2
        e_p = jnp.exp2(s_p - (m - cq2))
        e_n = jnp.exp2(s_n - (m - cq2))
        inv = 1.0 / (jnp.sum(e_p, axis=1, keepdims=True) + jnp.sum(e_n, axis=1, keepdims=True))
        o = (jnp.dot((e_p * inv).astype(BF16), vp[:, sl], preferred_element_type=F32)
             + jnp.dot((e_n * inv).astype(BF16), vn_ref[:, sl], preferred_element_type=F32))
        o_ref[:, sl] = (o * _silu(fg_ref[:, sl].astype(F32))).astype(o_ref.dtype)


def _fox_sample(uq, uk, uv, ug, k_cache, v_cache, layer, ct, nbatch, t):
    p = k_cache.shape[2]
    past = pl.BlockSpec((None, None, p, FOX_HEADS, FOX_HEAD_DIM), lambda b: (layer, b, 0, 0, 0))
    col = pl.BlockSpec((t, D_MODEL), lambda b: (b, 0))
    return pl.pallas_call(
        functools.partial(_fox_sample_kernel, p=p, t=t),
        out_shape=jax.ShapeDtypeStruct((nbatch * t, D_MODEL), BF16),
        grid=(nbatch,),
        in_specs=[col, past, past, col, col, col,
                  pl.BlockSpec((None, FOX_HEADS, 1, p + t), lambda b: (b, 0, 0, 0))],
        out_specs=pl.BlockSpec((t, D_MODEL), lambda b: (b, 0)),
        compiler_params=_params("parallel"),
        name="fox_sample",
    )(uq, k_cache, v_cache, uk, uv, ug, ct)


def _mem_kernel(q_ref, g_ref, k_ref, v_ref, o_ref):
    c1 = MEM_HEAD_DIM ** -0.5 * LOG2E
    nm = k_ref.shape[0]
    k = k_ref[...].reshape(nm, D_MODEL).astype(BF16)
    v = v_ref[...].reshape(nm, D_MODEL).astype(BF16)
    for h in range(MEM_HEADS):
        sl = slice(h * MEM_HEAD_DIM, (h + 1) * MEM_HEAD_DIM)
        s = lax.dot_general(q_ref[:, sl], k[:, sl], NT_DIMS, preferred_element_type=F32) * c1
        e = jnp.exp2(s - jnp.max(s, axis=1, keepdims=True))
        p = e * (1.0 / jnp.sum(e, axis=1, keepdims=True))
        o = jnp.dot(p.astype(BF16), v[:, sl], preferred_element_type=F32)
        o_ref[:, sl] = (o * _silu(g_ref[:, sl].astype(F32))).astype(o_ref.dtype)


def _mem_attend(u3, mk, mv, nbatch, t, layer=None):
    tq = min(512, t)
    nq = t // tq
    if layer is None:
        nm = mk.shape[1]
        kv = pl.BlockSpec((None, nm, D_MODEL), lambda b, i: (b, 0, 0))
    else:
        nm = mk.shape[2]
        kv = pl.BlockSpec((None, None, nm, MEM_HEADS, MEM_HEAD_DIM), lambda b, i: (layer, b, 0, 0, 0))
    return pl.pallas_call(
        _mem_kernel,
        out_shape=jax.ShapeDtypeStruct((nbatch * t, D_MODEL), BF16),
        grid=(nbatch, nq),
        in_specs=[pl.BlockSpec((tq, D_MODEL), lambda b, i: (b * nq + i, U3_Q)),
                  pl.BlockSpec((tq, D_MODEL), lambda b, i: (b * nq + i, U3_G)), kv, kv],
        out_specs=pl.BlockSpec((tq, D_MODEL), lambda b, i: (b * nq + i, 0)),
        compiler_params=_params("parallel", "arbitrary"),
        name="mem_attend",
    )(u3, u3, mk, mv)


def _merge_kernel(ys_ref, yf_ref, ym_ref, ws_ref, wf_ref, wm_ref, gs_ref, gf_ref, gm_ref, o_ref):
    def branch(y_ref, w_ref, g_ref):
        return _sigmoid(g_ref[...].astype(F32)) * jnp.dot(y_ref[...], w_ref[...], preferred_element_type=F32)
    o_ref[...] = (branch(ys_ref, ws_ref, gs_ref) + branch(yf_ref, wf_ref, gf_ref)
                  + branch(ym_ref, wm_ref, gm_ref)).astype(o_ref.dtype)


def _merge(u, y_ssd, y_fox, y_mem, w_s, w_f, w_m, tm, tn):
    m = y_ssd.shape[0]
    nj = D_MODEL // tn
    yspec = pl.BlockSpec((tm, D_MODEL), lambda i, j: (i, 0))
    wspec = pl.BlockSpec((D_MODEL, tn), lambda i, j: (0, j))
    gspec = lambda col: pl.BlockSpec((tm, tn), lambda i, j: (i, col * nj + j))
    return pl.pallas_call(
        _merge_kernel,
        out_shape=jax.ShapeDtypeStruct((m, D_MODEL), BF16),
        grid=(m // tm, nj),
        in_specs=[yspec, yspec, yspec, wspec, wspec, wspec, gspec(U3_GS), gspec(U3_GF), gspec(U3_GM)],
        out_specs=pl.BlockSpec((tm, tn), lambda i, j: (i, j)),
        compiler_params=_params("parallel", "arbitrary"),
        name="gated_merge",
    )(y_ssd, y_fox, y_mem, w_s, w_f, w_m, u, u, u)


def _final_kernel(mg_ref, w_ref, x_ref, g_ref, o_ref, *, normalize):
    xo = x_ref[...] + jnp.dot(mg_ref[...], w_ref[...], preferred_element_type=F32)
    if normalize:
        ms = jnp.mean(xo * xo, axis=-1, keepdims=True)
        xo = xo * lax.rsqrt(ms + EPS) * g_ref[...]
    o_ref[...] = xo


def _final(merged, w_out, x, g_final, tm, normalize):
    m = x.shape[0]
    return pl.pallas_call(
        functools.partial(_final_kernel, normalize=normalize),
        out_shape=jax.ShapeDtypeStruct((m, D_MODEL), F32),
        grid=(m // tm,),
        in_specs=[pl.BlockSpec((tm, D_MODEL), lambda i: (i, 0)),
                  pl.BlockSpec((D_MODEL, D_MODEL), lambda i: (0, 0)),
                  pl.BlockSpec((tm, D_MODEL), lambda i: (i, 0)),
                  pl.BlockSpec((1, D_MODEL), lambda i: (0, 0))],
        out_specs=pl.BlockSpec((tm, D_MODEL), lambda i: (i, 0)),
        compiler_params=_params("parallel"),
        name="out_proj_norm",
    )(merged, w_out, x, g_final.reshape(1, D_MODEL))


def _row_tile(m, pref):
    t = pref
    while m % t:
        t //= 2
    return t


def _layer(x, conv0, h0, caches, logf_past, mem_k, mem_v, wd, g_final, last):
    nbatch, t, d = x.shape
    m = nbatch * t
    x2 = x.reshape(m, d)
    tm = _row_tile(m, PROJ_TM)
    tm2 = _row_tile(m, PROJ_TM_WIDE)
    tn = PROJ_TN

    wt = wd["w_in_t"]
    _, (r2, _), (r3, n3) = W_IN_SEGMENTS
    uq, h = _mm_nt_norm(x2, wd["g_norm"], wt, r2, D_MODEL, tm, tn, "norm_in_proj_fox_q")
    u1, small = _mm_ssd(h, wt, tm2, tn)
    uk, k_new = _mm_nt_heads(h, wt, r2 + D_MODEL, tm, tn, "in_proj_fox_k")
    uv, v_new = _mm_nt_heads(h, wt, r2 + 2 * D_MODEL, tm, tn, "in_proj_fox_v")
    ug = _mm_nt(h, wt, r2 + 3 * D_MODEL, D_MODEL, BF16, tm2, tn, "in_proj_fox_gate")
    u3 = _mm_nt(h, wt, r3, n3, BF16, tm2, tn, "in_proj_mem_gates")

    past_t = None if logf_past is None else jnp.transpose(logf_past, (0, 2, 1))
    logf_t, ct = _logf(small, wd["b_forget"], past_t, nbatch, t)
    ct = ct.reshape(nbatch, FOX_HEADS, 1, ct.shape[-1])

    conv0p = jnp.pad(conv0, ((0, 0), (8 - (SSD_CONV - 1), 0), (0, 0)))
    y_ssd, h_final, new_conv = _ssd(u1, small, conv0p, h0, wd["w_conv"], wd["b_conv"], wd["dt_bias"], wd["a_log"],
                                    wd["d_skip"], wd["g_ssd_out"], nbatch, t)

    if caches is None:
        y_fox = _fox_prompt(uq, uk, uv, ug, ct, nbatch, t)
        y_mem = _mem_attend(u3, mem_k, mem_v, nbatch, t)
    else:
        layer, fox_k, fox_v = caches
        y_fox = _fox_sample(uq, uk, uv, ug, fox_k, fox_v, layer, ct, nbatch, t)
        y_mem = _mem_attend(u3, mem_k, mem_v, nbatch, t, layer)
    merged = _merge(u3, y_ssd, y_fox, y_mem, wd["w_o_ssd"], wd["w_o_fox"], wd["w_o_mem"], tm, tn // 2)
    y = _final(merged, wd["w_out"], x2, g_final, _row_tile(m, PROJ_TM // 2), last)

    return (y.reshape(nbatch, t, d), new_conv, h_final,
            k_new.reshape(nbatch, t, FOX_HEADS, FOX_HEAD_DIM), v_new.reshape(nbatch, t, FOX_HEADS, FOX_HEAD_DIM),
            jnp.transpose(logf_t, (0, 2, 1)))


def kernel(x_prompt, x_sample, mem_prompt, cache_fox_k, cache_fox_v, cache_fox_logf, state_ssd, state_ssd_conv,
           cache_mem_k, cache_mem_v, g_norm, w_in, w_conv, b_conv, dt_bias, a_log, d_skip, g_ssd_out, b_forget,
           g_mem, w_mem_kv, w_o_ssd, w_o_fox, w_o_mem, w_out, g_final):
    depth = w_in.shape[0]
    xp, xs = x_prompt, x_sample
    bp = xp.shape[0]
    n_mem = mem_prompt.shape[1]
    outs = [[] for _ in range(12)]
    for l in range(depth):
        wkv = _cast_bf16(w_mem_kv[l])
        wd = {
            "g_norm": g_norm[l],
            "w_in_t": _cast_bf16(jnp.transpose(w_in[l])),
            "w_conv": w_conv[l], "b_conv": b_conv[l], "dt_bias": dt_bias[l], "a_log": a_log[l], "d_skip": d_skip[l],
            "g_ssd_out": g_ssd_out[l], "b_forget": b_forget[l],
            "w_o_ssd": _cast_bf16(w_o_ssd[l]), "w_o_fox": _cast_bf16(w_o_fox[l]),
            "w_o_mem": _cast_bf16(w_o_mem[l]), "w_out": _cast_bf16(w_out[l]),
        }
        hm = _rmsnorm(mem_prompt.reshape(bp * n_mem, D_MODEL), g_mem[l], _row_tile(bp * n_mem, 256))
        tmm = _row_tile(bp * n_mem, 512)
        mk, mk_b = _mm_heads(hm, wkv, 0, MEM_HEADS, MEM_HEAD_DIM, tmm, "mem_k_proj")
        mv, mv_b = _mm_heads(hm, wkv, D_MODEL, MEM_HEADS, MEM_HEAD_DIM, tmm, "mem_v_proj")

        xp, c_p, h_p, k_p, v_p, lf_p = _layer(
            xp, jnp.zeros((bp, SSD_CONV - 1, SSD_CONV_CH), F32),
            jnp.zeros((bp, SSD_HEADS, SSD_HEAD_DIM, SSD_D_STATE), F32), None, None,
            mk_b.reshape(bp, n_mem, D_MODEL), mv_b.reshape(bp, n_mem, D_MODEL), wd, g_final, l == depth - 1)
        xs, c_s, h_s, k_s, v_s, lf_s = _layer(
            xs, state_ssd_conv[l], state_ssd[l], (l, cache_fox_k, cache_fox_v), cache_fox_logf[l],
            cache_mem_k, cache_mem_v, wd, g_final, l == depth - 1)
        for lst, val in zip(outs, (k_p, v_p, lf_p, h_p, c_p,
                                   mk.reshape(bp, n_mem, MEM_HEADS, MEM_HEAD_DIM),
                                   mv.reshape(bp, n_mem, MEM_HEADS, MEM_HEAD_DIM),
                                   k_s, v_s, lf_s, h_s, c_s)):
            lst.append(val)
    return (xp, xs) + tuple(jnp.stack(o) for o in outs)
```
